```python
import math
import jax, jax.numpy as jnp
from jax import lax
import numpy as np

D_MODEL = 2048
BATCH = 8
SEQ = 8192
DEPTH = 1

MEM_LEN = 256
D_MIX = D_MODEL
POOL_WIDTH = D_MIX // 4
POOL_WINDOWS = (2, 4, 8, 16)
POOL_GROUPS = len(POOL_WINDOWS)
POOL_GROUP_DIM = POOL_WIDTH // POOL_GROUPS
MLA_V_DIM = 128
MLA_HEADS = (D_MIX // 2) // MLA_V_DIM
MLA_NOPE_DIM = 128
MLA_ROPE_DIM = 64
MLA_QK_DIM = MLA_NOPE_DIM + MLA_ROPE_DIM
Q_LORA_RANK = 512
KV_LORA_RANK = 256
X_HEADS = 4
X_WIDTH = D_MIX // 4
X_HEAD_DIM = X_WIDTH // X_HEADS
D_FF = 5632
CONV_WIDTH = 3
ROPE_THETA = 10000.0
NORM_EPS = 1e-6
Q_BLOCK = 128

IN_COLS = POOL_WIDTH + Q_LORA_RANK + KV_LORA_RANK + MLA_ROPE_DIM + X_WIDTH
IN_SPLITS = (
    POOL_WIDTH,
    POOL_WIDTH + Q_LORA_RANK,
    POOL_WIDTH + Q_LORA_RANK + KV_LORA_RANK,
    POOL_WIDTH + Q_LORA_RANK + KV_LORA_RANK + MLA_ROPE_DIM,
)

kernel_name = "hybrid_pool_mla_memxattn_convglu"


def rms_norm(x, g):
    xf = x.astype(jnp.float32)
    y = xf * lax.rsqrt(jnp.mean(xf * xf, axis=-1, keepdims=True) + NORM_EPS)
    return (y * g.astype(jnp.float32)).astype(x.dtype)


def apply_rope(x, pos):
    half = x.shape[-1] // 2
    inv_freq = 1.0 / (ROPE_THETA ** (jnp.arange(half, dtype=jnp.float32) / half))
    ang = pos.astype(jnp.float32)[:, None] * inv_freq[None, :]
    cos = jnp.cos(ang)[None, :, None, :]
    sin = jnp.sin(ang)[None, :, None, :]
    xf = x.astype(jnp.float32)
    x1, x2 = xf[..., :half], xf[..., half:]
    return jnp.concatenate([x1 * cos - x2 * sin, x2 * cos + x1 * sin], axis=-1).astype(x.dtype)


def pool_mixer(p, w_pool, pool_scale):
    B, S, _ = p.shape
    pf = p.astype(jnp.float32).reshape(B, S, POOL_GROUPS, POOL_GROUP_DIM)
    csum = jnp.cumsum(pf, axis=1)
    t = jnp.arange(S)
    outs = []
    for gi, w in enumerate(POOL_WINDOWS):
        cg = csum[:, :, gi]
        lag = jnp.pad(cg, ((0, 0), (w, 0), (0, 0)))[:, :S]
        cnt = jnp.minimum(t + 1, w).astype(jnp.float32)[None, :, None]
        outs.append((cg - lag) / cnt - pf[:, :, gi])
    d = jnp.stack(outs, axis=2).astype(p.dtype)
    y = jnp.einsum('bsgc,gcd->bsgd', d, w_pool).reshape(B, S, POOL_WIDTH)
    return y * pool_scale


def causal_attention_blocks(q, k, v):
    B, S, H, D = q.shape
    Dv = v.shape[-1]
    nb = S // Q_BLOCK
    scale = 1.0 / math.sqrt(D)
    qb = q.reshape(B, nb, Q_BLOCK, H, D).transpose(1, 0, 2, 3, 4)
    kpos = jnp.arange(S)

    def one_block(args):
        qblk, i = args
        qpos = i * Q_BLOCK + jnp.arange(Q_BLOCK)
        s = jnp.einsum('bqhd,bkhd->bhqk', qblk, k).astype(jnp.float32) * scale
        mask = kpos[None, :] <= qpos[:, None]
        s = jnp.where(mask[None, None], s, -jnp.inf)
        pr = jax.nn.softmax(s, axis=-1).astype(v.dtype)
        return jnp.einsum('bhqk,bkhd->bqhd', pr, v)

    o = lax.map(one_block, (qb, jnp.arange(nb)))
    return o.transpose(1, 0, 2, 3, 4).reshape(B, S, H, Dv)


def mla_mixer(q_lat, kv_lat, k_rope, pos, g_q_lat, w_q_up, g_kv_lat, w_kv_up, g_q_mla, g_k_mla):
    B, S, _ = q_lat.shape
    q = (rms_norm(q_lat, g_q_lat) @ w_q_up).reshape(B, S, MLA_HEADS, MLA_QK_DIM)
    kv = (rms_norm(kv_lat, g_kv_lat) @ w_kv_up).reshape(B, S, MLA_HEADS, MLA_NOPE_DIM + MLA_V_DIM)
    k_nope, v = kv[..., :MLA_NOPE_DIM], kv[..., MLA_NOPE_DIM:]
    k_r = jnp.broadcast_to(k_rope[:, :, None, :], (B, S, MLA_HEADS, MLA_ROPE_DIM))
    k = jnp.concatenate([k_nope, k_r], axis=-1)
    q = rms_norm(q, g_q_mla)
    k = rms_norm(k, g_k_mla)
    q = jnp.concatenate([q[..., :MLA_NOPE_DIM], apply_rope(q[..., MLA_NOPE_DIM:], pos)], axis=-1)
    k = jnp.concatenate([k[..., :MLA_NOPE_DIM], apply_rope(k[..., MLA_NOPE_DIM:], pos)], axis=-1)
    o = causal_attention_blocks(q, k, v)
    return o.reshape(B, S, MLA_HEADS * MLA_V_DIM)


def memory_cross_attention(xq, mem, g_mem, w_mem_kv, g_q_x, g_k_x):
    B, S, _ = xq.shape
    M = mem.shape[1]
    q = rms_norm(xq.reshape(B, S, X_HEADS, X_HEAD_DIM), g_q_x)
    mkv = rms_norm(mem, g_mem) @ w_mem_kv
    k = rms_norm(mkv[..., :X_WIDTH].reshape(B, M, X_HEADS, X_HEAD_DIM), g_k_x)
    v = mkv[..., X_WIDTH:].reshape(B, M, X_HEADS, X_HEAD_DIM)
    s = jnp.einsum('bshd,bmhd->bhsm', q, k).astype(jnp.float32) * (1.0 / math.sqrt(X_HEAD_DIM))
    pr = jax.nn.softmax(s, axis=-1).astype(v.dtype)
    o = jnp.einsum('bhsm,bmhd->bshd', pr, v)
    return o.reshape(B, S, X_WIDTH)


def conv_glu_ffn(h, w_gate, w_up, conv_w, conv_b, w_down):
    S = h.shape[1]
    g = h @ w_gate
    gp = jnp.pad(g, ((0, 0), (CONV_WIDTH - 1, 0), (0, 0)))
    gc = conv_b
    for j in range(CONV_WIDTH):
        gc = gc + conv_w[j] * gp[:, j:j + S]
    return (jax.nn.silu(gc) * (h @ w_up)) @ w_down


def _fwd_setup_inputs(seed: int = 0) -> dict:
    key = jax.random.key(seed)
    ks = jax.random.split(key, 24)
    f32 = jnp.float32
    L = DEPTH

    def nrm(k, shape, scale):
        return jax.random.normal(k, shape, f32) * scale

    def gain(k, n):
        return 1.0 + 0.02 * jax.random.normal(k, (L, n), f32)

    return {
        "x": nrm(ks[0], (BATCH, SEQ, D_MODEL), 1.0),
        "mem": nrm(ks[1], (BATCH, MEM_LEN, D_MODEL), 1.0),
        "g_mix": gain(ks[2], D_MODEL),
        "w_in": nrm(ks[3], (L, D_MODEL, IN_COLS), D_MODEL ** -0.5),
        "g_q_lat": gain(ks[4], Q_LORA_RANK),
        "w_q_up": nrm(ks[5], (L, Q_LORA_RANK, MLA_HEADS * MLA_QK_DIM), Q_LORA_RANK ** -0.5),
        "g_kv_lat": gain(ks[6], KV_LORA_RANK),
        "w_kv_up": nrm(ks[7], (L, KV_LORA_RANK, MLA_HEADS * (MLA_NOPE_DIM + MLA_V_DIM)), KV_LORA_RANK ** -0.5),
        "g_q_mla": gain(ks[8], MLA_QK_DIM),
        "g_k_mla": gain(ks[9], MLA_QK_DIM),
        "w_pool": nrm(ks[10], (L, POOL_GROUPS, POOL_GROUP_DIM, POOL_GROUP_DIM), POOL_GROUP_DIM ** -0.5),
        "pool_scale": 1.0 + 0.1 * jax.random.normal(ks[11], (L, POOL_WIDTH), f32),
        "g_mem": gain(ks[12], D_MODEL),
        "w_mem_kv": nrm(ks[13], (L, D_MODEL, 2 * X_WIDTH), D_MODEL ** -0.5),
        "g_q_x": gain(ks[14], X_HEAD_DIM),
        "g_k_x": gain(ks[15], X_HEAD_DIM),
        "w_o": nrm(ks[16], (L, D_MIX, D_MODEL), D_MIX ** -0.5),
        "g_ffn": gain(ks[17], D_MODEL),
        "w_gate": nrm(ks[18], (L, D_MODEL, D_FF), D_MODEL ** -0.5),
        "w_up": nrm(ks[19], (L, D_MODEL, D_FF), D_MODEL ** -0.5),
        "conv_w": nrm(ks[20], (L, CONV_WIDTH, D_FF), CONV_WIDTH ** -0.5),
        "conv_b": nrm(ks[21], (L, D_FF), 0.01),
        "w_down": nrm(ks[22], (L, D_FF, D_MODEL), D_FF ** -0.5),
    }


def _fwd_reference(x, mem, g_mix, w_in, g_q_lat, w_q_up, g_kv_lat, w_kv_up, g_q_mla, g_k_mla,
              w_pool, pool_scale, g_mem, w_mem_kv, g_q_x, g_k_x, w_o, g_ffn,
              w_gate, w_up, conv_w, conv_b, w_down):
    S = x.shape[1]
    pos = jnp.arange(S)
    for l in range(DEPTH):
        h = rms_norm(x, g_mix[l])
        z = h @ w_in[l]
        z_pool, z_q, z_kv, z_kr, z_mq = jnp.split(z, IN_SPLITS, axis=-1)
        y_pool = pool_mixer(z_pool, w_pool[l], pool_scale[l])
        y_mla = mla_mixer(z_q, z_kv, z_kr, pos, g_q_lat[l], w_q_up[l], g_kv_lat[l],
                          w_kv_up[l], g_q_mla[l], g_k_mla[l])
        y_mem = memory_cross_attention(z_mq, mem, g_mem[l], w_mem_kv[l], g_q_x[l], g_k_x[l])
        x = x + jnp.concatenate([y_pool, y_mla, y_mem], axis=-1) @ w_o[l]
        x = x + conv_glu_ffn(rms_norm(x, g_ffn[l]), w_gate[l], w_up[l], conv_w[l], conv_b[l], w_down[l])
    return x


import jax as _jax
import jax.numpy as _jnp

TWIN_FORMAT = 'train_step'
FWD_PARAMS = ['x', 'mem', 'g_mix', 'w_in', 'g_q_lat', 'w_q_up', 'g_kv_lat', 'w_kv_up', 'g_q_mla', 'g_k_mla', 'w_pool', 'pool_scale', 'g_mem', 'w_mem_kv', 'g_q_x', 'g_k_x', 'w_o', 'g_ffn', 'w_gate', 'w_up', 'conv_w', 'conv_b', 'w_down']
TWIN_WEIGHTS = ['g_mix', 'w_in', 'g_q_lat', 'w_q_up', 'g_kv_lat', 'w_kv_up', 'g_q_mla', 'g_k_mla', 'w_pool', 'pool_scale', 'g_mem', 'w_mem_kv', 'g_q_x', 'g_k_x', 'w_o', 'g_ffn', 'w_gate', 'w_up', 'conv_w', 'conv_b', 'w_down']
TWIN_DIFF_INPUT = 'x'
TWIN_INPUTS = ['x', 'mem', 'g_mix', 'w_in', 'g_q_lat', 'w_q_up', 'g_kv_lat', 'w_kv_up', 'g_q_mla', 'g_k_mla', 'w_pool', 'pool_scale', 'g_mem', 'w_mem_kv', 'g_q_x', 'g_k_x', 'w_o', 'g_ffn', 'w_gate', 'w_up', 'conv_w', 'conv_b', 'w_down', 'loss_target', 'm_g_mix', 'm_w_in', 'm_g_q_lat', 'm_w_q_up', 'm_g_kv_lat', 'm_w_kv_up', 'm_g_q_mla', 'm_g_k_mla', 'm_w_pool', 'm_pool_scale', 'm_g_mem', 'm_w_mem_kv', 'm_g_q_x', 'm_g_k_x', 'm_w_o', 'm_g_ffn', 'm_w_gate', 'm_w_up', 'm_conv_w', 'm_conv_b', 'm_w_down', 'v_g_mix', 'v_w_in', 'v_g_q_lat', 'v_w_q_up', 'v_g_kv_lat', 'v_w_kv_up', 'v_g_q_mla', 'v_g_k_mla', 'v_w_pool', 'v_pool_scale', 'v_g_mem', 'v_w_mem_kv', 'v_g_q_x', 'v_g_k_x', 'v_w_o', 'v_g_ffn', 'v_w_gate', 'v_w_up', 'v_conv_w', 'v_conv_b', 'v_w_down']
TWIN_OUTPUTS = ['loss', 'grad_x', 'grad_g_mix', 'grad_w_in', 'grad_g_q_lat', 'grad_w_q_up', 'grad_g_kv_lat', 'grad_w_kv_up', 'grad_g_q_mla', 'grad_g_k_mla', 'grad_w_pool', 'grad_pool_scale', 'grad_g_mem', 'grad_w_mem_kv', 'grad_g_q_x', 'grad_g_k_x', 'grad_w_o', 'grad_g_ffn', 'grad_w_gate', 'grad_w_up', 'grad_conv_w', 'grad_conv_b', 'grad_w_down', 'delta_g_mix', 'delta_w_in', 'delta_g_q_lat', 'delta_w_q_up', 'delta_g_kv_lat', 'delta_w_kv_up', 'delta_g_q_mla', 'delta_g_k_mla', 'delta_w_pool', 'delta_pool_scale', 'delta_g_mem', 'delta_w_mem_kv', 'delta_g_q_x', 'delta_g_k_x', 'delta_w_o', 'delta_g_ffn', 'delta_w_gate', 'delta_w_up', 'delta_conv_w', 'delta_conv_b', 'delta_w_down', 'new_m_g_mix', 'new_m_w_in', 'new_m_g_q_lat', 'new_m_w_q_up', 'new_m_g_kv_lat', 'new_m_w_kv_up', 'new_m_g_q_mla', 'new_m_g_k_mla', 'new_m_w_pool', 'new_m_pool_scale', 'new_m_g_mem', 'new_m_w_mem_kv', 'new_m_g_q_x', 'new_m_g_k_x', 'new_m_w_o', 'new_m_g_ffn', 'new_m_w_gate', 'new_m_w_up', 'new_m_conv_w', 'new_m_conv_b', 'new_m_w_down', 'new_v_g_mix', 'new_v_w_in', 'new_v_g_q_lat', 'new_v_w_q_up', 'new_v_g_kv_lat', 'new_v_w_kv_up', 'new_v_g_q_mla', 'new_v_g_k_mla', 'new_v_w_pool', 'new_v_pool_scale', 'new_v_g_mem', 'new_v_w_mem_kv', 'new_v_g_q_x', 'new_v_g_k_x', 'new_v_w_o', 'new_v_g_ffn', 'new_v_w_gate', 'new_v_w_up', 'new_v_conv_w', 'new_v_conv_b', 'new_v_w_down']
TWIN_LEAF_KINDS = {'loss': 'loss', 'grad_x': 'grad_x', 'grad_g_mix': 'grad_w', 'grad_w_in': 'grad_w', 'grad_g_q_lat': 'grad_w', 'grad_w_q_up': 'grad_w', 'grad_g_kv_lat': 'grad_w', 'grad_w_kv_up': 'grad_w', 'grad_g_q_mla': 'grad_w', 'grad_g_k_mla': 'grad_w', 'grad_w_pool': 'grad_w', 'grad_pool_scale': 'grad_w', 'grad_g_mem': 'grad_w', 'grad_w_mem_kv': 'grad_w', 'grad_g_q_x': 'grad_w', 'grad_g_k_x': 'grad_w', 'grad_w_o': 'grad_w', 'grad_g_ffn': 'grad_w', 'grad_w_gate': 'grad_w', 'grad_w_up': 'grad_w', 'grad_conv_w': 'grad_w', 'grad_conv_b': 'grad_w', 'grad_w_down': 'grad_w', 'delta_g_mix': 'delta_w', 'delta_w_in': 'delta_w', 'delta_g_q_lat': 'delta_w', 'delta_w_q_up': 'delta_w', 'delta_g_kv_lat': 'delta_w', 'delta_w_kv_up': 'delta_w', 'delta_g_q_mla': 'delta_w', 'delta_g_k_mla': 'delta_w', 'delta_w_pool': 'delta_w', 'delta_pool_scale': 'delta_w', 'delta_g_mem': 'delta_w', 'delta_w_mem_kv': 'delta_w', 'delta_g_q_x': 'delta_w', 'delta_g_k_x': 'delta_w', 'delta_w_o': 'delta_w', 'delta_g_ffn': 'delta_w', 'delta_w_gate': 'delta_w', 'delta_w_up': 'delta_w', 'delta_conv_w': 'delta_w', 'delta_conv_b': 'delta_w', 'delta_w_down': 'delta_w', 'new_m_g_mix': 'new_m', 'new_m_w_in': 'new_m', 'new_m_g_q_lat': 'new_m', 'new_m_w_q_up': 'new_m', 'new_m_g_kv_lat': 'new_m', 'new_m_w_kv_up': 'new_m', 'new_m_g_q_mla': 'new_m', 'new_m_g_k_mla': 'new_m', 'new_m_w_pool': 'new_m', 'new_m_pool_scale': 'new_m', 'new_m_g_mem': 'new_m', 'new_m_w_mem_kv': 'new_m', 'new_m_g_q_x': 'new_m', 'new_m_g_k_x': 'new_m', 'new_m_w_o': 'new_m', 'new_m_g_ffn': 'new_m', 'new_m_w_gate': 'new_m', 'new_m_w_up': 'new_m', 'new_m_conv_w': 'new_m', 'new_m_conv_b': 'new_m', 'new_m_w_down': 'new_m', 'new_v_g_mix': 'new_v', 'new_v_w_in': 'new_v', 'new_v_g_q_lat': 'new_v', 'new_v_w_q_up': 'new_v', 'new_v_g_kv_lat': 'new_v', 'new_v_w_kv_up': 'new_v', 'new_v_g_q_mla': 'new_v', 'new_v_g_k_mla': 'new_v', 'new_v_w_pool': 'new_v', 'new_v_pool_scale': 'new_v', 'new_v_g_mem': 'new_v', 'new_v_w_mem_kv': 'new_v', 'new_v_g_q_x': 'new_v', 'new_v_g_k_x': 'new_v', 'new_v_w_o': 'new_v', 'new_v_g_ffn': 'new_v', 'new_v_w_gate': 'new_v', 'new_v_w_up': 'new_v', 'new_v_conv_w': 'new_v', 'new_v_conv_b': 'new_v', 'new_v_w_down': 'new_v'}


def _forward(args):
    return _fwd_reference(*[args[k] for k in FWD_PARAMS])


def _output_shape():
    def fwd():
        inp = _fwd_setup_inputs(0)
        return _fwd_reference(*[inp[k] for k in FWD_PARAMS])
    out = _jax.eval_shape(fwd)
    return out.shape, out.dtype

N_MICROBATCH = 1
ADAM_LR = 0.001
ADAM_B1 = 0.9
ADAM_B2 = 0.999
ADAM_EPS = 1e-08
ADAM_WD = 0.01
ADAM_STEP = 10
PER_EXAMPLE_BATCH_AXIS = {'x': 0, 'mem': 0, 'loss_target': 0}
SHARED_INPUTS = []
_WEIGHT_DTYPES = {'g_mix': _jnp.float32, 'w_in': _jnp.float32, 'g_q_lat': _jnp.float32, 'w_q_up': _jnp.float32, 'g_kv_lat': _jnp.float32, 'w_kv_up': _jnp.float32, 'g_q_mla': _jnp.float32, 'g_k_mla': _jnp.float32, 'w_pool': _jnp.float32, 'pool_scale': _jnp.float32, 'g_mem': _jnp.float32, 'w_mem_kv': _jnp.float32, 'g_q_x': _jnp.float32, 'g_k_x': _jnp.float32, 'w_o': _jnp.float32, 'g_ffn': _jnp.float32, 'w_gate': _jnp.float32, 'w_up': _jnp.float32, 'conv_w': _jnp.float32, 'conv_b': _jnp.float32, 'w_down': _jnp.float32}
MOMENT_SCALE = {'g_mix': 6.211378e+00, 'w_in': 3.644340e-01, 'g_q_lat': 8.700314e-02, 'w_q_up': 5.080506e-02, 'g_kv_lat': 1.156219e+00, 'w_kv_up': 7.033101e-02, 'g_q_mla': 8.586632e-01, 'g_k_mla': 8.580640e-01, 'w_pool': 2.249326e+00, 'pool_scale': 2.557634e+01, 'g_mem': 9.978951e-02, 'w_mem_kv': 9.005862e-02, 'g_q_x': 1.165099e+00, 'g_k_x': 1.166233e+00, 'w_o': 4.111911e-01, 'g_ffn': 2.569244e+01, 'w_gate': 1.490429e-01, 'w_up': 1.545268e-01, 'conv_w': 2.888220e+00, 'conv_b': 3.480764e+00, 'w_down': 2.233192e-01}


def _to_microbatches(a, axis):
    t = _jnp.moveaxis(a, axis, 0)
    t = t.reshape((N_MICROBATCH, t.shape[0] // N_MICROBATCH) + t.shape[1:])
    return _jnp.moveaxis(t, 1, axis + 1)


def setup_inputs(seed: int = 0) -> dict:
    inp = _fwd_setup_inputs(seed)
    key = _jax.random.fold_in(_jax.random.key(seed), 7919)
    shape, _ = _output_shape()
    out = dict(inp)
    out["loss_target"] = _jax.random.normal(_jax.random.fold_in(key, 0), shape, _jnp.float32)
    for i, name in enumerate(TWIN_WEIGHTS):
        w = inp[name].astype(_jnp.float32)
        if MOMENT_SCALE is None:
            s = _jnp.sqrt(_jnp.mean(_jnp.square(w)) + 1e-30)
        else:
            s = MOMENT_SCALE[name]
        km, kv = _jax.random.split(_jax.random.fold_in(key, i + 1))
        out[name] = w
        out["m_" + name] = s * _jax.random.normal(km, w.shape, _jnp.float32)
        out["v_" + name] = (s * s) * _jax.random.uniform(kv, w.shape, _jnp.float32, 0.5, 1.5)
    if N_MICROBATCH > 1:
        for name, axis in PER_EXAMPLE_BATCH_AXIS.items():
            out[name] = _to_microbatches(out[name], axis)
    return {'x': out['x'], 'mem': out['mem'], 'g_mix': out['g_mix'], 'w_in': out['w_in'], 'g_q_lat': out['g_q_lat'], 'w_q_up': out['w_q_up'], 'g_kv_lat': out['g_kv_lat'], 'w_kv_up': out['w_kv_up'], 'g_q_mla': out['g_q_mla'], 'g_k_mla': out['g_k_mla'], 'w_pool': out['w_pool'], 'pool_scale': out['pool_scale'], 'g_mem': out['g_mem'], 'w_mem_kv': out['w_mem_kv'], 'g_q_x': out['g_q_x'], 'g_k_x': out['g_k_x'], 'w_o': out['w_o'], 'g_ffn': out['g_ffn'], 'w_gate': out['w_gate'], 'w_up': out['w_up'], 'conv_w': out['conv_w'], 'conv_b': out['conv_b'], 'w_down': out['w_down'], 'loss_target': out['loss_target'], 'm_g_mix': out['m_g_mix'], 'm_w_in': out['m_w_in'], 'm_g_q_lat': out['m_g_q_lat'], 'm_w_q_up': out['m_w_q_up'], 'm_g_kv_lat': out['m_g_kv_lat'], 'm_w_kv_up': out['m_w_kv_up'], 'm_g_q_mla': out['m_g_q_mla'], 'm_g_k_mla': out['m_g_k_mla'], 'm_w_pool': out['m_w_pool'], 'm_pool_scale': out['m_pool_scale'], 'm_g_mem': out['m_g_mem'], 'm_w_mem_kv': out['m_w_mem_kv'], 'm_g_q_x': out['m_g_q_x'], 'm_g_k_x': out['m_g_k_x'], 'm_w_o': out['m_w_o'], 'm_g_ffn': out['m_g_ffn'], 'm_w_gate': out['m_w_gate'], 'm_w_up': out['m_w_up'], 'm_conv_w': out['m_conv_w'], 'm_conv_b': out['m_conv_b'], 'm_w_down': out['m_w_down'], 'v_g_mix': out['v_g_mix'], 'v_w_in': out['v_w_in'], 'v_g_q_lat': out['v_g_q_lat'], 'v_w_q_up': out['v_w_q_up'], 'v_g_kv_lat': out['v_g_kv_lat'], 'v_w_kv_up': out['v_w_kv_up'], 'v_g_q_mla': out['v_g_q_mla'], 'v_g_k_mla': out['v_g_k_mla'], 'v_w_pool': out['v_w_pool'], 'v_pool_scale': out['v_pool_scale'], 'v_g_mem': out['v_g_mem'], 'v_w_mem_kv': out['v_w_mem_kv'], 'v_g_q_x': out['v_g_q_x'], 'v_g_k_x': out['v_g_k_x'], 'v_w_o': out['v_w_o'], 'v_g_ffn': out['v_g_ffn'], 'v_w_gate': out['v_w_gate'], 'v_w_up': out['v_w_up'], 'v_conv_w': out['v_conv_w'], 'v_conv_b': out['v_conv_b'], 'v_w_down': out['v_w_down']}


def _loss(weights, diff, rest, loss_target):
    with _jax.named_scope("forward"):
        args = {**rest, TWIN_DIFF_INPUT: diff, **{k: w.astype(_WEIGHT_DTYPES[k]) for k, w in weights.items()}}
        y = _forward(args)
    with _jax.named_scope("loss_head"):
        err = _jnp.square(y.astype(_jnp.float32) - loss_target)
        return 0.5 * _jnp.sum(_jnp.mean(err, axis=-1)) if err.ndim else 0.5 * err


def _adamw(w, g, m, v):
    m = ADAM_B1 * m + (1.0 - ADAM_B1) * g
    v = ADAM_B2 * v + (1.0 - ADAM_B2) * _jnp.square(g)
    m_hat = m / (1.0 - ADAM_B1 ** ADAM_STEP)
    v_hat = v / (1.0 - ADAM_B2 ** ADAM_STEP)
    delta = -ADAM_LR * (m_hat / (_jnp.sqrt(v_hat) + ADAM_EPS) + ADAM_WD * w)
    return delta, m, v


def reference(x, mem, g_mix, w_in, g_q_lat, w_q_up, g_kv_lat, w_kv_up, g_q_mla, g_k_mla, w_pool, pool_scale, g_mem, w_mem_kv, g_q_x, g_k_x, w_o, g_ffn, w_gate, w_up, conv_w, conv_b, w_down, loss_target, m_g_mix, m_w_in, m_g_q_lat, m_w_q_up, m_g_kv_lat, m_w_kv_up, m_g_q_mla, m_g_k_mla, m_w_pool, m_pool_scale, m_g_mem, m_w_mem_kv, m_g_q_x, m_g_k_x, m_w_o, m_g_ffn, m_w_gate, m_w_up, m_conv_w, m_conv_b, m_w_down, v_g_mix, v_w_in, v_g_q_lat, v_w_q_up, v_g_kv_lat, v_w_kv_up, v_g_q_mla, v_g_k_mla, v_w_pool, v_pool_scale, v_g_mem, v_w_mem_kv, v_g_q_x, v_g_k_x, v_w_o, v_g_ffn, v_w_gate, v_w_up, v_conv_w, v_conv_b, v_w_down):
    given = dict(x=x, mem=mem, g_mix=g_mix, w_in=w_in, g_q_lat=g_q_lat, w_q_up=w_q_up, g_kv_lat=g_kv_lat, w_kv_up=w_kv_up, g_q_mla=g_q_mla, g_k_mla=g_k_mla, w_pool=w_pool, pool_scale=pool_scale, g_mem=g_mem, w_mem_kv=w_mem_kv, g_q_x=g_q_x, g_k_x=g_k_x, w_o=w_o, g_ffn=g_ffn, w_gate=w_gate, w_up=w_up, conv_w=conv_w, conv_b=conv_b, w_down=w_down, loss_target=loss_target, m_g_mix=m_g_mix, m_w_in=m_w_in, m_g_q_lat=m_g_q_lat, m_w_q_up=m_w_q_up, m_g_kv_lat=m_g_kv_lat, m_w_kv_up=m_w_kv_up, m_g_q_mla=m_g_q_mla, m_g_k_mla=m_g_k_mla, m_w_pool=m_w_pool, m_pool_scale=m_pool_scale, m_g_mem=m_g_mem, m_w_mem_kv=m_w_mem_kv, m_g_q_x=m_g_q_x, m_g_k_x=m_g_k_x, m_w_o=m_w_o, m_g_ffn=m_g_ffn, m_w_gate=m_w_gate, m_w_up=m_w_up, m_conv_w=m_conv_w, m_conv_b=m_conv_b, m_w_down=m_w_down, v_g_mix=v_g_mix, v_w_in=v_w_in, v_g_q_lat=v_g_q_lat, v_w_q_up=v_w_q_up, v_g_kv_lat=v_g_kv_lat, v_w_kv_up=v_w_kv_up, v_g_q_mla=v_g_q_mla, v_g_k_mla=v_g_k_mla, v_w_pool=v_w_pool, v_pool_scale=v_pool_scale, v_g_mem=v_g_mem, v_w_mem_kv=v_w_mem_kv, v_g_q_x=v_g_q_x, v_g_k_x=v_g_k_x, v_w_o=v_w_o, v_g_ffn=v_g_ffn, v_w_gate=v_w_gate, v_w_up=v_w_up, v_conv_w=v_conv_w, v_conv_b=v_conv_b, v_w_down=v_w_down)
    weights = {n: given[n] for n in TWIN_WEIGHTS}
    shared = {n: given[n] for n in SHARED_INPUTS}
    per_example = {n: given[n] for n in ['x', 'mem']}
    grad_fn = _jax.value_and_grad(_loss, argnums=(0, 1))

    def one_microbatch(ex, loss_target):
        ex = dict(ex)
        diff = ex.pop(TWIN_DIFF_INPUT)
        return grad_fn(weights, diff, {**shared, **ex}, loss_target)

    if N_MICROBATCH == 1:
        loss, (grad_w, grad_x) = one_microbatch(per_example, given["loss_target"])
    else:
        def body(carry, xs):
            loss_sum, grad_sum = carry
            l_k, (gw_k, gx_k) = one_microbatch(xs[0], xs[1])
            with _jax.named_scope("update"):
                return (loss_sum + l_k, _jax.tree.map(_jnp.add, grad_sum, gw_k)), gx_k

        init = (_jnp.zeros((), _jnp.float32), _jax.tree.map(_jnp.zeros_like, weights))
        (loss, grad_w), grad_x = _jax.lax.scan(body, init, (per_example, given["loss_target"]))
    with _jax.named_scope("update"):
        delta_w, new_m, new_v = {}, {}, {}
        for n in TWIN_WEIGHTS:
            delta_w[n], new_m[n], new_v[n] = _adamw(weights[n], grad_w[n], given["m_" + n], given["v_" + n])
    return (loss, grad_x, *[grad_w[n] for n in TWIN_WEIGHTS], *[delta_w[n] for n in TWIN_WEIGHTS],
            *[new_m[n] for n in TWIN_WEIGHTS], *[new_v[n] for n in TWIN_WEIGHTS])
```

```python
import functools
import math

import numpy as np
import jax
import jax.numpy as jnp
from jax import lax
from jax.experimental import pallas as pl
from jax.experimental.pallas import tpu as pltpu

F32, BF16 = jnp.float32, jnp.bfloat16
NORM_EPS = 1e-6
ROPE_THETA = 10000.0
V7X_VMEM_LIMIT_BYTES = 48 * 1024 * 1024
N_CHIPS = 4
N_DEV = 8

POOL_W = 512
POOL_WINDOWS = (2, 4, 8, 16)
HEADS = 8
NOPE, ROPE, QK = 128, 64, 192
HEAD_PAD = 256
Q_RANK, KV_RANK = 512, 256
X_HEADS, X_DIM = 4, 128
Z_COLS = 1920
Z_POOL_CB, Z_Q_CB, Z_MQ_CB = 0, 1, 2
Z_KV_CB = 6
Z_KR_CB = 14

ADAM_LR, ADAM_B1, ADAM_B2, ADAM_EPS, ADAM_WD, ADAM_STEP = 0.001, 0.9, 0.999, 1e-08, 0.01, 10

MESH = pl.DeviceIdType.MESH


def _cp(sem):
    return pltpu.CompilerParams(dimension_semantics=sem, vmem_limit_bytes=V7X_VMEM_LIMIT_BYTES)


def _row_tile(S):
    return 256 if S % 256 == 0 and S >= 2048 else 128


def _attn_tile(S):
    return 512 if S % 512 == 0 and S >= 4096 else 128


def _pick(dim, prefs):
    for p in prefs:
        if dim % p == 0:
            return p
    return dim


_DN = {"nn": (((1,), (0,)), ((), ())), "nt": (((1,), (1,)), ((), ())), "tn": (((0,), (0,)), ((), ()))}


def _mm(a, b, *, mode, grid, blocks, maps, out_shape, out_dtypes, add=None, name):
    nk = grid[2]
    dn = _DN[mode]
    n_out = len(out_dtypes)

    def body(*refs):
        a_ref, b_ref = refs[0], refs[1]
        add_ref = refs[2] if add is not None else None
        p = 2 + (add is not None)
        o_refs = refs[p:p + n_out]
        acc = refs[p + n_out]
        k = pl.program_id(2)

        @pl.when(k == 0)
        def _():
            acc[...] = jnp.zeros_like(acc)

        acc[...] += lax.dot_general(a_ref[...].astype(BF16), b_ref[...].astype(BF16), dn,
                                    preferred_element_type=F32)

        @pl.when(k == nk - 1)
        def _():
            r = acc[...]
            if add_ref is not None:
                r = r + add_ref[...]
            for o in o_refs:
                o[...] = r.astype(o.dtype)

    a_blk, b_blk, o_blk = blocks
    a_map, b_map, o_map = maps
    in_specs = [pl.BlockSpec(a_blk, a_map), pl.BlockSpec(b_blk, b_map)]
    args = [a, b]
    if add is not None:
        in_specs.append(pl.BlockSpec(o_blk, o_map))
        args.append(add)
    outs = pl.pallas_call(
        body, grid=grid, in_specs=in_specs,
        out_specs=[pl.BlockSpec(o_blk, o_map) for _ in out_dtypes],
        out_shape=[jax.ShapeDtypeStruct(out_shape, d) for d in out_dtypes],
        scratch_shapes=[pltpu.VMEM(o_blk, F32)],
        compiler_params=_cp(("parallel", "parallel", "arbitrary")), name=name)(*args)
    return outs[0] if n_out == 1 else outs


def _ident(j):
    return j


def mm_nn(a, w, *, nsh=1, tm, tn, tk, out_dtypes=(F32,), add=None, perm=_ident, name):
    M, K = a.shape
    n = w.shape[1]
    N = nsh * n
    assert w.shape[0] == nsh * K and n % tn == 0 and K % tk == 0 and M % tm == 0
    npt, kt = n // tn, K // tk
    return _mm(a, w, mode="nn", grid=(M // tm, N // tn, kt),
               blocks=((tm, tk), (tk, tn), (tm, tn)),
               maps=(lambda i, j, k: (i, k), lambda i, j, k: ((j // npt) * kt + k, j % npt),
                     lambda i, j, k: (i, perm(j))),
               out_shape=(M, N), out_dtypes=out_dtypes, add=add, name=name)


def mm_nt(d, w, *, nsh=1, tm, to, tc, out_dtypes=(F32,), add=None, perm=_ident, name):
    M, N = d.shape
    n = w.shape[1]
    K = w.shape[0] // nsh
    assert nsh * n == N and n % tc == 0 and K % to == 0 and M % tm == 0
    cpt, ot = n // tc, K // to
    return _mm(d, w, mode="nt", grid=(M // tm, ot, N // tc),
               blocks=((tm, tc), (to, tc), (tm, to)),
               maps=(lambda i, j, c: (i, perm(c)), lambda i, j, c: ((c // cpt) * ot + j, c % cpt),
                     lambda i, j, c: (i, j)),
               out_shape=(M, K), out_dtypes=out_dtypes, add=add, name=name)


def mm_tn(x, d, *, nsh=1, to, tn, tk, out_dtypes=(F32,), perm=_ident, name):
    M, K = x.shape
    N = d.shape[1]
    n = N // nsh
    assert n % tn == 0 and K % to == 0 and M % tk == 0
    npt, ot = n // tn, K // to
    return _mm(x, d, mode="tn", grid=(ot, N // tn, M // tk),
               blocks=((tk, to), (tk, tn), (to, tn)),
               maps=(lambda i, j, k: (k, i), lambda i, j, k: (k, perm(j)),
                     lambda i, j, k: ((j // npt) * ot + i, j % npt)),
               out_shape=(nsh * K, n), out_dtypes=out_dtypes, name=name)


def _rms_fwd(x, g, *, C, cb=0, name):
    S = x.shape[0]
    tm = _row_tile(S) if S >= 128 else S

    def body(x_ref, g_ref, o_ref):
        xv = x_ref[...]
        r = lax.rsqrt(jnp.mean(xv * xv, axis=-1, keepdims=True) + NORM_EPS)
        o_ref[...] = ((xv * r) * g_ref[...]).astype(o_ref.dtype)

    return pl.pallas_call(
        body, grid=(S // tm,),
        in_specs=[pl.BlockSpec((tm, C), lambda i: (i, cb)), pl.BlockSpec((1, C), lambda i: (0, 0))],
        out_specs=pl.BlockSpec((tm, C), lambda i: (i, 0)),
        out_shape=jax.ShapeDtypeStruct((S, C), BF16),
        compiler_params=_cp(("parallel",)), name=name)(x, g)


def _rms_bwd(x, dh, g, *, C, cb=0, res=None, out_dtypes=(F32,), name):
    S = x.shape[0]
    tm = _row_tile(S) if S >= 128 else S
    n_out = len(out_dtypes)

    def body(*refs):
        x_ref, dh_ref, g_ref = refs[:3]
        res_ref = refs[3] if res is not None else None
        p = 3 + (res is not None)
        outs = refs[p:p + n_out]
        dg_ref = refs[p + n_out]
        i = pl.program_id(0)
        xv = x_ref[...]
        r = lax.rsqrt(jnp.mean(xv * xv, axis=-1, keepdims=True) + NORM_EPS)
        n = xv * r
        dhv = dh_ref[...].astype(F32)
        dn = dhv * g_ref[...]
        c = jnp.mean(dn * n, axis=-1, keepdims=True)
        dx = r * (dn - n * c)
        if res_ref is not None:
            dx = res_ref[...] + dx
        for o in outs:
            o[...] = dx.astype(o.dtype)

        @pl.when(i == 0)
        def _():
            dg_ref[...] = jnp.zeros_like(dg_ref)

        dg_ref[...] += jnp.sum(dhv * n, axis=0, keepdims=True)

    row = pl.BlockSpec((tm, C), lambda i: (i, 0))
    in_specs = [pl.BlockSpec((tm, C), lambda i: (i, cb)), row, pl.BlockSpec((1, C), lambda i: (0, 0))]
    args = [x, dh, g]
    if res is not None:
        in_specs.append(row)
        args.append(res)
    return pl.pallas_call(
        body, grid=(S // tm,), in_specs=in_specs,
        out_specs=[row] * n_out + [pl.BlockSpec((1, C), lambda i: (0, 0))],
        out_shape=[jax.ShapeDtypeStruct((S, C), d) for d in out_dtypes] + [jax.ShapeDtypeStruct((1, C), F32)],
        compiler_params=_cp(("arbitrary",)), name=name)(*args)


def _pool_cnt(t0, rows, w):
    t = t0 + lax.broadcasted_iota(jnp.int32, (rows, 1), 0)
    return jnp.minimum(t + 1, w).astype(F32)


def _pool_d(halo, tile, gi, t0, tm):
    s = jnp.concatenate([halo, tile], axis=0)
    for step in (1, 2, 4, 8)[:gi + 1]:
        s = s + pltpu.roll(s, step, 0)
    return s[16:] / _pool_cnt(t0, tm, POOL_WINDOWS[gi]) - tile


def _pool_fwd(z, w_pool, pool_scale, *, name):
    S = z.shape[0]
    tm = _row_tile(S)
    hb = tm // 16

    def body(z_ref, h_ref, w_ref, sc_ref, o_ref):
        i = pl.program_id(0)
        halo = h_ref[...] * (i > 0).astype(F32)
        for gi in range(4):
            cs = slice(gi * 128, (gi + 1) * 128)
            d = _pool_d(halo[:, cs], z_ref[:, cs], gi, i * tm, tm)
            yp = jnp.dot(d.astype(BF16), w_ref[gi], preferred_element_type=F32)
            o_ref[:, cs] = (yp * sc_ref[:, cs]).astype(o_ref.dtype)

    return pl.pallas_call(
        body, grid=(S // tm,),
        in_specs=[pl.BlockSpec((tm, POOL_W), lambda i: (i, Z_POOL_CB)),
                  pl.BlockSpec((16, POOL_W), lambda i: (jnp.maximum(i * hb - 1, 0), Z_POOL_CB)),
                  pl.BlockSpec((4, 128, 128), lambda i: (0, 0, 0)),
                  pl.BlockSpec((1, POOL_W), lambda i: (0, 0))],
        out_specs=pl.BlockSpec((tm, POOL_W), lambda i: (i, 0)),
        out_shape=jax.ShapeDtypeStruct((S, POOL_W), BF16),
        compiler_params=_cp(("parallel",)), name=name)(z, z, w_pool, pool_scale)


def _pool_bwd(z, d_cat, w_pool, pool_scale, *, name):
    S = z.shape[0]
    tm = _row_tile(S)
    hb = tm // 16
    nt = S // tm
    E = tm + 16

    def body(z_ref, h_ref, dy_ref, dyn_ref, w_ref, sc_ref, dz_ref, gw_ref, gs_ref):
        i = pl.program_id(0)

        @pl.when(i == 0)
        def _():
            gw_ref[...] = jnp.zeros_like(gw_ref)
            gs_ref[...] = jnp.zeros_like(gs_ref)

        halo = h_ref[...] * (i > 0).astype(F32)
        dy_next = dyn_ref[...] * (i < nt - 1).astype(F32)
        for gi in range(4):
            cs = slice(gi * 128, (gi + 1) * 128)
            w = w_ref[gi]
            d = _pool_d(halo[:, cs], z_ref[:, cs], gi, i * tm, tm)
            db = d.astype(BF16)
            dy = dy_ref[:, cs]
            yp = jnp.dot(db, w, preferred_element_type=F32)
            gs_ref[:, cs] += jnp.sum(dy * yp, axis=0, keepdims=True)
            sc = sc_ref[:, cs]
            dys = (dy * sc).astype(BF16)
            gw_ref[gi] += lax.dot_general(db, dys, _DN["tn"], preferred_element_type=F32)
            dys_ext = jnp.concatenate([dys, (dy_next[:, cs] * sc).astype(BF16)], axis=0)
            dd = lax.dot_general(dys_ext, w, _DN["nt"], preferred_element_type=F32)
            r = dd / _pool_cnt(i * tm, E, POOL_WINDOWS[gi])
            for step in (1, 2, 4, 8)[:gi + 1]:
                r = r + pltpu.roll(r, E - step, 0)
            dz_ref[:, cs] = (r[:tm] - dd[:tm]).astype(dz_ref.dtype)

    return pl.pallas_call(
        body, grid=(nt,),
        in_specs=[pl.BlockSpec((tm, POOL_W), lambda i: (i, Z_POOL_CB)),
                  pl.BlockSpec((16, POOL_W), lambda i: (jnp.maximum(i * hb - 1, 0), Z_POOL_CB)),
                  pl.BlockSpec((tm, POOL_W), lambda i: (i, 0)),
                  pl.BlockSpec((16, POOL_W), lambda i: (jnp.minimum((i + 1) * hb, S // 16 - 1), 0)),
                  pl.BlockSpec((4, 128, 128), lambda i: (0, 0, 0)),
                  pl.BlockSpec((1, POOL_W), lambda i: (0, 0))],
        out_specs=[pl.BlockSpec((tm, POOL_W), lambda i: (i, 0)),
                   pl.BlockSpec((4, 128, 128), lambda i: (0, 0, 0)),
                   pl.BlockSpec((1, POOL_W), lambda i: (0, 0))],
        out_shape=[jax.ShapeDtypeStruct((S, POOL_W), BF16),
                   jax.ShapeDtypeStruct((4, 128, 128), F32),
                   jax.ShapeDtypeStruct((1, POOL_W), F32)],
        compiler_params=_cp(("arbitrary",)), name=name)(z, z, d_cat, d_cat, w_pool, pool_scale)


def _rope_tables(S):
    half = ROPE // 2
    inv_freq = 1.0 / (ROPE_THETA ** (jnp.arange(half, dtype=F32) / half))
    ang = jnp.arange(S).astype(F32)[:, None] * inv_freq[None, :]
    cos, sin = jnp.cos(ang), jnp.sin(ang)
    zero = jnp.zeros((S, half), F32)
    cos_t = jnp.concatenate([cos, cos, zero, zero], axis=1)
    sa_t = jnp.concatenate([-sin, zero, zero, zero], axis=1)
    sb_t = jnp.concatenate([zero, sin, zero, zero], axis=1)
    return cos_t, sa_t, sb_t


def _head_fwd(xn, xr, gn, gr, cos, sa, sb):
    ms = (jnp.sum(xn * xn, axis=-1, keepdims=True) + jnp.sum(xr * xr, axis=-1, keepdims=True)) * (1.0 / QK)
    r = lax.rsqrt(ms + NORM_EPS)
    on = (xn * r) * gn
    yr = (xr * r) * gr
    orr = yr * cos + pltpu.roll(yr, 96, 1) * sa + pltpu.roll(yr, 32, 1) * sb
    return on, orr


def _head_bwd(xn, xr, gn, gr, don, dor, cos, sa, sb):
    ms = (jnp.sum(xn * xn, axis=-1, keepdims=True) + jnp.sum(xr * xr, axis=-1, keepdims=True)) * (1.0 / QK)
    r = lax.rsqrt(ms + NORM_EPS)
    nn, nr = xn * r, xr * r
    dyr = dor * cos + pltpu.roll(dor * sa, 32, 1) + pltpu.roll(dor * sb, 96, 1)
    ggn, ggr = don * nn, dyr * nr
    dnn, dnr = don * gn, dyr * gr
    c = (jnp.sum(dnn * nn, axis=-1, keepdims=True) + jnp.sum(dnr * nr, axis=-1, keepdims=True)) * (1.0 / QK)
    return r * (dnn - nn * c), r * (dnr - nr * c), ggn, ggr


def _qkrope_fwd(qraw, kvraw, z, gq, gk, tabs, *, name):
    S = qraw.shape[0]
    tm = _row_tile(S)

    def body(q_ref, kv_ref, zkr_ref, gq_ref, gk_ref, cos_ref, sa_ref, sb_ref, qo_ref, ko_ref, vo_ref):
        cos, sa, sb = cos_ref[...], sa_ref[...], sb_ref[...]
        zkr = zkr_ref[...]
        gqn, gqr, gkn, gkr = gq_ref[:, :128], gq_ref[:, 128:], gk_ref[:, :128], gk_ref[:, 128:]
        for h in range(HEADS):
            b = h * HEAD_PAD
            on, orr = _head_fwd(q_ref[:, b:b + 128], q_ref[:, b + 128:b + 256], gqn, gqr, cos, sa, sb)
            qo_ref[:, b:b + 128] = on.astype(BF16)
            qo_ref[:, b + 128:b + 256] = orr.astype(BF16)
            on, orr = _head_fwd(kv_ref[:, h * 128:(h + 1) * 128], zkr, gkn, gkr, cos, sa, sb)
            ko_ref[:, b:b + 128] = on.astype(BF16)
            ko_ref[:, b + 128:b + 256] = orr.astype(BF16)
        vo_ref[...] = kv_ref[:, HEADS * 128:].astype(BF16)

    W = HEADS * HEAD_PAD
    row = lambda c: pl.BlockSpec((tm, c), lambda i: (i, 0))
    vec = lambda c: pl.BlockSpec((1, c), lambda i: (0, 0))
    return pl.pallas_call(
        body, grid=(S // tm,),
        in_specs=[row(W), row(W), pl.BlockSpec((tm, 128), lambda i: (i, Z_KR_CB)), vec(256), vec(256),
                  row(128), row(128), row(128)],
        out_specs=[row(W), row(W), row(HEADS * 128)],
        out_shape=[jax.ShapeDtypeStruct((S, W), BF16), jax.ShapeDtypeStruct((S, W), BF16),
                   jax.ShapeDtypeStruct((S, HEADS * 128), BF16)],
        compiler_params=_cp(("parallel",)), name=name)(qraw, kvraw, z, gq, gk, *tabs)


def _qkrope_bwd(qraw, kvraw, z, gq, gk, tabs, dq, dk, dv, *, name):
    S = qraw.shape[0]
    tm = _row_tile(S)

    def body(q_ref, kv_ref, zkr_ref, gq_ref, gk_ref, cos_ref, sa_ref, sb_ref, dq_ref, dk_ref, dv_ref,
             dqo_ref, dkvo_ref, dkr_ref, ggq_ref, ggk_ref):
        i = pl.program_id(0)

        @pl.when(i == 0)
        def _():
            ggq_ref[...] = jnp.zeros_like(ggq_ref)
            ggk_ref[...] = jnp.zeros_like(ggk_ref)

        cos, sa, sb = cos_ref[...], sa_ref[...], sb_ref[...]
        zkr = zkr_ref[...]
        gqn, gqr, gkn, gkr = gq_ref[:, :128], gq_ref[:, 128:], gk_ref[:, :128], gk_ref[:, 128:]
        dkr = jnp.zeros((tm, 128), F32)
        sq_n = jnp.zeros((1, 128), F32)
        sq_r = jnp.zeros((1, 128), F32)
        sk_n = jnp.zeros((1, 128), F32)
        sk_r = jnp.zeros((1, 128), F32)
        for h in range(HEADS):
            b = h * HEAD_PAD
            dxn, dxr, ggn, ggr = _head_bwd(q_ref[:, b:b + 128], q_ref[:, b + 128:b + 256], gqn, gqr,
                                           dq_ref[:, b:b + 128], dq_ref[:, b + 128:b + 256], cos, sa, sb)
            dqo_ref[:, b:b + 128] = dxn.astype(BF16)
            dqo_ref[:, b + 128:b + 256] = dxr.astype(BF16)
            sq_n += jnp.sum(ggn, axis=0, keepdims=True)
            sq_r += jnp.sum(ggr, axis=0, keepdims=True)
            dxn, dxr, ggn, ggr = _head_bwd(kv_ref[:, h * 128:(h + 1) * 128], zkr, gkn, gkr,
                                           dk_ref[:, b:b + 128], dk_ref[:, b + 128:b + 256], cos, sa, sb)
            dkvo_ref[:, h * 128:(h + 1) * 128] = dxn.astype(BF16)
            dkr += dxr
            sk_n += jnp.sum(ggn, axis=0, keepdims=True)
            sk_r += jnp.sum(ggr, axis=0, keepdims=True)
        dkvo_ref[:, HEADS * 128:] = dv_ref[...].astype(BF16)
        dkr_ref[...] = dkr.astype(BF16)
        ggq_ref[:, :128] += sq_n
        ggq_ref[:, 128:] += sq_r
        ggk_ref[:, :128] += sk_n
        ggk_ref[:, 128:] += sk_r

    W = HEADS * HEAD_PAD
    row = lambda c: pl.BlockSpec((tm, c), lambda i: (i, 0))
    vec = lambda c: pl.BlockSpec((1, c), lambda i: (0, 0))
    return pl.pallas_call(
        body, grid=(S // tm,),
        in_specs=[row(W), row(W), pl.BlockSpec((tm, 128), lambda i: (i, Z_KR_CB)), vec(256), vec(256),
                  row(128), row(128), row(128), row(W), row(W), row(HEADS * 128)],
        out_specs=[row(W), row(W), row(128), vec(256), vec(256)],
        out_shape=[jax.ShapeDtypeStruct((S, W), BF16), jax.ShapeDtypeStruct((S, W), BF16),
                   jax.ShapeDtypeStruct((S, 128), BF16),
                   jax.ShapeDtypeStruct((1, 256), F32), jax.ShapeDtypeStruct((1, 256), F32)],
        compiler_params=_cp(("arbitrary",)), name=name)(qraw, kvraw, z, gq, gk, *tabs, dq, dk, dv)


def _tri_tables(nq, by_k):
    if by_k:
        pairs = [(qi, ki) for ki in range(nq) for qi in range(ki, nq)]
    else:
        pairs = [(qi, ki) for qi in range(nq) for ki in range(qi + 1)]
    arr = np.asarray(pairs, np.int32)
    return jnp.asarray(arr[:, 0]), jnp.asarray(arr[:, 1])


def _scores(q_ref, k_ref, qi, ki, tq, scale):
    s = lax.dot_general(q_ref[...], k_ref[...], _DN["nt"], preferred_element_type=F32) * scale
    row = lax.broadcasted_iota(jnp.int32, (tq, tq), 0)
    col = lax.broadcasted_iota(jnp.int32, (tq, tq), 1)
    lim = jnp.where(ki == qi, 0, tq)
    return jnp.where(col > row + lim, -jnp.inf, s)


def _flash_fwd(q, k, v, *, name):
    S = q.shape[0]
    tq = _attn_tile(S)
    nq = S // tq
    qt, kt = _tri_tables(nq, by_k=False)
    scale = 1.0 / math.sqrt(QK)

    def body(qt_ref, kt_ref, q_ref, k_ref, v_ref, o_ref, ob_ref, lse_ref, m_sc, l_sc, acc_sc):
        t = pl.program_id(1)
        qi, ki = qt_ref[t], kt_ref[t]

        @pl.when(ki == 0)
        def _():
            m_sc[...] = jnp.full_like(m_sc, -jnp.inf)
            l_sc[...] = jnp.zeros_like(l_sc)
            acc_sc[...] = jnp.zeros_like(acc_sc)

        s = _scores(q_ref, k_ref, qi, ki, tq, scale)
        m_prev = m_sc[...]
        m_new = jnp.maximum(m_prev, jnp.max(s, axis=1, keepdims=True))
        alpha = jnp.exp(m_prev - m_new)
        p = jnp.exp(s - m_new)
        l_sc[...] = alpha * l_sc[...] + jnp.sum(p, axis=1, keepdims=True)
        acc_sc[...] = alpha * acc_sc[...] + jnp.dot(p.astype(BF16), v_ref[...], preferred_element_type=F32)
        m_sc[...] = m_new

        @pl.when(ki == qi)
        def _():
            o = acc_sc[...] / l_sc[...]
            o_ref[...] = o
            ob_ref[...] = o.astype(BF16)
            lse_ref[...] = jnp.broadcast_to(m_sc[...] + jnp.log(l_sc[...]), (tq, 128))

    grid_spec = pltpu.PrefetchScalarGridSpec(
        num_scalar_prefetch=2, grid=(HEADS, qt.shape[0]),
        in_specs=[pl.BlockSpec((tq, HEAD_PAD), lambda h, t, a, b: (a[t], h)),
                  pl.BlockSpec((tq, HEAD_PAD), lambda h, t, a, b: (b[t], h)),
                  pl.BlockSpec((tq, 128), lambda h, t, a, b: (b[t], h))],
        out_specs=[pl.BlockSpec((tq, 128), lambda h, t, a, b: (a[t], h)),
                   pl.BlockSpec((tq, 128), lambda h, t, a, b: (a[t], h)),
                   pl.BlockSpec((None, tq, 128), lambda h, t, a, b: (h, a[t], 0))],
        scratch_shapes=[pltpu.VMEM((tq, 1), F32), pltpu.VMEM((tq, 1), F32), pltpu.VMEM((tq, 128), F32)])
    return pl.pallas_call(
        body, grid_spec=grid_spec,
        out_shape=[jax.ShapeDtypeStruct((S, HEADS * 128), F32), jax.ShapeDtypeStruct((S, HEADS * 128), BF16),
                   jax.ShapeDtypeStruct((HEADS, S, 128), F32)],
        compiler_params=_cp(("parallel", "arbitrary")), name=name)(qt, kt, q, k, v)


def _flash_bwd_dq(q, k, v, o, lse, d_cat, *, name):
    S = q.shape[0]
    tq = _attn_tile(S)
    nq = S // tq
    qt, kt = _tri_tables(nq, by_k=False)
    scale = 1.0 / math.sqrt(QK)

    def body(qt_ref, kt_ref, q_ref, k_ref, v_ref, o_ref, lse_ref, do_ref, dq_ref, acc_sc):
        t = pl.program_id(1)
        qi, ki = qt_ref[t], kt_ref[t]

        @pl.when(ki == 0)
        def _():
            acc_sc[...] = jnp.zeros_like(acc_sc)

        s = _scores(q_ref, k_ref, qi, ki, tq, scale)
        p = jnp.exp(s - lse_ref[:, :1])
        do = do_ref[...]
        dp = lax.dot_general(do.astype(BF16), v_ref[...], _DN["nt"], preferred_element_type=F32)
        delta = jnp.sum(do * o_ref[...], axis=1, keepdims=True)
        ds = (p * (dp - delta)).astype(BF16)
        acc_sc[...] += jnp.dot(ds, k_ref[...], preferred_element_type=F32) * scale

        @pl.when(ki == qi)
        def _():
            dq_ref[...] = acc_sc[...]

    grid_spec = pltpu.PrefetchScalarGridSpec(
        num_scalar_prefetch=2, grid=(HEADS, qt.shape[0]),
        in_specs=[pl.BlockSpec((tq, HEAD_PAD), lambda h, t, a, b: (a[t], h)),
                  pl.BlockSpec((tq, HEAD_PAD), lambda h, t, a, b: (b[t], h)),
                  pl.BlockSpec((tq, 128), lambda h, t, a, b: (b[t], h)),
                  pl.BlockSpec((tq, 128), lambda h, t, a, b: (a[t], h)),
                  pl.BlockSpec((None, tq, 128), lambda h, t, a, b: (h, a[t], 0)),
                  pl.BlockSpec((tq, 128), lambda h, t, a, b: (a[t], 4 + h))],
        out_specs=pl.BlockSpec((tq, HEAD_PAD), lambda h, t, a, b: (a[t], h)),
        scratch_shapes=[pltpu.VMEM((tq, HEAD_PAD), F32)])
    return pl.pallas_call(
        body, grid_spec=grid_spec,
        out_shape=jax.ShapeDtypeStruct((S, HEADS * HEAD_PAD), F32),
        compiler_params=_cp(("parallel", "arbitrary")), name=name)(qt, kt, q, k, v, o, lse, d_cat)


def _flash_bwd_dkv(q, k, v, o, lse, d_cat, *, name):
    S = q.shape[0]
    tq = _attn_tile(S)
    nq = S // tq
    qt, kt = _tri_tables(nq, by_k=True)
    scale = 1.0 / math.sqrt(QK)

    def body(qt_ref, kt_ref, q_ref, k_ref, v_ref, o_ref, lse_ref, do_ref, dk_ref, dv_ref, dk_sc, dv_sc):
        t = pl.program_id(1)
        qi, ki = qt_ref[t], kt_ref[t]

        @pl.when(qi == ki)
        def _():
            dk_sc[...] = jnp.zeros_like(dk_sc)
            dv_sc[...] = jnp.zeros_like(dv_sc)

        s = _scores(q_ref, k_ref, qi, ki, tq, scale)
        p = jnp.exp(s - lse_ref[:, :1])
        do = do_ref[...]
        dob = do.astype(BF16)
        dv_sc[...] += lax.dot_general(p.astype(BF16), dob, _DN["tn"], preferred_element_type=F32)
        dp = lax.dot_general(dob, v_ref[...], _DN["nt"], preferred_element_type=F32)
        delta = jnp.sum(do * o_ref[...], axis=1, keepdims=True)
        ds = (p * (dp - delta)).astype(BF16)
        dk_sc[...] += lax.dot_general(ds, q_ref[...], _DN["tn"], preferred_element_type=F32) * scale

        @pl.when(qi == nq - 1)
        def _():
            dk_ref[...] = dk_sc[...]
            dv_ref[...] = dv_sc[...]

    grid_spec = pltpu.PrefetchScalarGridSpec(
        num_scalar_prefetch=2, grid=(HEADS, qt.shape[0]),
        in_specs=[pl.BlockSpec((tq, HEAD_PAD), lambda h, t, a, b: (a[t], h)),
                  pl.BlockSpec((tq, HEAD_PAD), lambda h, t, a, b: (b[t], h)),
                  pl.BlockSpec((tq, 128), lambda h, t, a, b: (b[t], h)),
                  pl.BlockSpec((tq, 128), lambda h, t, a, b: (a[t], h)),
                  pl.BlockSpec((None, tq, 128), lambda h, t, a, b: (h, a[t], 0)),
                  pl.BlockSpec((tq, 128), lambda h, t, a, b: (a[t], 4 + h))],
        out_specs=[pl.BlockSpec((tq, HEAD_PAD), lambda h, t, a, b: (b[t], h)),
                   pl.BlockSpec((tq, 128), lambda h, t, a, b: (b[t], h))],
        scratch_shapes=[pltpu.VMEM((tq, HEAD_PAD), F32), pltpu.VMEM((tq, 128), F32)])
    return pl.pallas_call(
        body, grid_spec=grid_spec,
        out_shape=[jax.ShapeDtypeStruct((S, HEADS * HEAD_PAD), F32), jax.ShapeDtypeStruct((S, HEADS * 128), F32)],
        compiler_params=_cp(("parallel", "arbitrary")), name=name)(qt, kt, q, k, v, o, lse, d_cat)


def _memk_fwd(mkv, gkx, *, name):
    M = mkv.shape[0]
    XW = X_HEADS * X_DIM

    def body(mkv_ref, g_ref, k_ref, v_ref):
        for h in range(X_HEADS):
            cs = slice(h * X_DIM, (h + 1) * X_DIM)
            xv = mkv_ref[:, cs]
            r = lax.rsqrt(jnp.mean(xv * xv, axis=-1, keepdims=True) + NORM_EPS)
            k_ref[:, cs] = ((xv * r) * g_ref[...]).astype(BF16)
        v_ref[...] = mkv_ref[:, XW:].astype(BF16)

    return pl.pallas_call(
        body, grid=(1,),
        in_specs=[pl.BlockSpec((M, 2 * XW), lambda i: (0, 0)), pl.BlockSpec((1, X_DIM), lambda i: (0, 0))],
        out_specs=[pl.BlockSpec((M, XW), lambda i: (0, 0)), pl.BlockSpec((M, XW), lambda i: (0, 0))],
        out_shape=[jax.ShapeDtypeStruct((M, XW), BF16), jax.ShapeDtypeStruct((M, XW), BF16)],
        compiler_params=_cp(("arbitrary",)), name=name)(mkv, gkx)


def _memk_bwd(mkv, gkx, dk, dv, *, name):
    M = mkv.shape[0]
    XW = X_HEADS * X_DIM

    def body(mkv_ref, g_ref, dk_ref, dv_ref, o_ref, gg_ref):
        gg = jnp.zeros((1, X_DIM), F32)
        for h in range(X_HEADS):
            cs = slice(h * X_DIM, (h + 1) * X_DIM)
            xv = mkv_ref[:, cs]
            r = lax.rsqrt(jnp.mean(xv * xv, axis=-1, keepdims=True) + NORM_EPS)
            n = xv * r
            dkv = dk_ref[:, cs]
            gg += jnp.sum(dkv * n, axis=0, keepdims=True)
            dn = dkv * g_ref[...]
            c = jnp.mean(dn * n, axis=-1, keepdims=True)
            o_ref[:, cs] = (r * (dn - n * c)).astype(BF16)
        o_ref[:, XW:] = dv_ref[...].astype(BF16)
        gg_ref[...] = gg

    full = lambda c: pl.BlockSpec((M, c), lambda i: (0, 0))
    return pl.pallas_call(
        body, grid=(1,),
        in_specs=[full(2 * XW), pl.BlockSpec((1, X_DIM), lambda i: (0, 0)), full(XW), full(XW)],
        out_specs=[full(2 * XW), pl.BlockSpec((1, X_DIM), lambda i: (0, 0))],
        out_shape=[jax.ShapeDtypeStruct((M, 2 * XW), BF16), jax.ShapeDtypeStruct((1, X_DIM), F32)],
        compiler_params=_cp(("arbitrary",)), name=name)(mkv, gkx, dk, dv)


def _xq_norm(z_ref, g_ref, h):
    xv = z_ref[:, h * X_DIM:(h + 1) * X_DIM]
    r = lax.rsqrt(jnp.mean(xv * xv, axis=-1, keepdims=True) + NORM_EPS)
    n = xv * r
    return n, r, n * g_ref[...]


def _xprobs(qb, k_ref, h):
    s = lax.dot_general(qb, k_ref[:, h * X_DIM:(h + 1) * X_DIM], _DN["nt"],
                        preferred_element_type=F32) * (1.0 / math.sqrt(X_DIM))
    e = jnp.exp(s - jnp.max(s, axis=-1, keepdims=True))
    return e / jnp.sum(e, axis=-1, keepdims=True)


def _memattn_fwd(z, kx, vx, gqx, *, name):
    S = z.shape[0]
    M = kx.shape[0]
    tm = _row_tile(S)
    XW = X_HEADS * X_DIM

    def body(z_ref, k_ref, v_ref, g_ref, o_ref):
        for h in range(X_HEADS):
            cs = slice(h * X_DIM, (h + 1) * X_DIM)
            _, _, qn = _xq_norm(z_ref, g_ref, h)
            p = _xprobs(qn.astype(BF16), k_ref, h)
            o_ref[:, cs] = jnp.dot(p.astype(BF16), v_ref[:, cs], preferred_element_type=F32).astype(BF16)

    return pl.pallas_call(
        body, grid=(S // tm,),
        in_specs=[pl.BlockSpec((tm, XW), lambda i: (i, Z_MQ_CB)), pl.BlockSpec((M, XW), lambda i: (0, 0)),
                  pl.BlockSpec((M, XW), lambda i: (0, 0)), pl.BlockSpec((1, X_DIM), lambda i: (0, 0))],
        out_specs=pl.BlockSpec((tm, XW), lambda i: (i, 0)),
        out_shape=jax.ShapeDtypeStruct((S, XW), BF16),
        compiler_params=_cp(("parallel",)), name=name)(z, kx, vx, gqx)


def _memattn_bwd(z, kx, vx, gqx, d_cat, *, name):
    S = z.shape[0]
    M = kx.shape[0]
    tm = _row_tile(S)
    XW = X_HEADS * X_DIM
    scale = 1.0 / math.sqrt(X_DIM)

    def body(z_ref, k_ref, v_ref, g_ref, do_ref, dz_ref, dk_ref, dv_ref, gg_ref):
        i = pl.program_id(0)

        @pl.when(i == 0)
        def _():
            dk_ref[...] = jnp.zeros_like(dk_ref)
            dv_ref[...] = jnp.zeros_like(dv_ref)
            gg_ref[...] = jnp.zeros_like(gg_ref)

        gg = jnp.zeros((1, X_DIM), F32)
        for h in range(X_HEADS):
            cs = slice(h * X_DIM, (h + 1) * X_DIM)
            n, r, qn = _xq_norm(z_ref, g_ref, h)
            qb = qn.astype(BF16)
            p = _xprobs(qb, k_ref, h)
            pb = p.astype(BF16)
            dob = do_ref[:, cs].astype(BF16)
            dv_ref[:, cs] += lax.dot_general(pb, dob, _DN["tn"], preferred_element_type=F32)
            dp = lax.dot_general(dob, v_ref[:, cs], _DN["nt"], preferred_element_type=F32)
            ds = (p * (dp - jnp.sum(dp * p, axis=-1, keepdims=True))).astype(BF16)
            dk_ref[:, cs] += lax.dot_general(ds, qb, _DN["tn"], preferred_element_type=F32) * scale
            dqn = jnp.dot(ds, k_ref[:, cs], preferred_element_type=F32) * scale
            gg += jnp.sum(dqn * n, axis=0, keepdims=True)
            dn = dqn * g_ref[...]
            c = jnp.mean(dn * n, axis=-1, keepdims=True)
            dz_ref[:, cs] = (r * (dn - n * c)).astype(BF16)
        gg_ref[...] += gg

    full = pl.BlockSpec((M, XW), lambda i: (0, 0))
    vec = pl.BlockSpec((1, X_DIM), lambda i: (0, 0))
    return pl.pallas_call(
        body, grid=(S // tm,),
        in_specs=[pl.BlockSpec((tm, XW), lambda i: (i, Z_MQ_CB)), full, full, vec,
                  pl.BlockSpec((tm, XW), lambda i: (i, 3))],
        out_specs=[pl.BlockSpec((tm, XW), lambda i: (i, 0)), full, full, vec],
        out_shape=[jax.ShapeDtypeStruct((S, XW), BF16), jax.ShapeDtypeStruct((M, XW), F32),
                   jax.ShapeDtypeStruct((M, XW), F32), jax.ShapeDtypeStruct((1, X_DIM), F32)],
        compiler_params=_cp(("arbitrary",)), name=name)(z, kx, vx, gqx, d_cat)


def _sigmoid(x):
    return 1.0 / (1.0 + jnp.exp(-x))


def _glu_tiles(S, F):
    return _row_tile(S), _pick(F, (1408, 512, 256, 128))


def _glu_fwd(g, u, conv_w, conv_b, *, name):
    S, F = g.shape
    tm, tc = _glu_tiles(S, F)
    hb = tm // 8

    def body(g_ref, gp_ref, u_ref, w_ref, b_ref, a_ref):
        i = pl.program_id(1)
        gt = g_ref[...]
        ext = jnp.concatenate([gp_ref[...] * (i > 0).astype(F32), gt], axis=0)
        gc = b_ref[...] + w_ref[0:1, :] * pltpu.roll(ext, 2, 0)[8:]
        gc = gc + w_ref[1:2, :] * pltpu.roll(ext, 1, 0)[8:]
        gc = gc + w_ref[2:3, :] * gt
        a_ref[...] = ((gc * _sigmoid(gc)) * u_ref[...]).astype(BF16)

    return pl.pallas_call(
        body, grid=(F // tc, S // tm),
        in_specs=[pl.BlockSpec((tm, tc), lambda j, i: (i, j)),
                  pl.BlockSpec((8, tc), lambda j, i: (jnp.maximum(i * hb - 1, 0), j)),
                  pl.BlockSpec((tm, tc), lambda j, i: (i, j)),
                  pl.BlockSpec((3, tc), lambda j, i: (0, j)),
                  pl.BlockSpec((1, tc), lambda j, i: (0, j))],
        out_specs=pl.BlockSpec((tm, tc), lambda j, i: (i, j)),
        out_shape=jax.ShapeDtypeStruct((S, F), BF16),
        compiler_params=_cp(("parallel", "parallel")), name=name)(g, g, u, conv_w, conv_b)


def _glu_bwd(g, u, d_a, conv_w, conv_b, *, name):
    S, F = g.shape
    tm, tc = _glu_tiles(S, F)
    hb = tm // 8
    nt = S // tm
    E = tm + 8

    def body(g_ref, gp_ref, gn_ref, u_ref, un_ref, da_ref, dan_ref, w_ref, b_ref,
             dg_ref, du_ref, gw_ref, gb_ref):
        i = pl.program_id(1)

        @pl.when(i == 0)
        def _():
            gw_ref[...] = jnp.zeros_like(gw_ref)
            gb_ref[...] = jnp.zeros_like(gb_ref)

        w0, w1, w2 = w_ref[0:1, :], w_ref[1:2, :], w_ref[2:3, :]
        gext = jnp.concatenate([gp_ref[...] * (i > 0).astype(F32), g_ref[...], gn_ref[...]], axis=0)
        g1 = pltpu.roll(gext, 1, 0)[8:]
        g2 = pltpu.roll(gext, 2, 0)[8:]
        g0 = gext[8:]
        gc = b_ref[...] + w0 * g2
        gc = gc + w1 * g1
        gc = gc + w2 * g0
        sig = _sigmoid(gc)
        da = jnp.concatenate([da_ref[...], dan_ref[...] * (i < nt - 1).astype(F32)], axis=0)
        uu = jnp.concatenate([u_ref[...], un_ref[...]], axis=0)
        du_ref[...] = (da[:tm] * (gc[:tm] * sig[:tm])).astype(BF16)
        dgc = (da * uu) * (sig * (1.0 + gc * (1.0 - sig)))
        dg = w2 * dgc[:tm] + w1 * pltpu.roll(dgc, E - 1, 0)[:tm] + w0 * pltpu.roll(dgc, E - 2, 0)[:tm]
        dg_ref[...] = dg.astype(BF16)
        dgt = dgc[:tm]
        gb_ref[...] += jnp.sum(dgt, axis=0, keepdims=True)
        gw_ref[0:1, :] += jnp.sum(dgt * g2[:tm], axis=0, keepdims=True)
        gw_ref[1:2, :] += jnp.sum(dgt * g1[:tm], axis=0, keepdims=True)
        gw_ref[2:3, :] += jnp.sum(dgt * g0[:tm], axis=0, keepdims=True)

    tile = pl.BlockSpec((tm, tc), lambda j, i: (i, j))
    nxt = pl.BlockSpec((8, tc), lambda j, i: (jnp.minimum((i + 1) * hb, S // 8 - 1), j))
    prv = pl.BlockSpec((8, tc), lambda j, i: (jnp.maximum(i * hb - 1, 0), j))
    return pl.pallas_call(
        body, grid=(F // tc, nt),
        in_specs=[tile, prv, nxt, tile, nxt, tile, nxt,
                  pl.BlockSpec((3, tc), lambda j, i: (0, j)), pl.BlockSpec((1, tc), lambda j, i: (0, j))],
        out_specs=[tile, tile, pl.BlockSpec((3, tc), lambda j, i: (0, j)), pl.BlockSpec((1, tc), lambda j, i: (0, j))],
        out_shape=[jax.ShapeDtypeStruct((S, F), BF16), jax.ShapeDtypeStruct((S, F), BF16),
                   jax.ShapeDtypeStruct((3, F), F32), jax.ShapeDtypeStruct((1, F), F32)],
        compiler_params=_cp(("parallel", "arbitrary")), name=name)(g, g, g, u, u, d_a, d_a, conv_w, conv_b)


def _loss_head(y, target, *, name):
    S, D = y.shape
    tm = _row_tile(S)
    nt = S // tm

    def body(y_ref, t_ref, dy_ref, dyb_ref, loss_ref, acc):
        i = pl.program_id(0)

        @pl.when(i == 0)
        def _():
            acc[...] = jnp.zeros_like(acc)

        e = y_ref[...] - t_ref[...]
        dy = e * (1.0 / D)
        dy_ref[...] = dy
        dyb_ref[...] = dy.astype(BF16)
        acc[...] += jnp.sum(e * e, axis=0, keepdims=True)

        @pl.when(i == nt - 1)
        def _():
            loss_ref[...] = jnp.broadcast_to(jnp.sum(acc[...], axis=1, keepdims=True) * (0.5 / D), (1, 128))

    row = pl.BlockSpec((tm, D), lambda i: (i, 0))
    return pl.pallas_call(
        body, grid=(nt,), in_specs=[row, row],
        out_specs=[row, row, pl.BlockSpec((1, 128), lambda i: (0, 0))],
        out_shape=[jax.ShapeDtypeStruct((S, D), F32), jax.ShapeDtypeStruct((S, D), BF16),
                   jax.ShapeDtypeStruct((1, 128), F32)],
        scratch_shapes=[pltpu.VMEM((1, D), F32)],
        compiler_params=_cp(("arbitrary",)), name=name)(y, target)


def _adamw_math(w, g, m, v):
    m = ADAM_B1 * m + (1.0 - ADAM_B1) * g
    v = ADAM_B2 * v + (1.0 - ADAM_B2) * (g * g)
    m_hat = m / (1.0 - ADAM_B1 ** ADAM_STEP)
    v_hat = v / (1.0 - ADAM_B2 ** ADAM_STEP)
    delta = -ADAM_LR * (m_hat / (jnp.sqrt(v_hat) + ADAM_EPS) + ADAM_WD * w)
    return delta, m, v


def _adamw(w, m, v, parts, *, name):
    R, C = w.shape
    tr = 128 if R % 128 == 0 else R
    n_parts = len(parts)

    def body(*refs):
        w_ref, m_ref, v_ref = refs[:3]
        p_refs = refs[3:3 + n_parts]
        g_ref, d_ref, mo_ref, vo_ref = refs[3 + n_parts:]
        g = p_refs[0][...]
        for p in p_refs[1:]:
            g = g + p[...]
        delta, mn, vn = _adamw_math(w_ref[...], g, m_ref[...], v_ref[...])
        g_ref[...] = g
        d_ref[...] = delta
        mo_ref[...] = mn
        vo_ref[...] = vn

    blk = pl.BlockSpec((tr, C), lambda i: (i, 0))
    return pl.pallas_call(
        body, grid=(R // tr,), in_specs=[blk] * (3 + n_parts), out_specs=[blk] * 4,
        out_shape=[jax.ShapeDtypeStruct((R, C), F32)] * 4,
        compiler_params=_cp(("parallel",)), name=name)(w, m, v, *parts)


def _sum4(g_stack, recv, me, *, name):
    _, R, C = g_stack.shape
    tr = 128 if R % 128 == 0 else R

    def body(me_ref, g_ref, r_ref, o_ref):
        acc = g_ref[...]
        for j in range(N_CHIPS - 1):
            acc = acc + r_ref[j].astype(F32)
        o_ref[...] = acc

    grid_spec = pltpu.PrefetchScalarGridSpec(
        num_scalar_prefetch=1, grid=(R // tr,),
        in_specs=[pl.BlockSpec((None, tr, C), lambda i, me_ref: (me_ref[0], i, 0)),
                  pl.BlockSpec((N_CHIPS - 1, tr, C), lambda i, me_ref: (0, i, 0))],
        out_specs=pl.BlockSpec((tr, C), lambda i, me_ref: (i, 0)))
    return pl.pallas_call(
        body, grid_spec=grid_spec, out_shape=jax.ShapeDtypeStruct((R, C), F32),
        compiler_params=_cp(("parallel",)), name=name)(me, g_stack, recv)


def _sum8(gathered, *, name):
    _, R, C = gathered.shape

    def body(g_ref, o_ref):
        acc = g_ref[0]
        for d in range(1, N_DEV):
            acc = acc + g_ref[d]
        o_ref[...] = acc

    return pl.pallas_call(
        body, grid=(1,), in_specs=[pl.BlockSpec((N_DEV, R, C), lambda i: (0, 0, 0))],
        out_specs=pl.BlockSpec((R, C), lambda i: (0, 0)),
        out_shape=jax.ShapeDtypeStruct((R, C), F32),
        compiler_params=_cp(("arbitrary",)), name=name)(gathered)


def _place():
    return lax.axis_index("x"), lax.axis_index("y"), lax.axis_index("c")


def _other_chips(x, y):
    return [(1 - x, y), (x, 1 - y), (1 - x, 1 - y)]


_ANY = pl.BlockSpec(memory_space=pl.ANY)


def _gather_shards(shards, *, name):
    n = len(shards)

    def body(*refs):
        ins, outs = refs[:n], refs[n:2 * n]
        send_sems, recv_sems, local_sems = refs[2 * n:]
        x, y, c = _place()
        me = 2 * x + y
        copies = []
        for i in range(n):
            lc = pltpu.make_async_copy(ins[i], outs[i].at[me], local_sems.at[i])
            lc.start()
            copies.append(lc)
        remote = []
        for i in range(n):
            for j, (px, py) in enumerate(_other_chips(x, y)):
                rc = pltpu.make_async_remote_copy(
                    src_ref=ins[i], dst_ref=outs[i].at[me], send_sem=send_sems.at[3 * i + j],
                    recv_sem=recv_sems.at[3 * i + j], device_id=(px, py, c), device_id_type=MESH)
                rc.start()
                remote.append(rc)
        for lc in copies:
            lc.wait()
        for rc in remote:
            rc.wait_send()
        for rc in remote:
            rc.wait_recv()

    return pl.pallas_call(
        body, in_specs=[_ANY] * n, out_specs=[_ANY] * n,
        out_shape=[jax.ShapeDtypeStruct((N_CHIPS,) + s.shape, s.dtype) for s in shards],
        scratch_shapes=[pltpu.SemaphoreType.DMA((3 * n,)), pltpu.SemaphoreType.DMA((3 * n,)),
                        pltpu.SemaphoreType.DMA((n,))],
        name=name)(*shards)


def _scatter_partials(stacks, *, name):
    n = len(stacks)

    def body(*refs):
        ins, outs = refs[:n], refs[n:2 * n]
        send_sems, recv_sems = refs[2 * n:]
        x, y, c = _place()
        remote = []
        for i in range(n):
            for j, (px, py) in enumerate(_other_chips(x, y)):
                rc = pltpu.make_async_remote_copy(
                    src_ref=ins[i].at[2 * px + py], dst_ref=outs[i].at[j], send_sem=send_sems.at[3 * i + j],
                    recv_sem=recv_sems.at[3 * i + j], device_id=(px, py, c), device_id_type=MESH)
                rc.start()
                remote.append(rc)
        for rc in remote:
            rc.wait_send()
        for rc in remote:
            rc.wait_recv()

    return pl.pallas_call(
        body, in_specs=[_ANY] * n, out_specs=[_ANY] * n,
        out_shape=[jax.ShapeDtypeStruct((N_CHIPS - 1,) + s.shape[1:], s.dtype) for s in stacks],
        scratch_shapes=[pltpu.SemaphoreType.DMA((3 * n,)), pltpu.SemaphoreType.DMA((3 * n,))],
        name=name)(*stacks)


def _swap_with_sibling(arrs, *, name):
    n = len(arrs)

    def body(*refs):
        ins, outs = refs[:n], refs[n:2 * n]
        send_sems, recv_sems = refs[2 * n:]
        x, y, c = _place()
        remote = []
        for i in range(n):
            rc = pltpu.make_async_remote_copy(
                src_ref=ins[i], dst_ref=outs[i], send_sem=send_sems.at[i], recv_sem=recv_sems.at[i],
                device_id=(x, y, 1 - c), device_id_type=MESH)
            rc.start()
            remote.append(rc)
        for rc in remote:
            rc.wait_send()
        for rc in remote:
            rc.wait_recv()

    return pl.pallas_call(
        body, in_specs=[_ANY] * n, out_specs=[_ANY] * n,
        out_shape=[jax.ShapeDtypeStruct(a.shape, a.dtype) for a in arrs],
        scratch_shapes=[pltpu.SemaphoreType.DMA((n,)), pltpu.SemaphoreType.DMA((n,))],
        name=name)(*arrs)


def _gather_all(buf, *, name):
    R, C = buf.shape

    def body(in_ref, out_ref, send_sems, recv_sems, local_sem):
        x, y, c = _place()
        me = 4 * x + 2 * y + c
        lc = pltpu.make_async_copy(in_ref, out_ref.at[me], local_sem)
        lc.start()
        remote = []
        for k in range(1, N_DEV):
            px = 1 - x if (k >> 2) & 1 else x
            py = 1 - y if (k >> 1) & 1 else y
            pc = 1 - c if k & 1 else c
            rc = pltpu.make_async_remote_copy(
                src_ref=in_ref, dst_ref=out_ref.at[me], send_sem=send_sems.at[k - 1],
                recv_sem=recv_sems.at[k - 1], device_id=(px, py, pc), device_id_type=MESH)
            rc.start()
            remote.append(rc)
        lc.wait()
        for rc in remote:
            rc.wait_send()
        for rc in remote:
            rc.wait_recv()

    return pl.pallas_call(
        body, in_specs=[_ANY], out_specs=_ANY,
        out_shape=jax.ShapeDtypeStruct((N_DEV, R, C), buf.dtype),
        scratch_shapes=[pltpu.SemaphoreType.DMA((N_DEV - 1,)), pltpu.SemaphoreType.DMA((N_DEV - 1,)),
                        pltpu.SemaphoreType.DMA],
        name=name)(buf)


def _w_in_to_z(w):
    pad = jnp.zeros(w.shape[:-1] + (64,), w.dtype)
    return jnp.concatenate([w[..., 0:512], w[..., 512:1024], w[..., 1344:1856], w[..., 1024:1280],
                            w[..., 1280:1344], pad], axis=-1)


def _z_to_w_in(g):
    return jnp.concatenate([g[..., 0:512], g[..., 512:1024], g[..., 1536:1792], g[..., 1792:1856],
                            g[..., 1024:1536]], axis=-1)


def _pad_heads(w, nh):
    w = w.reshape(w.shape[:-1] + (nh, QK))
    w = jnp.concatenate([w, jnp.zeros(w.shape[:-1] + (HEAD_PAD - QK,), w.dtype)], axis=-1)
    return w.reshape(w.shape[:-2] + (nh * HEAD_PAD,))


def _unpad_heads(g, nh):
    g = g.reshape(g.shape[:-1] + (nh, HEAD_PAD))[..., :QK]
    return g.reshape(g.shape[:-2] + (nh * QK,))


def _kv_split(w, nh):
    w = w.reshape(w.shape[:-1] + (nh, 2, 128))
    return jnp.swapaxes(w, -3, -2).reshape(w.shape[:-3] + (nh * 256,))


def _kv_join(g, nh):
    g = g.reshape(g.shape[:-1] + (2, nh, 128))
    return jnp.swapaxes(g, -3, -2).reshape(g.shape[:-3] + (nh * 256,))


def _kv_perm(j):
    return (j % 2) * N_CHIPS + j // 2


def _pad_gain(g):
    return jnp.concatenate([g, jnp.zeros((1, HEAD_PAD - QK), g.dtype)], axis=1)


_SMALL = ("g_mix", "g_q_lat", "g_kv_lat", "g_q_mla", "g_k_mla", "w_pool", "pool_scale", "g_mem", "g_q_x",
          "g_k_x", "g_ffn", "conv_b", "conv_w")


def _pack(arrs, extra=0):
    flat = jnp.concatenate([a.reshape(-1) for a in arrs])
    n = flat.shape[0] + extra
    rows = -(-n // 1024) * 8
    return jnp.pad(flat, (0, rows * 128 - flat.shape[0])).reshape(rows, 128)


def _unpack(buf, shapes):
    flat = buf.reshape(-1)
    out, off = [], 0
    for s in shapes:
        n = int(np.prod(s))
        out.append(flat[off:off + n].reshape(s))
        off += n
    return out, off


def _local_step(x, mem, target, W):
    S, D = x.shape
    F = W["conv_b"].shape[1]
    tabs = _rope_tables(S)
    tm = 512 if S % 512 == 0 else 128
    tk = _pick(S, (1024, 512, 128))

    h = _rms_fwd(x, W["g_mix"], C=D, name="norm_mix")
    z = mm_nn(h, W["w_in"], tm=tm, tn=Z_COLS, tk=D, name="z_proj")
    y_pool = _pool_fwd(z, W["w_pool"], W["pool_scale"], name="pool_fwd")
    ql = _rms_fwd(z, W["g_q_lat"], C=Q_RANK, cb=Z_Q_CB, name="norm_qlat")
    kvl = _rms_fwd(z, W["g_kv_lat"], C=KV_RANK, cb=Z_KV_CB, name="norm_kvlat")
    qraw = mm_nn(ql, W["w_q_up"], nsh=N_CHIPS, tm=tm, tn=512, tk=Q_RANK, name="q_up")
    kvraw = mm_nn(kvl, W["w_kv_up"], nsh=N_CHIPS, tm=tm, tn=256, tk=KV_RANK, perm=_kv_perm, name="kv_up")
    q, k, v = _qkrope_fwd(qraw, kvraw, z, W["g_q_mla"], W["g_k_mla"], tabs, name="qk_norm_rope")
    o, y_mla, lse = _flash_fwd(q, k, v, name="mla_fwd")
    memn = _rms_fwd(mem, W["g_mem"], C=D, name="norm_mem")
    M = mem.shape[0]
    mkv = mm_nn(memn, W["w_mem_kv"], tm=M, tn=1024, tk=D, name="mem_kv")
    kx, vx = _memk_fwd(mkv, W["g_k_x"], name="memk_fwd")
    y_mem = _memattn_fwd(z, kx, vx, W["g_q_x"], name="memattn_fwd")
    cat = jnp.concatenate([y_pool, y_mla, y_mem], axis=1)
    x2 = mm_nn(cat, W["w_o"], tm=tm, tn=D, tk=1024, add=x, name="o_proj")
    h2 = _rms_fwd(x2, W["g_ffn"], C=D, name="norm_ffn")
    fn = F // N_CHIPS
    g = mm_nn(h2, W["w_gate"], nsh=N_CHIPS, tm=tm, tn=fn, tk=D, name="gate_proj")
    u = mm_nn(h2, W["w_up"], nsh=N_CHIPS, tm=tm, tn=fn, tk=D, name="up_proj")
    a = _glu_fwd(g, u, W["conv_w"], W["conv_b"], name="glu_fwd")
    y = mm_nn(a, W["w_down"], tm=tm, tn=D, tk=fn, add=x2, name="down_proj")
    dy, dyb, loss_row = _loss_head(y, target, name="loss_head")

    G = {}
    d_a = mm_nt(dyb, W["w_down"], tm=tm, to=fn, tc=D, name="d_a")
    G["w_down"] = mm_tn(a, dyb, to=fn, tn=1024, tk=tk, out_dtypes=(F32, BF16), name="grad_w_down")
    d_g, d_u, G["conv_w"], G["conv_b"] = _glu_bwd(g, u, d_a, W["conv_w"], W["conv_b"], name="glu_bwd")
    G["w_gate"] = mm_tn(h2, d_g, nsh=N_CHIPS, to=1024, tn=fn, tk=tk, out_dtypes=(F32, BF16), name="grad_w_gate")
    G["w_up"] = mm_tn(h2, d_u, nsh=N_CHIPS, to=1024, tn=fn, tk=tk, out_dtypes=(F32, BF16), name="grad_w_up")
    d_h2 = mm_nt(d_g, W["w_gate"], nsh=N_CHIPS, tm=tm, to=D, tc=fn, name="d_h2_gate")
    d_h2 = mm_nt(d_u, W["w_up"], nsh=N_CHIPS, tm=tm, to=D, tc=fn, add=d_h2, name="d_h2_up")
    d_x2, d_x2b, G["g_ffn"] = _rms_bwd(x2, d_h2, W["g_ffn"], C=D, res=dy, out_dtypes=(F32, BF16), name="norm_ffn_bwd")

    d_cat = mm_nt(d_x2b, W["w_o"], tm=tm, to=D, tc=D, name="d_cat")
    G["w_o"] = mm_tn(cat, d_x2b, to=1024, tn=1024, tk=tk, out_dtypes=(F32, BF16), name="grad_w_o")
    dz_pool, G["w_pool"], G["pool_scale"] = _pool_bwd(z, d_cat, W["w_pool"], W["pool_scale"], name="pool_bwd")
    dz_mq, dkx, dvx, G["g_q_x"] = _memattn_bwd(z, kx, vx, W["g_q_x"], d_cat, name="memattn_bwd")
    d_mkv, G["g_k_x"] = _memk_bwd(mkv, W["g_k_x"], dkx, dvx, name="memk_bwd")
    G["w_mem_kv"] = mm_tn(memn, d_mkv, to=1024, tn=1024, tk=M, out_dtypes=(F32, BF16), name="grad_w_mem_kv")
    d_memn = mm_nt(d_mkv, W["w_mem_kv"], tm=M, to=D, tc=1024, name="d_memn")
    _, G["g_mem"] = _rms_bwd(mem, d_memn, W["g_mem"], C=D, name="norm_mem_bwd")
    dq = _flash_bwd_dq(q, k, v, o, lse, d_cat, name="mla_bwd_dq")
    dk, dv = _flash_bwd_dkv(q, k, v, o, lse, d_cat, name="mla_bwd_dkv")
    d_qraw, d_kvraw, dz_kr, G["g_q_mla"], G["g_k_mla"] = _qkrope_bwd(
        qraw, kvraw, z, W["g_q_mla"], W["g_k_mla"], tabs, dq, dk, dv, name="qk_norm_rope_bwd")
    G["w_q_up"] = mm_tn(ql, d_qraw, nsh=N_CHIPS, to=Q_RANK, tn=512, tk=tk, out_dtypes=(F32, BF16), name="grad_w_q_up")
    d_ql = mm_nt(d_qraw, W["w_q_up"], nsh=N_CHIPS, tm=tm, to=Q_RANK, tc=512, name="d_ql")
    G["w_kv_up"] = mm_tn(kvl, d_kvraw, nsh=N_CHIPS, to=KV_RANK, tn=256, tk=tk, out_dtypes=(F32, BF16),
                         perm=_kv_perm, name="grad_w_kv_up")
    d_kvl = mm_nt(d_kvraw, W["w_kv_up"], nsh=N_CHIPS, tm=tm, to=KV_RANK, tc=256, perm=_kv_perm, name="d_kvl")
    dz_q, G["g_q_lat"] = _rms_bwd(z, d_ql, W["g_q_lat"], C=Q_RANK, cb=Z_Q_CB, out_dtypes=(BF16,), name="norm_qlat_bwd")
    dz_kv, G["g_kv_lat"] = _rms_bwd(z, d_kvl, W["g_kv_lat"], C=KV_RANK, cb=Z_KV_CB, out_dtypes=(BF16,),
                                    name="norm_kvlat_bwd")
    d_z = jnp.concatenate([dz_pool, dz_q, dz_mq, dz_kv, dz_kr], axis=1)
    G["w_in"] = mm_tn(h, d_z, to=512, tn=Z_COLS, tk=tk, out_dtypes=(F32, BF16), name="grad_w_in")
    d_h = mm_nt(d_z, W["w_in"], tm=tm, to=D, tc=Z_COLS, name="d_h")
    grad_x, G["g_mix"] = _rms_bwd(x, d_h, W["g_mix"], C=D, res=d_x2, name="norm_mix_bwd")
    return loss_row, grad_x, G


_BIG = ("w_in", "w_q_up", "w_kv_up", "w_mem_kv", "w_o", "w_gate", "w_up", "w_down")
_WEIGHTS = ("g_mix", "w_in", "g_q_lat", "w_q_up", "g_kv_lat", "w_kv_up", "g_q_mla", "g_k_mla", "w_pool",
            "pool_scale", "g_mem", "w_mem_kv", "g_q_x", "g_k_x", "w_o", "g_ffn", "w_gate", "w_up", "conv_w",
            "conv_b", "w_down")


def _to_compute_layout(name, w):
    if name == "w_in":
        return _w_in_to_z(w)
    if name == "w_q_up":
        return _pad_heads(w, w.shape[-1] // QK)
    if name == "w_kv_up":
        return _kv_split(w, w.shape[-1] // 256)
    return w


def _from_compute_layout(name, g):
    if name == "w_in":
        return _z_to_w_in(g)
    if name == "w_q_up":
        return _unpad_heads(g, g.shape[-1] // HEAD_PAD)
    if name == "w_kv_up":
        return _kv_join(g, g.shape[-1] // 256)
    return g


def kernel(x, mem, g_mix, w_in, g_q_lat, w_q_up, g_kv_lat, w_kv_up, g_q_mla, g_k_mla, w_pool, pool_scale, g_mem, w_mem_kv, g_q_x, g_k_x, w_o, g_ffn, w_gate, w_up, conv_w, conv_b, w_down, loss_target, m_g_mix, m_w_in, m_g_q_lat, m_w_q_up, m_g_kv_lat, m_w_kv_up, m_g_q_mla, m_g_k_mla, m_w_pool, m_pool_scale, m_g_mem, m_w_mem_kv, m_g_q_x, m_g_k_x, m_w_o, m_g_ffn, m_w_gate, m_w_up, m_conv_w, m_conv_b, m_w_down, v_g_mix, v_w_in, v_g_q_lat, v_w_q_up, v_g_kv_lat, v_w_kv_up, v_g_q_mla, v_g_k_mla, v_w_pool, v_pool_scale, v_g_mem, v_w_mem_kv, v_g_q_x, v_g_k_x, v_w_o, v_g_ffn, v_w_gate, v_w_up, v_conv_w, v_conv_b, v_w_down):
    P = dict(g_mix=g_mix, w_in=w_in, g_q_lat=g_q_lat, w_q_up=w_q_up, g_kv_lat=g_kv_lat, w_kv_up=w_kv_up,
             g_q_mla=g_q_mla, g_k_mla=g_k_mla, w_pool=w_pool, pool_scale=pool_scale, g_mem=g_mem,
             w_mem_kv=w_mem_kv, g_q_x=g_q_x, g_k_x=g_k_x, w_o=w_o, g_ffn=g_ffn, w_gate=w_gate, w_up=w_up,
             conv_w=conv_w, conv_b=conv_b, w_down=w_down)
    Mo = dict(g_mix=m_g_mix, w_in=m_w_in, g_q_lat=m_g_q_lat, w_q_up=m_w_q_up, g_kv_lat=m_g_kv_lat,
              w_kv_up=m_w_kv_up, g_q_mla=m_g_q_mla, g_k_mla=m_g_k_mla, w_pool=m_w_pool,
              pool_scale=m_pool_scale, g_mem=m_g_mem, w_mem_kv=m_w_mem_kv, g_q_x=m_g_q_x, g_k_x=m_g_k_x,
              w_o=m_w_o, g_ffn=m_g_ffn, w_gate=m_w_gate, w_up=m_w_up, conv_w=m_conv_w, conv_b=m_conv_b,
              w_down=m_w_down)
    Vo = dict(g_mix=v_g_mix, w_in=v_w_in, g_q_lat=v_g_q_lat, w_q_up=v_w_q_up, g_kv_lat=v_g_kv_lat,
              w_kv_up=v_w_kv_up, g_q_mla=v_g_q_mla, g_k_mla=v_g_k_mla, w_pool=v_w_pool,
              pool_scale=v_pool_scale, g_mem=v_g_mem, w_mem_kv=v_w_mem_kv, g_q_x=v_g_q_x, g_k_x=v_g_k_x,
              w_o=v_w_o, g_ffn=v_g_ffn, w_gate=v_w_gate, w_up=v_w_up, conv_w=v_conv_w, conv_b=v_conv_b,
              w_down=v_w_down)
    xi, yi, ci = _place()
    me = (2 * xi + yi).astype(jnp.int32).reshape(1)

    shards = [_to_compute_layout(n, P[n][0]).astype(BF16) for n in _BIG] + [conv_w[0]]
    stacks = _gather_shards(shards, name="gather_weights")
    W = {n: s.reshape(-1, s.shape[-1]) for n, s in zip(_BIG, stacks[:-1])}
    W["conv_w"] = jnp.swapaxes(stacks[-1], 0, 1).reshape(3, -1)
    W["g_q_mla"], W["g_k_mla"] = _pad_gain(g_q_mla), _pad_gain(g_k_mla)
    W["w_pool"] = w_pool[0].astype(BF16)
    for n in ("g_mix", "g_q_lat", "g_kv_lat", "pool_scale", "g_mem", "g_q_x", "g_k_x", "g_ffn", "conv_b"):
        W[n] = P[n]

    loss_row, grad_x, G = _local_step(x[0], mem[0], loss_target[0], W)

    shard_shape = {n: _to_compute_layout(n, P[n][0]).shape for n in _BIG}
    g_f32 = [G[n][0].reshape((N_CHIPS,) + shard_shape[n]) for n in _BIG]
    g_b16 = [G[n][1].reshape((N_CHIPS,) + shard_shape[n]) for n in _BIG]
    recv = _scatter_partials(g_b16, name="scatter_grads")
    part = [_sum4(gf, r, me, name="sum4_" + n) for n, gf, r in zip(_BIG, g_f32, recv)]
    part = [_from_compute_layout(n, p) for n, p in zip(_BIG, part)]
    sib = _swap_with_sibling(part, name="swap_grads")
    out = {}
    for n, p, s in zip(_BIG, part, sib):
        out[n] = [r[None] for r in _adamw(P[n][0], Mo[n][0], Vo[n][0], [p, s], name="adamw_" + n)]

    conv_w_full_grad = G["conv_w"]
    small_g = [G["g_mix"], G["g_q_lat"], G["g_kv_lat"], G["g_q_mla"][:, :QK], G["g_k_mla"][:, :QK], G["w_pool"],
               G["pool_scale"], G["g_mem"], G["g_q_x"], G["g_k_x"], G["g_ffn"], G["conv_b"], conv_w_full_grad]
    packed = _pack(small_g + [loss_row[:, :1]])
    total = _sum8(_gather_all(packed, name="gather_small"), name="sum_small")
    shapes = [a.shape for a in small_g] + [(1, 1)]
    (parts, _) = _unpack(total, shapes)
    loss = parts[-1].reshape(())
    F = conv_b.shape[1]
    fn = F // N_CHIPS
    col0 = (2 * xi + yi) * fn
    sg = dict(zip(_SMALL, parts[:-1]))
    sg["conv_w"] = lax.dynamic_slice(sg["conv_w"], (0, col0), (3, fn))
    sw = [P[n].reshape(sg[n].shape) for n in _SMALL]
    sm = [Mo[n].reshape(sg[n].shape) for n in _SMALL]
    sv = [Vo[n].reshape(sg[n].shape) for n in _SMALL]
    gp = _pack([sg[n] for n in _SMALL])
    res = _adamw(_pack(sw), _pack(sm), _pack(sv), [gp], name="adamw_small")
    sshapes = [sg[n].shape for n in _SMALL]
    for kind, buf in zip(range(4), res):
        vals, _ = _unpack(buf, sshapes)
        for n, val in zip(_SMALL, vals):
            out.setdefault(n, [None] * 4)[kind] = val.reshape(P[n].shape)

    return (loss, grad_x[None], *[out[n][0] for n in _WEIGHTS], *[out[n][1] for n in _WEIGHTS],
            *[out[n][2] for n in _WEIGHTS], *[out[n][3] for n in _WEIGHTS])
```

```python
import functools
import math

import numpy as np
import jax
import jax.numpy as jnp
from jax import lax
from jax.experimental import pallas as pl
from jax.experimental.pallas import tpu as pltpu

F32, BF16 = jnp.float32, jnp.bfloat16
NORM_EPS = 1e-6
ROPE_THETA = 10000.0
V7X_VMEM_LIMIT_BYTES = 48 * 1024 * 1024
N_CHIPS = 4
N_DEV = 8

POOL_W = 512
POOL_WINDOWS = (2, 4, 8, 16)
HEADS = 8
NOPE, ROPE, QK = 128, 64, 192
HEAD_PAD = 256
Q_RANK, KV_RANK = 512, 256
X_HEADS, X_DIM = 4, 128
Z_COLS = 1920
Z_POOL_CB, Z_Q_CB, Z_MQ_CB = 0, 1, 2
Z_KV_CB = 6
Z_KR_CB = 14

ADAM_LR, ADAM_B1, ADAM_B2, ADAM_EPS, ADAM_WD, ADAM_STEP = 0.001, 0.9, 0.999, 1e-08, 0.01, 10

MESH = pl.DeviceIdType.MESH


def _cp(sem):
    return pltpu.CompilerParams(dimension_semantics=sem, vmem_limit_bytes=V7X_VMEM_LIMIT_BYTES)


def _row_tile(S):
    return 256 if S % 256 == 0 and S >= 2048 else 128


def _attn_tile(S):
    return 512 if S % 512 == 0 and S >= 4096 else 128


def _pick(dim, prefs):
    for p in prefs:
        if dim % p == 0:
            return p
    return dim


_DN = {"nn": (((1,), (0,)), ((), ())), "nt": (((1,), (1,)), ((), ())), "tn": (((0,), (0,)), ((), ()))}


def _mm(a, b, *, mode, grid, blocks, maps, out_shape, out_dtypes, add=None, name):
    nk = grid[2]
    dn = _DN[mode]
    n_out = len(out_dtypes)

    def body(*refs):
        a_ref, b_ref = refs[0], refs[1]
        add_ref = refs[2] if add is not None else None
        p = 2 + (add is not None)
        o_refs = refs[p:p + n_out]
        acc = refs[p + n_out]
        k = pl.program_id(2)

        @pl.when(k == 0)
        def _():
            acc[...] = jnp.zeros_like(acc)

        acc[...] += lax.dot_general(a_ref[...].astype(BF16), b_ref[...].astype(BF16), dn,
                                    preferred_element_type=F32)

        @pl.when(k == nk - 1)
        def _():
            r = acc[...]
            if add_ref is not None:
                r = r + add_ref[...]
            for o in o_refs:
                o[...] = r.astype(o.dtype)

    a_blk, b_blk, o_blk = blocks
    a_map, b_map, o_map = maps
    in_specs = [pl.BlockSpec(a_blk, a_map), pl.BlockSpec(b_blk, b_map)]
    args = [a, b]
    if add is not None:
        in_specs.append(pl.BlockSpec(o_blk, o_map))
        args.append(add)
    outs = pl.pallas_call(
        body, grid=grid, in_specs=in_specs,
        out_specs=[pl.BlockSpec(o_blk, o_map) for _ in out_dtypes],
        out_shape=[jax.ShapeDtypeStruct(out_shape, d) for d in out_dtypes],
        scratch_shapes=[pltpu.VMEM(o_blk, F32)],
        compiler_params=_cp(("parallel", "parallel", "arbitrary")), name=name)(*args)
    return outs[0] if n_out == 1 else outs


def _ident(j):
    return j


def mm_nn(a, w, *, nsh=1, tm, tn, tk, out_dtypes=(F32,), add=None, perm=_ident, name):
    M, K = a.shape
    n = w.shape[1]
    N = nsh * n
    assert w.shape[0] == nsh * K and n % tn == 0 and K % tk == 0 and M % tm == 0
    npt, kt = n // tn, K // tk
    return _mm(a, w, mode="nn", grid=(M // tm, N // tn, kt),
               blocks=((tm, tk), (tk, tn), (tm, tn)),
               maps=(lambda i, j, k: (i, k), lambda i, j, k: ((j // npt) * kt + k, j % npt),
                     lambda i, j, k: (i, perm(j))),
               out_shape=(M, N), out_dtypes=out_dtypes, add=add, name=name)


def mm_nt(d, w, *, nsh=1, tm, to, tc, out_dtypes=(F32,), add=None, perm=_ident, name):
    M, N = d.shape
    n = w.shape[1]
    K = w.shape[0] // nsh
    assert nsh * n == N and n % tc == 0 and K % to == 0 and M % tm == 0
    cpt, ot = n // tc, K // to
    return _mm(d, w, mode="nt", grid=(M // tm, ot, N // tc),
               blocks=((tm, tc), (to, tc), (tm, to)),
               maps=(lambda i, j, c: (i, perm(c)), lambda i, j, c: ((c // cpt) * ot + j, c % cpt),
                     lambda i, j, c: (i, j)),
               out_shape=(M, K), out_dtypes=out_dtypes, add=add, name=name)


def mm_tn(x, d, *, nsh=1, to, tn, tk, out_dtypes=(F32,), perm=_ident, name):
    M, K = x.shape
    N = d.shape[1]
    n = N // nsh
    assert n % tn == 0 and K % to == 0 and M % tk == 0
    npt, ot = n // tn, K // to
    return _mm(x, d, mode="tn", grid=(ot, N // tn, M // tk),
               blocks=((tk, to), (tk, tn), (to, tn)),
               maps=(lambda i, j, k: (k, i), lambda i, j, k: (k, perm(j)),
                     lambda i, j, k: ((j // npt) * ot + i, j % npt)),
               out_shape=(nsh * K, n), out_dtypes=out_dtypes, name=name)


def _rms_fwd(x, g, *, C, cb=0, name):
    S = x.shape[0]
    tm = _row_tile(S) if S >= 128 else S

    def body(x_ref, g_ref, o_ref):
        xv = x_ref[...]
        r = lax.rsqrt(jnp.mean(xv * xv, axis=-1, keepdims=True) + NORM_EPS)
        o_ref[...] = ((xv * r) * g_ref[...]).astype(o_ref.dtype)

    return pl.pallas_call(
        body, grid=(S // tm,),
        in_specs=[pl.BlockSpec((tm, C), lambda i: (i, cb)), pl.BlockSpec((1, C), lambda i: (0, 0))],
        out_specs=pl.BlockSpec((tm, C), lambda i: (i, 0)),
        out_shape=jax.ShapeDtypeStruct((S, C), BF16),
        compiler_params=_cp(("parallel",)), name=name)(x, g)


def _rms_bwd(x, dh, g, *, C, cb=0, res=None, out_dtypes=(F32,), name):
    S = x.shape[0]
    tm = _row_tile(S) if S >= 128 else S
    n_out = len(out_dtypes)

    def body(*refs):
        x_ref, dh_ref, g_ref = refs[:3]
        res_ref = refs[3] if res is not None else None
        p = 3 + (res is not None)
        outs = refs[p:p + n_out]
        dg_ref = refs[p + n_out]
        i = pl.program_id(0)
        xv = x_ref[...]
        r = lax.rsqrt(jnp.mean(xv * xv, axis=-1, keepdims=True) + NORM_EPS)
        n = xv * r
        dhv = dh_ref[...].astype(F32)
        dn = dhv * g_ref[...]
        c = jnp.mean(dn * n, axis=-1, keepdims=True)
        dx = r * (dn - n * c)
        if res_ref is not None:
            dx = res_ref[...] + dx
        for o in outs:
            o[...] = dx.astype(o.dtype)

        @pl.when(i == 0)
        def _():
            dg_ref[...] = jnp.zeros_like(dg_ref)

        dg_ref[...] += jnp.sum(dhv * n, axis=0, keepdims=True)

    row = pl.BlockSpec((tm, C), lambda i: (i, 0))
    in_specs = [pl.BlockSpec((tm, C), lambda i: (i, cb)), row, pl.BlockSpec((1, C), lambda i: (0, 0))]
    args = [x, dh, g]
    if res is not None:
        in_specs.append(row)
        args.append(res)
    return pl.pallas_call(
        body, grid=(S // tm,), in_specs=in_specs,
        out_specs=[row] * n_out + [pl.BlockSpec((1, C), lambda i: (0, 0))],
        out_shape=[jax.ShapeDtypeStruct((S, C), d) for d in out_dtypes] + [jax.ShapeDtypeStruct((1, C), F32)],
        compiler_params=_cp(("arbitrary",)), name=name)(*args)


def _pool_cnt(t0, rows, w):
    t = t0 + lax.broadcasted_iota(jnp.int32, (rows, 1), 0)
    return jnp.minimum(t + 1, w).astype(F32)


def _pool_d(halo, tile, gi, t0, tm):
    s = jnp.concatenate([halo, tile], axis=0)
    for step in (1, 2, 4, 8)[:gi + 1]:
        s = s + pltpu.roll(s, step, 0)
    return s[16:] / _pool_cnt(t0, tm, POOL_WINDOWS[gi]) - tile


def _pool_fwd(z, w_pool, pool_scale, *, name):
    S = z.shape[0]
    tm = _row_tile(S)
    hb = tm // 16

    def body(z_ref, h_ref, w_ref, sc_ref, o_ref):
        i = pl.program_id(0)
        halo = h_ref[...] * (i > 0).astype(F32)
        for gi in range(4):
            cs = slice(gi * 128, (gi + 1) * 128)
            d = _pool_d(halo[:, cs], z_ref[:, cs], gi, i * tm, tm)
            yp = jnp.dot(d.astype(BF16), w_ref[gi], preferred_element_type=F32)
            o_ref[:, cs] = (yp * sc_ref[:, cs]).astype(o_ref.dtype)

    return pl.pallas_call(
        body, grid=(S // tm,),
        in_specs=[pl.BlockSpec((tm, POOL_W), lambda i: (i, Z_POOL_CB)),
                  pl.BlockSpec((16, POOL_W), lambda i: (jnp.maximum(i * hb - 1, 0), Z_POOL_CB)),
                  pl.BlockSpec((4, 128, 128), lambda i: (0, 0, 0)),
                  pl.BlockSpec((1, POOL_W), lambda i: (0, 0))],
        out_specs=pl.BlockSpec((tm, POOL_W), lambda i: (i, 0)),
        out_shape=jax.ShapeDtypeStruct((S, POOL_W), BF16),
        compiler_params=_cp(("parallel",)), name=name)(z, z, w_pool, pool_scale)


def _pool_bwd(z, d_cat, w_pool, pool_scale, *, name):
    S = z.shape[0]
    tm = _row_tile(S)
    hb = tm // 16
    nt = S // tm
    E = tm + 16

    def body(z_ref, h_ref, dy_ref, dyn_ref, w_ref, sc_ref, dz_ref, gw_ref, gs_ref):
        i = pl.program_id(0)

        @pl.when(i == 0)
        def _():
            gw_ref[...] = jnp.zeros_like(gw_ref)
            gs_ref[...] = jnp.zeros_like(gs_ref)

        halo = h_ref[...] * (i > 0).astype(F32)
        dy_next = dyn_ref[...] * (i < nt - 1).astype(F32)
        for gi in range(4):
            cs = slice(gi * 128, (gi + 1) * 128)
            w = w_ref[gi]
            d = _pool_d(halo[:, cs], z_ref[:, cs], gi, i * tm, tm)
            db = d.astype(BF16)
            dy = dy_ref[:, cs]
            yp = jnp.dot(db, w, preferred_element_type=F32)
            gs_ref[:, cs] += jnp.sum(dy * yp, axis=0, keepdims=True)
            sc = sc_ref[:, cs]
            dys = (dy * sc).astype(BF16)
            gw_ref[gi] += lax.dot_general(db, dys, _DN["tn"], preferred_element_type=F32)
            dys_ext = jnp.concatenate([dys, (dy_next[:, cs] * sc).astype(BF16)], axis=0)
            dd = lax.dot_general(dys_ext, w, _DN["nt"], preferred_element_type=F32)
            r = dd / _pool_cnt(i * tm, E, POOL_WINDOWS[gi])
            for step in (1, 2, 4, 8)[:gi + 1]:
                r = r + pltpu.roll(r, E - step, 0)
            dz_ref[:, cs] = (r[:tm] - dd[:tm]).astype(dz_ref.dtype)

    return pl.pallas_call(
        body, grid=(nt,),
        in_specs=[pl.BlockSpec((tm, POOL_W), lambda i: (i, Z_POOL_CB)),
                  pl.BlockSpec((16, POOL_W), lambda i: (jnp.maximum(i * hb - 1, 0), Z_POOL_CB)),
                  pl.BlockSpec((tm, POOL_W), lambda i: (i, 0)),
                  pl.BlockSpec((16, POOL_W), lambda i: (jnp.minimum((i + 1) * hb, S // 16 - 1), 0)),
                  pl.BlockSpec((4, 128, 128), lambda i: (0, 0, 0)),
                  pl.BlockSpec((1, POOL_W), lambda i: (0, 0))],
        out_specs=[pl.BlockSpec((tm, POOL_W), lambda i: (i, 0)),
                   pl.BlockSpec((4, 128, 128), lambda i: (0, 0, 0)),
                   pl.BlockSpec((1, POOL_W), lambda i: (0, 0))],
        out_shape=[jax.ShapeDtypeStruct((S, POOL_W), BF16),
                   jax.ShapeDtypeStruct((4, 128, 128), F32),
                   jax.ShapeDtypeStruct((1, POOL_W), F32)],
        compiler_params=_cp(("arbitrary",)), name=name)(z, z, d_cat, d_cat, w_pool, pool_scale)


def _rope_tables(S):
    half = ROPE // 2
    inv_freq = 1.0 / (ROPE_THETA ** (jnp.arange(half, dtype=F32) / half))
    ang = jnp.arange(S).astype(F32)[:, None] * inv_freq[None, :]
    cos, sin = jnp.cos(ang), jnp.sin(ang)
    zero = jnp.zeros((S, half), F32)
    cos_t = jnp.concatenate([cos, cos, zero, zero], axis=1)
    sa_t = jnp.concatenate([-sin, zero, zero, zero], axis=1)
    sb_t = jnp.concatenate([zero, sin, zero, zero], axis=1)
    return cos_t, sa_t, sb_t


def _head_fwd(xn, xr, gn, gr, cos, sa, sb):
    ms = (jnp.sum(xn * xn, axis=-1, keepdims=True) + jnp.sum(xr * xr, axis=-1, keepdims=True)) * (1.0 / QK)
    r = lax.rsqrt(ms + NORM_EPS)
    on = (xn * r) * gn
    yr = (xr * r) * gr
    orr = yr * cos + pltpu.roll(yr, 96, 1) * sa + pltpu.roll(yr, 32, 1) * sb
    return on, orr


def _head_bwd(xn, xr, gn, gr, don, dor, cos, sa, sb):
    ms = (jnp.sum(xn * xn, axis=-1, keepdims=True) + jnp.sum(xr * xr, axis=-1, keepdims=True)) * (1.0 / QK)
    r = lax.rsqrt(ms + NORM_EPS)
    nn, nr = xn * r, xr * r
    dyr = dor * cos + pltpu.roll(dor * sa, 32, 1) + pltpu.roll(dor * sb, 96, 1)
    ggn, ggr = don * nn, dyr * nr
    dnn, dnr = don * gn, dyr * gr
    c = (jnp.sum(dnn * nn, axis=-1, keepdims=True) + jnp.sum(dnr * nr, axis=-1, keepdims=True)) * (1.0 / QK)
    return r * (dnn - nn * c), r * (dnr - nr * c), ggn, ggr


def _qkrope_fwd(qraw, kvraw, z, gq, gk, tabs, *, name):
    S = qraw.shape[0]
    tm = _row_tile(S)

    def body(q_ref, kv_ref, zkr_ref, gq_ref, gk_ref, cos_ref, sa_ref, sb_ref, qo_ref, ko_ref, vo_ref):
        cos, sa, sb = cos_ref[...], sa_ref[...], sb_ref[...]
        zkr = zkr_ref[...]
        gqn, gqr, gkn, gkr = gq_ref[:, :128], gq_ref[:, 128:], gk_ref[:, :128], gk_ref[:, 128:]
        for h in range(HEADS):
            b = h * HEAD_PAD
            on, orr = _head_fwd(q_ref[:, b:b + 128], q_ref[:, b + 128:b + 256], gqn, gqr, cos, sa, sb)
            qo_ref[:, b:b + 128] = on.astype(BF16)
            qo_ref[:, b + 128:b + 256] = orr.astype(BF16)
            on, orr = _head_fwd(kv_ref[:, h * 128:(h + 1) * 128], zkr, gkn, gkr, cos, sa, sb)
            ko_ref[:, b:b + 128] = on.astype(BF16)
            ko_ref[:, b + 128:b + 256] = orr.astype(BF16)
        vo_ref[...] = kv_ref[:, HEADS * 128:].astype(BF16)

    W = HEADS * HEAD_PAD
    row = lambda c: pl.BlockSpec((tm, c), lambda i: (i, 0))
    vec = lambda c: pl.BlockSpec((1, c), lambda i: (0, 0))
    return pl.pallas_call(
        body, grid=(S // tm,),
        in_specs=[row(W), row(W), pl.BlockSpec((tm, 128), lambda i: (i, Z_KR_CB)), vec(256), vec(256),
                  row(128), row(128), row(128)],
        out_specs=[row(W), row(W), row(HEADS * 128)],
        out_shape=[jax.ShapeDtypeStruct((S, W), BF16), jax.ShapeDtypeStruct((S, W), BF16),
                   jax.ShapeDtypeStruct((S, HEADS * 128), BF16)],
        compiler_params=_cp(("parallel",)), name=name)(qraw, kvraw, z, gq, gk, *tabs)


def _qkrope_bwd(qraw, kvraw, z, gq, gk, tabs, dq, dk, dv, *, name):
    S = qraw.shape[0]
    tm = _row_tile(S)

    def body(q_ref, kv_ref, zkr_ref, gq_ref, gk_ref, cos_ref, sa_ref, sb_ref, dq_ref, dk_ref, dv_ref,
             dqo_ref, dkvo_ref, dkr_ref, ggq_ref, ggk_ref):
        i = pl.program_id(0)

        @pl.when(i == 0)
        def _():
            ggq_ref[...] = jnp.zeros_like(ggq_ref)
            ggk_ref[...] = jnp.zeros_like(ggk_ref)

        cos, sa, sb = cos_ref[...], sa_ref[...], sb_ref[...]
        zkr = zkr_ref[...]
        gqn, gqr, gkn, gkr = gq_ref[:, :128], gq_ref[:, 128:], gk_ref[:, :128], gk_ref[:, 128:]
        dkr = jnp.zeros((tm, 128), F32)
        sq_n = jnp.zeros((1, 128), F32)
        sq_r = jnp.zeros((1, 128), F32)
        sk_n = jnp.zeros((1, 128), F32)
        sk_r = jnp.zeros((1, 128), F32)
        for h in range(HEADS):
            b = h * HEAD_PAD
            dxn, dxr, ggn, ggr = _head_bwd(q_ref[:, b:b + 128], q_ref[:, b + 128:b + 256], gqn, gqr,
                                           dq_ref[:, b:b + 128], dq_ref[:, b + 128:b + 256], cos, sa, sb)
            dqo_ref[:, b:b + 128] = dxn.astype(BF16)
            dqo_ref[:, b + 128:b + 256] = dxr.astype(BF16)
            sq_n += jnp.sum(ggn, axis=0, keepdims=True)
            sq_r += jnp.sum(ggr, axis=0, keepdims=True)
            dxn, dxr, ggn, ggr = _head_bwd(kv_ref[:, h * 128:(h + 1) * 128], zkr, gkn, gkr,
                                           dk_ref[:, b:b + 128], dk_ref[:, b + 128:b + 256], cos, sa, sb)
            dkvo_ref[:, h * 128:(h + 1) * 128] = dxn.astype(BF16)
            dkr += dxr
            sk_n += jnp.sum(ggn, axis=0, keepdims=True)
            sk_r += jnp.sum(ggr, axis=0, keepdims=True)
        dkvo_ref[:, HEADS * 128:] = dv_ref[...].astype(BF16)
        dkr_ref[...] = dkr.astype(BF16)
        ggq_ref[:, :128] += sq_n
        ggq_ref[:, 128:] += sq_r
        ggk_ref[:, :128] += sk_n
        ggk_ref[:, 128:] += sk_r

    W = HEADS * HEAD_PAD
    row = lambda c: pl.BlockSpec((tm, c), lambda i: (i, 0))
    vec = lambda c: pl.BlockSpec((1, c), lambda i: (0, 0))
    return pl.pallas_call(
        body, grid=(S // tm,),
        in_specs=[row(W), row(W), pl.BlockSpec((tm, 128), lambda i: (i, Z_KR_CB)), vec(256), vec(256),
                  row(128), row(128), row(128), row(W), row(W), row(HEADS * 128)],
        out_specs=[row(W), row(W), row(128), vec(256), vec(256)],
        out_shape=[jax.ShapeDtypeStruct((S, W), BF16), jax.ShapeDtypeStruct((S, W), BF16),
                   jax.ShapeDtypeStruct((S, 128), BF16),
                   jax.ShapeDtypeStruct((1, 256), F32), jax.ShapeDtypeStruct((1, 256), F32)],
        compiler_params=_cp(("arbitrary",)), name=name)(qraw, kvraw, z, gq, gk, *tabs, dq, dk, dv)


def _tri_tables(nq, by_k):
    if by_k:
        pairs = [(qi, ki) for ki in range(nq) for qi in range(ki, nq)]
    else:
        pairs = [(qi, ki) for qi in range(nq) for ki in range(qi + 1)]
    arr = np.asarray(pairs, np.int32)
    return jnp.asarray(arr[:, 0]), jnp.asarray(arr[:, 1])


def _scores(q_ref, k_ref, qi, ki, tq, scale):
    s = lax.dot_general(q_ref[...], k_ref[...], _DN["nt"], preferred_element_type=F32) * scale
    row = lax.broadcasted_iota(jnp.int32, (tq, tq), 0)
    col = lax.broadcasted_iota(jnp.int32, (tq, tq), 1)
    lim = jnp.where(ki == qi, 0, tq)
    return jnp.where(col > row + lim, -jnp.inf, s)


def _flash_fwd(q, k, v, *, name):
    S = q.shape[0]
    tq = _attn_tile(S)
    nq = S // tq
    qt, kt = _tri_tables(nq, by_k=False)
    scale = 1.0 / math.sqrt(QK)

    def body(qt_ref, kt_ref, q_ref, k_ref, v_ref, o_ref, ob_ref, lse_ref, m_sc, l_sc, acc_sc):
        t = pl.program_id(1)
        qi, ki = qt_ref[t], kt_ref[t]

        @pl.when(ki == 0)
        def _():
            m_sc[...] = jnp.full_like(m_sc, -jnp.inf)
            l_sc[...] = jnp.zeros_like(l_sc)
            acc_sc[...] = jnp.zeros_like(acc_sc)

        s = _scores(q_ref, k_ref, qi, ki, tq, scale)
        m_prev = m_sc[...]
        m_new = jnp.maximum(m_prev, jnp.max(s, axis=1, keepdims=True))
        alpha = jnp.exp(m_prev - m_new)
        p = jnp.exp(s - m_new)
        l_sc[...] = alpha * l_sc[...] + jnp.sum(p, axis=1, keepdims=True)
        acc_sc[...] = alpha * acc_sc[...] + jnp.dot(p.astype(BF16), v_ref[...], preferred_element_type=F32)
        m_sc[...] = m_new

        @pl.when(ki == qi)
        def _():
            o = acc_sc[...] / l_sc[...]
            o_ref[...] = o
            ob_ref[...] = o.astype(BF16)
            lse_ref[...] = jnp.broadcast_to(m_sc[...] + jnp.log(l_sc[...]), (tq, 128))

    grid_spec = pltpu.PrefetchScalarGridSpec(
        num_scalar_prefetch=2, grid=(HEADS, qt.shape[0]),
        in_specs=[pl.BlockSpec((tq, HEAD_PAD), lambda h, t, a, b: (a[t], h)),
                  pl.BlockSpec((tq, HEAD_PAD), lambda h, t, a, b: (b[t], h)),
                  pl.BlockSpec((tq, 128), lambda h, t, a, b: (b[t], h))],
        out_specs=[pl.BlockSpec((tq, 128), lambda h, t, a, b: (a[t], h)),
                   pl.BlockSpec((tq, 128), lambda h, t, a, b: (a[t], h)),
                   pl.BlockSpec((None, tq, 128), lambda h, t, a, b: (h, a[t], 0))],
        scratch_shapes=[pltpu.VMEM((tq, 1), F32), pltpu.VMEM((tq, 1), F32), pltpu.VMEM((tq, 128), F32)])
    return pl.pallas_call(
        body, grid_spec=grid_spec,
        out_shape=[jax.ShapeDtypeStruct((S, HEADS * 128), F32), jax.ShapeDtypeStruct((S, HEADS * 128), BF16),
                   jax.ShapeDtypeStruct((HEADS, S, 128), F32)],
        compiler_params=_cp(("parallel", "arbitrary")), name=name)(qt, kt, q, k, v)


def _flash_bwd_dq(q, k, v, o, lse, d_cat, *, name):
    S = q.shape[0]
    tq = _attn_tile(S)
    nq = S // tq
    qt, kt = _tri_tables(nq, by_k=False)
    scale = 1.0 / math.sqrt(QK)

    def body(qt_ref, kt_ref, q_ref, k_ref, v_ref, o_ref, lse_ref, do_ref, dq_ref, acc_sc):
        t = pl.program_id(1)
        qi, ki = qt_ref[t], kt_ref[t]

        @pl.when(ki == 0)
        def _():
            acc_sc[...] = jnp.zeros_like(acc_sc)

        s = _scores(q_ref, k_ref, qi, ki, tq, scale)
        p = jnp.exp(s - lse_ref[:, :1])
        do = do_ref[...]
        dp = lax.dot_general(do.astype(BF16), v_ref[...], _DN["nt"], preferred_element_type=F32)
        delta = jnp.sum(do * o_ref[...], axis=1, keepdims=True)
        ds = (p * (dp - delta)).astype(BF16)
        acc_sc[...] += jnp.dot(ds, k_ref[...], preferred_element_type=F32) * scale

        @pl.when(ki == qi)
        def _():
            dq_ref[...] = acc_sc[...]

    grid_spec = pltpu.PrefetchScalarGridSpec(
        num_scalar_prefetch=2, grid=(HEADS, qt.shape[0]),
        in_specs=[pl.BlockSpec((tq, HEAD_PAD), lambda h, t, a, b: (a[t], h)),
                  pl.BlockSpec((tq, HEAD_PAD), lambda h, t, a, b: (b[t], h)),
                  pl.BlockSpec((tq, 128), lambda h, t, a, b: (b[t], h)),
                  pl.BlockSpec((tq, 128), lambda h, t, a, b: (a[t], h)),
                  pl.BlockSpec((None, tq, 128), lambda h, t, a, b: (h, a[t], 0)),
                  pl.BlockSpec((tq, 128), lambda h, t, a, b: (a[t], 4 + h))],
        out_specs=pl.BlockSpec((tq, HEAD_PAD), lambda h, t, a, b: (a[t], h)),
        scratch_shapes=[pltpu.VMEM((tq, HEAD_PAD), F32)])
    return pl.pallas_call(
        body, grid_spec=grid_spec,
        out_shape=jax.ShapeDtypeStruct((S, HEADS * HEAD_PAD), F32),
        compiler_params=_cp(("parallel", "arbitrary")), name=name)(qt, kt, q, k, v, o, lse, d_cat)


def _flash_bwd_dkv(q, k, v, o, lse, d_cat, *, name):
    S = q.shape[0]
    tq = _attn_tile(S)
    nq = S // tq
    qt, kt = _tri_tables(nq, by_k=True)
    scale = 1.0 / math.sqrt(QK)

    def body(qt_ref, kt_ref, q_ref, k_ref, v_ref, o_ref, lse_ref, do_ref, dk_ref, dv_ref, dk_sc, dv_sc):
        t = pl.program_id(1)
        qi, ki = qt_ref[t], kt_ref[t]

        @pl.when(qi == ki)
        def _():
            dk_sc[...] = jnp.zeros_like(dk_sc)
            dv_sc[...] = jnp.zeros_like(dv_sc)

        s = _scores(q_ref, k_ref, qi, ki, tq, scale)
        p = jnp.exp(s - lse_ref[:, :1])
        do = do_ref[...]
        dob = do.astype(BF16)
        dv_sc[...] += lax.dot_general(p.astype(BF16), dob, _DN["tn"], preferred_element_type=F32)
        dp = lax.dot_general(dob, v_ref[...], _DN["nt"], preferred_element_type=F32)
        delta = jnp.sum(do * o_ref[...], axis=1, keepdims=True)
        ds = (p * (dp - delta)).astype(BF16)
        dk_sc[...] += lax.dot_general(ds, q_ref[...], _DN["tn"], preferred_element_type=F32) * scale

        @pl.when(qi == nq - 1)
        def _():
            dk_ref[...] = dk_sc[...]
            dv_ref[...] = dv_sc[...]

    grid_spec = pltpu.PrefetchScalarGridSpec(
        num_scalar_prefetch=2, grid=(HEADS, qt.shape[0]),
        in_specs=[pl.BlockSpec((tq, HEAD_PAD), lambda h, t, a, b: (a[t], h)),
                  pl.BlockSpec((tq, HEAD_PAD), lambda h, t, a, b: (b[t], h)),
                  pl.BlockSpec((tq, 128), lambda h, t, a, b: (b[t], h)),
                  pl.BlockSpec((tq, 128), lambda h, t, a, b: (a[t], h)),
                  pl.BlockSpec((None, tq, 128), lambda h, t, a, b: (h, a[t], 0)),
                  pl.BlockSpec((tq, 128), lambda h, t, a, b: (a[t], 4 + h))],
        out_specs=[pl.BlockSpec((tq, HEAD_PAD), lambda h, t, a, b: (b[t], h)),
                   pl.BlockSpec((tq, 128), lambda h, t, a, b: (b[t], h))],
        scratch_shapes=[pltpu.VMEM((tq, HEAD_PAD), F32), pltpu.VMEM((tq, 128), F32)])
    return pl.pallas_call(
        body, grid_spec=grid_spec,
        out_shape=[jax.ShapeDtypeStruct((S, HEADS * HEAD_PAD), F32), jax.ShapeDtypeStruct((S, HEADS * 128), F32)],
        compiler_params=_cp(("parallel", "arbitrary")), name=name)(qt, kt, q, k, v, o, lse, d_cat)


def _memk_fwd(mkv, gkx, *, name):
    M = mkv.shape[0]
    XW = X_HEADS * X_DIM

    def body(mkv_ref, g_ref, k_ref, v_ref):
        for h in range(X_HEADS):
            cs = slice(h * X_DIM, (h + 1) * X_DIM)
            xv = mkv_ref[:, cs]
            r = lax.rsqrt(jnp.mean(xv * xv, axis=-1, keepdims=True) + NORM_EPS)
            k_ref[:, cs] = ((xv * r) * g_ref[...]).astype(BF16)
        v_ref[...] = mkv_ref[:, XW:].astype(BF16)

    return pl.pallas_call(
        body, grid=(1,),
        in_specs=[pl.BlockSpec((M, 2 * XW), lambda i: (0, 0)), pl.BlockSpec((1, X_DIM), lambda i: (0, 0))],
        out_specs=[pl.BlockSpec((M, XW), lambda i: (0, 0)), pl.BlockSpec((M, XW), lambda i: (0, 0))],
        out_shape=[jax.ShapeDtypeStruct((M, XW), BF16), jax.ShapeDtypeStruct((M, XW), BF16)],
        compiler_params=_cp(("arbitrary",)), name=name)(mkv, gkx)


def _memk_bwd(mkv, gkx, dk, dv, *, name):
    M = mkv.shape[0]
    XW = X_HEADS * X_DIM

    def body(mkv_ref, g_ref, dk_ref, dv_ref, o_ref, gg_ref):
        gg = jnp.zeros((1, X_DIM), F32)
        for h in range(X_HEADS):
            cs = slice(h * X_DIM, (h + 1) * X_DIM)
            xv = mkv_ref[:, cs]
            r = lax.rsqrt(jnp.mean(xv * xv, axis=-1, keepdims=True) + NORM_EPS)
            n = xv * r
            dkv = dk_ref[:, cs]
            gg += jnp.sum(dkv * n, axis=0, keepdims=True)
            dn = dkv * g_ref[...]
            c = jnp.mean(dn * n, axis=-1, keepdims=True)
            o_ref[:, cs] = (r * (dn - n * c)).astype(BF16)
        o_ref[:, XW:] = dv_ref[...].astype(BF16)
        gg_ref[...] = gg

    full = lambda c: pl.BlockSpec((M, c), lambda i: (0, 0))
    return pl.pallas_call(
        body, grid=(1,),
        in_specs=[full(2 * XW), pl.BlockSpec((1, X_DIM), lambda i: (0, 0)), full(XW), full(XW)],
        out_specs=[full(2 * XW), pl.BlockSpec((1, X_DIM), lambda i: (0, 0))],
        out_shape=[jax.ShapeDtypeStruct((M, 2 * XW), BF16), jax.ShapeDtypeStruct((1, X_DIM), F32)],
        compiler_params=_cp(("arbitrary",)), name=name)(mkv, gkx, dk, dv)


def _xq_norm(z_ref, g_ref, h):
    xv = z_ref[:, h * X_DIM:(h + 1) * X_DIM]
    r = lax.rsqrt(jnp.mean(xv * xv, axis=-1, keepdims=True) + NORM_EPS)
    n = xv * r
    return n, r, n * g_ref[...]


def _xprobs(qb, k_ref, h):
    s = lax.dot_general(qb, k_ref[:, h * X_DIM:(h + 1) * X_DIM], _DN["nt"],
                        preferred_element_type=F32) * (1.0 / math.sqrt(X_DIM))
    e = jnp.exp(s - jnp.max(s, axis=-1, keepdims=True))
    return e / jnp.sum(e, axis=-1, keepdims=True)


def _memattn_fwd(z, kx, vx, gqx, *, name):
    S = z.shape[0]
    M = kx.shape[0]
    tm = _row_tile(S)
    XW = X_HEADS * X_DIM

    def body(z_ref, k_ref, v_ref, g_ref, o_ref):
        for h in range(X_HEADS):
            cs = slice(h * X_DIM, (h + 1) * X_DIM)
            _, _, qn = _xq_norm(z_ref, g_ref, h)
            p = _xprobs(qn.astype(BF16), k_ref, h)
            o_ref[:, cs] = jnp.dot(p.astype(BF16), v_ref[:, cs], preferred_element_type=F32).astype(BF16)

    return pl.pallas_call(
        body, grid=(S // tm,),
        in_specs=[pl.BlockSpec((tm, XW), lambda i: (i, Z_MQ_CB)), pl.BlockSpec((M, XW), lambda i: (0, 0)),
                  pl.BlockSpec((M, XW), lambda i: (0, 0)), pl.BlockSpec((1, X_DIM), lambda i: (0, 0))],
        out_specs=pl.BlockSpec((tm, XW), lambda i: (i, 0)),
        out_shape=jax.ShapeDtypeStruct((S, XW), BF16),
        compiler_params=_cp(("parallel",)), name=name)(z, kx, vx, gqx)


def _memattn_bwd(z, kx, vx, gqx, d_cat, *, name):
    S = z.shape[0]
    M = kx.shape[0]
    tm = _row_tile(S)
    XW = X_HEADS * X_DIM
    scale = 1.0 / math.sqrt(X_DIM)

    def body(z_ref, k_ref, v_ref, g_ref, do_ref, dz_ref, dk_ref, dv_ref, gg_ref):
        i = pl.program_id(0)

        @pl.when(i == 0)
        def _():
            dk_ref[...] = jnp.zeros_like(dk_ref)
            dv_ref[...] = jnp.zeros_like(dv_ref)
            gg_ref[...] = jnp.zeros_like(gg_ref)

        gg = jnp.zeros((1, X_DIM), F32)
        for h in range(X_HEADS):
            cs = slice(h * X_DIM, (h + 1) * X_DIM)
            n, r, qn = _xq_norm(z_ref, g_ref, h)
            qb = qn.astype(BF16)
            p = _xprobs(qb, k_ref, h)
            pb = p.astype(BF16)
            dob = do_ref[:, cs].astype(BF16)
            dv_ref[:, cs] += lax.dot_general(pb, dob, _DN["tn"], preferred_element_type=F32)
            dp = lax.dot_general(dob, v_ref[:, cs], _DN["nt"], preferred_element_type=F32)
            ds = (p * (dp - jnp.sum(dp * p, axis=-1, keepdims=True))).astype(BF16)
            dk_ref[:, cs] += lax.dot_general(ds, qb, _DN["tn"], preferred_element_type=F32) * scale
            dqn = jnp.dot(ds, k_ref[:, cs], preferred_element_type=F32) * scale
            gg += jnp.sum(dqn * n, axis=0, keepdims=True)
            dn = dqn * g_ref[...]
            c = jnp.mean(dn * n, axis=-1, keepdims=True)
            dz_ref[:, cs] = (r * (dn - n * c)).astype(BF16)
        gg_ref[...] += gg

    full = pl.BlockSpec((M, XW), lambda i: (0, 0))
    vec = pl.BlockSpec((1, X_DIM), lambda i: (0, 0))
    return pl.pallas_call(
        body, grid=(S // tm,),
        in_specs=[pl.BlockSpec((tm, XW), lambda i: (i, Z_MQ_CB)), full, full, vec,
                  pl.BlockSpec((tm, XW), lambda i: (i, 3))],
        out_specs=[pl.BlockSpec((tm, XW), lambda i: (i, 0)), full, full, vec],
        out_shape=[jax.ShapeDtypeStruct((S, XW), BF16), jax.ShapeDtypeStruct((M, XW), F32),
                   jax.ShapeDtypeStruct((M, XW), F32), jax.ShapeDtypeStruct((1, X_DIM), F32)],
        compiler_params=_cp(("arbitrary",)), name=name)(z, kx, vx, gqx, d_cat)


def _sigmoid(x):
    return 1.0 / (1.0 + jnp.exp(-x))


def _glu_tiles(S, F):
    return _row_tile(S), _pick(F, (1408, 512, 256, 128))


def _glu_fwd(g, u, conv_w, conv_b, *, name):
    S, F = g.shape
    tm, tc = _glu_tiles(S, F)
    hb = tm // 8

    def body(g_ref, gp_ref, u_ref, w_ref, b_ref, a_ref):
        i = pl.program_id(1)
        gt = g_ref[...]
        ext = jnp.concatenate([gp_ref[...] * (i > 0).astype(F32), gt], axis=0)
        gc = b_ref[...] + w_ref[0:1, :] * pltpu.roll(ext, 2, 0)[8:]
        gc = gc + w_ref[1:2, :] * pltpu.roll(ext, 1, 0)[8:]
        gc = gc + w_ref[2:3, :] * gt
        a_ref[...] = ((gc * _sigmoid(gc)) * u_ref[...]).astype(BF16)

    return pl.pallas_call(
        body, grid=(F // tc, S // tm),
        in_specs=[pl.BlockSpec((tm, tc), lambda j, i: (i, j)),
                  pl.BlockSpec((8, tc), lambda j, i: (jnp.maximum(i * hb - 1, 0), j)),
                  pl.BlockSpec((tm, tc), lambda j, i: (i, j)),
                  pl.BlockSpec((3, tc), lambda j, i: (0, j)),
                  pl.BlockSpec((1, tc), lambda j, i: (0, j))],
        out_specs=pl.BlockSpec((tm, tc), lambda j, i: (i, j)),
        out_shape=jax.ShapeDtypeStruct((S, F), BF16),
        compiler_params=_cp(("parallel", "parallel")), name=name)(g, g, u, conv_w, conv_b)


def _glu_bwd(g, u, d_a, conv_w, conv_b, *, name):
    S, F = g.shape
    tm, tc = _glu_tiles(S, F)
    hb = tm // 8
    nt = S // tm
    E = tm + 8

    def body(g_ref, gp_ref, gn_ref, u_ref, un_ref, da_ref, dan_ref, w_ref, b_ref,
             dg_ref, du_ref, gw_ref, gb_ref):
        i = pl.program_id(1)

        @pl.when(i == 0)
        def _():
            gw_ref[...] = jnp.zeros_like(gw_ref)
            gb_ref[...] = jnp.zeros_like(gb_ref)

        w0, w1, w2 = w_ref[0:1, :], w_ref[1:2, :], w_ref[2:3, :]
        gext = jnp.concatenate([gp_ref[...] * (i > 0).astype(F32), g_ref[...], gn_ref[...]], axis=0)
        g1 = pltpu.roll(gext, 1, 0)[8:]
        g2 = pltpu.roll(gext, 2, 0)[8:]
        g0 = gext[8:]
        gc = b_ref[...] + w0 * g2
        gc = gc + w1 * g1
        gc = gc + w2 * g0
        sig = _sigmoid(gc)
        da = jnp.concatenate([da_ref[...], dan_ref[...] * (i < nt - 1).astype(F32)], axis=0)
        uu = jnp.concatenate([u_ref[...], un_ref[...]], axis=0)
        du_ref[...] = (da[:tm] * (gc[:tm] * sig[:tm])).astype(BF16)
        dgc = (da * uu) * (sig * (1.0 + gc * (1.0 - sig)))
        dg = w2 * dgc[:tm] + w1 * pltpu.roll(dgc, E - 1, 0)[:tm] + w0 * pltpu.roll(dgc, E - 2, 0)[:tm]
        dg_ref[...] = dg.astype(BF16)
        dgt = dgc[:tm]
        gb_ref[...] += jnp.sum(dgt, axis=0, keepdims=True)
        gw_ref[0:1, :] += jnp.sum(dgt * g2[:tm], axis=0, keepdims=True)
        gw_ref[1:2, :] += jnp.sum(dgt * g1[:tm], axis=0, keepdims=True)
        gw_ref[2:3, :] += jnp.sum(dgt * g0[:tm], axis=0, keepdims=True)

    tile = pl.BlockSpec((tm, tc), lambda j, i: (i, j))
    nxt = pl.BlockSpec((8, tc), lambda j, i: (jnp.minimum((i + 1) * hb, S // 8 - 1), j))
    prv = pl.BlockSpec((8, tc), lambda j, i: (jnp.maximum(i * hb - 1, 0), j))
    return pl.pallas_call(
        body, grid=(F // tc, nt),
        in_specs=[tile, prv, nxt, tile, nxt, tile, nxt,
                  pl.BlockSpec((3, tc), lambda j, i: (0, j)), pl.BlockSpec((1, tc), lambda j, i: (0, j))],
        out_specs=[tile, tile, pl.BlockSpec((3, tc), lambda j, i: (0, j)), pl.BlockSpec((1, tc), lambda j, i: (0, j))],
        out_shape=[jax.ShapeDtypeStruct((S, F), BF16), jax.ShapeDtypeStruct((S, F), BF16),
                   jax.ShapeDtypeStruct((3, F), F32), jax.ShapeDtypeStruct((1, F), F32)],
        compiler_params=_cp(("parallel", "arbitrary")), name=name)(g, g, g, u, u, d_a, d_a, conv_w, conv_b)


def _loss_head(y, target, *, name):
    S, D = y.shape
    tm = _row_tile(S)
    nt = S // tm

    def body(y_ref, t_ref, dy_ref, dyb_ref, loss_ref, acc):
        i = pl.program_id(0)

        @pl.when(i == 0)
        def _():
            acc[...] = jnp.zeros_like(acc)

        e = y_ref[...] - t_ref[...]
        dy = e * (1.0 / D)
        dy_ref[...] = dy
        dyb_ref[...] = dy.astype(BF16)
        acc[...] += jnp.sum(e * e, axis=0, keepdims=True)

        @pl.when(i == nt - 1)
        def _():
            loss_ref[...] = jnp.broadcast_to(jnp.sum(acc[...], axis=1, keepdims=True) * (0.5 / D), (1, 128))

    row = pl.BlockSpec((tm, D), lambda i: (i, 0))
    return pl.pallas_call(
        body, grid=(nt,), in_specs=[row, row],
        out_specs=[row, row, pl.BlockSpec((1, 128), lambda i: (0, 0))],
        out_shape=[jax.ShapeDtypeStruct((S, D), F32), jax.ShapeDtypeStruct((S, D), BF16),
                   jax.ShapeDtypeStruct((1, 128), F32)],
        scratch_shapes=[pltpu.VMEM((1, D), F32)],
        compiler_params=_cp(("arbitrary",)), name=name)(y, target)


def _adamw_math(w, g, m, v):
    m = ADAM_B1 * m + (1.0 - ADAM_B1) * g
    v = ADAM_B2 * v + (1.0 - ADAM_B2) * (g * g)
    m_hat = m / (1.0 - ADAM_B1 ** ADAM_STEP)
    v_hat = v / (1.0 - ADAM_B2 ** ADAM_STEP)
    delta = -ADAM_LR * (m_hat / (jnp.sqrt(v_hat) + ADAM_EPS) + ADAM_WD * w)
    return delta, m, v


def _adamw(w, m, v, parts, *, name):
    R, C = w.shape
    tr = 128 if R % 128 == 0 else R
    n_parts = len(parts)

    def body(*refs):
        w_ref, m_ref, v_ref = refs[:3]
        p_refs = refs[3:3 + n_parts]
        g_ref, d_ref, mo_ref, vo_ref = refs[3 + n_parts:]
        g = p_refs[0][...]
        for p in p_refs[1:]:
            g = g + p[...]
        delta, mn, vn = _adamw_math(w_ref[...], g, m_ref[...], v_ref[...])
        g_ref[...] = g
        d_ref[...] = delta
        mo_ref[...] = mn
        vo_ref[...] = vn

    blk = pl.BlockSpec((tr, C), lambda i: (i, 0))
    return pl.pallas_call(
        body, grid=(R // tr,), in_specs=[blk] * (3 + n_parts), out_specs=[blk] * 4,
        out_shape=[jax.ShapeDtypeStruct((R, C), F32)] * 4,
        compiler_params=_cp(("parallel",)), name=name)(w, m, v, *parts)


def _sum4(g_stack, recv, me, *, name):
    _, R, C = g_stack.shape
    tr = 128 if R % 128 == 0 else R

    def body(me_ref, g_ref, r_ref, o_ref):
        acc = g_ref[...]
        for j in range(N_CHIPS - 1):
            acc = acc + r_ref[j].astype(F32)
        o_ref[...] = acc

    grid_spec = pltpu.PrefetchScalarGridSpec(
        num_scalar_prefetch=1, grid=(R // tr,),
        in_specs=[pl.BlockSpec((None, tr, C), lambda i, me_ref: (me_ref[0], i, 0)),
                  pl.BlockSpec((N_CHIPS - 1, tr, C), lambda i, me_ref: (0, i, 0))],
        out_specs=pl.BlockSpec((tr, C), lambda i, me_ref: (i, 0)))
    return pl.pallas_call(
        body, grid_spec=grid_spec, out_shape=jax.ShapeDtypeStruct((R, C), F32),
        compiler_params=_cp(("parallel",)), name=name)(me, g_stack, recv)


def _sum8(gathered, *, name):
    _, R, C = gathered.shape

    def body(g_ref, o_ref):
        acc = g_ref[0]
        for d in range(1, N_DEV):
            acc = acc + g_ref[d]
        o_ref[...] = acc

    return pl.pallas_call(
        body, grid=(1,), in_specs=[pl.BlockSpec((N_DEV, R, C), lambda i: (0, 0, 0))],
        out_specs=pl.BlockSpec((R, C), lambda i: (0, 0)),
        out_shape=jax.ShapeDtypeStruct((R, C), F32),
        compiler_params=_cp(("arbitrary",)), name=name)(gathered)


def _place():
    return lax.axis_index("x"), lax.axis_index("y"), lax.axis_index("c")


def _other_chips(x, y):
    return [(1 - x, y), (x, 1 - y), (1 - x, 1 - y)]


_ANY = pl.BlockSpec(memory_space=pl.ANY)


_HBM = pl.BlockSpec(memory_space=pltpu.HBM)
_SEM = pl.BlockSpec(memory_space=pltpu.SEMAPHORE)
_EFFECT = pltpu.SideEffectType.DATAFLOW_SIDE_EFFECTING


def _gather_copies(srcs, lands, send_sems, recv_sems):
    x, y, c = _place()
    me = 2 * x + y
    return [pltpu.make_async_remote_copy(
        src_ref=srcs[i], dst_ref=lands[i].at[me], send_sem=send_sems.at[3 * i + j],
        recv_sem=recv_sems.at[3 * i + j], device_id=(px, py, c), device_id_type=MESH)
        for i in range(len(srcs)) for j, (px, py) in enumerate(_other_chips(x, y))]


def _scatter_copies(srcs, lands, send_sems, recv_sems):
    x, y, c = _place()
    return [pltpu.make_async_remote_copy(
        src_ref=srcs[i].at[2 * px + py], dst_ref=lands[i].at[j], send_sem=send_sems.at[3 * i + j],
        recv_sem=recv_sems.at[3 * i + j], device_id=(px, py, c), device_id_type=MESH)
        for i in range(len(srcs)) for j, (px, py) in enumerate(_other_chips(x, y))]


def _copies_start(srcs, lands, make_copies, *, name):
    n = len(srcs)

    def body(*refs):
        send_sems, recv_sems = refs[2 * n], refs[2 * n + 1]
        for cp in make_copies(refs[:n], refs[n:2 * n], send_sems, recv_sems):
            cp.start()
        refs[-1][...] = jnp.zeros_like(refs[-1])

    ops = list(srcs) + list(lands)
    outs = pl.pallas_call(
        body, name=name,
        out_shape=(pltpu.SemaphoreType.DMA((3 * n,)), pltpu.SemaphoreType.DMA((3 * n,)),
                   *[pltpu.HBM(a.shape, a.dtype) for a in ops], jax.ShapeDtypeStruct((8, 128), F32)),
        in_specs=[_HBM] * (2 * n),
        out_specs=(_SEM, _SEM, *[_HBM] * (2 * n), pl.BlockSpec(memory_space=pltpu.VMEM)),
        input_output_aliases={i: 2 + i for i in range(2 * n)},
        compiler_params=pltpu.CompilerParams(has_side_effects=_EFFECT),
    )(*[pltpu.with_memory_space_constraint(a, pltpu.HBM) for a in ops])
    return outs[0], outs[1], list(outs[2:2 + n]), list(outs[2 + n:2 + 2 * n]), outs[-1]


def _copies_wait(handle, after, make_copies, *, name):
    send_sems, recv_sems, srcs, lands, _ = handle
    n = len(srcs)

    def body(*refs):
        for cp in make_copies(refs[:n], refs[n:2 * n], refs[2 * n], refs[2 * n + 1]):
            cp.wait_send()
            cp.wait_recv()

    ops = list(srcs) + list(lands)
    outs = pl.pallas_call(
        body, name=name,
        out_shape=tuple(pltpu.HBM(a.shape, a.dtype) for a in ops),
        in_specs=[_HBM] * (2 * n) + [_SEM, _SEM, _ANY],
        out_specs=tuple([_HBM] * (2 * n)),
        input_output_aliases={i: i for i in range(2 * n)},
        compiler_params=pltpu.CompilerParams(has_side_effects=_EFFECT),
    )(*ops, send_sems, recv_sems, after)
    return list(outs[n:])


def _swap_with_sibling(arrs, *, name):
    n = len(arrs)

    def body(*refs):
        ins, outs = refs[:n], refs[n:2 * n]
        send_sems, recv_sems = refs[2 * n:]
        x, y, c = _place()
        remote = []
        for i in range(n):
            rc = pltpu.make_async_remote_copy(
                src_ref=ins[i], dst_ref=outs[i], send_sem=send_sems.at[i], recv_sem=recv_sems.at[i],
                device_id=(x, y, 1 - c), device_id_type=MESH)
            rc.start()
            remote.append(rc)
        for rc in remote:
            rc.wait_send()
        for rc in remote:
            rc.wait_recv()

    return pl.pallas_call(
        body, in_specs=[_ANY] * n, out_specs=[_ANY] * n,
        out_shape=[jax.ShapeDtypeStruct(a.shape, a.dtype) for a in arrs],
        scratch_shapes=[pltpu.SemaphoreType.DMA((n,)), pltpu.SemaphoreType.DMA((n,))],
        name=name)(*arrs)


def _gather_all(buf, *, name):
    R, C = buf.shape

    def body(in_ref, out_ref, send_sems, recv_sems, local_sem):
        x, y, c = _place()
        me = 4 * x + 2 * y + c
        lc = pltpu.make_async_copy(in_ref, out_ref.at[me], local_sem)
        lc.start()
        remote = []
        for k in range(1, N_DEV):
            px = 1 - x if (k >> 2) & 1 else x
            py = 1 - y if (k >> 1) & 1 else y
            pc = 1 - c if k & 1 else c
            rc = pltpu.make_async_remote_copy(
                src_ref=in_ref, dst_ref=out_ref.at[me], send_sem=send_sems.at[k - 1],
                recv_sem=recv_sems.at[k - 1], device_id=(px, py, pc), device_id_type=MESH)
            rc.start()
            remote.append(rc)
        lc.wait()
        for rc in remote:
            rc.wait_send()
        for rc in remote:
            rc.wait_recv()

    return pl.pallas_call(
        body, in_specs=[_ANY], out_specs=_ANY,
        out_shape=jax.ShapeDtypeStruct((N_DEV, R, C), buf.dtype),
        scratch_shapes=[pltpu.SemaphoreType.DMA((N_DEV - 1,)), pltpu.SemaphoreType.DMA((N_DEV - 1,)),
                        pltpu.SemaphoreType.DMA],
        name=name)(buf)


def _w_in_to_z(w):
    pad = jnp.zeros(w.shape[:-1] + (64,), w.dtype)
    return jnp.concatenate([w[..., 0:512], w[..., 512:1024], w[..., 1344:1856], w[..., 1024:1280],
                            w[..., 1280:1344], pad], axis=-1)


def _z_to_w_in(g):
    return jnp.concatenate([g[..., 0:512], g[..., 512:1024], g[..., 1536:1792], g[..., 1792:1856],
                            g[..., 1024:1536]], axis=-1)


def _pad_heads(w, nh):
    w = w.reshape(w.shape[:-1] + (nh, QK))
    w = jnp.concatenate([w, jnp.zeros(w.shape[:-1] + (HEAD_PAD - QK,), w.dtype)], axis=-1)
    return w.reshape(w.shape[:-2] + (nh * HEAD_PAD,))


def _unpad_heads(g, nh):
    g = g.reshape(g.shape[:-1] + (nh, HEAD_PAD))[..., :QK]
    return g.reshape(g.shape[:-2] + (nh * QK,))


def _kv_split(w, nh):
    w = w.reshape(w.shape[:-1] + (nh, 2, 128))
    return jnp.swapaxes(w, -3, -2).reshape(w.shape[:-3] + (nh * 256,))


def _kv_join(g, nh):
    g = g.reshape(g.shape[:-1] + (2, nh, 128))
    return jnp.swapaxes(g, -3, -2).reshape(g.shape[:-3] + (nh * 256,))


def _kv_perm(j):
    return (j % 2) * N_CHIPS + j // 2


def _pad_gain(g):
    return jnp.concatenate([g, jnp.zeros((1, HEAD_PAD - QK), g.dtype)], axis=1)


_SMALL = ("g_mix", "g_q_lat", "g_kv_lat", "g_q_mla", "g_k_mla", "w_pool", "pool_scale", "g_mem", "g_q_x",
          "g_k_x", "g_ffn", "conv_b", "conv_w")


def _pack(arrs, extra=0):
    flat = jnp.concatenate([a.reshape(-1) for a in arrs])
    n = flat.shape[0] + extra
    rows = -(-n // 1024) * 8
    return jnp.pad(flat, (0, rows * 128 - flat.shape[0])).reshape(rows, 128)


def _unpack(buf, shapes):
    flat = buf.reshape(-1)
    out, off = [], 0
    for s in shapes:
        n = int(np.prod(s))
        out.append(flat[off:off + n].reshape(s))
        off += n
    return out, off


def _tied(a, token):
    return a + token[:1, :1].astype(a.dtype)


def _local_step(x, mem, target, W, fetch=None, ship=None):
    fetch = fetch or (lambda group, after: None)
    ship = ship or (lambda group, G: jnp.zeros((8, 128), F32))
    S, D = x.shape
    F = W["conv_b"].shape[1]
    tabs = _rope_tables(S)
    tm = 512 if S % 512 == 0 else 128
    tk = _pick(S, (1024, 512, 128))

    h = _rms_fwd(x, W["g_mix"], C=D, name="norm_mix")
    fetch("g1", h)
    z = mm_nn(h, W["w_in"], tm=tm, tn=Z_COLS, tk=D, name="z_proj")
    fetch("g2", z)
    y_pool = _pool_fwd(z, W["w_pool"], W["pool_scale"], name="pool_fwd")
    ql = _rms_fwd(z, W["g_q_lat"], C=Q_RANK, cb=Z_Q_CB, name="norm_qlat")
    kvl = _rms_fwd(z, W["g_kv_lat"], C=KV_RANK, cb=Z_KV_CB, name="norm_kvlat")
    qraw = mm_nn(ql, W["w_q_up"], nsh=N_CHIPS, tm=tm, tn=512, tk=Q_RANK, name="q_up")
    kvraw = mm_nn(kvl, W["w_kv_up"], nsh=N_CHIPS, tm=tm, tn=256, tk=KV_RANK, perm=_kv_perm, name="kv_up")
    q, k, v = _qkrope_fwd(qraw, kvraw, z, W["g_q_mla"], W["g_k_mla"], tabs, name="qk_norm_rope")
    o, y_mla, lse = _flash_fwd(q, k, v, name="mla_fwd")
    memn = _rms_fwd(mem, W["g_mem"], C=D, name="norm_mem")
    M = mem.shape[0]
    mkv = mm_nn(memn, W["w_mem_kv"], tm=M, tn=1024, tk=D, name="mem_kv")
    kx, vx = _memk_fwd(mkv, W["g_k_x"], name="memk_fwd")
    y_mem = _memattn_fwd(z, kx, vx, W["g_q_x"], name="memattn_fwd")
    cat = jnp.concatenate([y_pool, y_mla, y_mem], axis=1)
    x2 = mm_nn(cat, W["w_o"], tm=tm, tn=D, tk=1024, add=x, name="o_proj")
    h2 = _rms_fwd(x2, W["g_ffn"], C=D, name="norm_ffn")
    fetch("g3", h2)
    fn = F // N_CHIPS
    g = mm_nn(h2, W["w_gate"], nsh=N_CHIPS, tm=tm, tn=fn, tk=D, name="gate_proj")
    u = mm_nn(h2, W["w_up"], nsh=N_CHIPS, tm=tm, tn=fn, tk=D, name="up_proj")
    a = _glu_fwd(g, u, W["conv_w"], W["conv_b"], name="glu_fwd")
    y = mm_nn(a, W["w_down"], tm=tm, tn=D, tk=fn, add=x2, name="down_proj")
    dy, dyb, loss_row = _loss_head(y, target, name="loss_head")

    G = {}
    d_a = mm_nt(dyb, W["w_down"], tm=tm, to=fn, tc=D, name="d_a")
    G["w_down"] = mm_tn(a, dyb, to=fn, tn=1024, tk=tk, out_dtypes=(F32, BF16), name="grad_w_down")
    d_g, d_u, G["conv_w"], G["conv_b"] = _glu_bwd(g, u, d_a, W["conv_w"], W["conv_b"], name="glu_bwd")
    G["w_gate"] = mm_tn(h2, d_g, nsh=N_CHIPS, to=1024, tn=fn, tk=tk, out_dtypes=(F32, BF16), name="grad_w_gate")
    G["w_up"] = mm_tn(h2, d_u, nsh=N_CHIPS, to=1024, tn=fn, tk=tk, out_dtypes=(F32, BF16), name="grad_w_up")
    d_h2 = mm_nt(d_g, W["w_gate"], nsh=N_CHIPS, tm=tm, to=D, tc=fn, name="d_h2_gate")
    tok = ship("s1", G)
    d_h2 = mm_nt(d_u, W["w_up"], nsh=N_CHIPS, tm=tm, to=D, tc=fn, add=d_h2, name="d_h2_up")
    d_x2, d_x2b, G["g_ffn"] = _rms_bwd(x2, d_h2, _tied(W["g_ffn"], tok), C=D, res=dy, out_dtypes=(F32, BF16),
                                       name="norm_ffn_bwd")

    d_cat = mm_nt(d_x2b, W["w_o"], tm=tm, to=D, tc=D, name="d_cat")
    G["w_o"] = mm_tn(cat, d_x2b, to=1024, tn=1024, tk=tk, out_dtypes=(F32, BF16), name="grad_w_o")
    tok = ship("s2", G)
    dz_pool, G["w_pool"], G["pool_scale"] = _pool_bwd(z, d_cat, W["w_pool"], _tied(W["pool_scale"], tok),
                                                      name="pool_bwd")
    dz_mq, dkx, dvx, G["g_q_x"] = _memattn_bwd(z, kx, vx, W["g_q_x"], d_cat, name="memattn_bwd")
    d_mkv, G["g_k_x"] = _memk_bwd(mkv, W["g_k_x"], dkx, dvx, name="memk_bwd")
    G["w_mem_kv"] = mm_tn(memn, d_mkv, to=1024, tn=1024, tk=M, out_dtypes=(F32, BF16), name="grad_w_mem_kv")
    d_memn = mm_nt(d_mkv, W["w_mem_kv"], tm=M, to=D, tc=1024, name="d_memn")
    _, G["g_mem"] = _rms_bwd(mem, d_memn, W["g_mem"], C=D, name="norm_mem_bwd")
    dq = _flash_bwd_dq(q, k, v, o, lse, d_cat, name="mla_bwd_dq")
    dk, dv = _flash_bwd_dkv(q, k, v, o, lse, d_cat, name="mla_bwd_dkv")
    d_qraw, d_kvraw, dz_kr, G["g_q_mla"], G["g_k_mla"] = _qkrope_bwd(
        qraw, kvraw, z, W["g_q_mla"], W["g_k_mla"], tabs, dq, dk, dv, name="qk_norm_rope_bwd")
    G["w_q_up"] = mm_tn(ql, d_qraw, nsh=N_CHIPS, to=Q_RANK, tn=512, tk=tk, out_dtypes=(F32, BF16), name="grad_w_q_up")
    d_ql = mm_nt(d_qraw, W["w_q_up"], nsh=N_CHIPS, tm=tm, to=Q_RANK, tc=512, name="d_ql")
    G["w_kv_up"] = mm_tn(kvl, d_kvraw, nsh=N_CHIPS, to=KV_RANK, tn=256, tk=tk, out_dtypes=(F32, BF16),
                         perm=_kv_perm, name="grad_w_kv_up")
    d_kvl = mm_nt(d_kvraw, W["w_kv_up"], nsh=N_CHIPS, tm=tm, to=KV_RANK, tc=256, perm=_kv_perm, name="d_kvl")
    dz_q, G["g_q_lat"] = _rms_bwd(z, d_ql, W["g_q_lat"], C=Q_RANK, cb=Z_Q_CB, out_dtypes=(BF16,), name="norm_qlat_bwd")
    dz_kv, G["g_kv_lat"] = _rms_bwd(z, d_kvl, W["g_kv_lat"], C=KV_RANK, cb=Z_KV_CB, out_dtypes=(BF16,),
                                    name="norm_kvlat_bwd")
    d_z = jnp.concatenate([dz_pool, dz_q, dz_mq, dz_kv, dz_kr], axis=1)
    G["w_in"] = mm_tn(h, d_z, to=512, tn=Z_COLS, tk=tk, out_dtypes=(F32, BF16), name="grad_w_in")
    tok = ship("s3", G)
    d_h = mm_nt(d_z, W["w_in"], tm=tm, to=D, tc=Z_COLS, name="d_h")
    grad_x, G["g_mix"] = _rms_bwd(x, d_h, _tied(W["g_mix"], tok), C=D, res=d_x2, name="norm_mix_bwd")
    return loss_row, grad_x, G


_BIG = ("w_in", "w_q_up", "w_kv_up", "w_mem_kv", "w_o", "w_gate", "w_up", "w_down")
_WEIGHTS = ("g_mix", "w_in", "g_q_lat", "w_q_up", "g_kv_lat", "w_kv_up", "g_q_mla", "g_k_mla", "w_pool",
            "pool_scale", "g_mem", "w_mem_kv", "g_q_x", "g_k_x", "w_o", "g_ffn", "w_gate", "w_up", "conv_w",
            "conv_b", "w_down")


def _to_compute_layout(name, w):
    if name == "w_in":
        return _w_in_to_z(w)
    if name == "w_q_up":
        return _pad_heads(w, w.shape[-1] // QK)
    if name == "w_kv_up":
        return _kv_split(w, w.shape[-1] // 256)
    return w


def _from_compute_layout(name, g):
    if name == "w_in":
        return _z_to_w_in(g)
    if name == "w_q_up":
        return _unpad_heads(g, g.shape[-1] // HEAD_PAD)
    if name == "w_kv_up":
        return _kv_join(g, g.shape[-1] // 256)
    return g


def kernel(x, mem, g_mix, w_in, g_q_lat, w_q_up, g_kv_lat, w_kv_up, g_q_mla, g_k_mla, w_pool, pool_scale, g_mem, w_mem_kv, g_q_x, g_k_x, w_o, g_ffn, w_gate, w_up, conv_w, conv_b, w_down, loss_target, m_g_mix, m_w_in, m_g_q_lat, m_w_q_up, m_g_kv_lat, m_w_kv_up, m_g_q_mla, m_g_k_mla, m_w_pool, m_pool_scale, m_g_mem, m_w_mem_kv, m_g_q_x, m_g_k_x, m_w_o, m_g_ffn, m_w_gate, m_w_up, m_conv_w, m_conv_b, m_w_down, v_g_mix, v_w_in, v_g_q_lat, v_w_q_up, v_g_kv_lat, v_w_kv_up, v_g_q_mla, v_g_k_mla, v_w_pool, v_pool_scale, v_g_mem, v_w_mem_kv, v_g_q_x, v_g_k_x, v_w_o, v_g_ffn, v_w_gate, v_w_up, v_conv_w, v_conv_b, v_w_down):
    P = dict(g_mix=g_mix, w_in=w_in, g_q_lat=g_q_lat, w_q_up=w_q_up, g_kv_lat=g_kv_lat, w_kv_up=w_kv_up,
             g_q_mla=g_q_mla, g_k_mla=g_k_mla, w_pool=w_pool, pool_scale=pool_scale, g_mem=g_mem,
             w_mem_kv=w_mem_kv, g_q_x=g_q_x, g_k_x=g_k_x, w_o=w_o, g_ffn=g_ffn, w_gate=w_gate, w_up=w_up,
             conv_w=conv_w, conv_b=conv_b, w_down=w_down)
    Mo = dict(g_mix=m_g_mix, w_in=m_w_in, g_q_lat=m_g_q_lat, w_q_up=m_w_q_up, g_kv_lat=m_g_kv_lat,
              w_kv_up=m_w_kv_up, g_q_mla=m_g_q_mla, g_k_mla=m_g_k_mla, w_pool=m_w_pool,
              pool_scale=m_pool_scale, g_mem=m_g_mem, w_mem_kv=m_w_mem_kv, g_q_x=m_g_q_x, g_k_x=m_g_k_x,
              w_o=m_w_o, g_ffn=m_g_ffn, w_gate=m_w_gate, w_up=m_w_up, conv_w=m_conv_w, conv_b=m_conv_b,
              w_down=m_w_down)
    Vo = dict(g_mix=v_g_mix, w_in=v_w_in, g_q_lat=v_g_q_lat, w_q_up=v_w_q_up, g_kv_lat=v_g_kv_lat,
              w_kv_up=v_w_kv_up, g_q_mla=v_g_q_mla, g_k_mla=v_g_k_mla, w_pool=v_w_pool,
              pool_scale=v_pool_scale, g_mem=v_g_mem, w_mem_kv=v_w_mem_kv, g_q_x=v_g_q_x, g_k_x=v_g_k_x,
              w_o=v_w_o, g_ffn=v_g_ffn, w_gate=v_w_gate, w_up=v_w_up, conv_w=v_conv_w, conv_b=v_conv_b,
              w_down=v_w_down)
    xi, yi, ci = _place()
    me = (2 * xi + yi).astype(jnp.int32).reshape(1)

    shard = {n: _to_compute_layout(n, P[n][0]).astype(BF16) for n in _BIG}
    shard["conv_w"] = conv_w[0]
    gather_groups = {"g1": ("w_in",), "g2": ("w_q_up", "w_kv_up", "w_mem_kv", "w_o"),
                     "g3": ("w_gate", "w_up", "w_down", "conv_w")}
    gathers = {}
    tok = jnp.zeros((1, 1), F32)
    for grp, names in gather_groups.items():
        lands = [lax.dynamic_update_slice(lax.empty((N_CHIPS,) + shard[n].shape, shard[n].dtype), shard[n][None],
                                          (me[0], 0, 0)) for n in names]
        gathers[grp] = _copies_start([shard[n] for n in names], lands, _gather_copies, name="gather_start_" + grp)
        tok = tok + gathers[grp][4][:1, :1]
    W = {}
    W["g_q_mla"], W["g_k_mla"] = _pad_gain(g_q_mla), _pad_gain(g_k_mla)
    W["w_pool"] = w_pool[0].astype(BF16)
    for n in ("g_mix", "g_q_lat", "g_kv_lat", "pool_scale", "g_mem", "g_q_x", "g_k_x", "g_ffn", "conv_b"):
        W[n] = P[n]
    W["g_mix"] = _tied(W["g_mix"], tok)

    def fetch(grp, after):
        stacks = _copies_wait(gathers[grp], after, _gather_copies, name="gather_wait_" + grp)
        for n, s in zip(gather_groups[grp], stacks):
            if n == "conv_w":
                W[n] = jnp.swapaxes(s, 0, 1).reshape(3, -1)
            else:
                W[n] = s.reshape(-1, s.shape[-1])

    shard_shape = {n: shard[n].shape for n in _BIG}
    scatter_groups = {"s1": ("w_down", "w_gate", "w_up"), "s2": ("w_o",),
                      "s3": ("w_mem_kv", "w_q_up", "w_kv_up", "w_in")}
    scatters = {}

    def ship(grp, G):
        names = scatter_groups[grp]
        srcs = [G[n][1].reshape((N_CHIPS,) + shard_shape[n]) for n in names]
        lands = [lax.empty((N_CHIPS - 1,) + shard_shape[n], BF16) for n in names]
        scatters[grp] = _copies_start(srcs, lands, _scatter_copies, name="scatter_start_" + grp)
        return scatters[grp][4]

    loss_row, grad_x, G = _local_step(x[0], mem[0], loss_target[0], W, fetch, ship)

    recv = {}
    for grp, names in scatter_groups.items():
        for n, r in zip(names, _copies_wait(scatters[grp], grad_x, _scatter_copies, name="scatter_wait_" + grp)):
            recv[n] = r
    part = [_sum4(G[n][0].reshape((N_CHIPS,) + shard_shape[n]), recv[n], me, name="sum4_" + n) for n in _BIG]
    part = [_from_compute_layout(n, p) for n, p in zip(_BIG, part)]
    sib = _swap_with_sibling(part, name="swap_grads")
    out = {}
    for n, p, s in zip(_BIG, part, sib):
        out[n] = [r[None] for r in _adamw(P[n][0], Mo[n][0], Vo[n][0], [p, s], name="adamw_" + n)]

    conv_w_full_grad = G["conv_w"]
    small_g = [G["g_mix"], G["g_q_lat"], G["g_kv_lat"], G["g_q_mla"][:, :QK], G["g_k_mla"][:, :QK], G["w_pool"],
               G["pool_scale"], G["g_mem"], G["g_q_x"], G["g_k_x"], G["g_ffn"], G["conv_b"], conv_w_full_grad]
    packed = _pack(small_g + [loss_row[:, :1]])
    total = _sum8(_gather_all(packed, name="gather_small"), name="sum_small")
    shapes = [a.shape for a in small_g] + [(1, 1)]
    (parts, _) = _unpack(total, shapes)
    loss = parts[-1].reshape(())
    F = conv_b.shape[1]
    fn = F // N_CHIPS
    col0 = (2 * xi + yi) * fn
    sg = dict(zip(_SMALL, parts[:-1]))
    sg["conv_w"] = lax.dynamic_slice(sg["conv_w"], (0, col0), (3, fn))
    sw = [P[n].reshape(sg[n].shape) for n in _SMALL]
    sm = [Mo[n].reshape(sg[n].shape) for n in _SMALL]
    sv = [Vo[n].reshape(sg[n].shape) for n in _SMALL]
    gp = _pack([sg[n] for n in _SMALL])
    res = _adamw(_pack(sw), _pack(sm), _pack(sv), [gp], name="adamw_small")
    sshapes = [sg[n].shape for n in _SMALL]
    for kind, buf in zip(range(4), res):
        vals, _ = _unpack(buf, sshapes)
        for n, val in zip(_SMALL, vals):
            out.setdefault(n, [None] * 4)[kind] = val.reshape(P[n].shape)

    return (loss, grad_x[None], *[out[n][0] for n in _WEIGHTS], *[out[n][1] for n in _WEIGHTS],
            *[out[n][2] for n in _WEIGHTS], *[out[n][3] for n in _WEIGHTS])
```

```python
import functools
import math

import numpy as np
import jax
import jax.numpy as jnp
from jax import lax
from jax.experimental import pallas as pl
from jax.experimental.pallas import tpu as pltpu

F32, BF16 = jnp.float32, jnp.bfloat16
NORM_EPS = 1e-6
ROPE_THETA = 10000.0
V7X_VMEM_LIMIT_BYTES = 48 * 1024 * 1024
N_CHIPS = 4
N_DEV = 8

POOL_W = 512
POOL_WINDOWS = (2, 4, 8, 16)
HEADS = 8
NOPE, ROPE, QK = 128, 64, 192
HEAD_PAD = 256
Q_RANK, KV_RANK = 512, 256
X_HEADS, X_DIM = 4, 128
Z_COLS = 1920
Z_POOL_CB, Z_Q_CB, Z_MQ_CB = 0, 1, 2
Z_KV_CB = 6
Z_KR_CB = 14

ADAM_LR, ADAM_B1, ADAM_B2, ADAM_EPS, ADAM_WD, ADAM_STEP = 0.001, 0.9, 0.999, 1e-08, 0.01, 10

MESH = pl.DeviceIdType.MESH


def _cp(sem):
    return pltpu.CompilerParams(dimension_semantics=sem, vmem_limit_bytes=V7X_VMEM_LIMIT_BYTES)


def _row_tile(S):
    return 256 if S % 256 == 0 and S >= 2048 else 128


def _pick(dim, prefs):
    for p in prefs:
        if dim % p == 0:
            return p
    return dim


_DN = {"nn": (((1,), (0,)), ((), ())), "nt": (((1,), (1,)), ((), ())), "tn": (((0,), (0,)), ((), ()))}


def _mm(a, b, *, mode, grid, blocks, maps, out_shape, out_dtypes, add=None, name):
    nk = grid[2]
    dn = _DN[mode]
    n_out = len(out_dtypes)

    def body(*refs):
        a_ref, b_ref = refs[0], refs[1]
        add_ref = refs[2] if add is not None else None
        p = 2 + (add is not None)
        o_refs = refs[p:p + n_out]
        acc = refs[p + n_out]
        k = pl.program_id(2)

        @pl.when(k == 0)
        def _():
            acc[...] = jnp.zeros_like(acc)

        acc[...] += lax.dot_general(a_ref[...].astype(BF16), b_ref[...].astype(BF16), dn,
                                    preferred_element_type=F32)

        @pl.when(k == nk - 1)
        def _():
            r = acc[...]
            if add_ref is not None:
                r = r + add_ref[...]
            for o in o_refs:
                o[...] = r.astype(o.dtype)

    a_blk, b_blk, o_blk = blocks
    a_map, b_map, o_map = maps
    in_specs = [pl.BlockSpec(a_blk, a_map), pl.BlockSpec(b_blk, b_map)]
    args = [a, b]
    if add is not None:
        in_specs.append(pl.BlockSpec(o_blk, o_map))
        args.append(add)
    outs = pl.pallas_call(
        body, grid=grid, in_specs=in_specs,
        out_specs=[pl.BlockSpec(o_blk, o_map) for _ in out_dtypes],
        out_shape=[jax.ShapeDtypeStruct(out_shape, d) for d in out_dtypes],
        scratch_shapes=[pltpu.VMEM(o_blk, F32)],
        compiler_params=_cp(("parallel", "parallel", "arbitrary")), name=name)(*args)
    return outs[0] if n_out == 1 else outs


def _ident(j):
    return j


def mm_nn(a, w, *, nsh=1, tm, tn, tk, out_dtypes=(F32,), add=None, perm=_ident, name):
    M, K = a.shape
    n = w.shape[1]
    N = nsh * n
    assert w.shape[0] == nsh * K and n % tn == 0 and K % tk == 0 and M % tm == 0
    npt, kt = n // tn, K // tk
    return _mm(a, w, mode="nn", grid=(M // tm, N // tn, kt),
               blocks=((tm, tk), (tk, tn), (tm, tn)),
               maps=(lambda i, j, k: (i, k), lambda i, j, k: ((j // npt) * kt + k, j % npt),
                     lambda i, j, k: (i, perm(j))),
               out_shape=(M, N), out_dtypes=out_dtypes, add=add, name=name)


def mm_nt(d, w, *, nsh=1, tm, to, tc, out_dtypes=(F32,), add=None, perm=_ident, name):
    M, N = d.shape
    n = w.shape[1]
    K = w.shape[0] // nsh
    assert nsh * n == N and n % tc == 0 and K % to == 0 and M % tm == 0
    cpt, ot = n // tc, K // to
    return _mm(d, w, mode="nt", grid=(M // tm, ot, N // tc),
               blocks=((tm, tc), (to, tc), (tm, to)),
               maps=(lambda i, j, c: (i, perm(c)), lambda i, j, c: ((c // cpt) * ot + j, c % cpt),
                     lambda i, j, c: (i, j)),
               out_shape=(M, K), out_dtypes=out_dtypes, add=add, name=name)


def mm_tn(x, d, *, nsh=1, to, tn, tk, out_dtypes=(F32,), perm=_ident, name):
    M, K = x.shape
    N = d.shape[1]
    n = N // nsh
    assert n % tn == 0 and K % to == 0 and M % tk == 0
    npt, ot = n // tn, K // to
    return _mm(x, d, mode="tn", grid=(ot, N // tn, M // tk),
               blocks=((tk, to), (tk, tn), (to, tn)),
               maps=(lambda i, j, k: (k, i), lambda i, j, k: (k, perm(j)),
                     lambda i, j, k: ((j // npt) * ot + i, j % npt)),
               out_shape=(nsh * K, n), out_dtypes=out_dtypes, name=name)


def _rms_fwd(x, g, *, C, cb=0, name):
    S = x.shape[0]
    tm = _row_tile(S) if S >= 128 else S

    def body(x_ref, g_ref, o_ref):
        xv = x_ref[...]
        r = lax.rsqrt(jnp.mean(xv * xv, axis=-1, keepdims=True) + NORM_EPS)
        o_ref[...] = ((xv * r) * g_ref[...]).astype(o_ref.dtype)

    return pl.pallas_call(
        body, grid=(S // tm,),
        in_specs=[pl.BlockSpec((tm, C), lambda i: (i, cb)), pl.BlockSpec((1, C), lambda i: (0, 0))],
        out_specs=pl.BlockSpec((tm, C), lambda i: (i, 0)),
        out_shape=jax.ShapeDtypeStruct((S, C), BF16),
        compiler_params=_cp(("parallel",)), name=name)(x, g)


def _rms_bwd(x, dh, g, *, C, cb=0, res=None, out_dtypes=(F32,), name):
    S = x.shape[0]
    tm = _row_tile(S) if S >= 128 else S
    n_out = len(out_dtypes)

    def body(*refs):
        x_ref, dh_ref, g_ref = refs[:3]
        res_ref = refs[3] if res is not None else None
        p = 3 + (res is not None)
        outs = refs[p:p + n_out]
        dg_ref = refs[p + n_out]
        i = pl.program_id(0)
        xv = x_ref[...]
        r = lax.rsqrt(jnp.mean(xv * xv, axis=-1, keepdims=True) + NORM_EPS)
        n = xv * r
        dhv = dh_ref[...].astype(F32)
        dn = dhv * g_ref[...]
        c = jnp.mean(dn * n, axis=-1, keepdims=True)
        dx = r * (dn - n * c)
        if res_ref is not None:
            dx = res_ref[...] + dx
        for o in outs:
            o[...] = dx.astype(o.dtype)

        @pl.when(i == 0)
        def _():
            dg_ref[...] = jnp.zeros_like(dg_ref)

        dg_ref[...] += jnp.sum(dhv * n, axis=0, keepdims=True)

    row = pl.BlockSpec((tm, C), lambda i: (i, 0))
    in_specs = [pl.BlockSpec((tm, C), lambda i: (i, cb)), row, pl.BlockSpec((1, C), lambda i: (0, 0))]
    args = [x, dh, g]
    if res is not None:
        in_specs.append(row)
        args.append(res)
    return pl.pallas_call(
        body, grid=(S // tm,), in_specs=in_specs,
        out_specs=[row] * n_out + [pl.BlockSpec((1, C), lambda i: (0, 0))],
        out_shape=[jax.ShapeDtypeStruct((S, C), d) for d in out_dtypes] + [jax.ShapeDtypeStruct((1, C), F32)],
        compiler_params=_cp(("arbitrary",)), name=name)(*args)


def _pool_cnt(t0, rows, w):
    t = t0 + lax.broadcasted_iota(jnp.int32, (rows, 1), 0)
    return jnp.minimum(t + 1, w).astype(F32)


def _pool_d(halo, tile, gi, t0, tm):
    s = jnp.concatenate([halo, tile], axis=0)
    for step in (1, 2, 4, 8)[:gi + 1]:
        s = s + pltpu.roll(s, step, 0)
    return s[16:] / _pool_cnt(t0, tm, POOL_WINDOWS[gi]) - tile


def _pool_fwd(z, w_pool, pool_scale, *, name):
    S = z.shape[0]
    tm = _row_tile(S)
    hb = tm // 16

    def body(z_ref, h_ref, w_ref, sc_ref, o_ref):
        i = pl.program_id(0)
        halo = h_ref[...] * (i > 0).astype(F32)
        for gi in range(4):
            cs = slice(gi * 128, (gi + 1) * 128)
            d = _pool_d(halo[:, cs], z_ref[:, cs], gi, i * tm, tm)
            yp = jnp.dot(d.astype(BF16), w_ref[gi], preferred_element_type=F32)
            o_ref[:, cs] = (yp * sc_ref[:, cs]).astype(o_ref.dtype)

    return pl.pallas_call(
        body, grid=(S // tm,),
        in_specs=[pl.BlockSpec((tm, POOL_W), lambda i: (i, Z_POOL_CB)),
                  pl.BlockSpec((16, POOL_W), lambda i: (jnp.maximum(i * hb - 1, 0), Z_POOL_CB)),
                  pl.BlockSpec((4, 128, 128), lambda i: (0, 0, 0)),
                  pl.BlockSpec((1, POOL_W), lambda i: (0, 0))],
        out_specs=pl.BlockSpec((tm, POOL_W), lambda i: (i, 0)),
        out_shape=jax.ShapeDtypeStruct((S, POOL_W), BF16),
        compiler_params=_cp(("parallel",)), name=name)(z, z, w_pool, pool_scale)


def _pool_bwd(z, d_cat, w_pool, pool_scale, *, name):
    S = z.shape[0]
    tm = _row_tile(S)
    hb = tm // 16
    nt = S // tm
    E = tm + 16

    def body(z_ref, h_ref, dy_ref, dyn_ref, w_ref, sc_ref, dz_ref, gw_ref, gs_ref):
        i = pl.program_id(0)

        @pl.when(i == 0)
        def _():
            gw_ref[...] = jnp.zeros_like(gw_ref)
            gs_ref[...] = jnp.zeros_like(gs_ref)

        halo = h_ref[...] * (i > 0).astype(F32)
        dy_next = dyn_ref[...] * (i < nt - 1).astype(F32)
        for gi in range(4):
            cs = slice(gi * 128, (gi + 1) * 128)
            w = w_ref[gi]
            d = _pool_d(halo[:, cs], z_ref[:, cs], gi, i * tm, tm)
            db = d.astype(BF16)
            dy = dy_ref[:, cs]
            yp = jnp.dot(db, w, preferred_element_type=F32)
            gs_ref[:, cs] += jnp.sum(dy * yp, axis=0, keepdims=True)
            sc = sc_ref[:, cs]
            dys = (dy * sc).astype(BF16)
            gw_ref[gi] += lax.dot_general(db, dys, _DN["tn"], preferred_element_type=F32)
            dys_ext = jnp.concatenate([dys, (dy_next[:, cs] * sc).astype(BF16)], axis=0)
            dd = lax.dot_general(dys_ext, w, _DN["nt"], preferred_element_type=F32)
            r = dd / _pool_cnt(i * tm, E, POOL_WINDOWS[gi])
            for step in (1, 2, 4, 8)[:gi + 1]:
                r = r + pltpu.roll(r, E - step, 0)
            dz_ref[:, cs] = (r[:tm] - dd[:tm]).astype(dz_ref.dtype)

    return pl.pallas_call(
        body, grid=(nt,),
        in_specs=[pl.BlockSpec((tm, POOL_W), lambda i: (i, Z_POOL_CB)),
                  pl.BlockSpec((16, POOL_W), lambda i: (jnp.maximum(i * hb - 1, 0), Z_POOL_CB)),
                  pl.BlockSpec((tm, POOL_W), lambda i: (i, 0)),
                  pl.BlockSpec((16, POOL_W), lambda i: (jnp.minimum((i + 1) * hb, S // 16 - 1), 0)),
                  pl.BlockSpec((4, 128, 128), lambda i: (0, 0, 0)),
                  pl.BlockSpec((1, POOL_W), lambda i: (0, 0))],
        out_specs=[pl.BlockSpec((tm, POOL_W), lambda i: (i, 0)),
                   pl.BlockSpec((4, 128, 128), lambda i: (0, 0, 0)),
                   pl.BlockSpec((1, POOL_W), lambda i: (0, 0))],
        out_shape=[jax.ShapeDtypeStruct((S, POOL_W), BF16),
                   jax.ShapeDtypeStruct((4, 128, 128), F32),
                   jax.ShapeDtypeStruct((1, POOL_W), F32)],
        compiler_params=_cp(("arbitrary",)), name=name)(z, z, d_cat, d_cat, w_pool, pool_scale)


def _rope_tables(S):
    half = ROPE // 2
    inv_freq = 1.0 / (ROPE_THETA ** (jnp.arange(half, dtype=F32) / half))
    ang = jnp.arange(S).astype(F32)[:, None] * inv_freq[None, :]
    cos, sin = jnp.cos(ang), jnp.sin(ang)
    zero = jnp.zeros((S, half), F32)
    cos_t = jnp.concatenate([cos, cos, zero, zero], axis=1)
    sa_t = jnp.concatenate([-sin, zero, zero, zero], axis=1)
    sb_t = jnp.concatenate([zero, sin, zero, zero], axis=1)
    return cos_t, sa_t, sb_t


def _head_fwd(xn, xr, gn, gr, cos, sa, sb):
    ms = (jnp.sum(xn * xn, axis=-1, keepdims=True) + jnp.sum(xr * xr, axis=-1, keepdims=True)) * (1.0 / QK)
    r = lax.rsqrt(ms + NORM_EPS)
    on = (xn * r) * gn
    yr = (xr * r) * gr
    orr = yr * cos + pltpu.roll(yr, 96, 1) * sa + pltpu.roll(yr, 32, 1) * sb
    return on, orr


def _head_bwd(xn, xr, gn, gr, don, dor, cos, sa, sb):
    ms = (jnp.sum(xn * xn, axis=-1, keepdims=True) + jnp.sum(xr * xr, axis=-1, keepdims=True)) * (1.0 / QK)
    r = lax.rsqrt(ms + NORM_EPS)
    nn, nr = xn * r, xr * r
    dyr = dor * cos + pltpu.roll(dor * sa, 32, 1) + pltpu.roll(dor * sb, 96, 1)
    ggn, ggr = don * nn, dyr * nr
    dnn, dnr = don * gn, dyr * gr
    c = (jnp.sum(dnn * nn, axis=-1, keepdims=True) + jnp.sum(dnr * nr, axis=-1, keepdims=True)) * (1.0 / QK)
    return r * (dnn - nn * c), r * (dnr - nr * c), ggn, ggr


def _qkrope_fwd(qraw, kvraw, z, gq, gk, tabs, *, name):
    S = qraw.shape[0]
    tm = _row_tile(S)

    def body(q_ref, kv_ref, zkr_ref, gq_ref, gk_ref, cos_ref, sa_ref, sb_ref, qo_ref, ko_ref, vo_ref, vt_ref):
        cos, sa, sb = cos_ref[...], sa_ref[...], sb_ref[...]
        zkr = zkr_ref[...]
        gqn, gqr, gkn, gkr = gq_ref[:, :128], gq_ref[:, 128:], gk_ref[:, :128], gk_ref[:, 128:]
        for h in range(HEADS):
            b = h * HEAD_PAD
            on, orr = _head_fwd(q_ref[:, b:b + 128], q_ref[:, b + 128:b + 256], gqn, gqr, cos, sa, sb)
            qo_ref[:, b:b + 128] = on.astype(BF16)
            qo_ref[:, b + 128:b + 256] = orr.astype(BF16)
            on, orr = _head_fwd(kv_ref[:, h * 128:(h + 1) * 128], zkr, gkn, gkr, cos, sa, sb)
            ko_ref[:, b:b + 128] = on.astype(BF16)
            ko_ref[:, b + 128:b + 256] = orr.astype(BF16)
        vv = kv_ref[:, HEADS * 128:]
        vo_ref[...] = vv.astype(BF16)
        vt_ref[...] = jnp.transpose(vv).astype(BF16)

    W = HEADS * HEAD_PAD
    row = lambda c: pl.BlockSpec((tm, c), lambda i: (i, 0))
    vec = lambda c: pl.BlockSpec((1, c), lambda i: (0, 0))
    return pl.pallas_call(
        body, grid=(S // tm,),
        in_specs=[row(W), row(W), pl.BlockSpec((tm, 128), lambda i: (i, Z_KR_CB)), vec(256), vec(256),
                  row(128), row(128), row(128)],
        out_specs=[row(W), row(W), row(HEADS * 128), pl.BlockSpec((HEADS * 128, tm), lambda i: (0, i))],
        out_shape=[jax.ShapeDtypeStruct((S, W), BF16), jax.ShapeDtypeStruct((S, W), BF16),
                   jax.ShapeDtypeStruct((S, HEADS * 128), BF16), jax.ShapeDtypeStruct((HEADS * 128, S), BF16)],
        compiler_params=_cp(("parallel",)), name=name)(qraw, kvraw, z, gq, gk, *tabs)


def _qkrope_bwd(qraw, kvraw, z, gq, gk, tabs, dq, dk, dv, *, name):
    S = qraw.shape[0]
    tm = _row_tile(S)

    def body(q_ref, kv_ref, zkr_ref, gq_ref, gk_ref, cos_ref, sa_ref, sb_ref, dq_ref, dk_ref, dv_ref,
             dqo_ref, dkvo_ref, dkr_ref, ggq_ref, ggk_ref):
        i = pl.program_id(0)

        @pl.when(i == 0)
        def _():
            ggq_ref[...] = jnp.zeros_like(ggq_ref)
            ggk_ref[...] = jnp.zeros_like(ggk_ref)

        cos, sa, sb = cos_ref[...], sa_ref[...], sb_ref[...]
        zkr = zkr_ref[...]
        gqn, gqr, gkn, gkr = gq_ref[:, :128], gq_ref[:, 128:], gk_ref[:, :128], gk_ref[:, 128:]
        dkr = jnp.zeros((tm, 128), F32)
        sq_n = jnp.zeros((1, 128), F32)
        sq_r = jnp.zeros((1, 128), F32)
        sk_n = jnp.zeros((1, 128), F32)
        sk_r = jnp.zeros((1, 128), F32)
        for h in range(HEADS):
            b = h * HEAD_PAD
            dxn, dxr, ggn, ggr = _head_bwd(q_ref[:, b:b + 128], q_ref[:, b + 128:b + 256], gqn, gqr,
                                           dq_ref[:, b:b + 128], dq_ref[:, b + 128:b + 256], cos, sa, sb)
            dqo_ref[:, b:b + 128] = dxn.astype(BF16)
            dqo_ref[:, b + 128:b + 256] = dxr.astype(BF16)
            sq_n += jnp.sum(ggn, axis=0, keepdims=True)
            sq_r += jnp.sum(ggr, axis=0, keepdims=True)
            dxn, dxr, ggn, ggr = _head_bwd(kv_ref[:, h * 128:(h + 1) * 128], zkr, gkn, gkr,
                                           dk_ref[:, b:b + 128], dk_ref[:, b + 128:b + 256], cos, sa, sb)
            dkvo_ref[:, h * 128:(h + 1) * 128] = dxn.astype(BF16)
            dkr += dxr
            sk_n += jnp.sum(ggn, axis=0, keepdims=True)
            sk_r += jnp.sum(ggr, axis=0, keepdims=True)
        dkvo_ref[:, HEADS * 128:] = dv_ref[...].astype(BF16)
        dkr_ref[...] = dkr.astype(BF16)
        ggq_ref[:, :128] += sq_n
        ggq_ref[:, 128:] += sq_r
        ggk_ref[:, :128] += sk_n
        ggk_ref[:, 128:] += sk_r

    W = HEADS * HEAD_PAD
    row = lambda c: pl.BlockSpec((tm, c), lambda i: (i, 0))
    vec = lambda c: pl.BlockSpec((1, c), lambda i: (0, 0))
    return pl.pallas_call(
        body, grid=(S // tm,),
        in_specs=[row(W), row(W), pl.BlockSpec((tm, 128), lambda i: (i, Z_KR_CB)), vec(256), vec(256),
                  row(128), row(128), row(128), row(W), row(W), row(HEADS * 128)],
        out_specs=[row(W), row(W), row(128), vec(256), vec(256)],
        out_shape=[jax.ShapeDtypeStruct((S, W), BF16), jax.ShapeDtypeStruct((S, W), BF16),
                   jax.ShapeDtypeStruct((S, 128), BF16),
                   jax.ShapeDtypeStruct((1, 256), F32), jax.ShapeDtypeStruct((1, 256), F32)],
        compiler_params=_cp(("arbitrary",)), name=name)(qraw, kvraw, z, gq, gk, *tabs, dq, dk, dv)


LOG2E = 1.4426950408889634
SCORE_SCALE = 1.0 / math.sqrt(QK)
SCORE_SCALE_LOG2 = SCORE_SCALE * LOG2E


def _fa_tile(S):
    return 512 if S % 512 == 0 and S >= 2048 else 128


def _flash_fwd(q, k, vt, *, name):
    S = q.shape[0]
    ts = _fa_tile(S)

    def body(q_ref, k_ref, vt_ref, o_ref, ob_ref, lse_ref, m_sc, l_sc, acc_sc):
        qi = pl.program_id(1)
        m_sc[...] = jnp.full_like(m_sc, -jnp.inf)
        l_sc[...] = jnp.zeros_like(l_sc)
        acc_sc[...] = jnp.zeros_like(acc_sc)
        qb = q_ref[...]

        def scores(kidx, masked):
            k0 = pl.multiple_of(kidx * ts, ts)
            st = lax.dot_general(k_ref[pl.ds(k0, ts), :], qb, _DN["nt"], preferred_element_type=F32) * SCORE_SCALE_LOG2
            if masked:
                st = jnp.where(lax.broadcasted_iota(jnp.int32, (ts, ts), 0) > lax.broadcasted_iota(jnp.int32, (ts, ts), 1),
                               -jnp.inf, st)
            return st

        def update(st, kidx):
            k0 = pl.multiple_of(kidx * ts, ts)
            m_prev = m_sc[...]
            m_new = jnp.maximum(m_prev, jnp.max(st, axis=0, keepdims=True))
            alpha = jnp.exp2(m_prev - m_new)
            pt = jnp.exp2(st - m_new[0:1, :])
            l_sc[...] = alpha * l_sc[...] + jnp.sum(pt, axis=0, keepdims=True)
            acc_sc[...] = alpha[0:1, :] * acc_sc[...] + jnp.dot(vt_ref[:, pl.ds(k0, ts)], pt.astype(BF16),
                                                                preferred_element_type=F32)
            m_sc[...] = m_new

        def pair(t, carry):
            sa, sb = scores(2 * t, False), scores(2 * t + 1, False)
            update(sa, 2 * t)
            update(sb, 2 * t + 1)
            return carry

        lax.fori_loop(0, qi // 2, pair, 0)

        @pl.when(qi % 2 == 1)
        def _():
            update(scores(qi - 1, False), qi - 1)

        update(scores(qi, True), qi)
        ot = acc_sc[...] / l_sc[0:1, :]
        o = jnp.transpose(ot)
        o_ref[...] = o
        ob_ref[...] = o.astype(BF16)
        lse_ref[...] = m_sc[...] + jnp.log2(l_sc[...])

    return pl.pallas_call(
        body, grid=(HEADS, S // ts),
        in_specs=[pl.BlockSpec((ts, HEAD_PAD), lambda h, i: (i, h)),
                  pl.BlockSpec((S, HEAD_PAD), lambda h, i: (0, h)),
                  pl.BlockSpec((128, S), lambda h, i: (h, 0))],
        out_specs=[pl.BlockSpec((ts, 128), lambda h, i: (i, h)),
                   pl.BlockSpec((ts, 128), lambda h, i: (i, h)),
                   pl.BlockSpec((None, 8, ts), lambda h, i: (h, 0, i))],
        out_shape=[jax.ShapeDtypeStruct((S, HEADS * 128), F32), jax.ShapeDtypeStruct((S, HEADS * 128), BF16),
                   jax.ShapeDtypeStruct((HEADS, 8, S), F32)],
        scratch_shapes=[pltpu.VMEM((8, ts), F32), pltpu.VMEM((8, ts), F32), pltpu.VMEM((128, ts), F32)],
        compiler_params=_cp(("parallel", "arbitrary")), name=name)(q, k, vt)


def _attn_bwd_prep(o, d_cat, *, name):
    S = o.shape[0]
    tm = _row_tile(S)
    H = HEADS * 128
    half = H // 2

    def body(o_ref, da_ref, db_ref, dob_ref, delta_ref):
        for h in range(HEADS):
            src, c0 = (da_ref, h * 128) if h * 128 < half else (db_ref, h * 128 - half)
            do = src[:, c0:c0 + 128]
            dob_ref[:, h * 128:(h + 1) * 128] = do.astype(BF16)
            prod = jnp.transpose(do * o_ref[:, h * 128:(h + 1) * 128])
            delta_ref[h] = jnp.broadcast_to(jnp.sum(prod, axis=0, keepdims=True), (8, tm))

    return pl.pallas_call(
        body, grid=(S // tm,),
        in_specs=[pl.BlockSpec((tm, H), lambda i: (i, 0)),
                  pl.BlockSpec((tm, half), lambda i: (i, 1)), pl.BlockSpec((tm, half), lambda i: (i, 2))],
        out_specs=[pl.BlockSpec((tm, H), lambda i: (i, 0)), pl.BlockSpec((HEADS, 8, tm), lambda i: (0, 0, i))],
        out_shape=[jax.ShapeDtypeStruct((S, H), BF16), jax.ShapeDtypeStruct((HEADS, 8, S), F32)],
        compiler_params=_cp(("parallel",)), name=name)(o, d_cat, d_cat)


def _flash_bwd(q, k, v, dob, lse, delta, *, name):
    S = q.shape[0]
    ts = _fa_tile(S)
    nb = S // ts

    def body(q_ref, do_ref, lse_ref, delta_ref, k_ref, v_ref, dq_ref, dk_ref, dv_ref, dk_sc, dv_sc):
        j = pl.program_id(1)

        @pl.when(j == 0)
        def _():
            dq_ref[...] = jnp.zeros_like(dq_ref)

        dk_sc[...] = jnp.zeros_like(dk_sc)
        dv_sc[...] = jnp.zeros_like(dv_sc)
        kb, vb = k_ref[...], v_ref[...]

        def step(i, masked):
            q0 = pl.multiple_of(i * ts, ts)
            qb = q_ref[pl.ds(q0, ts), :]
            dob_ = do_ref[pl.ds(q0, ts), :]
            st = lax.dot_general(kb, qb, _DN["nt"], preferred_element_type=F32) * SCORE_SCALE_LOG2
            pt = jnp.exp2(st - lse_ref[0:1, pl.ds(q0, ts)])
            if masked:
                pt = jnp.where(lax.broadcasted_iota(jnp.int32, (ts, ts), 0) > lax.broadcasted_iota(jnp.int32, (ts, ts), 1),
                               0.0, pt)
            dv_sc[...] += jnp.dot(pt.astype(BF16), dob_, preferred_element_type=F32)
            dpt = lax.dot_general(vb, dob_, _DN["nt"], preferred_element_type=F32)
            dst = (pt * (dpt - delta_ref[0:1, pl.ds(q0, ts)])).astype(BF16)
            dk_sc[...] += jnp.dot(dst, qb, preferred_element_type=F32) * SCORE_SCALE
            dq_ref[pl.ds(q0, ts), :] += lax.dot_general(dst, kb, _DN["tn"], preferred_element_type=F32) * SCORE_SCALE

        step(j, True)

        def below(i, carry):
            step(i, False)
            return carry

        lax.fori_loop(j + 1, nb, below, 0)
        dk_ref[...] = dk_sc[...]
        dv_ref[...] = dv_sc[...]

    return pl.pallas_call(
        body, grid=(HEADS, nb),
        in_specs=[pl.BlockSpec((S, HEAD_PAD), lambda h, j: (0, h)),
                  pl.BlockSpec((S, 128), lambda h, j: (0, h)),
                  pl.BlockSpec((None, 8, S), lambda h, j: (h, 0, 0)),
                  pl.BlockSpec((None, 8, S), lambda h, j: (h, 0, 0)),
                  pl.BlockSpec((ts, HEAD_PAD), lambda h, j: (j, h)),
                  pl.BlockSpec((ts, 128), lambda h, j: (j, h))],
        out_specs=[pl.BlockSpec((S, HEAD_PAD), lambda h, j: (0, h)),
                   pl.BlockSpec((ts, HEAD_PAD), lambda h, j: (j, h)),
                   pl.BlockSpec((ts, 128), lambda h, j: (j, h))],
        out_shape=[jax.ShapeDtypeStruct((S, HEADS * HEAD_PAD), F32), jax.ShapeDtypeStruct((S, HEADS * HEAD_PAD), F32),
                   jax.ShapeDtypeStruct((S, HEADS * 128), F32)],
        scratch_shapes=[pltpu.VMEM((ts, HEAD_PAD), F32), pltpu.VMEM((ts, 128), F32)],
        compiler_params=_cp(("parallel", "arbitrary")), name=name)(q, dob, lse, delta, k, v)


def _memk_fwd(mkv, gkx, *, name):
    M = mkv.shape[0]
    XW = X_HEADS * X_DIM

    def body(mkv_ref, g_ref, k_ref, v_ref):
        for h in range(X_HEADS):
            cs = slice(h * X_DIM, (h + 1) * X_DIM)
            xv = mkv_ref[:, cs]
            r = lax.rsqrt(jnp.mean(xv * xv, axis=-1, keepdims=True) + NORM_EPS)
            k_ref[:, cs] = ((xv * r) * g_ref[...]).astype(BF16)
        v_ref[...] = mkv_ref[:, XW:].astype(BF16)

    return pl.pallas_call(
        body, grid=(1,),
        in_specs=[pl.BlockSpec((M, 2 * XW), lambda i: (0, 0)), pl.BlockSpec((1, X_DIM), lambda i: (0, 0))],
        out_specs=[pl.BlockSpec((M, XW), lambda i: (0, 0)), pl.BlockSpec((M, XW), lambda i: (0, 0))],
        out_shape=[jax.ShapeDtypeStruct((M, XW), BF16), jax.ShapeDtypeStruct((M, XW), BF16)],
        compiler_params=_cp(("arbitrary",)), name=name)(mkv, gkx)


def _memk_bwd(mkv, gkx, dk, dv, *, name):
    M = mkv.shape[0]
    XW = X_HEADS * X_DIM

    def body(mkv_ref, g_ref, dk_ref, dv_ref, o_ref, gg_ref):
        gg = jnp.zeros((1, X_DIM), F32)
        for h in range(X_HEADS):
            cs = slice(h * X_DIM, (h + 1) * X_DIM)
            xv = mkv_ref[:, cs]
            r = lax.rsqrt(jnp.mean(xv * xv, axis=-1, keepdims=True) + NORM_EPS)
            n = xv * r
            dkv = dk_ref[:, cs]
            gg += jnp.sum(dkv * n, axis=0, keepdims=True)
            dn = dkv * g_ref[...]
            c = jnp.mean(dn * n, axis=-1, keepdims=True)
            o_ref[:, cs] = (r * (dn - n * c)).astype(BF16)
        o_ref[:, XW:] = dv_ref[...].astype(BF16)
        gg_ref[...] = gg

    full = lambda c: pl.BlockSpec((M, c), lambda i: (0, 0))
    return pl.pallas_call(
        body, grid=(1,),
        in_specs=[full(2 * XW), pl.BlockSpec((1, X_DIM), lambda i: (0, 0)), full(XW), full(XW)],
        out_specs=[full(2 * XW), pl.BlockSpec((1, X_DIM), lambda i: (0, 0))],
        out_shape=[jax.ShapeDtypeStruct((M, 2 * XW), BF16), jax.ShapeDtypeStruct((1, X_DIM), F32)],
        compiler_params=_cp(("arbitrary",)), name=name)(mkv, gkx, dk, dv)


def _xq_norm(z_ref, g_ref, h):
    xv = z_ref[:, h * X_DIM:(h + 1) * X_DIM]
    r = lax.rsqrt(jnp.mean(xv * xv, axis=-1, keepdims=True) + NORM_EPS)
    n = xv * r
    return n, r, n * g_ref[...]


def _xprobs(qb, k_ref, h):
    s = lax.dot_general(qb, k_ref[:, h * X_DIM:(h + 1) * X_DIM], _DN["nt"],
                        preferred_element_type=F32) * (1.0 / math.sqrt(X_DIM))
    e = jnp.exp(s - jnp.max(s, axis=-1, keepdims=True))
    return e / jnp.sum(e, axis=-1, keepdims=True)


def _memattn_fwd(z, kx, vx, gqx, *, name):
    S = z.shape[0]
    M = kx.shape[0]
    tm = _row_tile(S)
    XW = X_HEADS * X_DIM

    def body(z_ref, k_ref, v_ref, g_ref, o_ref):
        for h in range(X_HEADS):
            cs = slice(h * X_DIM, (h + 1) * X_DIM)
            _, _, qn = _xq_norm(z_ref, g_ref, h)
            p = _xprobs(qn.astype(BF16), k_ref, h)
            o_ref[:, cs] = jnp.dot(p.astype(BF16), v_ref[:, cs], preferred_element_type=F32).astype(BF16)

    return pl.pallas_call(
        body, grid=(S // tm,),
        in_specs=[pl.BlockSpec((tm, XW), lambda i: (i, Z_MQ_CB)), pl.BlockSpec((M, XW), lambda i: (0, 0)),
                  pl.BlockSpec((M, XW), lambda i: (0, 0)), pl.BlockSpec((1, X_DIM), lambda i: (0, 0))],
        out_specs=pl.BlockSpec((tm, XW), lambda i: (i, 0)),
        out_shape=jax.ShapeDtypeStruct((S, XW), BF16),
        compiler_params=_cp(("parallel",)), name=name)(z, kx, vx, gqx)


def _memattn_bwd(z, kx, vx, gqx, d_cat, *, name):
    S = z.shape[0]
    M = kx.shape[0]
    tm = _row_tile(S)
    XW = X_HEADS * X_DIM
    scale = 1.0 / math.sqrt(X_DIM)

    def body(z_ref, k_ref, v_ref, g_ref, do_ref, dz_ref, dk_ref, dv_ref, gg_ref):
        i = pl.program_id(0)

        @pl.when(i == 0)
        def _():
            dk_ref[...] = jnp.zeros_like(dk_ref)
            dv_ref[...] = jnp.zeros_like(dv_ref)
            gg_ref[...] = jnp.zeros_like(gg_ref)

        gg = jnp.zeros((1, X_DIM), F32)
        for h in range(X_HEADS):
            cs = slice(h * X_DIM, (h + 1) * X_DIM)
            n, r, qn = _xq_norm(z_ref, g_ref, h)
            qb = qn.astype(BF16)
            p = _xprobs(qb, k_ref, h)
            pb = p.astype(BF16)
            dob = do_ref[:, cs].astype(BF16)
            dv_ref[:, cs] += lax.dot_general(pb, dob, _DN["tn"], preferred_element_type=F32)
            dp = lax.dot_general(dob, v_ref[:, cs], _DN["nt"], preferred_element_type=F32)
            ds = (p * (dp - jnp.sum(dp * p, axis=-1, keepdims=True))).astype(BF16)
            dk_ref[:, cs] += lax.dot_general(ds, qb, _DN["tn"], preferred_element_type=F32) * scale
            dqn = jnp.dot(ds, k_ref[:, cs], preferred_element_type=F32) * scale
            gg += jnp.sum(dqn * n, axis=0, keepdims=True)
            dn = dqn * g_ref[...]
            c = jnp.mean(dn * n, axis=-1, keepdims=True)
            dz_ref[:, cs] = (r * (dn - n * c)).astype(BF16)
        gg_ref[...] += gg

    full = pl.BlockSpec((M, XW), lambda i: (0, 0))
    vec = pl.BlockSpec((1, X_DIM), lambda i: (0, 0))
    return pl.pallas_call(
        body, grid=(S // tm,),
        in_specs=[pl.BlockSpec((tm, XW), lambda i: (i, Z_MQ_CB)), full, full, vec,
                  pl.BlockSpec((tm, XW), lambda i: (i, 3))],
        out_specs=[pl.BlockSpec((tm, XW), lambda i: (i, 0)), full, full, vec],
        out_shape=[jax.ShapeDtypeStruct((S, XW), BF16), jax.ShapeDtypeStruct((M, XW), F32),
                   jax.ShapeDtypeStruct((M, XW), F32), jax.ShapeDtypeStruct((1, X_DIM), F32)],
        compiler_params=_cp(("arbitrary",)), name=name)(z, kx, vx, gqx, d_cat)


def _sigmoid(x):
    return 1.0 / (1.0 + jnp.exp(-x))


def _glu_tiles(S, F):
    return _row_tile(S), _pick(F, (1408, 512, 256, 128))


def _glu_fwd(g, u, conv_w, conv_b, *, name):
    S, F = g.shape
    tm, tc = _glu_tiles(S, F)
    hb = tm // 8

    def body(g_ref, gp_ref, u_ref, w_ref, b_ref, a_ref):
        i = pl.program_id(1)
        gt = g_ref[...]
        ext = jnp.concatenate([gp_ref[...] * (i > 0).astype(F32), gt], axis=0)
        gc = b_ref[...] + w_ref[0:1, :] * pltpu.roll(ext, 2, 0)[8:]
        gc = gc + w_ref[1:2, :] * pltpu.roll(ext, 1, 0)[8:]
        gc = gc + w_ref[2:3, :] * gt
        a_ref[...] = ((gc * _sigmoid(gc)) * u_ref[...]).astype(BF16)

    return pl.pallas_call(
        body, grid=(F // tc, S // tm),
        in_specs=[pl.BlockSpec((tm, tc), lambda j, i: (i, j)),
                  pl.BlockSpec((8, tc), lambda j, i: (jnp.maximum(i * hb - 1, 0), j)),
                  pl.BlockSpec((tm, tc), lambda j, i: (i, j)),
                  pl.BlockSpec((3, tc), lambda j, i: (0, j)),
                  pl.BlockSpec((1, tc), lambda j, i: (0, j))],
        out_specs=pl.BlockSpec((tm, tc), lambda j, i: (i, j)),
        out_shape=jax.ShapeDtypeStruct((S, F), BF16),
        compiler_params=_cp(("parallel", "parallel")), name=name)(g, g, u, conv_w, conv_b)


def _glu_bwd(g, u, d_a, conv_w, conv_b, *, name):
    S, F = g.shape
    tm, tc = _glu_tiles(S, F)
    hb = tm // 8
    nt = S // tm
    E = tm + 8

    def body(g_ref, gp_ref, gn_ref, u_ref, un_ref, da_ref, dan_ref, w_ref, b_ref,
             dg_ref, du_ref, gw_ref, gb_ref):
        i = pl.program_id(1)

        @pl.when(i == 0)
        def _():
            gw_ref[...] = jnp.zeros_like(gw_ref)
            gb_ref[...] = jnp.zeros_like(gb_ref)

        w0, w1, w2 = w_ref[0:1, :], w_ref[1:2, :], w_ref[2:3, :]
        gext = jnp.concatenate([gp_ref[...] * (i > 0).astype(F32), g_ref[...], gn_ref[...]], axis=0)
        g1 = pltpu.roll(gext, 1, 0)[8:]
        g2 = pltpu.roll(gext, 2, 0)[8:]
        g0 = gext[8:]
        gc = b_ref[...] + w0 * g2
        gc = gc + w1 * g1
        gc = gc + w2 * g0
        sig = _sigmoid(gc)
        da = jnp.concatenate([da_ref[...], dan_ref[...] * (i < nt - 1).astype(F32)], axis=0)
        uu = jnp.concatenate([u_ref[...], un_ref[...]], axis=0)
        du_ref[...] = (da[:tm] * (gc[:tm] * sig[:tm])).astype(BF16)
        dgc = (da * uu) * (sig * (1.0 + gc * (1.0 - sig)))
        dg = w2 * dgc[:tm] + w1 * pltpu.roll(dgc, E - 1, 0)[:tm] + w0 * pltpu.roll(dgc, E - 2, 0)[:tm]
        dg_ref[...] = dg.astype(BF16)
        dgt = dgc[:tm]
        gb_ref[...] += jnp.sum(dgt, axis=0, keepdims=True)
        gw_ref[0:1, :] += jnp.sum(dgt * g2[:tm], axis=0, keepdims=True)
        gw_ref[1:2, :] += jnp.sum(dgt * g1[:tm], axis=0, keepdims=True)
        gw_ref[2:3, :] += jnp.sum(dgt * g0[:tm], axis=0, keepdims=True)

    tile = pl.BlockSpec((tm, tc), lambda j, i: (i, j))
    nxt = pl.BlockSpec((8, tc), lambda j, i: (jnp.minimum((i + 1) * hb, S // 8 - 1), j))
    prv = pl.BlockSpec((8, tc), lambda j, i: (jnp.maximum(i * hb - 1, 0), j))
    return pl.pallas_call(
        body, grid=(F // tc, nt),
        in_specs=[tile, prv, nxt, tile, nxt, tile, nxt,
                  pl.BlockSpec((3, tc), lambda j, i: (0, j)), pl.BlockSpec((1, tc), lambda j, i: (0, j))],
        out_specs=[tile, tile, pl.BlockSpec((3, tc), lambda j, i: (0, j)), pl.BlockSpec((1, tc), lambda j, i: (0, j))],
        out_shape=[jax.ShapeDtypeStruct((S, F), BF16), jax.ShapeDtypeStruct((S, F), BF16),
                   jax.ShapeDtypeStruct((3, F), F32), jax.ShapeDtypeStruct((1, F), F32)],
        compiler_params=_cp(("parallel", "arbitrary")), name=name)(g, g, g, u, u, d_a, d_a, conv_w, conv_b)


def _loss_head(y, target, *, name):
    S, D = y.shape
    tm = _row_tile(S)
    nt = S // tm

    def body(y_ref, t_ref, dy_ref, dyb_ref, loss_ref, acc):
        i = pl.program_id(0)

        @pl.when(i == 0)
        def _():
            acc[...] = jnp.zeros_like(acc)

        e = y_ref[...] - t_ref[...]
        dy = e * (1.0 / D)
        dy_ref[...] = dy
        dyb_ref[...] = dy.astype(BF16)
        acc[...] += jnp.sum(e * e, axis=0, keepdims=True)

        @pl.when(i == nt - 1)
        def _():
            loss_ref[...] = jnp.broadcast_to(jnp.sum(acc[...], axis=1, keepdims=True) * (0.5 / D), (1, 128))

    row = pl.BlockSpec((tm, D), lambda i: (i, 0))
    return pl.pallas_call(
        body, grid=(nt,), in_specs=[row, row],
        out_specs=[row, row, pl.BlockSpec((1, 128), lambda i: (0, 0))],
        out_shape=[jax.ShapeDtypeStruct((S, D), F32), jax.ShapeDtypeStruct((S, D), BF16),
                   jax.ShapeDtypeStruct((1, 128), F32)],
        scratch_shapes=[pltpu.VMEM((1, D), F32)],
        compiler_params=_cp(("arbitrary",)), name=name)(y, target)


def _adamw_math(w, g, m, v):
    m = ADAM_B1 * m + (1.0 - ADAM_B1) * g
    v = ADAM_B2 * v + (1.0 - ADAM_B2) * (g * g)
    m_hat = m / (1.0 - ADAM_B1 ** ADAM_STEP)
    v_hat = v / (1.0 - ADAM_B2 ** ADAM_STEP)
    delta = -ADAM_LR * (m_hat / (jnp.sqrt(v_hat) + ADAM_EPS) + ADAM_WD * w)
    return delta, m, v


def _adamw(w, m, v, parts, *, name):
    R, C = w.shape
    tr = 128 if R % 128 == 0 else R
    n_parts = len(parts)

    def body(*refs):
        w_ref, m_ref, v_ref = refs[:3]
        p_refs = refs[3:3 + n_parts]
        g_ref, d_ref, mo_ref, vo_ref = refs[3 + n_parts:]
        g = p_refs[0][...]
        for p in p_refs[1:]:
            g = g + p[...]
        delta, mn, vn = _adamw_math(w_ref[...], g, m_ref[...], v_ref[...])
        g_ref[...] = g
        d_ref[...] = delta
        mo_ref[...] = mn
        vo_ref[...] = vn

    blk = pl.BlockSpec((tr, C), lambda i: (i, 0))
    return pl.pallas_call(
        body, grid=(R // tr,), in_specs=[blk] * (3 + n_parts), out_specs=[blk] * 4,
        out_shape=[jax.ShapeDtypeStruct((R, C), F32)] * 4,
        compiler_params=_cp(("parallel",)), name=name)(w, m, v, *parts)


def _sum4(g_stack, recv, me, *, name):
    _, R, C = g_stack.shape
    tr = 128 if R % 128 == 0 else R

    def body(me_ref, g_ref, r_ref, o_ref):
        acc = g_ref[...]
        for j in range(N_CHIPS - 1):
            acc = acc + r_ref[j].astype(F32)
        o_ref[...] = acc

    grid_spec = pltpu.PrefetchScalarGridSpec(
        num_scalar_prefetch=1, grid=(R // tr,),
        in_specs=[pl.BlockSpec((None, tr, C), lambda i, me_ref: (me_ref[0], i, 0)),
                  pl.BlockSpec((N_CHIPS - 1, tr, C), lambda i, me_ref: (0, i, 0))],
        out_specs=pl.BlockSpec((tr, C), lambda i, me_ref: (i, 0)))
    return pl.pallas_call(
        body, grid_spec=grid_spec, out_shape=jax.ShapeDtypeStruct((R, C), F32),
        compiler_params=_cp(("parallel",)), name=name)(me, g_stack, recv)


def _sum8(gathered, *, name):
    _, R, C = gathered.shape

    def body(g_ref, o_ref):
        acc = g_ref[0]
        for d in range(1, N_DEV):
            acc = acc + g_ref[d]
        o_ref[...] = acc

    return pl.pallas_call(
        body, grid=(1,), in_specs=[pl.BlockSpec((N_DEV, R, C), lambda i: (0, 0, 0))],
        out_specs=pl.BlockSpec((R, C), lambda i: (0, 0)),
        out_shape=jax.ShapeDtypeStruct((R, C), F32),
        compiler_params=_cp(("arbitrary",)), name=name)(gathered)


def _place():
    return lax.axis_index("x"), lax.axis_index("y"), lax.axis_index("c")


def _other_chips(x, y):
    return [(1 - x, y), (x, 1 - y), (1 - x, 1 - y)]


_ANY = pl.BlockSpec(memory_space=pl.ANY)


_HBM = pl.BlockSpec(memory_space=pltpu.HBM)
_SEM = pl.BlockSpec(memory_space=pltpu.SEMAPHORE)
_EFFECT = pltpu.SideEffectType.DATAFLOW_SIDE_EFFECTING


def _gather_copies(srcs, lands, send_sems, recv_sems):
    x, y, c = _place()
    me = 2 * x + y
    return [pltpu.make_async_remote_copy(
        src_ref=srcs[i], dst_ref=lands[i].at[me], send_sem=send_sems.at[3 * i + j],
        recv_sem=recv_sems.at[3 * i + j], device_id=(px, py, c), device_id_type=MESH)
        for i in range(len(srcs)) for j, (px, py) in enumerate(_other_chips(x, y))]


def _scatter_copies(srcs, lands, send_sems, recv_sems):
    x, y, c = _place()
    return [pltpu.make_async_remote_copy(
        src_ref=srcs[i].at[2 * px + py], dst_ref=lands[i].at[j], send_sem=send_sems.at[3 * i + j],
        recv_sem=recv_sems.at[3 * i + j], device_id=(px, py, c), device_id_type=MESH)
        for i in range(len(srcs)) for j, (px, py) in enumerate(_other_chips(x, y))]


def _copies_start(srcs, lands, make_copies, *, name):
    n = len(srcs)

    def body(*refs):
        send_sems, recv_sems = refs[2 * n], refs[2 * n + 1]
        for cp in make_copies(refs[:n], refs[n:2 * n], send_sems, recv_sems):
            cp.start()
        refs[-1][...] = jnp.zeros_like(refs[-1])

    ops = list(srcs) + list(lands)
    outs = pl.pallas_call(
        body, name=name,
        out_shape=(pltpu.SemaphoreType.DMA((3 * n,)), pltpu.SemaphoreType.DMA((3 * n,)),
                   *[pltpu.HBM(a.shape, a.dtype) for a in ops], jax.ShapeDtypeStruct((8, 128), F32)),
        in_specs=[_HBM] * (2 * n),
        out_specs=(_SEM, _SEM, *[_HBM] * (2 * n), pl.BlockSpec(memory_space=pltpu.VMEM)),
        input_output_aliases={i: 2 + i for i in range(2 * n)},
        compiler_params=pltpu.CompilerParams(has_side_effects=_EFFECT),
    )(*[pltpu.with_memory_space_constraint(a, pltpu.HBM) for a in ops])
    return outs[0], outs[1], list(outs[2:2 + n]), list(outs[2 + n:2 + 2 * n]), outs[-1]


def _copies_wait(handle, after, make_copies, *, name):
    send_sems, recv_sems, srcs, lands, _ = handle
    n = len(srcs)

    def body(*refs):
        for cp in make_copies(refs[:n], refs[n:2 * n], refs[2 * n], refs[2 * n + 1]):
            cp.wait_send()
            cp.wait_recv()

    ops = list(srcs) + list(lands)
    outs = pl.pallas_call(
        body, name=name,
        out_shape=tuple(pltpu.HBM(a.shape, a.dtype) for a in ops),
        in_specs=[_HBM] * (2 * n) + [_SEM, _SEM, _ANY],
        out_specs=tuple([_HBM] * (2 * n)),
        input_output_aliases={i: i for i in range(2 * n)},
        compiler_params=pltpu.CompilerParams(has_side_effects=_EFFECT),
    )(*ops, send_sems, recv_sems, after)
    return list(outs[n:])


def _swap_with_sibling(arrs, *, name):
    n = len(arrs)

    def body(*refs):
        ins, outs = refs[:n], refs[n:2 * n]
        send_sems, recv_sems = refs[2 * n:]
        x, y, c = _place()
        remote = []
        for i in range(n):
            rc = pltpu.make_async_remote_copy(
                src_ref=ins[i], dst_ref=outs[i], send_sem=send_sems.at[i], recv_sem=recv_sems.at[i],
                device_id=(x, y, 1 - c), device_id_type=MESH)
            rc.start()
            remote.append(rc)
        for rc in remote:
            rc.wait_send()
        for rc in remote:
            rc.wait_recv()

    return pl.pallas_call(
        body, in_specs=[_ANY] * n, out_specs=[_ANY] * n,
        out_shape=[jax.ShapeDtypeStruct(a.shape, a.dtype) for a in arrs],
        scratch_shapes=[pltpu.SemaphoreType.DMA((n,)), pltpu.SemaphoreType.DMA((n,))],
        name=name)(*arrs)


def _gather_all(buf, *, name):
    R, C = buf.shape

    def body(in_ref, out_ref, send_sems, recv_sems, local_sem):
        x, y, c = _place()
        me = 4 * x + 2 * y + c
        lc = pltpu.make_async_copy(in_ref, out_ref.at[me], local_sem)
        lc.start()
        remote = []
        for k in range(1, N_DEV):
            px = 1 - x if (k >> 2) & 1 else x
            py = 1 - y if (k >> 1) & 1 else y
            pc = 1 - c if k & 1 else c
            rc = pltpu.make_async_remote_copy(
                src_ref=in_ref, dst_ref=out_ref.at[me], send_sem=send_sems.at[k - 1],
                recv_sem=recv_sems.at[k - 1], device_id=(px, py, pc), device_id_type=MESH)
            rc.start()
            remote.append(rc)
        lc.wait()
        for rc in remote:
            rc.wait_send()
        for rc in remote:
            rc.wait_recv()

    return pl.pallas_call(
        body, in_specs=[_ANY], out_specs=_ANY,
        out_shape=jax.ShapeDtypeStruct((N_DEV, R, C), buf.dtype),
        scratch_shapes=[pltpu.SemaphoreType.DMA((N_DEV - 1,)), pltpu.SemaphoreType.DMA((N_DEV - 1,)),
                        pltpu.SemaphoreType.DMA],
        name=name)(buf)


def _w_in_to_z(w):
    pad = jnp.zeros(w.shape[:-1] + (64,), w.dtype)
    return jnp.concatenate([w[..., 0:512], w[..., 512:1024], w[..., 1344:1856], w[..., 1024:1280],
                            w[..., 1280:1344], pad], axis=-1)


def _z_to_w_in(g):
    return jnp.concatenate([g[..., 0:512], g[..., 512:1024], g[..., 1536:1792], g[..., 1792:1856],
                            g[..., 1024:1536]], axis=-1)


def _pad_heads(w, nh):
    w = w.reshape(w.shape[:-1] + (nh, QK))
    w = jnp.concatenate([w, jnp.zeros(w.shape[:-1] + (HEAD_PAD - QK,), w.dtype)], axis=-1)
    return w.reshape(w.shape[:-2] + (nh * HEAD_PAD,))


def _unpad_heads(g, nh):
    g = g.reshape(g.shape[:-1] + (nh, HEAD_PAD))[..., :QK]
    return g.reshape(g.shape[:-2] + (nh * QK,))


def _kv_split(w, nh):
    w = w.reshape(w.shape[:-1] + (nh, 2, 128))
    return jnp.swapaxes(w, -3, -2).reshape(w.shape[:-3] + (nh * 256,))


def _kv_join(g, nh):
    g = g.reshape(g.shape[:-1] + (2, nh, 128))
    return jnp.swapaxes(g, -3, -2).reshape(g.shape[:-3] + (nh * 256,))


def _kv_perm(j):
    return (j % 2) * N_CHIPS + j // 2


def _pad_gain(g):
    return jnp.concatenate([g, jnp.zeros((1, HEAD_PAD - QK), g.dtype)], axis=1)


_SMALL = ("g_mix", "g_q_lat", "g_kv_lat", "g_q_mla", "g_k_mla", "w_pool", "pool_scale", "g_mem", "g_q_x",
          "g_k_x", "g_ffn", "conv_b", "conv_w")


def _pack(arrs, extra=0):
    flat = jnp.concatenate([a.reshape(-1) for a in arrs])
    n = flat.shape[0] + extra
    rows = -(-n // 1024) * 8
    return jnp.pad(flat, (0, rows * 128 - flat.shape[0])).reshape(rows, 128)


def _unpack(buf, shapes):
    flat = buf.reshape(-1)
    out, off = [], 0
    for s in shapes:
        n = int(np.prod(s))
        out.append(flat[off:off + n].reshape(s))
        off += n
    return out, off


def _tied(a, token):
    return a + token[:1, :1].astype(a.dtype)


def _local_step(x, mem, target, W, fetch=None, ship=None):
    fetch = fetch or (lambda group, after: None)
    ship = ship or (lambda group, G: jnp.zeros((8, 128), F32))
    S, D = x.shape
    F = W["conv_b"].shape[1]
    tabs = _rope_tables(S)
    tm = 512 if S % 512 == 0 else 128
    tk = _pick(S, (1024, 512, 128))

    h = _rms_fwd(x, W["g_mix"], C=D, name="norm_mix")
    fetch("g1", h)
    z = mm_nn(h, W["w_in"], tm=tm, tn=Z_COLS, tk=D, name="z_proj")
    fetch("g2", z)
    y_pool = _pool_fwd(z, W["w_pool"], W["pool_scale"], name="pool_fwd")
    ql = _rms_fwd(z, W["g_q_lat"], C=Q_RANK, cb=Z_Q_CB, name="norm_qlat")
    kvl = _rms_fwd(z, W["g_kv_lat"], C=KV_RANK, cb=Z_KV_CB, name="norm_kvlat")
    qraw = mm_nn(ql, W["w_q_up"], nsh=N_CHIPS, tm=tm, tn=512, tk=Q_RANK, name="q_up")
    kvraw = mm_nn(kvl, W["w_kv_up"], nsh=N_CHIPS, tm=tm, tn=256, tk=KV_RANK, perm=_kv_perm, name="kv_up")
    q, k, v, vt = _qkrope_fwd(qraw, kvraw, z, W["g_q_mla"], W["g_k_mla"], tabs, name="qk_norm_rope")
    o, y_mla, lse = _flash_fwd(q, k, vt, name="mla_fwd")
    memn = _rms_fwd(mem, W["g_mem"], C=D, name="norm_mem")
    M = mem.shape[0]
    mkv = mm_nn(memn, W["w_mem_kv"], tm=M, tn=1024, tk=D, name="mem_kv")
    kx, vx = _memk_fwd(mkv, W["g_k_x"], name="memk_fwd")
    y_mem = _memattn_fwd(z, kx, vx, W["g_q_x"], name="memattn_fwd")
    cat = jnp.concatenate([y_pool, y_mla, y_mem], axis=1)
    x2 = mm_nn(cat, W["w_o"], tm=tm, tn=D, tk=1024, add=x, name="o_proj")
    h2 = _rms_fwd(x2, W["g_ffn"], C=D, name="norm_ffn")
    fetch("g3", h2)
    fn = F // N_CHIPS
    g = mm_nn(h2, W["w_gate"], nsh=N_CHIPS, tm=tm, tn=fn, tk=D, name="gate_proj")
    u = mm_nn(h2, W["w_up"], nsh=N_CHIPS, tm=tm, tn=fn, tk=D, name="up_proj")
    a = _glu_fwd(g, u, W["conv_w"], W["conv_b"], name="glu_fwd")
    y = mm_nn(a, W["w_down"], tm=tm, tn=D, tk=fn, add=x2, name="down_proj")
    dy, dyb, loss_row = _loss_head(y, target, name="loss_head")

    G = {}
    d_a = mm_nt(dyb, W["w_down"], tm=tm, to=fn, tc=D, name="d_a")
    G["w_down"] = mm_tn(a, dyb, to=fn, tn=1024, tk=tk, out_dtypes=(F32, BF16), name="grad_w_down")
    d_g, d_u, G["conv_w"], G["conv_b"] = _glu_bwd(g, u, d_a, W["conv_w"], W["conv_b"], name="glu_bwd")
    G["w_gate"] = mm_tn(h2, d_g, nsh=N_CHIPS, to=1024, tn=fn, tk=tk, out_dtypes=(F32, BF16), name="grad_w_gate")
    G["w_up"] = mm_tn(h2, d_u, nsh=N_CHIPS, to=1024, tn=fn, tk=tk, out_dtypes=(F32, BF16), name="grad_w_up")
    d_h2 = mm_nt(d_g, W["w_gate"], nsh=N_CHIPS, tm=tm, to=D, tc=fn, name="d_h2_gate")
    tok = ship("s1", G)
    d_h2 = mm_nt(d_u, W["w_up"], nsh=N_CHIPS, tm=tm, to=D, tc=fn, add=d_h2, name="d_h2_up")
    d_x2, d_x2b, G["g_ffn"] = _rms_bwd(x2, d_h2, _tied(W["g_ffn"], tok), C=D, res=dy, out_dtypes=(F32, BF16),
                                       name="norm_ffn_bwd")

    d_cat = mm_nt(d_x2b, W["w_o"], tm=tm, to=D, tc=D, name="d_cat")
    G["w_o"] = mm_tn(cat, d_x2b, to=1024, tn=1024, tk=tk, out_dtypes=(F32, BF16), name="grad_w_o")
    tok = ship("s2", G)
    dz_pool, G["w_pool"], G["pool_scale"] = _pool_bwd(z, d_cat, W["w_pool"], _tied(W["pool_scale"], tok),
                                                      name="pool_bwd")
    dz_mq, dkx, dvx, G["g_q_x"] = _memattn_bwd(z, kx, vx, W["g_q_x"], d_cat, name="memattn_bwd")
    d_mkv, G["g_k_x"] = _memk_bwd(mkv, W["g_k_x"], dkx, dvx, name="memk_bwd")
    G["w_mem_kv"] = mm_tn(memn, d_mkv, to=1024, tn=1024, tk=M, out_dtypes=(F32, BF16), name="grad_w_mem_kv")
    d_memn = mm_nt(d_mkv, W["w_mem_kv"], tm=M, to=D, tc=1024, name="d_memn")
    _, G["g_mem"] = _rms_bwd(mem, d_memn, W["g_mem"], C=D, name="norm_mem_bwd")
    dob, delta = _attn_bwd_prep(o, d_cat, name="mla_bwd_prep")
    dq, dk, dv = _flash_bwd(q, k, v, dob, lse, delta, name="mla_bwd")
    d_qraw, d_kvraw, dz_kr, G["g_q_mla"], G["g_k_mla"] = _qkrope_bwd(
        qraw, kvraw, z, W["g_q_mla"], W["g_k_mla"], tabs, dq, dk, dv, name="qk_norm_rope_bwd")
    G["w_q_up"] = mm_tn(ql, d_qraw, nsh=N_CHIPS, to=Q_RANK, tn=512, tk=tk, out_dtypes=(F32, BF16), name="grad_w_q_up")
    d_ql = mm_nt(d_qraw, W["w_q_up"], nsh=N_CHIPS, tm=tm, to=Q_RANK, tc=512, name="d_ql")
    G["w_kv_up"] = mm_tn(kvl, d_kvraw, nsh=N_CHIPS, to=KV_RANK, tn=256, tk=tk, out_dtypes=(F32, BF16),
                         perm=_kv_perm, name="grad_w_kv_up")
    d_kvl = mm_nt(d_kvraw, W["w_kv_up"], nsh=N_CHIPS, tm=tm, to=KV_RANK, tc=256, perm=_kv_perm, name="d_kvl")
    dz_q, G["g_q_lat"] = _rms_bwd(z, d_ql, W["g_q_lat"], C=Q_RANK, cb=Z_Q_CB, out_dtypes=(BF16,), name="norm_qlat_bwd")
    dz_kv, G["g_kv_lat"] = _rms_bwd(z, d_kvl, W["g_kv_lat"], C=KV_RANK, cb=Z_KV_CB, out_dtypes=(BF16,),
                                    name="norm_kvlat_bwd")
    d_z = jnp.concatenate([dz_pool, dz_q, dz_mq, dz_kv, dz_kr], axis=1)
    G["w_in"] = mm_tn(h, d_z, to=512, tn=Z_COLS, tk=tk, out_dtypes=(F32, BF16), name="grad_w_in")
    tok = ship("s3", G)
    d_h = mm_nt(d_z, W["w_in"], tm=tm, to=D, tc=Z_COLS, name="d_h")
    grad_x, G["g_mix"] = _rms_bwd(x, d_h, _tied(W["g_mix"], tok), C=D, res=d_x2, name="norm_mix_bwd")
    return loss_row, grad_x, G


_BIG = ("w_in", "w_q_up", "w_kv_up", "w_mem_kv", "w_o", "w_gate", "w_up", "w_down")
_WEIGHTS = ("g_mix", "w_in", "g_q_lat", "w_q_up", "g_kv_lat", "w_kv_up", "g_q_mla", "g_k_mla", "w_pool",
            "pool_scale", "g_mem", "w_mem_kv", "g_q_x", "g_k_x", "w_o", "g_ffn", "w_gate", "w_up", "conv_w",
            "conv_b", "w_down")


def _to_compute_layout(name, w):
    if name == "w_in":
        return _w_in_to_z(w)
    if name == "w_q_up":
        return _pad_heads(w, w.shape[-1] // QK)
    if name == "w_kv_up":
        return _kv_split(w, w.shape[-1] // 256)
    return w


def _from_compute_layout(name, g):
    if name == "w_in":
        return _z_to_w_in(g)
    if name == "w_q_up":
        return _unpad_heads(g, g.shape[-1] // HEAD_PAD)
    if name == "w_kv_up":
        return _kv_join(g, g.shape[-1] // 256)
    return g


def kernel(x, mem, g_mix, w_in, g_q_lat, w_q_up, g_kv_lat, w_kv_up, g_q_mla, g_k_mla, w_pool, pool_scale, g_mem, w_mem_kv, g_q_x, g_k_x, w_o, g_ffn, w_gate, w_up, conv_w, conv_b, w_down, loss_target, m_g_mix, m_w_in, m_g_q_lat, m_w_q_up, m_g_kv_lat, m_w_kv_up, m_g_q_mla, m_g_k_mla, m_w_pool, m_pool_scale, m_g_mem, m_w_mem_kv, m_g_q_x, m_g_k_x, m_w_o, m_g_ffn, m_w_gate, m_w_up, m_conv_w, m_conv_b, m_w_down, v_g_mix, v_w_in, v_g_q_lat, v_w_q_up, v_g_kv_lat, v_w_kv_up, v_g_q_mla, v_g_k_mla, v_w_pool, v_pool_scale, v_g_mem, v_w_mem_kv, v_g_q_x, v_g_k_x, v_w_o, v_g_ffn, v_w_gate, v_w_up, v_conv_w, v_conv_b, v_w_down):
    P = dict(g_mix=g_mix, w_in=w_in, g_q_lat=g_q_lat, w_q_up=w_q_up, g_kv_lat=g_kv_lat, w_kv_up=w_kv_up,
             g_q_mla=g_q_mla, g_k_mla=g_k_mla, w_pool=w_pool, pool_scale=pool_scale, g_mem=g_mem,
             w_mem_kv=w_mem_kv, g_q_x=g_q_x, g_k_x=g_k_x, w_o=w_o, g_ffn=g_ffn, w_gate=w_gate, w_up=w_up,
             conv_w=conv_w, conv_b=conv_b, w_down=w_down)
    Mo = dict(g_mix=m_g_mix, w_in=m_w_in, g_q_lat=m_g_q_lat, w_q_up=m_w_q_up, g_kv_lat=m_g_kv_lat,
              w_kv_up=m_w_kv_up, g_q_mla=m_g_q_mla, g_k_mla=m_g_k_mla, w_pool=m_w_pool,
              pool_scale=m_pool_scale, g_mem=m_g_mem, w_mem_kv=m_w_mem_kv, g_q_x=m_g_q_x, g_k_x=m_g_k_x,
              w_o=m_w_o, g_ffn=m_g_ffn, w_gate=m_w_gate, w_up=m_w_up, conv_w=m_conv_w, conv_b=m_conv_b,
              w_down=m_w_down)
    Vo = dict(g_mix=v_g_mix, w_in=v_w_in, g_q_lat=v_g_q_lat, w_q_up=v_w_q_up, g_kv_lat=v_g_kv_lat,
              w_kv_up=v_w_kv_up, g_q_mla=v_g_q_mla, g_k_mla=v_g_k_mla, w_pool=v_w_pool,
              pool_scale=v_pool_scale, g_mem=v_g_mem, w_mem_kv=v_w_mem_kv, g_q_x=v_g_q_x, g_k_x=v_g_k_x,
              w_o=v_w_o, g_ffn=v_g_ffn, w_gate=v_w_gate, w_up=v_w_up, conv_w=v_conv_w, conv_b=v_conv_b,
              w_down=v_w_down)
    xi, yi, ci = _place()
    me = (2 * xi + yi).astype(jnp.int32).reshape(1)

    shard = {n: _to_compute_layout(n, P[n][0]).astype(BF16) for n in _BIG}
    shard["conv_w"] = conv_w[0]
    gather_groups = {"g1": ("w_in",), "g2": ("w_q_up", "w_kv_up", "w_mem_kv", "w_o"),
                     "g3": ("w_gate", "w_up", "w_down", "conv_w")}
    gathers = {}
    tok = jnp.zeros((1, 1), F32)
    for grp, names in gather_groups.items():
        lands = [lax.dynamic_update_slice(lax.empty((N_CHIPS,) + shard[n].shape, shard[n].dtype), shard[n][None],
                                          (me[0], 0, 0)) for n in names]
        gathers[grp] = _copies_start([shard[n] for n in names], lands, _gather_copies, name="gather_start_" + grp)
        tok = tok + gathers[grp][4][:1, :1]
    W = {}
    W["g_q_mla"], W["g_k_mla"] = _pad_gain(g_q_mla), _pad_gain(g_k_mla)
    W["w_pool"] = w_pool[0].astype(BF16)
    for n in ("g_mix", "g_q_lat", "g_kv_lat", "pool_scale", "g_mem", "g_q_x", "g_k_x", "g_ffn", "conv_b"):
        W[n] = P[n]
    W["g_mix"] = _tied(W["g_mix"], tok)

    def fetch(grp, after):
        stacks = _copies_wait(gathers[grp], after, _gather_copies, name="gather_wait_" + grp)
        for n, s in zip(gather_groups[grp], stacks):
            if n == "conv_w":
                W[n] = jnp.swapaxes(s, 0, 1).reshape(3, -1)
            else:
                W[n] = s.reshape(-1, s.shape[-1])

    shard_shape = {n: shard[n].shape for n in _BIG}
    scatter_groups = {"s1": ("w_down", "w_gate", "w_up"), "s2": ("w_o",),
                      "s3": ("w_mem_kv", "w_q_up", "w_kv_up", "w_in")}
    scatters = {}

    def ship(grp, G):
        names = scatter_groups[grp]
        srcs = [G[n][1].reshape((N_CHIPS,) + shard_shape[n]) for n in names]
        lands = [lax.empty((N_CHIPS - 1,) + shard_shape[n], BF16) for n in names]
        scatters[grp] = _copies_start(srcs, lands, _scatter_copies, name="scatter_start_" + grp)
        return scatters[grp][4]

    loss_row, grad_x, G = _local_step(x[0], mem[0], loss_target[0], W, fetch, ship)

    recv = {}
    for grp, names in scatter_groups.items():
        for n, r in zip(names, _copies_wait(scatters[grp], grad_x, _scatter_copies, name="scatter_wait_" + grp)):
            recv[n] = r
    part = [_sum4(G[n][0].reshape((N_CHIPS,) + shard_shape[n]), recv[n], me, name="sum4_" + n) for n in _BIG]
    part = [_from_compute_layout(n, p) for n, p in zip(_BIG, part)]
    sib = _swap_with_sibling(part, name="swap_grads")
    out = {}
    for n, p, s in zip(_BIG, part, sib):
        out[n] = [r[None] for r in _adamw(P[n][0], Mo[n][0], Vo[n][0], [p, s], name="adamw_" + n)]

    conv_w_full_grad = G["conv_w"]
    small_g = [G["g_mix"], G["g_q_lat"], G["g_kv_lat"], G["g_q_mla"][:, :QK], G["g_k_mla"][:, :QK], G["w_pool"],
               G["pool_scale"], G["g_mem"], G["g_q_x"], G["g_k_x"], G["g_ffn"], G["conv_b"], conv_w_full_grad]
    packed = _pack(small_g + [loss_row[:, :1]])
    total = _sum8(_gather_all(packed, name="gather_small"), name="sum_small")
    shapes = [a.shape for a in small_g] + [(1, 1)]
    (parts, _) = _unpack(total, shapes)
    loss = parts[-1].reshape(())
    F = conv_b.shape[1]
    fn = F // N_CHIPS
    col0 = (2 * xi + yi) * fn
    sg = dict(zip(_SMALL, parts[:-1]))
    sg["conv_w"] = lax.dynamic_slice(sg["conv_w"], (0, col0), (3, fn))
    sw = [P[n].reshape(sg[n].shape) for n in _SMALL]
    sm = [Mo[n].reshape(sg[n].shape) for n in _SMALL]
    sv = [Vo[n].reshape(sg[n].shape) for n in _SMALL]
    gp = _pack([sg[n] for n in _SMALL])
    res = _adamw(_pack(sw), _pack(sm), _pack(sv), [gp], name="adamw_small")
    sshapes = [sg[n].shape for n in _SMALL]
    for kind, buf in zip(range(4), res):
        vals, _ = _unpack(buf, sshapes)
        for n, val in zip(_SMALL, vals):
            out.setdefault(n, [None] * 4)[kind] = val.reshape(P[n].shape)

    return (loss, grad_x[None], *[out[n][0] for n in _WEIGHTS], *[out[n][1] for n in _WEIGHTS],
            *[out[n][2] for n in _WEIGHTS], *[out[n][3] for n in _WEIGHTS])
```

```python
import functools
import math

import numpy as np
import jax
import jax.numpy as jnp
from jax import lax
from jax.experimental import pallas as pl
from jax.experimental.pallas import tpu as pltpu

F32, BF16 = jnp.float32, jnp.bfloat16
NORM_EPS = 1e-6
ROPE_THETA = 10000.0
V7X_VMEM_LIMIT_BYTES = 48 * 1024 * 1024
N_CHIPS = 4
N_DEV = 8

POOL_W = 512
POOL_WINDOWS = (2, 4, 8, 16)
HEADS = 8
NOPE, ROPE, QK = 128, 64, 192
HEAD_PAD = 256
Q_RANK, KV_RANK = 512, 256
X_HEADS, X_DIM = 4, 128
Z_COLS = 1920
Z_POOL_CB, Z_Q_CB, Z_MQ_CB = 0, 1, 2
Z_KV_CB = 6
Z_KR_CB = 14

ADAM_LR, ADAM_B1, ADAM_B2, ADAM_EPS, ADAM_WD, ADAM_STEP = 0.001, 0.9, 0.999, 1e-08, 0.01, 10

MESH = pl.DeviceIdType.MESH


def _cp(sem):
    return pltpu.CompilerParams(dimension_semantics=sem, vmem_limit_bytes=V7X_VMEM_LIMIT_BYTES)


def _row_tile(S):
    return 256 if S % 256 == 0 and S >= 2048 else 128


def _pick(dim, prefs):
    for p in prefs:
        if dim % p == 0:
            return p
    return dim


_DN = {"nn": (((1,), (0,)), ((), ())), "nt": (((1,), (1,)), ((), ())), "tn": (((0,), (0,)), ((), ()))}


def _mm(a, b, *, mode, grid, blocks, maps, out_shape, out_dtypes, add=None, name):
    nk = grid[2]
    dn = _DN[mode]
    n_out = len(out_dtypes)

    def body(*refs):
        a_ref, b_ref = refs[0], refs[1]
        add_ref = refs[2] if add is not None else None
        p = 2 + (add is not None)
        o_refs = refs[p:p + n_out]

        def finish(r):
            if add_ref is not None:
                r = r + add_ref[...]
            for o in o_refs:
                o[...] = r.astype(o.dtype)

        prod = lax.dot_general(a_ref[...].astype(BF16), b_ref[...].astype(BF16), dn, preferred_element_type=F32)
        if nk == 1:
            finish(prod)
            return
        acc = refs[p + n_out]
        k = pl.program_id(2)

        @pl.when(k == 0)
        def _():
            acc[...] = prod

        @pl.when(k > 0)
        def _():
            acc[...] += prod

        @pl.when(k == nk - 1)
        def _():
            finish(acc[...])

    a_blk, b_blk, o_blk = blocks
    a_map, b_map, o_map = maps
    in_specs = [pl.BlockSpec(a_blk, a_map), pl.BlockSpec(b_blk, b_map)]
    args = [a, b]
    if add is not None:
        in_specs.append(pl.BlockSpec(o_blk, o_map))
        args.append(add)
    outs = pl.pallas_call(
        body, grid=grid, in_specs=in_specs,
        out_specs=[pl.BlockSpec(o_blk, o_map) for _ in out_dtypes],
        out_shape=[jax.ShapeDtypeStruct(out_shape, d) for d in out_dtypes],
        scratch_shapes=[pltpu.VMEM(o_blk, F32)] if nk > 1 else [],
        compiler_params=_cp(("parallel", "parallel", "arbitrary")), name=name)(*args)
    return outs[0] if n_out == 1 else outs


def mm_nn(a, w, *, nsh=1, tm, tn, tk, out_dtypes=(F32,), add=None, name):
    M, K = a.shape
    n = w.shape[1]
    N = nsh * n
    assert w.shape[0] == nsh * K and n % tn == 0 and K % tk == 0 and M % tm == 0
    npt, kt = n // tn, K // tk
    return _mm(a, w, mode="nn", grid=(M // tm, N // tn, kt),
               blocks=((tm, tk), (tk, tn), (tm, tn)),
               maps=(lambda i, j, k: (i, k), lambda i, j, k: ((j // npt) * kt + k, j % npt),
                     lambda i, j, k: (i, j)),
               out_shape=(M, N), out_dtypes=out_dtypes, add=add, name=name)


def mm_nt(d, w, *, nsh=1, tm, to, tc, out_dtypes=(F32,), add=None, name):
    M, N = d.shape
    n = w.shape[1]
    K = w.shape[0] // nsh
    assert nsh * n == N and n % tc == 0 and K % to == 0 and M % tm == 0
    cpt, ot = n // tc, K // to
    return _mm(d, w, mode="nt", grid=(M // tm, ot, N // tc),
               blocks=((tm, tc), (to, tc), (tm, to)),
               maps=(lambda i, j, c: (i, c), lambda i, j, c: ((c // cpt) * ot + j, c % cpt),
                     lambda i, j, c: (i, j)),
               out_shape=(M, K), out_dtypes=out_dtypes, add=add, name=name)


def mm_tn(x, d, *, nsh=1, to, tn, tk, out_dtypes=(F32,), name):
    M, K = x.shape
    N = d.shape[1]
    n = N // nsh
    assert n % tn == 0 and K % to == 0 and M % tk == 0
    npt, ot = n // tn, K // to
    return _mm(x, d, mode="tn", grid=(ot, N // tn, M // tk),
               blocks=((tk, to), (tk, tn), (to, tn)),
               maps=(lambda i, j, k: (k, i), lambda i, j, k: (k, j),
                     lambda i, j, k: ((j // npt) * ot + i, j % npt)),
               out_shape=(nsh * K, n), out_dtypes=out_dtypes, name=name)


def _rms_fwd(x, g, *, C, cb=0, name):
    S = x.shape[0]
    tm = _row_tile(S) if S >= 128 else S

    def body(x_ref, g_ref, o_ref):
        xv = x_ref[...]
        r = lax.rsqrt(jnp.mean(xv * xv, axis=-1, keepdims=True) + NORM_EPS)
        o_ref[...] = ((xv * r) * g_ref[...]).astype(o_ref.dtype)

    return pl.pallas_call(
        body, grid=(S // tm,),
        in_specs=[pl.BlockSpec((tm, C), lambda i: (i, cb)), pl.BlockSpec((1, C), lambda i: (0, 0))],
        out_specs=pl.BlockSpec((tm, C), lambda i: (i, 0)),
        out_shape=jax.ShapeDtypeStruct((S, C), BF16),
        compiler_params=_cp(("parallel",)), name=name)(x, g)


def _rms_bwd(x, dh, g, *, C, cb=0, res=None, out_dtypes=(F32,), name):
    S = x.shape[0]
    tm = _row_tile(S) if S >= 128 else S
    n_out = len(out_dtypes)

    def body(*refs):
        x_ref, dh_ref, g_ref = refs[:3]
        res_ref = refs[3] if res is not None else None
        p = 3 + (res is not None)
        outs = refs[p:p + n_out]
        dg_ref = refs[p + n_out]
        i = pl.program_id(0)
        xv = x_ref[...]
        r = lax.rsqrt(jnp.mean(xv * xv, axis=-1, keepdims=True) + NORM_EPS)
        n = xv * r
        dhv = dh_ref[...].astype(F32)
        dn = dhv * g_ref[...]
        c = jnp.mean(dn * n, axis=-1, keepdims=True)
        dx = r * (dn - n * c)
        if res_ref is not None:
            dx = res_ref[...] + dx
        for o in outs:
            o[...] = dx.astype(o.dtype)

        @pl.when(i == 0)
        def _():
            dg_ref[...] = jnp.zeros_like(dg_ref)

        dg_ref[...] += jnp.sum(dhv * n, axis=0, keepdims=True)

    row = pl.BlockSpec((tm, C), lambda i: (i, 0))
    in_specs = [pl.BlockSpec((tm, C), lambda i: (i, cb)), row, pl.BlockSpec((1, C), lambda i: (0, 0))]
    args = [x, dh, g]
    if res is not None:
        in_specs.append(row)
        args.append(res)
    return pl.pallas_call(
        body, grid=(S // tm,), in_specs=in_specs,
        out_specs=[row] * n_out + [pl.BlockSpec((1, C), lambda i: (0, 0))],
        out_shape=[jax.ShapeDtypeStruct((S, C), d) for d in out_dtypes] + [jax.ShapeDtypeStruct((1, C), F32)],
        compiler_params=_cp(("arbitrary",)), name=name)(*args)


def _pool_cnt(t0, rows, w):
    t = t0 + lax.broadcasted_iota(jnp.int32, (rows, 1), 0)
    return jnp.minimum(t + 1, w).astype(F32)


def _pool_d(halo, tile, gi, t0, tm):
    s = jnp.concatenate([halo, tile], axis=0)
    for step in (1, 2, 4, 8)[:gi + 1]:
        s = s + pltpu.roll(s, step, 0)
    return s[16:] / _pool_cnt(t0, tm, POOL_WINDOWS[gi]) - tile


def _pool_fwd(z, w_pool, pool_scale, *, name):
    S = z.shape[0]
    tm = _row_tile(S)
    hb = tm // 16

    def body(z_ref, h_ref, w_ref, sc_ref, o_ref):
        i = pl.program_id(0)
        halo = h_ref[...] * (i > 0).astype(F32)
        for gi in range(4):
            cs = slice(gi * 128, (gi + 1) * 128)
            d = _pool_d(halo[:, cs], z_ref[:, cs], gi, i * tm, tm)
            yp = jnp.dot(d.astype(BF16), w_ref[gi], preferred_element_type=F32)
            o_ref[:, cs] = (yp * sc_ref[:, cs]).astype(o_ref.dtype)

    return pl.pallas_call(
        body, grid=(S // tm,),
        in_specs=[pl.BlockSpec((tm, POOL_W), lambda i: (i, Z_POOL_CB)),
                  pl.BlockSpec((16, POOL_W), lambda i: (jnp.maximum(i * hb - 1, 0), Z_POOL_CB)),
                  pl.BlockSpec((4, 128, 128), lambda i: (0, 0, 0)),
                  pl.BlockSpec((1, POOL_W), lambda i: (0, 0))],
        out_specs=pl.BlockSpec((tm, POOL_W), lambda i: (i, 0)),
        out_shape=jax.ShapeDtypeStruct((S, POOL_W), BF16),
        compiler_params=_cp(("parallel",)), name=name)(z, z, w_pool, pool_scale)


def _pool_bwd(z, d_cat, w_pool, pool_scale, *, name):
    S = z.shape[0]
    tm = _row_tile(S)
    hb = tm // 16
    nt = S // tm
    E = tm + 16

    def body(z_ref, h_ref, dy_ref, dyn_ref, w_ref, sc_ref, dz_ref, gw_ref, gs_ref):
        i = pl.program_id(0)

        @pl.when(i == 0)
        def _():
            gw_ref[...] = jnp.zeros_like(gw_ref)
            gs_ref[...] = jnp.zeros_like(gs_ref)

        halo = h_ref[...] * (i > 0).astype(F32)
        dy_next = dyn_ref[...] * (i < nt - 1).astype(F32)
        for gi in range(4):
            cs = slice(gi * 128, (gi + 1) * 128)
            w = w_ref[gi]
            d = _pool_d(halo[:, cs], z_ref[:, cs], gi, i * tm, tm)
            db = d.astype(BF16)
            dy = dy_ref[:, cs]
            yp = jnp.dot(db, w, preferred_element_type=F32)
            gs_ref[:, cs] += jnp.sum(dy * yp, axis=0, keepdims=True)
            sc = sc_ref[:, cs]
            dys = (dy * sc).astype(BF16)
            gw_ref[gi] += lax.dot_general(db, dys, _DN["tn"], preferred_element_type=F32)
            dys_ext = jnp.concatenate([dys, (dy_next[:, cs] * sc).astype(BF16)], axis=0)
            dd = lax.dot_general(dys_ext, w, _DN["nt"], preferred_element_type=F32)
            r = dd / _pool_cnt(i * tm, E, POOL_WINDOWS[gi])
            for step in (1, 2, 4, 8)[:gi + 1]:
                r = r + pltpu.roll(r, E - step, 0)
            dz_ref[:, cs] = (r[:tm] - dd[:tm]).astype(dz_ref.dtype)

    return pl.pallas_call(
        body, grid=(nt,),
        in_specs=[pl.BlockSpec((tm, POOL_W), lambda i: (i, Z_POOL_CB)),
                  pl.BlockSpec((16, POOL_W), lambda i: (jnp.maximum(i * hb - 1, 0), Z_POOL_CB)),
                  pl.BlockSpec((tm, POOL_W), lambda i: (i, 0)),
                  pl.BlockSpec((16, POOL_W), lambda i: (jnp.minimum((i + 1) * hb, S // 16 - 1), 0)),
                  pl.BlockSpec((4, 128, 128), lambda i: (0, 0, 0)),
                  pl.BlockSpec((1, POOL_W), lambda i: (0, 0))],
        out_specs=[pl.BlockSpec((tm, POOL_W), lambda i: (i, 0)),
                   pl.BlockSpec((4, 128, 128), lambda i: (0, 0, 0)),
                   pl.BlockSpec((1, POOL_W), lambda i: (0, 0))],
        out_shape=[jax.ShapeDtypeStruct((S, POOL_W), BF16),
                   jax.ShapeDtypeStruct((4, 128, 128), F32),
                   jax.ShapeDtypeStruct((1, POOL_W), F32)],
        compiler_params=_cp(("arbitrary",)), name=name)(z, z, d_cat, d_cat, w_pool, pool_scale)


def _rope_tables(S):
    half = ROPE // 2
    inv_freq = 1.0 / (ROPE_THETA ** (jnp.arange(half, dtype=F32) / half))
    ang = jnp.arange(S).astype(F32)[:, None] * inv_freq[None, :]
    cos, sin = jnp.cos(ang), jnp.sin(ang)
    zero = jnp.zeros((S, half), F32)
    cos_t = jnp.concatenate([cos, cos, zero, zero], axis=1)
    sa_t = jnp.concatenate([-sin, zero, zero, zero], axis=1)
    sb_t = jnp.concatenate([zero, sin, zero, zero], axis=1)
    return cos_t, sa_t, sb_t


def _head_fwd(xn, xr, gn, gr, cos, sa, sb):
    ms = (jnp.sum(xn * xn, axis=-1, keepdims=True) + jnp.sum(xr * xr, axis=-1, keepdims=True)) * (1.0 / QK)
    r = lax.rsqrt(ms + NORM_EPS)
    on = (xn * r) * gn
    yr = (xr * r) * gr
    orr = yr * cos + pltpu.roll(yr, 96, 1) * sa + pltpu.roll(yr, 32, 1) * sb
    return on, orr


def _head_bwd(xn, xr, gn, gr, don, dor, cos, sa, sb):
    ms = (jnp.sum(xn * xn, axis=-1, keepdims=True) + jnp.sum(xr * xr, axis=-1, keepdims=True)) * (1.0 / QK)
    r = lax.rsqrt(ms + NORM_EPS)
    nn, nr = xn * r, xr * r
    dyr = dor * cos + pltpu.roll(dor * sa, 32, 1) + pltpu.roll(dor * sb, 96, 1)
    ggn, ggr = don * nn, dyr * nr
    dnn, dnr = don * gn, dyr * gr
    c = (jnp.sum(dnn * nn, axis=-1, keepdims=True) + jnp.sum(dnr * nr, axis=-1, keepdims=True)) * (1.0 / QK)
    return r * (dnn - nn * c), r * (dnr - nr * c), ggn, ggr


def _kv_cols(h):
    base = (h // 2) * 512 + (h % 2) * 128
    return base, base + 256


def _qkrope_fwd(qraw, kvraw, z, gq, gk, tabs, *, name):
    S = qraw.shape[0]
    tm = _row_tile(S)

    def body(q_ref, kv_ref, zkr_ref, gq_ref, gk_ref, cos_ref, sa_ref, sb_ref, qo_ref, ko_ref, vo_ref, vt_ref):
        cos, sa, sb = cos_ref[...], sa_ref[...], sb_ref[...]
        zkr = zkr_ref[...]
        gqn, gqr, gkn, gkr = gq_ref[:, :128], gq_ref[:, 128:], gk_ref[:, :128], gk_ref[:, 128:]
        for h in range(HEADS):
            b = h * HEAD_PAD
            on, orr = _head_fwd(q_ref[:, b:b + 128], q_ref[:, b + 128:b + 256], gqn, gqr, cos, sa, sb)
            qo_ref[:, b:b + 128] = on.astype(BF16)
            qo_ref[:, b + 128:b + 256] = orr.astype(BF16)
            kc, vc = _kv_cols(h)
            on, orr = _head_fwd(kv_ref[:, kc:kc + 128], zkr, gkn, gkr, cos, sa, sb)
            ko_ref[:, b:b + 128] = on.astype(BF16)
            ko_ref[:, b + 128:b + 256] = orr.astype(BF16)
            vv = kv_ref[:, vc:vc + 128]
            vo_ref[:, h * 128:(h + 1) * 128] = vv.astype(BF16)
            vt_ref[h * 128:(h + 1) * 128, :] = jnp.transpose(vv).astype(BF16)

    W = HEADS * HEAD_PAD
    row = lambda c: pl.BlockSpec((tm, c), lambda i: (i, 0))
    vec = lambda c: pl.BlockSpec((1, c), lambda i: (0, 0))
    return pl.pallas_call(
        body, grid=(S // tm,),
        in_specs=[row(W), row(W), pl.BlockSpec((tm, 128), lambda i: (i, Z_KR_CB)), vec(256), vec(256),
                  row(128), row(128), row(128)],
        out_specs=[row(W), row(W), row(HEADS * 128), pl.BlockSpec((HEADS * 128, tm), lambda i: (0, i))],
        out_shape=[jax.ShapeDtypeStruct((S, W), BF16), jax.ShapeDtypeStruct((S, W), BF16),
                   jax.ShapeDtypeStruct((S, HEADS * 128), BF16), jax.ShapeDtypeStruct((HEADS * 128, S), BF16)],
        compiler_params=_cp(("parallel",)), name=name)(qraw, kvraw, z, gq, gk, *tabs)


def _qkrope_bwd(qraw, kvraw, z, gq, gk, tabs, dq, dk, dv, *, name):
    S = qraw.shape[0]
    tm = _row_tile(S)

    def body(q_ref, kv_ref, zkr_ref, gq_ref, gk_ref, cos_ref, sa_ref, sb_ref, dq_ref, dk_ref, dv_ref,
             dqo_ref, dkvo_ref, dkr_ref, ggq_ref, ggk_ref):
        i = pl.program_id(0)

        @pl.when(i == 0)
        def _():
            ggq_ref[...] = jnp.zeros_like(ggq_ref)
            ggk_ref[...] = jnp.zeros_like(ggk_ref)

        cos, sa, sb = cos_ref[...], sa_ref[...], sb_ref[...]
        zkr = zkr_ref[...]
        gqn, gqr, gkn, gkr = gq_ref[:, :128], gq_ref[:, 128:], gk_ref[:, :128], gk_ref[:, 128:]
        dkr = jnp.zeros((tm, 128), F32)
        sq_n = jnp.zeros((1, 128), F32)
        sq_r = jnp.zeros((1, 128), F32)
        sk_n = jnp.zeros((1, 128), F32)
        sk_r = jnp.zeros((1, 128), F32)
        for h in range(HEADS):
            b = h * HEAD_PAD
            dxn, dxr, ggn, ggr = _head_bwd(q_ref[:, b:b + 128], q_ref[:, b + 128:b + 256], gqn, gqr,
                                           dq_ref[:, b:b + 128], dq_ref[:, b + 128:b + 256], cos, sa, sb)
            dqo_ref[:, b:b + 128] = dxn.astype(BF16)
            dqo_ref[:, b + 128:b + 256] = dxr.astype(BF16)
            sq_n += jnp.sum(ggn, axis=0, keepdims=True)
            sq_r += jnp.sum(ggr, axis=0, keepdims=True)
            kc, vc = _kv_cols(h)
            dxn, dxr, ggn, ggr = _head_bwd(kv_ref[:, kc:kc + 128], zkr, gkn, gkr,
                                           dk_ref[:, b:b + 128], dk_ref[:, b + 128:b + 256], cos, sa, sb)
            dkvo_ref[:, kc:kc + 128] = dxn.astype(BF16)
            dkvo_ref[:, vc:vc + 128] = dv_ref[:, h * 128:(h + 1) * 128].astype(BF16)
            dkr += dxr
            sk_n += jnp.sum(ggn, axis=0, keepdims=True)
            sk_r += jnp.sum(ggr, axis=0, keepdims=True)
        dkr_ref[...] = dkr.astype(BF16)
        ggq_ref[:, :128] += sq_n
        ggq_ref[:, 128:] += sq_r
        ggk_ref[:, :128] += sk_n
        ggk_ref[:, 128:] += sk_r

    W = HEADS * HEAD_PAD
    row = lambda c: pl.BlockSpec((tm, c), lambda i: (i, 0))
    vec = lambda c: pl.BlockSpec((1, c), lambda i: (0, 0))
    return pl.pallas_call(
        body, grid=(S // tm,),
        in_specs=[row(W), row(W), pl.BlockSpec((tm, 128), lambda i: (i, Z_KR_CB)), vec(256), vec(256),
                  row(128), row(128), row(128), row(W), row(W), row(HEADS * 128)],
        out_specs=[row(W), row(W), row(128), vec(256), vec(256)],
        out_shape=[jax.ShapeDtypeStruct((S, W), BF16), jax.ShapeDtypeStruct((S, W), BF16),
                   jax.ShapeDtypeStruct((S, 128), BF16),
                   jax.ShapeDtypeStruct((1, 256), F32), jax.ShapeDtypeStruct((1, 256), F32)],
        compiler_params=_cp(("arbitrary",)), name=name)(qraw, kvraw, z, gq, gk, *tabs, dq, dk, dv)


LOG2E = 1.4426950408889634
SCORE_SCALE = 1.0 / math.sqrt(QK)
SCORE_SCALE_LOG2 = SCORE_SCALE * LOG2E


def _fa_tile(S):
    return 512 if S % 512 == 0 and S >= 2048 else 128


def _flash_fwd(q, k, vt, *, name):
    S = q.shape[0]
    ts = _fa_tile(S)

    def body(q_ref, k_ref, vt_ref, o_ref, ob_ref, lse_ref, m_sc, l_sc, acc_sc):
        qi = pl.program_id(1)
        m_sc[...] = jnp.full_like(m_sc, -jnp.inf)
        l_sc[...] = jnp.zeros_like(l_sc)
        acc_sc[...] = jnp.zeros_like(acc_sc)
        qb = q_ref[...]

        def scores(kidx, masked):
            k0 = pl.multiple_of(kidx * ts, ts)
            st = lax.dot_general(k_ref[pl.ds(k0, ts), :], qb, _DN["nt"], preferred_element_type=F32) * SCORE_SCALE_LOG2
            if masked:
                st = jnp.where(lax.broadcasted_iota(jnp.int32, (ts, ts), 0) > lax.broadcasted_iota(jnp.int32, (ts, ts), 1),
                               -jnp.inf, st)
            return st

        def update(st, kidx):
            k0 = pl.multiple_of(kidx * ts, ts)
            m_prev = m_sc[...]
            m_new = jnp.maximum(m_prev, jnp.max(st, axis=0, keepdims=True))
            alpha = jnp.exp2(m_prev - m_new)
            pt = jnp.exp2(st - m_new[0:1, :])
            l_sc[...] = alpha * l_sc[...] + jnp.sum(pt, axis=0, keepdims=True)
            acc_sc[...] = alpha[0:1, :] * acc_sc[...] + jnp.dot(vt_ref[:, pl.ds(k0, ts)], pt.astype(BF16),
                                                                preferred_element_type=F32)
            m_sc[...] = m_new

        def pair(t, carry):
            sa, sb = scores(2 * t, False), scores(2 * t + 1, False)
            update(sa, 2 * t)
            update(sb, 2 * t + 1)
            return carry

        lax.fori_loop(0, qi // 2, pair, 0)

        @pl.when(qi % 2 == 1)
        def _():
            update(scores(qi - 1, False), qi - 1)

        update(scores(qi, True), qi)
        ot = acc_sc[...] / l_sc[0:1, :]
        o = jnp.transpose(ot)
        o_ref[...] = o
        ob_ref[...] = o.astype(BF16)
        lse_ref[...] = m_sc[...] + jnp.log2(l_sc[...])

    return pl.pallas_call(
        body, grid=(HEADS, S // ts),
        in_specs=[pl.BlockSpec((ts, HEAD_PAD), lambda h, i: (i, h)),
                  pl.BlockSpec((S, HEAD_PAD), lambda h, i: (0, h)),
                  pl.BlockSpec((128, S), lambda h, i: (h, 0))],
        out_specs=[pl.BlockSpec((ts, 128), lambda h, i: (i, h)),
                   pl.BlockSpec((ts, 128), lambda h, i: (i, h)),
                   pl.BlockSpec((None, 8, ts), lambda h, i: (h, 0, i))],
        out_shape=[jax.ShapeDtypeStruct((S, HEADS * 128), F32), jax.ShapeDtypeStruct((S, HEADS * 128), BF16),
                   jax.ShapeDtypeStruct((HEADS, 8, S), F32)],
        scratch_shapes=[pltpu.VMEM((8, ts), F32), pltpu.VMEM((8, ts), F32), pltpu.VMEM((128, ts), F32)],
        compiler_params=_cp(("parallel", "arbitrary")), name=name)(q, k, vt)


def _attn_bwd_prep(o, d_cat, *, name):
    S = o.shape[0]
    tm = _row_tile(S)
    H = HEADS * 128
    half = H // 2

    def body(o_ref, da_ref, db_ref, dob_ref, delta_ref):
        for h in range(HEADS):
            src, c0 = (da_ref, h * 128) if h * 128 < half else (db_ref, h * 128 - half)
            do = src[:, c0:c0 + 128]
            dob_ref[:, h * 128:(h + 1) * 128] = do.astype(BF16)
            prod = jnp.transpose(do * o_ref[:, h * 128:(h + 1) * 128])
            delta_ref[h] = jnp.broadcast_to(jnp.sum(prod, axis=0, keepdims=True), (8, tm))

    return pl.pallas_call(
        body, grid=(S // tm,),
        in_specs=[pl.BlockSpec((tm, H), lambda i: (i, 0)),
                  pl.BlockSpec((tm, half), lambda i: (i, 1)), pl.BlockSpec((tm, half), lambda i: (i, 2))],
        out_specs=[pl.BlockSpec((tm, H), lambda i: (i, 0)), pl.BlockSpec((HEADS, 8, tm), lambda i: (0, 0, i))],
        out_shape=[jax.ShapeDtypeStruct((S, H), BF16), jax.ShapeDtypeStruct((HEADS, 8, S), F32)],
        compiler_params=_cp(("parallel",)), name=name)(o, d_cat, d_cat)


def _flash_bwd(q, k, v, dob, lse, delta, *, name):
    S = q.shape[0]
    ts = _fa_tile(S)
    nb = S // ts

    def body(q_ref, do_ref, lse_ref, delta_ref, k_ref, v_ref, dq_ref, dk_ref, dv_ref, dk_sc, dv_sc):
        j = pl.program_id(1)

        @pl.when(j == 0)
        def _():
            dq_ref[...] = jnp.zeros_like(dq_ref)

        dk_sc[...] = jnp.zeros_like(dk_sc)
        dv_sc[...] = jnp.zeros_like(dv_sc)
        kb, vb = k_ref[...], v_ref[...]

        def step(i, masked):
            q0 = pl.multiple_of(i * ts, ts)
            qb = q_ref[pl.ds(q0, ts), :]
            dob_ = do_ref[pl.ds(q0, ts), :]
            st = lax.dot_general(kb, qb, _DN["nt"], preferred_element_type=F32) * SCORE_SCALE_LOG2
            pt = jnp.exp2(st - lse_ref[0:1, pl.ds(q0, ts)])
            if masked:
                pt = jnp.where(lax.broadcasted_iota(jnp.int32, (ts, ts), 0) > lax.broadcasted_iota(jnp.int32, (ts, ts), 1),
                               0.0, pt)
            dv_sc[...] += jnp.dot(pt.astype(BF16), dob_, preferred_element_type=F32)
            dpt = lax.dot_general(vb, dob_, _DN["nt"], preferred_element_type=F32)
            dst = (pt * (dpt - delta_ref[0:1, pl.ds(q0, ts)])).astype(BF16)
            dk_sc[...] += jnp.dot(dst, qb, preferred_element_type=F32) * SCORE_SCALE
            dq_ref[pl.ds(q0, ts), :] += lax.dot_general(dst, kb, _DN["tn"], preferred_element_type=F32) * SCORE_SCALE

        step(j, True)

        def below(i, carry):
            step(i, False)
            return carry

        lax.fori_loop(j + 1, nb, below, 0)
        dk_ref[...] = dk_sc[...]
        dv_ref[...] = dv_sc[...]

    return pl.pallas_call(
        body, grid=(HEADS, nb),
        in_specs=[pl.BlockSpec((S, HEAD_PAD), lambda h, j: (0, h)),
                  pl.BlockSpec((S, 128), lambda h, j: (0, h)),
                  pl.BlockSpec((None, 8, S), lambda h, j: (h, 0, 0)),
                  pl.BlockSpec((None, 8, S), lambda h, j: (h, 0, 0)),
                  pl.BlockSpec((ts, HEAD_PAD), lambda h, j: (j, h)),
                  pl.BlockSpec((ts, 128), lambda h, j: (j, h))],
        out_specs=[pl.BlockSpec((S, HEAD_PAD), lambda h, j: (0, h)),
                   pl.BlockSpec((ts, HEAD_PAD), lambda h, j: (j, h)),
                   pl.BlockSpec((ts, 128), lambda h, j: (j, h))],
        out_shape=[jax.ShapeDtypeStruct((S, HEADS * HEAD_PAD), F32), jax.ShapeDtypeStruct((S, HEADS * HEAD_PAD), F32),
                   jax.ShapeDtypeStruct((S, HEADS * 128), F32)],
        scratch_shapes=[pltpu.VMEM((ts, HEAD_PAD), F32), pltpu.VMEM((ts, 128), F32)],
        compiler_params=_cp(("parallel", "arbitrary")), name=name)(q, dob, lse, delta, k, v)


def _memk_fwd(mkv, gkx, *, name):
    M = mkv.shape[0]
    XW = X_HEADS * X_DIM

    def body(mkv_ref, g_ref, k_ref, v_ref):
        for h in range(X_HEADS):
            cs = slice(h * X_DIM, (h + 1) * X_DIM)
            xv = mkv_ref[:, cs]
            r = lax.rsqrt(jnp.mean(xv * xv, axis=-1, keepdims=True) + NORM_EPS)
            k_ref[:, cs] = ((xv * r) * g_ref[...]).astype(BF16)
        v_ref[...] = mkv_ref[:, XW:].astype(BF16)

    return pl.pallas_call(
        body, grid=(1,),
        in_specs=[pl.BlockSpec((M, 2 * XW), lambda i: (0, 0)), pl.BlockSpec((1, X_DIM), lambda i: (0, 0))],
        out_specs=[pl.BlockSpec((M, XW), lambda i: (0, 0)), pl.BlockSpec((M, XW), lambda i: (0, 0))],
        out_shape=[jax.ShapeDtypeStruct((M, XW), BF16), jax.ShapeDtypeStruct((M, XW), BF16)],
        compiler_params=_cp(("arbitrary",)), name=name)(mkv, gkx)


def _memk_bwd(mkv, gkx, dk, dv, *, name):
    M = mkv.shape[0]
    XW = X_HEADS * X_DIM

    def body(mkv_ref, g_ref, dk_ref, dv_ref, o_ref, gg_ref):
        gg = jnp.zeros((1, X_DIM), F32)
        for h in range(X_HEADS):
            cs = slice(h * X_DIM, (h + 1) * X_DIM)
            xv = mkv_ref[:, cs]
            r = lax.rsqrt(jnp.mean(xv * xv, axis=-1, keepdims=True) + NORM_EPS)
            n = xv * r
            dkv = dk_ref[:, cs]
            gg += jnp.sum(dkv * n, axis=0, keepdims=True)
            dn = dkv * g_ref[...]
            c = jnp.mean(dn * n, axis=-1, keepdims=True)
            o_ref[:, cs] = (r * (dn - n * c)).astype(BF16)
        o_ref[:, XW:] = dv_ref[...].astype(BF16)
        gg_ref[...] = gg

    full = lambda c: pl.BlockSpec((M, c), lambda i: (0, 0))
    return pl.pallas_call(
        body, grid=(1,),
        in_specs=[full(2 * XW), pl.BlockSpec((1, X_DIM), lambda i: (0, 0)), full(XW), full(XW)],
        out_specs=[full(2 * XW), pl.BlockSpec((1, X_DIM), lambda i: (0, 0))],
        out_shape=[jax.ShapeDtypeStruct((M, 2 * XW), BF16), jax.ShapeDtypeStruct((1, X_DIM), F32)],
        compiler_params=_cp(("arbitrary",)), name=name)(mkv, gkx, dk, dv)


def _xq_norm(z_ref, g_ref, h):
    xv = z_ref[:, h * X_DIM:(h + 1) * X_DIM]
    r = lax.rsqrt(jnp.mean(xv * xv, axis=-1, keepdims=True) + NORM_EPS)
    n = xv * r
    return n, r, n * g_ref[...]


def _xprobs(qb, k_ref, h):
    s = lax.dot_general(qb, k_ref[:, h * X_DIM:(h + 1) * X_DIM], _DN["nt"],
                        preferred_element_type=F32) * (1.0 / math.sqrt(X_DIM))
    e = jnp.exp(s - jnp.max(s, axis=-1, keepdims=True))
    return e / jnp.sum(e, axis=-1, keepdims=True)


def _memattn_fwd(z, kx, vx, gqx, *, name):
    S = z.shape[0]
    M = kx.shape[0]
    tm = _row_tile(S)
    XW = X_HEADS * X_DIM

    def body(z_ref, k_ref, v_ref, g_ref, o_ref):
        for h in range(X_HEADS):
            cs = slice(h * X_DIM, (h + 1) * X_DIM)
            _, _, qn = _xq_norm(z_ref, g_ref, h)
            p = _xprobs(qn.astype(BF16), k_ref, h)
            o_ref[:, cs] = jnp.dot(p.astype(BF16), v_ref[:, cs], preferred_element_type=F32).astype(BF16)

    return pl.pallas_call(
        body, grid=(S // tm,),
        in_specs=[pl.BlockSpec((tm, XW), lambda i: (i, Z_MQ_CB)), pl.BlockSpec((M, XW), lambda i: (0, 0)),
                  pl.BlockSpec((M, XW), lambda i: (0, 0)), pl.BlockSpec((1, X_DIM), lambda i: (0, 0))],
        out_specs=pl.BlockSpec((tm, XW), lambda i: (i, 0)),
        out_shape=jax.ShapeDtypeStruct((S, XW), BF16),
        compiler_params=_cp(("parallel",)), name=name)(z, kx, vx, gqx)


def _memattn_bwd(z, kx, vx, gqx, d_cat, *, name):
    S = z.shape[0]
    M = kx.shape[0]
    tm = _row_tile(S)
    XW = X_HEADS * X_DIM
    scale = 1.0 / math.sqrt(X_DIM)

    def body(z_ref, k_ref, v_ref, g_ref, do_ref, dz_ref, dk_ref, dv_ref, gg_ref):
        i = pl.program_id(0)

        @pl.when(i == 0)
        def _():
            dk_ref[...] = jnp.zeros_like(dk_ref)
            dv_ref[...] = jnp.zeros_like(dv_ref)
            gg_ref[...] = jnp.zeros_like(gg_ref)

        gg = jnp.zeros((1, X_DIM), F32)
        for h in range(X_HEADS):
            cs = slice(h * X_DIM, (h + 1) * X_DIM)
            n, r, qn = _xq_norm(z_ref, g_ref, h)
            qb = qn.astype(BF16)
            p = _xprobs(qb, k_ref, h)
            pb = p.astype(BF16)
            dob = do_ref[:, cs].astype(BF16)
            dv_ref[:, cs] += lax.dot_general(pb, dob, _DN["tn"], preferred_element_type=F32)
            dp = lax.dot_general(dob, v_ref[:, cs], _DN["nt"], preferred_element_type=F32)
            ds = (p * (dp - jnp.sum(dp * p, axis=-1, keepdims=True))).astype(BF16)
            dk_ref[:, cs] += lax.dot_general(ds, qb, _DN["tn"], preferred_element_type=F32) * scale
            dqn = jnp.dot(ds, k_ref[:, cs], preferred_element_type=F32) * scale
            gg += jnp.sum(dqn * n, axis=0, keepdims=True)
            dn = dqn * g_ref[...]
            c = jnp.mean(dn * n, axis=-1, keepdims=True)
            dz_ref[:, cs] = (r * (dn - n * c)).astype(BF16)
        gg_ref[...] += gg

    full = pl.BlockSpec((M, XW), lambda i: (0, 0))
    vec = pl.BlockSpec((1, X_DIM), lambda i: (0, 0))
    return pl.pallas_call(
        body, grid=(S // tm,),
        in_specs=[pl.BlockSpec((tm, XW), lambda i: (i, Z_MQ_CB)), full, full, vec,
                  pl.BlockSpec((tm, XW), lambda i: (i, 3))],
        out_specs=[pl.BlockSpec((tm, XW), lambda i: (i, 0)), full, full, vec],
        out_shape=[jax.ShapeDtypeStruct((S, XW), BF16), jax.ShapeDtypeStruct((M, XW), F32),
                   jax.ShapeDtypeStruct((M, XW), F32), jax.ShapeDtypeStruct((1, X_DIM), F32)],
        compiler_params=_cp(("arbitrary",)), name=name)(z, kx, vx, gqx, d_cat)


def _silu_parts(x):
    h = 0.5 * x
    return h, jnp.tanh(h)


def _glu_tiles(S, F):
    return _row_tile(S), _pick(F, (1408, 512, 256, 128))


def _glu_fwd(g, u, conv_w, conv_b, *, name):
    S, F = g.shape
    tm, tc = _glu_tiles(S, F)
    hb = tm // 8

    def body(g_ref, gp_ref, u_ref, w_ref, b_ref, a_ref):
        i = pl.program_id(1)
        gt = g_ref[...]
        ext = jnp.concatenate([gp_ref[...] * (i > 0).astype(F32), gt], axis=0)
        gc = b_ref[...] + w_ref[0:1, :] * pltpu.roll(ext, 2, 0)[8:]
        gc = gc + w_ref[1:2, :] * pltpu.roll(ext, 1, 0)[8:]
        gc = gc + w_ref[2:3, :] * gt
        h, t = _silu_parts(gc)
        a_ref[...] = ((h * (1.0 + t)) * u_ref[...]).astype(BF16)

    return pl.pallas_call(
        body, grid=(F // tc, S // tm),
        in_specs=[pl.BlockSpec((tm, tc), lambda j, i: (i, j)),
                  pl.BlockSpec((8, tc), lambda j, i: (jnp.maximum(i * hb - 1, 0), j)),
                  pl.BlockSpec((tm, tc), lambda j, i: (i, j)),
                  pl.BlockSpec((3, tc), lambda j, i: (0, j)),
                  pl.BlockSpec((1, tc), lambda j, i: (0, j))],
        out_specs=pl.BlockSpec((tm, tc), lambda j, i: (i, j)),
        out_shape=jax.ShapeDtypeStruct((S, F), BF16),
        compiler_params=_cp(("parallel", "parallel")), name=name)(g, g, u, conv_w, conv_b)


def _glu_bwd(g, u, d_a, conv_w, conv_b, *, name):
    S, F = g.shape
    tm, tc = _glu_tiles(S, F)
    hb = tm // 8
    nt = S // tm
    E = tm + 8

    def body(g_ref, gp_ref, gn_ref, u_ref, un_ref, da_ref, dan_ref, w_ref, b_ref,
             dg_ref, du_ref, gw_ref, gb_ref):
        i = pl.program_id(1)

        @pl.when(i == 0)
        def _():
            gw_ref[...] = jnp.zeros_like(gw_ref)
            gb_ref[...] = jnp.zeros_like(gb_ref)

        w0, w1, w2 = w_ref[0:1, :], w_ref[1:2, :], w_ref[2:3, :]
        gext = jnp.concatenate([gp_ref[...] * (i > 0).astype(F32), g_ref[...], gn_ref[...]], axis=0)
        g1 = pltpu.roll(gext, 1, 0)[8:]
        g2 = pltpu.roll(gext, 2, 0)[8:]
        g0 = gext[8:]
        gc = b_ref[...] + w0 * g2
        gc = gc + w1 * g1
        gc = gc + w2 * g0
        h, t = _silu_parts(gc)
        t1 = 1.0 + t
        da = jnp.concatenate([da_ref[...], dan_ref[...] * (i < nt - 1).astype(F32)], axis=0)
        uu = jnp.concatenate([u_ref[...], un_ref[...]], axis=0)
        du_ref[...] = (da[:tm] * (h[:tm] * t1[:tm])).astype(BF16)
        dgc = (da * uu) * (0.5 * (t1 + h * (1.0 - t * t)))
        dg = w2 * dgc[:tm] + w1 * pltpu.roll(dgc, E - 1, 0)[:tm] + w0 * pltpu.roll(dgc, E - 2, 0)[:tm]
        dg_ref[...] = dg.astype(BF16)
        dgt = dgc[:tm]
        gb_ref[...] += jnp.sum(dgt, axis=0, keepdims=True)
        gw_ref[0:1, :] += jnp.sum(dgt * g2[:tm], axis=0, keepdims=True)
        gw_ref[1:2, :] += jnp.sum(dgt * g1[:tm], axis=0, keepdims=True)
        gw_ref[2:3, :] += jnp.sum(dgt * g0[:tm], axis=0, keepdims=True)

    tile = pl.BlockSpec((tm, tc), lambda j, i: (i, j))
    nxt = pl.BlockSpec((8, tc), lambda j, i: (jnp.minimum((i + 1) * hb, S // 8 - 1), j))
    prv = pl.BlockSpec((8, tc), lambda j, i: (jnp.maximum(i * hb - 1, 0), j))
    return pl.pallas_call(
        body, grid=(F // tc, nt),
        in_specs=[tile, prv, nxt, tile, nxt, tile, nxt,
                  pl.BlockSpec((3, tc), lambda j, i: (0, j)), pl.BlockSpec((1, tc), lambda j, i: (0, j))],
        out_specs=[tile, tile, pl.BlockSpec((3, tc), lambda j, i: (0, j)), pl.BlockSpec((1, tc), lambda j, i: (0, j))],
        out_shape=[jax.ShapeDtypeStruct((S, F), BF16), jax.ShapeDtypeStruct((S, F), BF16),
                   jax.ShapeDtypeStruct((3, F), F32), jax.ShapeDtypeStruct((1, F), F32)],
        compiler_params=_cp(("parallel", "arbitrary")), name=name)(g, g, g, u, u, d_a, d_a, conv_w, conv_b)


def _down_proj_loss(a, w, x2, target, *, tm, tk, name):
    S, K = a.shape
    D = w.shape[1]
    ni, nk = S // tm, K // tk

    def body(a_ref, w_ref, x_ref, t_ref, dy_ref, dyb_ref, loss_ref, acc, sq):
        i, k = pl.program_id(0), pl.program_id(1)
        prod = jnp.dot(a_ref[...], w_ref[...], preferred_element_type=F32)

        @pl.when(k == 0)
        def _():
            acc[...] = prod

        @pl.when(k > 0)
        def _():
            acc[...] += prod

        @pl.when(jnp.logical_and(i == 0, k == 0))
        def _():
            sq[...] = jnp.zeros_like(sq)

        @pl.when(k == nk - 1)
        def _():
            e = (x_ref[...] + acc[...]) - t_ref[...]
            dy = e * (1.0 / D)
            dy_ref[...] = dy
            dyb_ref[...] = dy.astype(BF16)
            sq[...] += jnp.sum(e * e, axis=0, keepdims=True)

        @pl.when(jnp.logical_and(i == ni - 1, k == nk - 1))
        def _():
            loss_ref[...] = jnp.broadcast_to(jnp.sum(sq[...], axis=1, keepdims=True) * (0.5 / D), (1, 128))

    row = pl.BlockSpec((tm, D), lambda i, k: (i, 0))
    return pl.pallas_call(
        body, grid=(ni, nk),
        in_specs=[pl.BlockSpec((tm, tk), lambda i, k: (i, k)), pl.BlockSpec((tk, D), lambda i, k: (k, 0)), row, row],
        out_specs=[row, row, pl.BlockSpec((1, 128), lambda i, k: (0, 0))],
        out_shape=[jax.ShapeDtypeStruct((S, D), F32), jax.ShapeDtypeStruct((S, D), BF16),
                   jax.ShapeDtypeStruct((1, 128), F32)],
        scratch_shapes=[pltpu.VMEM((tm, D), F32), pltpu.VMEM((1, D), F32)],
        compiler_params=_cp(("arbitrary", "arbitrary")), name=name)(a, w, x2, target)


def _adamw_math(w, g, m, v):
    m = ADAM_B1 * m + (1.0 - ADAM_B1) * g
    v = ADAM_B2 * v + (1.0 - ADAM_B2) * (g * g)
    m_hat = m / (1.0 - ADAM_B1 ** ADAM_STEP)
    v_hat = v / (1.0 - ADAM_B2 ** ADAM_STEP)
    delta = -ADAM_LR * (m_hat / (jnp.sqrt(v_hat) + ADAM_EPS) + ADAM_WD * w)
    return delta, m, v


def _adamw(w, m, v, parts, *, name):
    R, C = w.shape
    tr = 128 if R % 128 == 0 else R
    n_parts = len(parts)

    def body(*refs):
        w_ref, m_ref, v_ref = refs[:3]
        p_refs = refs[3:3 + n_parts]
        g_ref, d_ref, mo_ref, vo_ref = refs[3 + n_parts:]
        g = p_refs[0][...]
        for p in p_refs[1:]:
            g = g + p[...]
        delta, mn, vn = _adamw_math(w_ref[...], g, m_ref[...], v_ref[...])
        g_ref[...] = g
        d_ref[...] = delta
        mo_ref[...] = mn
        vo_ref[...] = vn

    blk = pl.BlockSpec((tr, C), lambda i: (i, 0))
    return pl.pallas_call(
        body, grid=(R // tr,), in_specs=[blk] * (3 + n_parts), out_specs=[blk] * 4,
        out_shape=[jax.ShapeDtypeStruct((R, C), F32)] * 4,
        compiler_params=_cp(("parallel",)), name=name)(w, m, v, *parts)


def _sum4(g_stack, recv, me, *, name):
    _, R, C = g_stack.shape
    tr = 128 if R % 128 == 0 else R

    def body(me_ref, g_ref, r_ref, o_ref):
        acc = g_ref[...]
        for j in range(N_CHIPS - 1):
            acc = acc + r_ref[j].astype(F32)
        o_ref[...] = acc

    grid_spec = pltpu.PrefetchScalarGridSpec(
        num_scalar_prefetch=1, grid=(R // tr,),
        in_specs=[pl.BlockSpec((None, tr, C), lambda i, me_ref: (me_ref[0], i, 0)),
                  pl.BlockSpec((N_CHIPS - 1, tr, C), lambda i, me_ref: (0, i, 0))],
        out_specs=pl.BlockSpec((tr, C), lambda i, me_ref: (i, 0)))
    return pl.pallas_call(
        body, grid_spec=grid_spec, out_shape=jax.ShapeDtypeStruct((R, C), F32),
        compiler_params=_cp(("parallel",)), name=name)(me, g_stack, recv)


def _sum8(gathered, *, name):
    _, R, C = gathered.shape

    def body(g_ref, o_ref):
        acc = g_ref[0]
        for d in range(1, N_DEV):
            acc = acc + g_ref[d]
        o_ref[...] = acc

    return pl.pallas_call(
        body, grid=(1,), in_specs=[pl.BlockSpec((N_DEV, R, C), lambda i: (0, 0, 0))],
        out_specs=pl.BlockSpec((R, C), lambda i: (0, 0)),
        out_shape=jax.ShapeDtypeStruct((R, C), F32),
        compiler_params=_cp(("arbitrary",)), name=name)(gathered)


def _place():
    return lax.axis_index("x"), lax.axis_index("y"), lax.axis_index("c")


def _other_chips(x, y):
    return [(1 - x, y), (x, 1 - y), (1 - x, 1 - y)]


_ANY = pl.BlockSpec(memory_space=pl.ANY)


_HBM = pl.BlockSpec(memory_space=pltpu.HBM)
_SEM = pl.BlockSpec(memory_space=pltpu.SEMAPHORE)
_EFFECT = pltpu.SideEffectType.DATAFLOW_SIDE_EFFECTING


def _gather_copies(srcs, lands, send_sems, recv_sems):
    x, y, c = _place()
    me = 2 * x + y
    return [pltpu.make_async_remote_copy(
        src_ref=srcs[i], dst_ref=lands[i].at[me], send_sem=send_sems.at[3 * i + j],
        recv_sem=recv_sems.at[3 * i + j], device_id=(px, py, c), device_id_type=MESH)
        for i in range(len(srcs)) for j, (px, py) in enumerate(_other_chips(x, y))]


def _scatter_copies(srcs, lands, send_sems, recv_sems):
    x, y, c = _place()
    return [pltpu.make_async_remote_copy(
        src_ref=srcs[i].at[2 * px + py], dst_ref=lands[i].at[j], send_sem=send_sems.at[3 * i + j],
        recv_sem=recv_sems.at[3 * i + j], device_id=(px, py, c), device_id_type=MESH)
        for i in range(len(srcs)) for j, (px, py) in enumerate(_other_chips(x, y))]


def _copies_start(srcs, lands, make_copies, *, name):
    n = len(srcs)

    def body(*refs):
        send_sems, recv_sems = refs[2 * n], refs[2 * n + 1]
        for cp in make_copies(refs[:n], refs[n:2 * n], send_sems, recv_sems):
            cp.start()
        refs[-1][...] = jnp.zeros_like(refs[-1])

    ops = list(srcs) + list(lands)
    outs = pl.pallas_call(
        body, name=name,
        out_shape=(pltpu.SemaphoreType.DMA((3 * n,)), pltpu.SemaphoreType.DMA((3 * n,)),
                   *[pltpu.HBM(a.shape, a.dtype) for a in ops], jax.ShapeDtypeStruct((8, 128), F32)),
        in_specs=[_HBM] * (2 * n),
        out_specs=(_SEM, _SEM, *[_HBM] * (2 * n), pl.BlockSpec(memory_space=pltpu.VMEM)),
        input_output_aliases={i: 2 + i for i in range(2 * n)},
        compiler_params=pltpu.CompilerParams(has_side_effects=_EFFECT),
    )(*[pltpu.with_memory_space_constraint(a, pltpu.HBM) for a in ops])
    return outs[0], outs[1], list(outs[2:2 + n]), list(outs[2 + n:2 + 2 * n]), outs[-1]


def _copies_wait(handle, after, make_copies, *, name):
    send_sems, recv_sems, srcs, lands, _ = handle
    n = len(srcs)

    def body(*refs):
        for cp in make_copies(refs[:n], refs[n:2 * n], refs[2 * n], refs[2 * n + 1]):
            cp.wait_send()
            cp.wait_recv()

    ops = list(srcs) + list(lands)
    outs = pl.pallas_call(
        body, name=name,
        out_shape=tuple(pltpu.HBM(a.shape, a.dtype) for a in ops),
        in_specs=[_HBM] * (2 * n) + [_SEM, _SEM, _ANY],
        out_specs=tuple([_HBM] * (2 * n)),
        input_output_aliases={i: i for i in range(2 * n)},
        compiler_params=pltpu.CompilerParams(has_side_effects=_EFFECT),
    )(*ops, send_sems, recv_sems, after)
    return list(outs[n:])


def _swap_with_sibling(arrs, *, name):
    n = len(arrs)

    def body(*refs):
        ins, outs = refs[:n], refs[n:2 * n]
        send_sems, recv_sems = refs[2 * n:]
        x, y, c = _place()
        remote = []
        for i in range(n):
            rc = pltpu.make_async_remote_copy(
                src_ref=ins[i], dst_ref=outs[i], send_sem=send_sems.at[i], recv_sem=recv_sems.at[i],
                device_id=(x, y, 1 - c), device_id_type=MESH)
            rc.start()
            remote.append(rc)
        for rc in remote:
            rc.wait_send()
        for rc in remote:
            rc.wait_recv()

    return pl.pallas_call(
        body, in_specs=[_ANY] * n, out_specs=[_ANY] * n,
        out_shape=[jax.ShapeDtypeStruct(a.shape, a.dtype) for a in arrs],
        scratch_shapes=[pltpu.SemaphoreType.DMA((n,)), pltpu.SemaphoreType.DMA((n,))],
        name=name)(*arrs)


def _gather_all(buf, *, name):
    R, C = buf.shape

    def body(in_ref, out_ref, send_sems, recv_sems, local_sem):
        x, y, c = _place()
        me = 4 * x + 2 * y + c
        lc = pltpu.make_async_copy(in_ref, out_ref.at[me], local_sem)
        lc.start()
        remote = []
        for k in range(1, N_DEV):
            px = 1 - x if (k >> 2) & 1 else x
            py = 1 - y if (k >> 1) & 1 else y
            pc = 1 - c if k & 1 else c
            rc = pltpu.make_async_remote_copy(
                src_ref=in_ref, dst_ref=out_ref.at[me], send_sem=send_sems.at[k - 1],
                recv_sem=recv_sems.at[k - 1], device_id=(px, py, pc), device_id_type=MESH)
            rc.start()
            remote.append(rc)
        lc.wait()
        for rc in remote:
            rc.wait_send()
        for rc in remote:
            rc.wait_recv()

    return pl.pallas_call(
        body, in_specs=[_ANY], out_specs=_ANY,
        out_shape=jax.ShapeDtypeStruct((N_DEV, R, C), buf.dtype),
        scratch_shapes=[pltpu.SemaphoreType.DMA((N_DEV - 1,)), pltpu.SemaphoreType.DMA((N_DEV - 1,)),
                        pltpu.SemaphoreType.DMA],
        name=name)(buf)


def _w_in_to_z(w):
    pad = jnp.zeros(w.shape[:-1] + (64,), w.dtype)
    return jnp.concatenate([w[..., 0:512], w[..., 512:1024], w[..., 1344:1856], w[..., 1024:1280],
                            w[..., 1280:1344], pad], axis=-1)


def _z_to_w_in(g):
    return jnp.concatenate([g[..., 0:512], g[..., 512:1024], g[..., 1536:1792], g[..., 1792:1856],
                            g[..., 1024:1536]], axis=-1)


def _pad_heads(w, nh):
    w = w.reshape(w.shape[:-1] + (nh, QK))
    w = jnp.concatenate([w, jnp.zeros(w.shape[:-1] + (HEAD_PAD - QK,), w.dtype)], axis=-1)
    return w.reshape(w.shape[:-2] + (nh * HEAD_PAD,))


def _unpad_heads(g, nh):
    g = g.reshape(g.shape[:-1] + (nh, HEAD_PAD))[..., :QK]
    return g.reshape(g.shape[:-2] + (nh * QK,))


def _kv_split(w, nh):
    w = w.reshape(w.shape[:-1] + (nh, 2, 128))
    return jnp.swapaxes(w, -3, -2).reshape(w.shape[:-3] + (nh * 256,))


def _kv_join(g, nh):
    g = g.reshape(g.shape[:-1] + (2, nh, 128))
    return jnp.swapaxes(g, -3, -2).reshape(g.shape[:-3] + (nh * 256,))


def _pad_gain(g):
    return jnp.concatenate([g, jnp.zeros((1, HEAD_PAD - QK), g.dtype)], axis=1)


_SMALL = ("g_mix", "g_q_lat", "g_kv_lat", "g_q_mla", "g_k_mla", "w_pool", "pool_scale", "g_mem", "g_q_x",
          "g_k_x", "g_ffn", "conv_b", "conv_w")


def _pack(arrs, extra=0):
    flat = jnp.concatenate([a.reshape(-1) for a in arrs])
    n = flat.shape[0] + extra
    rows = -(-n // 1024) * 8
    return jnp.pad(flat, (0, rows * 128 - flat.shape[0])).reshape(rows, 128)


def _unpack(buf, shapes):
    flat = buf.reshape(-1)
    out, off = [], 0
    for s in shapes:
        n = int(np.prod(s))
        out.append(flat[off:off + n].reshape(s))
        off += n
    return out, off


def _tied(a, token):
    return a + token[:1, :1].astype(a.dtype)


def _local_step(x, mem, target, W, fetch=None, ship=None):
    fetch = fetch or (lambda group, after: None)
    ship = ship or (lambda group, G: jnp.zeros((8, 128), F32))
    S, D = x.shape
    F = W["conv_b"].shape[1]
    tabs = _rope_tables(S)
    tm = 512 if S % 512 == 0 else 128
    tl = 1024 if S % 1024 == 0 else tm
    tk = _pick(S, (1024, 512, 128))

    h = _rms_fwd(x, W["g_mix"], C=D, name="norm_mix")
    fetch("g1", h)
    z = mm_nn(h, W["w_in"], tm=tl, tn=Z_COLS, tk=D, name="z_proj")
    fetch("g2", z)
    y_pool = _pool_fwd(z, W["w_pool"], W["pool_scale"], name="pool_fwd")
    ql = _rms_fwd(z, W["g_q_lat"], C=Q_RANK, cb=Z_Q_CB, name="norm_qlat")
    kvl = _rms_fwd(z, W["g_kv_lat"], C=KV_RANK, cb=Z_KV_CB, name="norm_kvlat")
    qraw = mm_nn(ql, W["w_q_up"], nsh=N_CHIPS, tm=tl, tn=512, tk=Q_RANK, name="q_up")
    kvraw = mm_nn(kvl, W["w_kv_up"], nsh=N_CHIPS, tm=tl, tn=512, tk=KV_RANK, name="kv_up")
    q, k, v, vt = _qkrope_fwd(qraw, kvraw, z, W["g_q_mla"], W["g_k_mla"], tabs, name="qk_norm_rope")
    o, y_mla, lse = _flash_fwd(q, k, vt, name="mla_fwd")
    memn = _rms_fwd(mem, W["g_mem"], C=D, name="norm_mem")
    M = mem.shape[0]
    mkv = mm_nn(memn, W["w_mem_kv"], tm=M, tn=1024, tk=D, name="mem_kv")
    kx, vx = _memk_fwd(mkv, W["g_k_x"], name="memk_fwd")
    y_mem = _memattn_fwd(z, kx, vx, W["g_q_x"], name="memattn_fwd")
    cat = jnp.concatenate([y_pool, y_mla, y_mem], axis=1)
    x2 = mm_nn(cat, W["w_o"], tm=tm, tn=D, tk=D, add=x, name="o_proj")
    h2 = _rms_fwd(x2, W["g_ffn"], C=D, name="norm_ffn")
    fetch("g3", h2)
    fn = F // N_CHIPS
    g = mm_nn(h2, W["w_gate"], nsh=N_CHIPS, tm=tl, tn=fn, tk=D, name="gate_proj")
    u = mm_nn(h2, W["w_up"], nsh=N_CHIPS, tm=tl, tn=fn, tk=D, name="up_proj")
    a = _glu_fwd(g, u, W["conv_w"], W["conv_b"], name="glu_fwd")
    dy, dyb, loss_row = _down_proj_loss(a, W["w_down"], x2, target, tm=tm, tk=_pick(F, (512, 128)),
                                        name="down_proj_loss")

    G = {}
    d_a = mm_nt(dyb, W["w_down"], tm=tl, to=512, tc=D, name="d_a")
    G["w_down"] = mm_tn(a, dyb, to=fn, tn=1024, tk=tk, out_dtypes=(F32, BF16), name="grad_w_down")
    d_g, d_u, G["conv_w"], G["conv_b"] = _glu_bwd(g, u, d_a, W["conv_w"], W["conv_b"], name="glu_bwd")
    G["w_gate"] = mm_tn(h2, d_g, nsh=N_CHIPS, to=1024, tn=fn, tk=tk, out_dtypes=(F32, BF16), name="grad_w_gate")
    G["w_up"] = mm_tn(h2, d_u, nsh=N_CHIPS, to=1024, tn=fn, tk=tk, out_dtypes=(F32, BF16), name="grad_w_up")
    d_h2 = mm_nt(d_g, W["w_gate"], nsh=N_CHIPS, tm=tm, to=D, tc=fn, name="d_h2_gate")
    tok = ship("s1", G)
    d_h2 = mm_nt(d_u, W["w_up"], nsh=N_CHIPS, tm=tm, to=D, tc=fn, add=d_h2, name="d_h2_up")
    d_x2, d_x2b, G["g_ffn"] = _rms_bwd(x2, d_h2, _tied(W["g_ffn"], tok), C=D, res=dy, out_dtypes=(F32, BF16),
                                       name="norm_ffn_bwd")

    d_cat = mm_nt(d_x2b, W["w_o"], tm=tl, to=1024, tc=D, name="d_cat")
    G["w_o"] = mm_tn(cat, d_x2b, to=1024, tn=1024, tk=tk, out_dtypes=(F32, BF16), name="grad_w_o")
    tok = ship("s2", G)
    dz_pool, G["w_pool"], G["pool_scale"] = _pool_bwd(z, d_cat, W["w_pool"], _tied(W["pool_scale"], tok),
                                                      name="pool_bwd")
    dz_mq, dkx, dvx, G["g_q_x"] = _memattn_bwd(z, kx, vx, W["g_q_x"], d_cat, name="memattn_bwd")
    d_mkv, G["g_k_x"] = _memk_bwd(mkv, W["g_k_x"], dkx, dvx, name="memk_bwd")
    G["w_mem_kv"] = mm_tn(memn, d_mkv, to=1024, tn=1024, tk=M, out_dtypes=(F32, BF16), name="grad_w_mem_kv")
    d_memn = mm_nt(d_mkv, W["w_mem_kv"], tm=M, to=D, tc=1024, name="d_memn")
    _, G["g_mem"] = _rms_bwd(mem, d_memn, W["g_mem"], C=D, name="norm_mem_bwd")
    dob, delta = _attn_bwd_prep(o, d_cat, name="mla_bwd_prep")
    dq, dk, dv = _flash_bwd(q, k, v, dob, lse, delta, name="mla_bwd")
    d_qraw, d_kvraw, dz_kr, G["g_q_mla"], G["g_k_mla"] = _qkrope_bwd(
        qraw, kvraw, z, W["g_q_mla"], W["g_k_mla"], tabs, dq, dk, dv, name="qk_norm_rope_bwd")
    G["w_q_up"] = mm_tn(ql, d_qraw, nsh=N_CHIPS, to=Q_RANK, tn=512, tk=tk, out_dtypes=(F32, BF16), name="grad_w_q_up")
    d_ql = mm_nt(d_qraw, W["w_q_up"], nsh=N_CHIPS, tm=tl, to=Q_RANK, tc=512, name="d_ql")
    G["w_kv_up"] = mm_tn(kvl, d_kvraw, nsh=N_CHIPS, to=KV_RANK, tn=512, tk=tk, out_dtypes=(F32, BF16),
                         name="grad_w_kv_up")
    d_kvl = mm_nt(d_kvraw, W["w_kv_up"], nsh=N_CHIPS, tm=tl, to=KV_RANK, tc=512, name="d_kvl")
    dz_q, G["g_q_lat"] = _rms_bwd(z, d_ql, W["g_q_lat"], C=Q_RANK, cb=Z_Q_CB, out_dtypes=(BF16,), name="norm_qlat_bwd")
    dz_kv, G["g_kv_lat"] = _rms_bwd(z, d_kvl, W["g_kv_lat"], C=KV_RANK, cb=Z_KV_CB, out_dtypes=(BF16,),
                                    name="norm_kvlat_bwd")
    d_z = jnp.concatenate([dz_pool, dz_q, dz_mq, dz_kv, dz_kr], axis=1)
    G["w_in"] = mm_tn(h, d_z, to=512, tn=Z_COLS, tk=tk, out_dtypes=(F32, BF16), name="grad_w_in")
    tok = ship("s3", G)
    d_h = mm_nt(d_z, W["w_in"], tm=tl, to=1024, tc=Z_COLS, name="d_h")
    grad_x, G["g_mix"] = _rms_bwd(x, d_h, _tied(W["g_mix"], tok), C=D, res=d_x2, name="norm_mix_bwd")
    return loss_row, grad_x, G


_BIG = ("w_in", "w_q_up", "w_kv_up", "w_mem_kv", "w_o", "w_gate", "w_up", "w_down")
_WEIGHTS = ("g_mix", "w_in", "g_q_lat", "w_q_up", "g_kv_lat", "w_kv_up", "g_q_mla", "g_k_mla", "w_pool",
            "pool_scale", "g_mem", "w_mem_kv", "g_q_x", "g_k_x", "w_o", "g_ffn", "w_gate", "w_up", "conv_w",
            "conv_b", "w_down")


def _to_compute_layout(name, w):
    if name == "w_in":
        return _w_in_to_z(w)
    if name == "w_q_up":
        return _pad_heads(w, w.shape[-1] // QK)
    if name == "w_kv_up":
        return _kv_split(w, w.shape[-1] // 256)
    return w


def _from_compute_layout(name, g):
    if name == "w_in":
        return _z_to_w_in(g)
    if name == "w_q_up":
        return _unpad_heads(g, g.shape[-1] // HEAD_PAD)
    if name == "w_kv_up":
        return _kv_join(g, g.shape[-1] // 256)
    return g


def kernel(x, mem, g_mix, w_in, g_q_lat, w_q_up, g_kv_lat, w_kv_up, g_q_mla, g_k_mla, w_pool, pool_scale, g_mem, w_mem_kv, g_q_x, g_k_x, w_o, g_ffn, w_gate, w_up, conv_w, conv_b, w_down, loss_target, m_g_mix, m_w_in, m_g_q_lat, m_w_q_up, m_g_kv_lat, m_w_kv_up, m_g_q_mla, m_g_k_mla, m_w_pool, m_pool_scale, m_g_mem, m_w_mem_kv, m_g_q_x, m_g_k_x, m_w_o, m_g_ffn, m_w_gate, m_w_up, m_conv_w, m_conv_b, m_w_down, v_g_mix, v_w_in, v_g_q_lat, v_w_q_up, v_g_kv_lat, v_w_kv_up, v_g_q_mla, v_g_k_mla, v_w_pool, v_pool_scale, v_g_mem, v_w_mem_kv, v_g_q_x, v_g_k_x, v_w_o, v_g_ffn, v_w_gate, v_w_up, v_conv_w, v_conv_b, v_w_down):
    P = dict(g_mix=g_mix, w_in=w_in, g_q_lat=g_q_lat, w_q_up=w_q_up, g_kv_lat=g_kv_lat, w_kv_up=w_kv_up,
             g_q_mla=g_q_mla, g_k_mla=g_k_mla, w_pool=w_pool, pool_scale=pool_scale, g_mem=g_mem,
             w_mem_kv=w_mem_kv, g_q_x=g_q_x, g_k_x=g_k_x, w_o=w_o, g_ffn=g_ffn, w_gate=w_gate, w_up=w_up,
             conv_w=conv_w, conv_b=conv_b, w_down=w_down)
    Mo = dict(g_mix=m_g_mix, w_in=m_w_in, g_q_lat=m_g_q_lat, w_q_up=m_w_q_up, g_kv_lat=m_g_kv_lat,
              w_kv_up=m_w_kv_up, g_q_mla=m_g_q_mla, g_k_mla=m_g_k_mla, w_pool=m_w_pool,
              pool_scale=m_pool_scale, g_mem=m_g_mem, w_mem_kv=m_w_mem_kv, g_q_x=m_g_q_x, g_k_x=m_g_k_x,
              w_o=m_w_o, g_ffn=m_g_ffn, w_gate=m_w_gate, w_up=m_w_up, conv_w=m_conv_w, conv_b=m_conv_b,
              w_down=m_w_down)
    Vo = dict(g_mix=v_g_mix, w_in=v_w_in, g_q_lat=v_g_q_lat, w_q_up=v_w_q_up, g_kv_lat=v_g_kv_lat,
              w_kv_up=v_w_kv_up, g_q_mla=v_g_q_mla, g_k_mla=v_g_k_mla, w_pool=v_w_pool,
              pool_scale=v_pool_scale, g_mem=v_g_mem, w_mem_kv=v_w_mem_kv, g_q_x=v_g_q_x, g_k_x=v_g_k_x,
              w_o=v_w_o, g_ffn=v_g_ffn, w_gate=v_w_gate, w_up=v_w_up, conv_w=v_conv_w, conv_b=v_conv_b,
              w_down=v_w_down)
    xi, yi, ci = _place()
    me = (2 * xi + yi).astype(jnp.int32).reshape(1)

    shard = {n: _to_compute_layout(n, P[n][0]).astype(BF16) for n in _BIG}
    shard["conv_w"] = conv_w[0]
    gather_groups = {"g1": ("w_in",), "g2": ("w_q_up", "w_kv_up", "w_mem_kv", "w_o"),
                     "g3": ("w_gate", "w_up", "w_down", "conv_w")}
    gathers = {}
    tok = jnp.zeros((1, 1), F32)
    for grp, names in gather_groups.items():
        lands = [lax.dynamic_update_slice(lax.empty((N_CHIPS,) + shard[n].shape, shard[n].dtype), shard[n][None],
                                          (me[0], 0, 0)) for n in names]
        gathers[grp] = _copies_start([shard[n] for n in names], lands, _gather_copies, name="gather_start_" + grp)
        tok = tok + gathers[grp][4][:1, :1]
    W = {}
    W["g_q_mla"], W["g_k_mla"] = _pad_gain(g_q_mla), _pad_gain(g_k_mla)
    W["w_pool"] = w_pool[0].astype(BF16)
    for n in ("g_mix", "g_q_lat", "g_kv_lat", "pool_scale", "g_mem", "g_q_x", "g_k_x", "g_ffn", "conv_b"):
        W[n] = P[n]
    W["g_mix"] = _tied(W["g_mix"], tok)

    def fetch(grp, after):
        stacks = _copies_wait(gathers[grp], after, _gather_copies, name="gather_wait_" + grp)
        for n, s in zip(gather_groups[grp], stacks):
            if n == "conv_w":
                W[n] = jnp.swapaxes(s, 0, 1).reshape(3, -1)
            else:
                W[n] = s.reshape(-1, s.shape[-1])

    shard_shape = {n: shard[n].shape for n in _BIG}
    scatter_groups = {"s1": ("w_down", "w_gate", "w_up"), "s2": ("w_o",),
                      "s3": ("w_mem_kv", "w_q_up", "w_kv_up", "w_in")}
    scatters = {}

    def ship(grp, G):
        names = scatter_groups[grp]
        srcs = [G[n][1].reshape((N_CHIPS,) + shard_shape[n]) for n in names]
        lands = [lax.empty((N_CHIPS - 1,) + shard_shape[n], BF16) for n in names]
        scatters[grp] = _copies_start(srcs, lands, _scatter_copies, name="scatter_start_" + grp)
        return scatters[grp][4]

    loss_row, grad_x, G = _local_step(x[0], mem[0], loss_target[0], W, fetch, ship)

    recv = {}
    for grp, names in scatter_groups.items():
        for n, r in zip(names, _copies_wait(scatters[grp], grad_x, _scatter_copies, name="scatter_wait_" + grp)):
            recv[n] = r
    part = [_sum4(G[n][0].reshape((N_CHIPS,) + shard_shape[n]), recv[n], me, name="sum4_" + n) for n in _BIG]
    part = [_from_compute_layout(n, p) for n, p in zip(_BIG, part)]
    sib = _swap_with_sibling(part, name="swap_grads")
    out = {}
    for n, p, s in zip(_BIG, part, sib):
        out[n] = [r[None] for r in _adamw(P[n][0], Mo[n][0], Vo[n][0], [p, s], name="adamw_" + n)]

    conv_w_full_grad = G["conv_w"]
    small_g = [G["g_mix"], G["g_q_lat"], G["g_kv_lat"], G["g_q_mla"][:, :QK], G["g_k_mla"][:, :QK], G["w_pool"],
               G["pool_scale"], G["g_mem"], G["g_q_x"], G["g_k_x"], G["g_ffn"], G["conv_b"], conv_w_full_grad]
    packed = _pack(small_g + [loss_row[:, :1]])
    total = _sum8(_gather_all(packed, name="gather_small"), name="sum_small")
    shapes = [a.shape for a in small_g] + [(1, 1)]
    (parts, _) = _unpack(total, shapes)
    loss = parts[-1].reshape(())
    F = conv_b.shape[1]
    fn = F // N_CHIPS
    col0 = (2 * xi + yi) * fn
    sg = dict(zip(_SMALL, parts[:-1]))
    sg["conv_w"] = lax.dynamic_slice(sg["conv_w"], (0, col0), (3, fn))
    sw = [P[n].reshape(sg[n].shape) for n in _SMALL]
    sm = [Mo[n].reshape(sg[n].shape) for n in _SMALL]
    sv = [Vo[n].reshape(sg[n].shape) for n in _SMALL]
    gp = _pack([sg[n] for n in _SMALL])
    res = _adamw(_pack(sw), _pack(sm), _pack(sv), [gp], name="adamw_small")
    sshapes = [sg[n].shape for n in _SMALL]
    for kind, buf in zip(range(4), res):
        vals, _ = _unpack(buf, sshapes)
        for n, val in zip(_SMALL, vals):
            out.setdefault(n, [None] * 4)[kind] = val.reshape(P[n].shape)

    return (loss, grad_x[None], *[out[n][0] for n in _WEIGHTS], *[out[n][1] for n in _WEIGHTS],
            *[out[n][2] for n in _WEIGHTS], *[out[n][3] for n in _WEIGHTS])
```

```python
import functools
import math

import numpy as np
import jax
import jax.numpy as jnp
from jax import lax
from jax.experimental import pallas as pl
from jax.experimental.pallas import tpu as pltpu

F32, BF16 = jnp.float32, jnp.bfloat16
NORM_EPS = 1e-6
ROPE_THETA = 10000.0
V7X_VMEM_LIMIT_BYTES = 48 * 1024 * 1024
N_CHIPS = 4
N_DEV = 8

POOL_W = 512
POOL_WINDOWS = (2, 4, 8, 16)
HEADS = 8
NOPE, ROPE, QK = 128, 64, 192
HEAD_PAD = 256
Q_RANK, KV_RANK = 512, 256
X_HEADS, X_DIM = 4, 128
Z_COLS = 1920
Z_POOL_CB, Z_Q_CB, Z_MQ_CB = 0, 1, 2
Z_KV_CB = 6
Z_KR_CB = 14

ADAM_LR, ADAM_B1, ADAM_B2, ADAM_EPS, ADAM_WD, ADAM_STEP = 0.001, 0.9, 0.999, 1e-08, 0.01, 10

MESH = pl.DeviceIdType.MESH


def _cp(sem):
    return pltpu.CompilerParams(dimension_semantics=sem, vmem_limit_bytes=V7X_VMEM_LIMIT_BYTES)


def _row_tile(S):
    return 256 if S % 256 == 0 and S >= 2048 else 128


def _big_row_tile(S):
    return 512 if S % 512 == 0 and S >= 2048 else _row_tile(S)


def _pick(dim, prefs):
    for p in prefs:
        if dim % p == 0:
            return p
    return dim


_DN = {"nn": (((1,), (0,)), ((), ())), "nt": (((1,), (1,)), ((), ())), "tn": (((0,), (0,)), ((), ()))}


def _mm(a, b, *, mode, grid, blocks, maps, out_shape, out_dtypes, add=None, name):
    nk = grid[2]
    dn = _DN[mode]
    n_out = len(out_dtypes)

    def body(*refs):
        a_ref, b_ref = refs[0], refs[1]
        add_ref = refs[2] if add is not None else None
        p = 2 + (add is not None)
        o_refs = refs[p:p + n_out]

        def finish(r):
            if add_ref is not None:
                r = r + add_ref[...]
            for o in o_refs:
                o[...] = r.astype(o.dtype)

        def product():
            return lax.dot_general(a_ref[...].astype(BF16), b_ref[...].astype(BF16), dn, preferred_element_type=F32)

        if nk == 1:
            finish(product())
            return
        acc = refs[p + n_out]
        k = pl.program_id(2)

        @pl.when(k == 0)
        def _():
            acc[...] = jnp.zeros_like(acc)

        acc[...] += product()

        @pl.when(k == nk - 1)
        def _():
            finish(acc[...])

    a_blk, b_blk, o_blk = blocks
    a_map, b_map, o_map = maps
    in_specs = [pl.BlockSpec(a_blk, a_map), pl.BlockSpec(b_blk, b_map)]
    args = [a, b]
    if add is not None:
        in_specs.append(pl.BlockSpec(o_blk, o_map))
        args.append(add)
    outs = pl.pallas_call(
        body, grid=grid, in_specs=in_specs,
        out_specs=[pl.BlockSpec(o_blk, o_map) for _ in out_dtypes],
        out_shape=[jax.ShapeDtypeStruct(out_shape, d) for d in out_dtypes],
        scratch_shapes=[pltpu.VMEM(o_blk, F32)] if nk > 1 else [],
        compiler_params=_cp(("parallel", "parallel", "arbitrary")), name=name)(*args)
    return outs[0] if n_out == 1 else outs


def mm_nn(a, w, *, nsh=1, tm, tn, tk, out_dtypes=(F32,), add=None, name):
    M, K = a.shape
    n = w.shape[1]
    N = nsh * n
    assert w.shape[0] == nsh * K and n % tn == 0 and K % tk == 0 and M % tm == 0
    npt, kt = n // tn, K // tk
    return _mm(a, w, mode="nn", grid=(M // tm, N // tn, kt),
               blocks=((tm, tk), (tk, tn), (tm, tn)),
               maps=(lambda i, j, k: (i, k), lambda i, j, k: ((j // npt) * kt + k, j % npt),
                     lambda i, j, k: (i, j)),
               out_shape=(M, N), out_dtypes=out_dtypes, add=add, name=name)


def mm_nt(d, w, *, nsh=1, tm, to, tc, out_dtypes=(F32,), add=None, name):
    M, N = d.shape
    n = w.shape[1]
    K = w.shape[0] // nsh
    assert nsh * n == N and n % tc == 0 and K % to == 0 and M % tm == 0
    cpt, ot = n // tc, K // to
    return _mm(d, w, mode="nt", grid=(M // tm, ot, N // tc),
               blocks=((tm, tc), (to, tc), (tm, to)),
               maps=(lambda i, j, c: (i, c), lambda i, j, c: ((c // cpt) * ot + j, c % cpt),
                     lambda i, j, c: (i, j)),
               out_shape=(M, K), out_dtypes=out_dtypes, add=add, name=name)


def mm_tn(x, d, *, nsh=1, to, tn, tk, out_dtypes=(F32,), name):
    M, K = x.shape
    N = d.shape[1]
    n = N // nsh
    assert n % tn == 0 and K % to == 0 and M % tk == 0
    npt, ot = n // tn, K // to
    return _mm(x, d, mode="tn", grid=(ot, N // tn, M // tk),
               blocks=((tk, to), (tk, tn), (to, tn)),
               maps=(lambda i, j, k: (k, i), lambda i, j, k: (k, j),
                     lambda i, j, k: ((j // npt) * ot + i, j % npt)),
               out_shape=(nsh * K, n), out_dtypes=out_dtypes, name=name)


def _rms_fwd(x, g, *, C, cb=0, name):
    S = x.shape[0]
    tm = _big_row_tile(S) if S >= 128 else S

    def body(x_ref, g_ref, o_ref):
        xv = x_ref[...]
        r = lax.rsqrt(jnp.mean(xv * xv, axis=-1, keepdims=True) + NORM_EPS)
        o_ref[...] = ((xv * r) * g_ref[...]).astype(o_ref.dtype)

    return pl.pallas_call(
        body, grid=(S // tm,),
        in_specs=[pl.BlockSpec((tm, C), lambda i: (i, cb)), pl.BlockSpec((1, C), lambda i: (0, 0))],
        out_specs=pl.BlockSpec((tm, C), lambda i: (i, 0)),
        out_shape=jax.ShapeDtypeStruct((S, C), BF16),
        compiler_params=_cp(("parallel",)), name=name)(x, g)


def _rms_bwd(x, dh, g, *, C, cb=0, res=None, out_dtypes=(F32,), name):
    S = x.shape[0]
    tm = (_big_row_tile(S) if C <= 512 else _row_tile(S)) if S >= 128 else S
    n_out = len(out_dtypes)

    def body(*refs):
        x_ref, dh_ref, g_ref = refs[:3]
        res_ref = refs[3] if res is not None else None
        p = 3 + (res is not None)
        outs = refs[p:p + n_out]
        dg_ref = refs[p + n_out]
        i = pl.program_id(0)
        xv = x_ref[...]
        r = lax.rsqrt(jnp.mean(xv * xv, axis=-1, keepdims=True) + NORM_EPS)
        n = xv * r
        dhv = dh_ref[...].astype(F32)
        dn = dhv * g_ref[...]
        c = jnp.mean(dn * n, axis=-1, keepdims=True)
        dx = r * (dn - n * c)
        if res_ref is not None:
            dx = res_ref[...] + dx
        for o in outs:
            o[...] = dx.astype(o.dtype)

        @pl.when(i == 0)
        def _():
            dg_ref[...] = jnp.zeros_like(dg_ref)

        dg_ref[...] += jnp.sum(dhv * n, axis=0, keepdims=True)

    row = pl.BlockSpec((tm, C), lambda i: (i, 0))
    in_specs = [pl.BlockSpec((tm, C), lambda i: (i, cb)), row, pl.BlockSpec((1, C), lambda i: (0, 0))]
    args = [x, dh, g]
    if res is not None:
        in_specs.append(row)
        args.append(res)
    return pl.pallas_call(
        body, grid=(S // tm,), in_specs=in_specs,
        out_specs=[row] * n_out + [pl.BlockSpec((1, C), lambda i: (0, 0))],
        out_shape=[jax.ShapeDtypeStruct((S, C), d) for d in out_dtypes] + [jax.ShapeDtypeStruct((1, C), F32)],
        compiler_params=_cp(("arbitrary",)), name=name)(*args)


def _pool_cnt(t0, rows, w):
    t = t0 + lax.broadcasted_iota(jnp.int32, (rows, 1), 0)
    return jnp.minimum(t + 1, w).astype(F32)


def _pool_d(halo, tile, gi, t0, tm):
    s = jnp.concatenate([halo, tile], axis=0)
    for step in (1, 2, 4, 8)[:gi + 1]:
        s = s + pltpu.roll(s, step, 0)
    return s[16:] / _pool_cnt(t0, tm, POOL_WINDOWS[gi]) - tile


def _pool_fwd(z, w_pool, pool_scale, *, name):
    S = z.shape[0]
    tm = _row_tile(S)
    hb = tm // 16

    def body(z_ref, h_ref, w_ref, sc_ref, o_ref):
        i = pl.program_id(0)
        halo = h_ref[...] * (i > 0).astype(F32)
        for gi in range(4):
            cs = slice(gi * 128, (gi + 1) * 128)
            d = _pool_d(halo[:, cs], z_ref[:, cs], gi, i * tm, tm)
            yp = jnp.dot(d.astype(BF16), w_ref[gi], preferred_element_type=F32)
            o_ref[:, cs] = (yp * sc_ref[:, cs]).astype(o_ref.dtype)

    return pl.pallas_call(
        body, grid=(S // tm,),
        in_specs=[pl.BlockSpec((tm, POOL_W), lambda i: (i, Z_POOL_CB)),
                  pl.BlockSpec((16, POOL_W), lambda i: (jnp.maximum(i * hb - 1, 0), Z_POOL_CB)),
                  pl.BlockSpec((4, 128, 128), lambda i: (0, 0, 0)),
                  pl.BlockSpec((1, POOL_W), lambda i: (0, 0))],
        out_specs=pl.BlockSpec((tm, POOL_W), lambda i: (i, 0)),
        out_shape=jax.ShapeDtypeStruct((S, POOL_W), BF16),
        compiler_params=_cp(("parallel",)), name=name)(z, z, w_pool, pool_scale)


def _pool_bwd(z, d_cat, w_pool, pool_scale, *, name):
    S = z.shape[0]
    tm = _row_tile(S)
    hb = tm // 16
    nt = S // tm
    E = tm + 16

    def body(z_ref, h_ref, dy_ref, dyn_ref, w_ref, sc_ref, dz_ref, gw_ref, gs_ref):
        i = pl.program_id(0)

        @pl.when(i == 0)
        def _():
            gw_ref[...] = jnp.zeros_like(gw_ref)
            gs_ref[...] = jnp.zeros_like(gs_ref)

        halo = h_ref[...] * (i > 0).astype(F32)
        dy_next = dyn_ref[...] * (i < nt - 1).astype(F32)
        for gi in range(4):
            cs = slice(gi * 128, (gi + 1) * 128)
            w = w_ref[gi]
            d = _pool_d(halo[:, cs], z_ref[:, cs], gi, i * tm, tm)
            db = d.astype(BF16)
            dy = dy_ref[:, cs]
            yp = jnp.dot(db, w, preferred_element_type=F32)
            gs_ref[:, cs] += jnp.sum(dy * yp, axis=0, keepdims=True)
            sc = sc_ref[:, cs]
            dys = (dy * sc).astype(BF16)
            gw_ref[gi] += lax.dot_general(db, dys, _DN["tn"], preferred_element_type=F32)
            dys_ext = jnp.concatenate([dys, (dy_next[:, cs] * sc).astype(BF16)], axis=0)
            dd = lax.dot_general(dys_ext, w, _DN["nt"], preferred_element_type=F32)
            r = dd / _pool_cnt(i * tm, E, POOL_WINDOWS[gi])
            for step in (1, 2, 4, 8)[:gi + 1]:
                r = r + pltpu.roll(r, E - step, 0)
            dz_ref[:, cs] = (r[:tm] - dd[:tm]).astype(dz_ref.dtype)

    return pl.pallas_call(
        body, grid=(nt,),
        in_specs=[pl.BlockSpec((tm, POOL_W), lambda i: (i, Z_POOL_CB)),
                  pl.BlockSpec((16, POOL_W), lambda i: (jnp.maximum(i * hb - 1, 0), Z_POOL_CB)),
                  pl.BlockSpec((tm, POOL_W), lambda i: (i, 0)),
                  pl.BlockSpec((16, POOL_W), lambda i: (jnp.minimum((i + 1) * hb, S // 16 - 1), 0)),
                  pl.BlockSpec((4, 128, 128), lambda i: (0, 0, 0)),
                  pl.BlockSpec((1, POOL_W), lambda i: (0, 0))],
        out_specs=[pl.BlockSpec((tm, POOL_W), lambda i: (i, 0)),
                   pl.BlockSpec((4, 128, 128), lambda i: (0, 0, 0)),
                   pl.BlockSpec((1, POOL_W), lambda i: (0, 0))],
        out_shape=[jax.ShapeDtypeStruct((S, POOL_W), BF16),
                   jax.ShapeDtypeStruct((4, 128, 128), F32),
                   jax.ShapeDtypeStruct((1, POOL_W), F32)],
        compiler_params=_cp(("arbitrary",)), name=name)(z, z, d_cat, d_cat, w_pool, pool_scale)


def _rope_tables(S):
    half = ROPE // 2
    inv_freq = 1.0 / (ROPE_THETA ** (jnp.arange(half, dtype=F32) / half))
    ang = jnp.arange(S).astype(F32)[:, None] * inv_freq[None, :]
    cos, sin = jnp.cos(ang), jnp.sin(ang)
    zero = jnp.zeros((S, half), F32)
    cos_t = jnp.concatenate([cos, cos, zero, zero], axis=1)
    sa_t = jnp.concatenate([-sin, zero, zero, zero], axis=1)
    sb_t = jnp.concatenate([zero, sin, zero, zero], axis=1)
    return cos_t, sa_t, sb_t


def _head_fwd(xn, xr, gn, gr, cos, sa, sb):
    ms = (jnp.sum(xn * xn, axis=-1, keepdims=True) + jnp.sum(xr * xr, axis=-1, keepdims=True)) * (1.0 / QK)
    r = lax.rsqrt(ms + NORM_EPS)
    on = (xn * r) * gn
    yr = (xr * r) * gr
    orr = yr * cos + pltpu.roll(yr, 96, 1) * sa + pltpu.roll(yr, 32, 1) * sb
    return on, orr


def _head_bwd(xn, xr, gn, gr, don, dor, cos, sa, sb):
    ms = (jnp.sum(xn * xn, axis=-1, keepdims=True) + jnp.sum(xr * xr, axis=-1, keepdims=True)) * (1.0 / QK)
    r = lax.rsqrt(ms + NORM_EPS)
    nn, nr = xn * r, xr * r
    dyr = dor * cos + pltpu.roll(dor * sa, 32, 1) + pltpu.roll(dor * sb, 96, 1)
    ggn, ggr = don * nn, dyr * nr
    dnn, dnr = don * gn, dyr * gr
    c = (jnp.sum(dnn * nn, axis=-1, keepdims=True) + jnp.sum(dnr * nr, axis=-1, keepdims=True)) * (1.0 / QK)
    return r * (dnn - nn * c), r * (dnr - nr * c), ggn, ggr


def _kv_cols(h):
    base = (h // 2) * 512 + (h % 2) * 128
    return base, base + 256


def _qkrope_fwd(qraw, kvraw, z, gq, gk, tabs, *, name):
    S = qraw.shape[0]
    tm = _row_tile(S)

    def body(q_ref, kv_ref, zkr_ref, gq_ref, gk_ref, cos_ref, sa_ref, sb_ref, qo_ref, ko_ref, vo_ref, vt_ref):
        cos, sa, sb = cos_ref[...], sa_ref[...], sb_ref[...]
        zkr = zkr_ref[...]
        gqn, gqr, gkn, gkr = gq_ref[:, :128], gq_ref[:, 128:], gk_ref[:, :128], gk_ref[:, 128:]
        for h in range(HEADS):
            b = h * HEAD_PAD
            on, orr = _head_fwd(q_ref[:, b:b + 128], q_ref[:, b + 128:b + 256], gqn, gqr, cos, sa, sb)
            qo_ref[:, b:b + 128] = on.astype(BF16)
            qo_ref[:, b + 128:b + 256] = orr.astype(BF16)
            kc, vc = _kv_cols(h)
            on, orr = _head_fwd(kv_ref[:, kc:kc + 128], zkr, gkn, gkr, cos, sa, sb)
            ko_ref[:, b:b + 128] = on.astype(BF16)
            ko_ref[:, b + 128:b + 256] = orr.astype(BF16)
            vv = kv_ref[:, vc:vc + 128]
            vo_ref[:, h * 128:(h + 1) * 128] = vv.astype(BF16)
            vt_ref[h * 128:(h + 1) * 128, :] = jnp.transpose(vv).astype(BF16)

    W = HEADS * HEAD_PAD
    row = lambda c: pl.BlockSpec((tm, c), lambda i: (i, 0))
    vec = lambda c: pl.BlockSpec((1, c), lambda i: (0, 0))
    return pl.pallas_call(
        body, grid=(S // tm,),
        in_specs=[row(W), row(W), pl.BlockSpec((tm, 128), lambda i: (i, Z_KR_CB)), vec(256), vec(256),
                  row(128), row(128), row(128)],
        out_specs=[row(W), row(W), row(HEADS * 128), pl.BlockSpec((HEADS * 128, tm), lambda i: (0, i))],
        out_shape=[jax.ShapeDtypeStruct((S, W), BF16), jax.ShapeDtypeStruct((S, W), BF16),
                   jax.ShapeDtypeStruct((S, HEADS * 128), BF16), jax.ShapeDtypeStruct((HEADS * 128, S), BF16)],
        compiler_params=_cp(("parallel",)), name=name)(qraw, kvraw, z, gq, gk, *tabs)


def _qkrope_bwd(qraw, kvraw, z, gq, gk, tabs, dq, dk, dv, *, name):
    S = qraw.shape[0]
    tm = _row_tile(S)

    def body(q_ref, kv_ref, zkr_ref, gq_ref, gk_ref, cos_ref, sa_ref, sb_ref, dq_ref, dk_ref, dv_ref,
             dqo_ref, dkvo_ref, dkr_ref, ggq_ref, ggk_ref):
        i = pl.program_id(0)

        @pl.when(i == 0)
        def _():
            ggq_ref[...] = jnp.zeros_like(ggq_ref)
            ggk_ref[...] = jnp.zeros_like(ggk_ref)

        cos, sa, sb = cos_ref[...], sa_ref[...], sb_ref[...]
        zkr = zkr_ref[...]
        gqn, gqr, gkn, gkr = gq_ref[:, :128], gq_ref[:, 128:], gk_ref[:, :128], gk_ref[:, 128:]
        dkr = jnp.zeros((tm, 128), F32)
        sq_n = jnp.zeros((1, 128), F32)
        sq_r = jnp.zeros((1, 128), F32)
        sk_n = jnp.zeros((1, 128), F32)
        sk_r = jnp.zeros((1, 128), F32)
        for h in range(HEADS):
            b = h * HEAD_PAD
            dxn, dxr, ggn, ggr = _head_bwd(q_ref[:, b:b + 128], q_ref[:, b + 128:b + 256], gqn, gqr,
                                           dq_ref[:, b:b + 128], dq_ref[:, b + 128:b + 256], cos, sa, sb)
            dqo_ref[:, b:b + 128] = dxn.astype(BF16)
            dqo_ref[:, b + 128:b + 256] = dxr.astype(BF16)
            sq_n += jnp.sum(ggn, axis=0, keepdims=True)
            sq_r += jnp.sum(ggr, axis=0, keepdims=True)
            kc, vc = _kv_cols(h)
            dxn, dxr, ggn, ggr = _head_bwd(kv_ref[:, kc:kc + 128], zkr, gkn, gkr,
                                           dk_ref[:, b:b + 128], dk_ref[:, b + 128:b + 256], cos, sa, sb)
            dkvo_ref[:, kc:kc + 128] = dxn.astype(BF16)
            dkvo_ref[:, vc:vc + 128] = dv_ref[:, h * 128:(h + 1) * 128].astype(BF16)
            dkr += dxr
            sk_n += jnp.sum(ggn, axis=0, keepdims=True)
            sk_r += jnp.sum(ggr, axis=0, keepdims=True)
        dkr_ref[...] = dkr.astype(BF16)
        ggq_ref[:, :128] += sq_n
        ggq_ref[:, 128:] += sq_r
        ggk_ref[:, :128] += sk_n
        ggk_ref[:, 128:] += sk_r

    W = HEADS * HEAD_PAD
    row = lambda c: pl.BlockSpec((tm, c), lambda i: (i, 0))
    vec = lambda c: pl.BlockSpec((1, c), lambda i: (0, 0))
    return pl.pallas_call(
        body, grid=(S // tm,),
        in_specs=[row(W), row(W), pl.BlockSpec((tm, 128), lambda i: (i, Z_KR_CB)), vec(256), vec(256),
                  row(128), row(128), row(128), row(W), row(W), row(HEADS * 128)],
        out_specs=[row(W), row(W), row(128), vec(256), vec(256)],
        out_shape=[jax.ShapeDtypeStruct((S, W), BF16), jax.ShapeDtypeStruct((S, W), BF16),
                   jax.ShapeDtypeStruct((S, 128), BF16),
                   jax.ShapeDtypeStruct((1, 256), F32), jax.ShapeDtypeStruct((1, 256), F32)],
        compiler_params=_cp(("arbitrary",)), name=name)(qraw, kvraw, z, gq, gk, *tabs, dq, dk, dv)


LOG2E = 1.4426950408889634
SCORE_SCALE = 1.0 / math.sqrt(QK)
SCORE_SCALE_LOG2 = SCORE_SCALE * LOG2E


def _fa_tile(S):
    return 512 if S % 512 == 0 and S >= 2048 else 128


def _flash_fwd(q, k, vt, *, name):
    S = q.shape[0]
    ts = _fa_tile(S)

    def body(q_ref, k_ref, vt_ref, o_ref, ob_ref, lse_ref, m_sc, l_sc, acc_sc):
        qi = pl.program_id(1)
        m_sc[...] = jnp.full_like(m_sc, -jnp.inf)
        l_sc[...] = jnp.zeros_like(l_sc)
        acc_sc[...] = jnp.zeros_like(acc_sc)
        qb = q_ref[...]

        def scores(kidx, masked):
            k0 = pl.multiple_of(kidx * ts, ts)
            st = lax.dot_general(k_ref[pl.ds(k0, ts), :], qb, _DN["nt"], preferred_element_type=F32) * SCORE_SCALE_LOG2
            if masked:
                st = jnp.where(lax.broadcasted_iota(jnp.int32, (ts, ts), 0) > lax.broadcasted_iota(jnp.int32, (ts, ts), 1),
                               -jnp.inf, st)
            return st

        def update(st, kidx):
            k0 = pl.multiple_of(kidx * ts, ts)
            m_prev = m_sc[...]
            m_new = jnp.maximum(m_prev, jnp.max(st, axis=0, keepdims=True))
            alpha = jnp.exp2(m_prev - m_new)
            pt = jnp.exp2(st - m_new[0:1, :])
            l_sc[...] = alpha * l_sc[...] + jnp.sum(pt, axis=0, keepdims=True)
            acc_sc[...] = alpha[0:1, :] * acc_sc[...] + jnp.dot(vt_ref[:, pl.ds(k0, ts)], pt.astype(BF16),
                                                                preferred_element_type=F32)
            m_sc[...] = m_new

        def pair(t, carry):
            sa, sb = scores(2 * t, False), scores(2 * t + 1, False)
            update(sa, 2 * t)
            update(sb, 2 * t + 1)
            return carry

        lax.fori_loop(0, qi // 2, pair, 0)

        @pl.when(qi % 2 == 1)
        def _():
            update(scores(qi - 1, False), qi - 1)

        update(scores(qi, True), qi)
        ot = acc_sc[...] / l_sc[0:1, :]
        o = jnp.transpose(ot)
        o_ref[...] = o
        ob_ref[...] = o.astype(BF16)
        lse_ref[...] = m_sc[...] + jnp.log2(l_sc[...])

    return pl.pallas_call(
        body, grid=(HEADS, S // ts),
        in_specs=[pl.BlockSpec((ts, HEAD_PAD), lambda h, i: (i, h)),
                  pl.BlockSpec((S, HEAD_PAD), lambda h, i: (0, h)),
                  pl.BlockSpec((128, S), lambda h, i: (h, 0))],
        out_specs=[pl.BlockSpec((ts, 128), lambda h, i: (i, h)),
                   pl.BlockSpec((ts, 128), lambda h, i: (i, h)),
                   pl.BlockSpec((None, 8, ts), lambda h, i: (h, 0, i))],
        out_shape=[jax.ShapeDtypeStruct((S, HEADS * 128), F32), jax.ShapeDtypeStruct((S, HEADS * 128), BF16),
                   jax.ShapeDtypeStruct((HEADS, 8, S), F32)],
        scratch_shapes=[pltpu.VMEM((8, ts), F32), pltpu.VMEM((8, ts), F32), pltpu.VMEM((128, ts), F32)],
        compiler_params=_cp(("parallel", "arbitrary")), name=name)(q, k, vt)


def _attn_bwd_prep(o, d_cat, *, name):
    S = o.shape[0]
    tm = _row_tile(S)
    H = HEADS * 128
    half = H // 2

    def body(o_ref, da_ref, db_ref, dob_ref, delta_ref):
        for h in range(HEADS):
            src, c0 = (da_ref, h * 128) if h * 128 < half else (db_ref, h * 128 - half)
            do = src[:, c0:c0 + 128]
            dob_ref[:, h * 128:(h + 1) * 128] = do.astype(BF16)
            prod = jnp.transpose(do * o_ref[:, h * 128:(h + 1) * 128])
            delta_ref[h] = jnp.broadcast_to(jnp.sum(prod, axis=0, keepdims=True), (8, tm))

    return pl.pallas_call(
        body, grid=(S // tm,),
        in_specs=[pl.BlockSpec((tm, H), lambda i: (i, 0)),
                  pl.BlockSpec((tm, half), lambda i: (i, 1)), pl.BlockSpec((tm, half), lambda i: (i, 2))],
        out_specs=[pl.BlockSpec((tm, H), lambda i: (i, 0)), pl.BlockSpec((HEADS, 8, tm), lambda i: (0, 0, i))],
        out_shape=[jax.ShapeDtypeStruct((S, H), BF16), jax.ShapeDtypeStruct((HEADS, 8, S), F32)],
        compiler_params=_cp(("parallel",)), name=name)(o, d_cat, d_cat)


def _flash_bwd(q, k, v, dob, lse, delta, *, name):
    S = q.shape[0]
    ts = _fa_tile(S)
    nb = S // ts

    def body(q_ref, do_ref, lse_ref, delta_ref, k_ref, v_ref, dq_ref, dk_ref, dv_ref, dk_sc, dv_sc):
        j = pl.program_id(1)

        @pl.when(j == 0)
        def _():
            dq_ref[...] = jnp.zeros_like(dq_ref)

        dk_sc[...] = jnp.zeros_like(dk_sc)
        dv_sc[...] = jnp.zeros_like(dv_sc)
        kb, vb = k_ref[...], v_ref[...]

        def step(i, masked):
            q0 = pl.multiple_of(i * ts, ts)
            qb = q_ref[pl.ds(q0, ts), :]
            dob_ = do_ref[pl.ds(q0, ts), :]
            st = lax.dot_general(kb, qb, _DN["nt"], preferred_element_type=F32) * SCORE_SCALE_LOG2
            pt = jnp.exp2(st - lse_ref[0:1, pl.ds(q0, ts)])
            if masked:
                pt = jnp.where(lax.broadcasted_iota(jnp.int32, (ts, ts), 0) > lax.broadcasted_iota(jnp.int32, (ts, ts), 1),
                               0.0, pt)
            dv_sc[...] += jnp.dot(pt.astype(BF16), dob_, preferred_element_type=F32)
            dpt = lax.dot_general(vb, dob_, _DN["nt"], preferred_element_type=F32)
            dst = (pt * (dpt - delta_ref[0:1, pl.ds(q0, ts)])).astype(BF16)
            dk_sc[...] += jnp.dot(dst, qb, preferred_element_type=F32) * SCORE_SCALE
            dq_ref[pl.ds(q0, ts), :] += lax.dot_general(dst, kb, _DN["tn"], preferred_element_type=F32) * SCORE_SCALE

        step(j, True)

        def below(i, carry):
            step(i, False)
            return carry

        lax.fori_loop(j + 1, nb, below, 0)
        dk_ref[...] = dk_sc[...]
        dv_ref[...] = dv_sc[...]

    return pl.pallas_call(
        body, grid=(HEADS, nb),
        in_specs=[pl.BlockSpec((S, HEAD_PAD), lambda h, j: (0, h)),
                  pl.BlockSpec((S, 128), lambda h, j: (0, h)),
                  pl.BlockSpec((None, 8, S), lambda h, j: (h, 0, 0)),
                  pl.BlockSpec((None, 8, S), lambda h, j: (h, 0, 0)),
                  pl.BlockSpec((ts, HEAD_PAD), lambda h, j: (j, h)),
                  pl.BlockSpec((ts, 128), lambda h, j: (j, h))],
        out_specs=[pl.BlockSpec((S, HEAD_PAD), lambda h, j: (0, h)),
                   pl.BlockSpec((ts, HEAD_PAD), lambda h, j: (j, h)),
                   pl.BlockSpec((ts, 128), lambda h, j: (j, h))],
        out_shape=[jax.ShapeDtypeStruct((S, HEADS * HEAD_PAD), F32), jax.ShapeDtypeStruct((S, HEADS * HEAD_PAD), F32),
                   jax.ShapeDtypeStruct((S, HEADS * 128), F32)],
        scratch_shapes=[pltpu.VMEM((ts, HEAD_PAD), F32), pltpu.VMEM((ts, 128), F32)],
        compiler_params=_cp(("parallel", "arbitrary")), name=name)(q, dob, lse, delta, k, v)


def _memk_fwd(mkv, gkx, *, name):
    M = mkv.shape[0]
    XW = X_HEADS * X_DIM

    def body(mkv_ref, g_ref, k_ref, v_ref):
        for h in range(X_HEADS):
            cs = slice(h * X_DIM, (h + 1) * X_DIM)
            xv = mkv_ref[:, cs]
            r = lax.rsqrt(jnp.mean(xv * xv, axis=-1, keepdims=True) + NORM_EPS)
            k_ref[:, cs] = ((xv * r) * g_ref[...]).astype(BF16)
        v_ref[...] = mkv_ref[:, XW:].astype(BF16)

    return pl.pallas_call(
        body, grid=(1,),
        in_specs=[pl.BlockSpec((M, 2 * XW), lambda i: (0, 0)), pl.BlockSpec((1, X_DIM), lambda i: (0, 0))],
        out_specs=[pl.BlockSpec((M, XW), lambda i: (0, 0)), pl.BlockSpec((M, XW), lambda i: (0, 0))],
        out_shape=[jax.ShapeDtypeStruct((M, XW), BF16), jax.ShapeDtypeStruct((M, XW), BF16)],
        compiler_params=_cp(("arbitrary",)), name=name)(mkv, gkx)


def _memk_bwd(mkv, gkx, dk, dv, *, name):
    M = mkv.shape[0]
    XW = X_HEADS * X_DIM

    def body(mkv_ref, g_ref, dk_ref, dv_ref, o_ref, gg_ref):
        gg = jnp.zeros((1, X_DIM), F32)
        for h in range(X_HEADS):
            cs = slice(h * X_DIM, (h + 1) * X_DIM)
            xv = mkv_ref[:, cs]
            r = lax.rsqrt(jnp.mean(xv * xv, axis=-1, keepdims=True) + NORM_EPS)
            n = xv * r
            dkv = dk_ref[:, cs]
            gg += jnp.sum(dkv * n, axis=0, keepdims=True)
            dn = dkv * g_ref[...]
            c = jnp.mean(dn * n, axis=-1, keepdims=True)
            o_ref[:, cs] = (r * (dn - n * c)).astype(BF16)
        o_ref[:, XW:] = dv_ref[...].astype(BF16)
        gg_ref[...] = gg

    full = lambda c: pl.BlockSpec((M, c), lambda i: (0, 0))
    return pl.pallas_call(
        body, grid=(1,),
        in_specs=[full(2 * XW), pl.BlockSpec((1, X_DIM), lambda i: (0, 0)), full(XW), full(XW)],
        out_specs=[full(2 * XW), pl.BlockSpec((1, X_DIM), lambda i: (0, 0))],
        out_shape=[jax.ShapeDtypeStruct((M, 2 * XW), BF16), jax.ShapeDtypeStruct((1, X_DIM), F32)],
        compiler_params=_cp(("arbitrary",)), name=name)(mkv, gkx, dk, dv)


def _xq_norm(z_ref, g_ref, h):
    xv = z_ref[:, h * X_DIM:(h + 1) * X_DIM]
    r = lax.rsqrt(jnp.mean(xv * xv, axis=-1, keepdims=True) + NORM_EPS)
    n = xv * r
    return n, r, n * g_ref[...]


def _xprobs(qb, k_ref, h):
    s = lax.dot_general(qb, k_ref[:, h * X_DIM:(h + 1) * X_DIM], _DN["nt"],
                        preferred_element_type=F32) * (1.0 / math.sqrt(X_DIM))
    e = jnp.exp(s - jnp.max(s, axis=-1, keepdims=True))
    return e / jnp.sum(e, axis=-1, keepdims=True)


def _memattn_fwd(z, kx, vx, gqx, *, name):
    S = z.shape[0]
    M = kx.shape[0]
    tm = _row_tile(S)
    XW = X_HEADS * X_DIM

    def body(z_ref, k_ref, v_ref, g_ref, o_ref):
        for h in range(X_HEADS):
            cs = slice(h * X_DIM, (h + 1) * X_DIM)
            _, _, qn = _xq_norm(z_ref, g_ref, h)
            p = _xprobs(qn.astype(BF16), k_ref, h)
            o_ref[:, cs] = jnp.dot(p.astype(BF16), v_ref[:, cs], preferred_element_type=F32).astype(BF16)

    return pl.pallas_call(
        body, grid=(S // tm,),
        in_specs=[pl.BlockSpec((tm, XW), lambda i: (i, Z_MQ_CB)), pl.BlockSpec((M, XW), lambda i: (0, 0)),
                  pl.BlockSpec((M, XW), lambda i: (0, 0)), pl.BlockSpec((1, X_DIM), lambda i: (0, 0))],
        out_specs=pl.BlockSpec((tm, XW), lambda i: (i, 0)),
        out_shape=jax.ShapeDtypeStruct((S, XW), BF16),
        compiler_params=_cp(("parallel",)), name=name)(z, kx, vx, gqx)


def _memattn_bwd(z, kx, vx, gqx, d_cat, *, name):
    S = z.shape[0]
    M = kx.shape[0]
    tm = _row_tile(S)
    XW = X_HEADS * X_DIM
    scale = 1.0 / math.sqrt(X_DIM)

    def body(z_ref, k_ref, v_ref, g_ref, do_ref, dz_ref, dk_ref, dv_ref, gg_ref):
        i = pl.program_id(0)

        @pl.when(i == 0)
        def _():
            dk_ref[...] = jnp.zeros_like(dk_ref)
            dv_ref[...] = jnp.zeros_like(dv_ref)
            gg_ref[...] = jnp.zeros_like(gg_ref)

        gg = jnp.zeros((1, X_DIM), F32)
        for h in range(X_HEADS):
            cs = slice(h * X_DIM, (h + 1) * X_DIM)
            n, r, qn = _xq_norm(z_ref, g_ref, h)
            qb = qn.astype(BF16)
            p = _xprobs(qb, k_ref, h)
            pb = p.astype(BF16)
            dob = do_ref[:, cs].astype(BF16)
            dv_ref[:, cs] += lax.dot_general(pb, dob, _DN["tn"], preferred_element_type=F32)
            dp = lax.dot_general(dob, v_ref[:, cs], _DN["nt"], preferred_element_type=F32)
            ds = (p * (dp - jnp.sum(dp * p, axis=-1, keepdims=True))).astype(BF16)
            dk_ref[:, cs] += lax.dot_general(ds, qb, _DN["tn"], preferred_element_type=F32) * scale
            dqn = jnp.dot(ds, k_ref[:, cs], preferred_element_type=F32) * scale
            gg += jnp.sum(dqn * n, axis=0, keepdims=True)
            dn = dqn * g_ref[...]
            c = jnp.mean(dn * n, axis=-1, keepdims=True)
            dz_ref[:, cs] = (r * (dn - n * c)).astype(BF16)
        gg_ref[...] += gg

    full = pl.BlockSpec((M, XW), lambda i: (0, 0))
    vec = pl.BlockSpec((1, X_DIM), lambda i: (0, 0))
    return pl.pallas_call(
        body, grid=(S // tm,),
        in_specs=[pl.BlockSpec((tm, XW), lambda i: (i, Z_MQ_CB)), full, full, vec,
                  pl.BlockSpec((tm, XW), lambda i: (i, 3))],
        out_specs=[pl.BlockSpec((tm, XW), lambda i: (i, 0)), full, full, vec],
        out_shape=[jax.ShapeDtypeStruct((S, XW), BF16), jax.ShapeDtypeStruct((M, XW), F32),
                   jax.ShapeDtypeStruct((M, XW), F32), jax.ShapeDtypeStruct((1, X_DIM), F32)],
        compiler_params=_cp(("arbitrary",)), name=name)(z, kx, vx, gqx, d_cat)


def _silu_parts(x):
    h = 0.5 * x
    return h, jnp.tanh(h)


GLU_HALO = 16


def _glu_tiles(S, F):
    return _big_row_tile(S), _pick(F, (1408, 512, 256, 128))


def _glu_fwd(g, u, conv_w, conv_b, *, name):
    S, F = g.shape
    tm, tc = _glu_tiles(S, F)
    hb = tm // GLU_HALO

    def body(g_ref, gp_ref, u_ref, w_ref, b_ref, a_ref):
        i = pl.program_id(1)
        gt = g_ref[...].astype(F32)
        ext = jnp.concatenate([gp_ref[...].astype(F32) * (i > 0).astype(F32), gt], axis=0)
        gc = b_ref[...] + w_ref[0:1, :] * pltpu.roll(ext, 2, 0)[GLU_HALO:]
        gc = gc + w_ref[1:2, :] * pltpu.roll(ext, 1, 0)[GLU_HALO:]
        gc = gc + w_ref[2:3, :] * gt
        h, t = _silu_parts(gc)
        a_ref[...] = ((h * (1.0 + t)) * u_ref[...].astype(F32)).astype(BF16)

    return pl.pallas_call(
        body, grid=(F // tc, S // tm),
        in_specs=[pl.BlockSpec((tm, tc), lambda j, i: (i, j)),
                  pl.BlockSpec((GLU_HALO, tc), lambda j, i: (jnp.maximum(i * hb - 1, 0), j)),
                  pl.BlockSpec((tm, tc), lambda j, i: (i, j)),
                  pl.BlockSpec((3, tc), lambda j, i: (0, j)),
                  pl.BlockSpec((1, tc), lambda j, i: (0, j))],
        out_specs=pl.BlockSpec((tm, tc), lambda j, i: (i, j)),
        out_shape=jax.ShapeDtypeStruct((S, F), BF16),
        compiler_params=_cp(("parallel", "parallel")), name=name)(g, g, u, conv_w, conv_b)


def _glu_bwd(g, u, d_a, conv_w, conv_b, *, name):
    S, F = g.shape
    tm, tc = _glu_tiles(S, F)
    hb = tm // GLU_HALO
    nt = S // tm
    E = tm + GLU_HALO

    def body(g_ref, gp_ref, gn_ref, u_ref, un_ref, da_ref, dan_ref, w_ref, b_ref,
             dg_ref, du_ref, gw_ref, gb_ref):
        i = pl.program_id(1)

        @pl.when(i == 0)
        def _():
            gw_ref[...] = jnp.zeros_like(gw_ref)
            gb_ref[...] = jnp.zeros_like(gb_ref)

        w0, w1, w2 = w_ref[0:1, :], w_ref[1:2, :], w_ref[2:3, :]
        gext = jnp.concatenate([gp_ref[...].astype(F32) * (i > 0).astype(F32), g_ref[...].astype(F32),
                                gn_ref[...].astype(F32)], axis=0)
        g1 = pltpu.roll(gext, 1, 0)[GLU_HALO:]
        g2 = pltpu.roll(gext, 2, 0)[GLU_HALO:]
        g0 = gext[GLU_HALO:]
        gc = b_ref[...] + w0 * g2
        gc = gc + w1 * g1
        gc = gc + w2 * g0
        h, t = _silu_parts(gc)
        t1 = 1.0 + t
        da = jnp.concatenate([da_ref[...], dan_ref[...] * (i < nt - 1).astype(F32)], axis=0)
        uu = jnp.concatenate([u_ref[...].astype(F32), un_ref[...].astype(F32)], axis=0)
        du_ref[...] = (da[:tm] * (h[:tm] * t1[:tm])).astype(BF16)
        dgc = (da * uu) * (0.5 * (t1 + h * (1.0 - t * t)))
        dg = w2 * dgc[:tm] + w1 * pltpu.roll(dgc, E - 1, 0)[:tm] + w0 * pltpu.roll(dgc, E - 2, 0)[:tm]
        dg_ref[...] = dg.astype(BF16)
        dgt = dgc[:tm]
        gb_ref[...] += jnp.sum(dgt, axis=0, keepdims=True)
        gw_ref[0:1, :] += jnp.sum(dgt * g2[:tm], axis=0, keepdims=True)
        gw_ref[1:2, :] += jnp.sum(dgt * g1[:tm], axis=0, keepdims=True)
        gw_ref[2:3, :] += jnp.sum(dgt * g0[:tm], axis=0, keepdims=True)

    tile = pl.BlockSpec((tm, tc), lambda j, i: (i, j))
    nxt = pl.BlockSpec((GLU_HALO, tc), lambda j, i: (jnp.minimum((i + 1) * hb, S // GLU_HALO - 1), j))
    prv = pl.BlockSpec((GLU_HALO, tc), lambda j, i: (jnp.maximum(i * hb - 1, 0), j))
    return pl.pallas_call(
        body, grid=(F // tc, nt),
        in_specs=[tile, prv, nxt, tile, nxt, tile, nxt,
                  pl.BlockSpec((3, tc), lambda j, i: (0, j)), pl.BlockSpec((1, tc), lambda j, i: (0, j))],
        out_specs=[tile, tile, pl.BlockSpec((3, tc), lambda j, i: (0, j)), pl.BlockSpec((1, tc), lambda j, i: (0, j))],
        out_shape=[jax.ShapeDtypeStruct((S, F), BF16), jax.ShapeDtypeStruct((S, F), BF16),
                   jax.ShapeDtypeStruct((3, F), F32), jax.ShapeDtypeStruct((1, F), F32)],
        compiler_params=_cp(("parallel", "arbitrary")), name=name)(g, g, g, u, u, d_a, d_a, conv_w, conv_b)


def _loss_head(y, target, *, name):
    S, D = y.shape
    tm = _big_row_tile(S)
    nt = S // tm

    def body(y_ref, t_ref, dy_ref, dyb_ref, loss_ref, acc):
        i = pl.program_id(0)

        @pl.when(i == 0)
        def _():
            acc[...] = jnp.zeros_like(acc)

        e = y_ref[...] - t_ref[...]
        dy = e * (1.0 / D)
        dy_ref[...] = dy
        dyb_ref[...] = dy.astype(BF16)
        acc[...] += jnp.sum(e * e, axis=0, keepdims=True)

        @pl.when(i == nt - 1)
        def _():
            loss_ref[...] = jnp.broadcast_to(jnp.sum(acc[...], axis=1, keepdims=True) * (0.5 / D), (1, 128))

    row = pl.BlockSpec((tm, D), lambda i: (i, 0))
    return pl.pallas_call(
        body, grid=(nt,), in_specs=[row, row],
        out_specs=[row, row, pl.BlockSpec((1, 128), lambda i: (0, 0))],
        out_shape=[jax.ShapeDtypeStruct((S, D), F32), jax.ShapeDtypeStruct((S, D), BF16),
                   jax.ShapeDtypeStruct((1, 128), F32)],
        scratch_shapes=[pltpu.VMEM((1, D), F32)],
        compiler_params=_cp(("arbitrary",)), name=name)(y, target)


def _adamw_math(w, g, m, v):
    m = ADAM_B1 * m + (1.0 - ADAM_B1) * g
    v = ADAM_B2 * v + (1.0 - ADAM_B2) * (g * g)
    m_hat = m / (1.0 - ADAM_B1 ** ADAM_STEP)
    v_hat = v / (1.0 - ADAM_B2 ** ADAM_STEP)
    delta = -ADAM_LR * (m_hat / (jnp.sqrt(v_hat) + ADAM_EPS) + ADAM_WD * w)
    return delta, m, v


def _adamw(w, m, v, parts, *, name):
    R, C = w.shape
    tr = 128 if R % 128 == 0 else R
    n_parts = len(parts)

    def body(*refs):
        w_ref, m_ref, v_ref = refs[:3]
        p_refs = refs[3:3 + n_parts]
        g_ref, d_ref, mo_ref, vo_ref = refs[3 + n_parts:]
        g = p_refs[0][...]
        for p in p_refs[1:]:
            g = g + p[...]
        delta, mn, vn = _adamw_math(w_ref[...], g, m_ref[...], v_ref[...])
        g_ref[...] = g
        d_ref[...] = delta
        mo_ref[...] = mn
        vo_ref[...] = vn

    blk = pl.BlockSpec((tr, C), lambda i: (i, 0))
    return pl.pallas_call(
        body, grid=(R // tr,), in_specs=[blk] * (3 + n_parts), out_specs=[blk] * 4,
        out_shape=[jax.ShapeDtypeStruct((R, C), F32)] * 4,
        compiler_params=_cp(("parallel",)), name=name)(w, m, v, *parts)


def _sum4(g_stack, recv, me, *, name):
    _, R, C = g_stack.shape
    tr = 128 if R % 128 == 0 else R

    def body(me_ref, g_ref, r_ref, o_ref):
        acc = g_ref[...]
        for j in range(N_CHIPS - 1):
            acc = acc + r_ref[j].astype(F32)
        o_ref[...] = acc

    grid_spec = pltpu.PrefetchScalarGridSpec(
        num_scalar_prefetch=1, grid=(R // tr,),
        in_specs=[pl.BlockSpec((None, tr, C), lambda i, me_ref: (me_ref[0], i, 0)),
                  pl.BlockSpec((N_CHIPS - 1, tr, C), lambda i, me_ref: (0, i, 0))],
        out_specs=pl.BlockSpec((tr, C), lambda i, me_ref: (i, 0)))
    return pl.pallas_call(
        body, grid_spec=grid_spec, out_shape=jax.ShapeDtypeStruct((R, C), F32),
        compiler_params=_cp(("parallel",)), name=name)(me, g_stack, recv)


def _sum8(gathered, *, name):
    _, R, C = gathered.shape

    def body(g_ref, o_ref):
        acc = g_ref[0]
        for d in range(1, N_DEV):
            acc = acc + g_ref[d]
        o_ref[...] = acc

    return pl.pallas_call(
        body, grid=(1,), in_specs=[pl.BlockSpec((N_DEV, R, C), lambda i: (0, 0, 0))],
        out_specs=pl.BlockSpec((R, C), lambda i: (0, 0)),
        out_shape=jax.ShapeDtypeStruct((R, C), F32),
        compiler_params=_cp(("arbitrary",)), name=name)(gathered)


def _place():
    return lax.axis_index("x"), lax.axis_index("y"), lax.axis_index("c")


def _other_chips(x, y):
    return [(1 - x, y), (x, 1 - y), (1 - x, 1 - y)]


_ANY = pl.BlockSpec(memory_space=pl.ANY)


_HBM = pl.BlockSpec(memory_space=pltpu.HBM)
_SEM = pl.BlockSpec(memory_space=pltpu.SEMAPHORE)
_EFFECT = pltpu.SideEffectType.DATAFLOW_SIDE_EFFECTING


def _gather_copies(srcs, lands, send_sems, recv_sems):
    x, y, c = _place()
    me = 2 * x + y
    return [pltpu.make_async_remote_copy(
        src_ref=srcs[i], dst_ref=lands[i].at[me], send_sem=send_sems.at[3 * i + j],
        recv_sem=recv_sems.at[3 * i + j], device_id=(px, py, c), device_id_type=MESH)
        for i in range(len(srcs)) for j, (px, py) in enumerate(_other_chips(x, y))]


def _scatter_copies(srcs, lands, send_sems, recv_sems):
    x, y, c = _place()
    return [pltpu.make_async_remote_copy(
        src_ref=srcs[i].at[2 * px + py], dst_ref=lands[i].at[j], send_sem=send_sems.at[3 * i + j],
        recv_sem=recv_sems.at[3 * i + j], device_id=(px, py, c), device_id_type=MESH)
        for i in range(len(srcs)) for j, (px, py) in enumerate(_other_chips(x, y))]


def _copies_start(srcs, lands, make_copies, thru=(), *, name):
    n = len(srcs)
    n_ops = 2 * n + len(thru)

    def body(*refs):
        send_sems, recv_sems = refs[n_ops], refs[n_ops + 1]
        for cp in make_copies(refs[:n], refs[n:2 * n], send_sems, recv_sems):
            cp.start()
        refs[-1][...] = jnp.zeros_like(refs[-1])

    ops = list(srcs) + list(lands) + list(thru)
    outs = pl.pallas_call(
        body, name=name,
        out_shape=(pltpu.SemaphoreType.DMA((3 * n,)), pltpu.SemaphoreType.DMA((3 * n,)),
                   *[pltpu.HBM(a.shape, a.dtype) for a in ops], jax.ShapeDtypeStruct((8, 128), F32)),
        in_specs=[_HBM] * n_ops,
        out_specs=(_SEM, _SEM, *[_HBM] * n_ops, pl.BlockSpec(memory_space=pltpu.VMEM)),
        input_output_aliases={i: 2 + i for i in range(n_ops)},
        compiler_params=pltpu.CompilerParams(has_side_effects=_EFFECT),
    )(*[pltpu.with_memory_space_constraint(a, pltpu.HBM) for a in ops])
    return outs[0], outs[1], list(outs[2:2 + n]), list(outs[2 + n:2 + 2 * n]), outs[-1], list(outs[2 + 2 * n:-1])


def _copies_wait(handle, after, make_copies, *, name):
    send_sems, recv_sems, srcs, lands = handle[:4]
    n = len(srcs)

    def body(*refs):
        for cp in make_copies(refs[:n], refs[n:2 * n], refs[2 * n], refs[2 * n + 1]):
            cp.wait_send()
            cp.wait_recv()

    ops = list(srcs) + list(lands)
    outs = pl.pallas_call(
        body, name=name,
        out_shape=tuple(pltpu.HBM(a.shape, a.dtype) for a in ops),
        in_specs=[_HBM] * (2 * n) + [_SEM, _SEM, _ANY],
        out_specs=tuple([_HBM] * (2 * n)),
        input_output_aliases={i: i for i in range(2 * n)},
        compiler_params=pltpu.CompilerParams(has_side_effects=_EFFECT),
    )(*ops, send_sems, recv_sems, after)
    return list(outs[n:])


def _swap_with_sibling(arrs, *, name):
    n = len(arrs)

    def body(*refs):
        ins, outs = refs[:n], refs[n:2 * n]
        send_sems, recv_sems = refs[2 * n:]
        x, y, c = _place()
        remote = []
        for i in range(n):
            rc = pltpu.make_async_remote_copy(
                src_ref=ins[i], dst_ref=outs[i], send_sem=send_sems.at[i], recv_sem=recv_sems.at[i],
                device_id=(x, y, 1 - c), device_id_type=MESH)
            rc.start()
            remote.append(rc)
        for rc in remote:
            rc.wait_send()
        for rc in remote:
            rc.wait_recv()

    return pl.pallas_call(
        body, in_specs=[_ANY] * n, out_specs=[_ANY] * n,
        out_shape=[jax.ShapeDtypeStruct(a.shape, a.dtype) for a in arrs],
        scratch_shapes=[pltpu.SemaphoreType.DMA((n,)), pltpu.SemaphoreType.DMA((n,))],
        name=name)(*arrs)


def _gather_all(buf, *, name):
    R, C = buf.shape

    def body(in_ref, out_ref, send_sems, recv_sems, local_sem):
        x, y, c = _place()
        me = 4 * x + 2 * y + c
        lc = pltpu.make_async_copy(in_ref, out_ref.at[me], local_sem)
        lc.start()
        remote = []
        for k in range(1, N_DEV):
            px = 1 - x if (k >> 2) & 1 else x
            py = 1 - y if (k >> 1) & 1 else y
            pc = 1 - c if k & 1 else c
            rc = pltpu.make_async_remote_copy(
                src_ref=in_ref, dst_ref=out_ref.at[me], send_sem=send_sems.at[k - 1],
                recv_sem=recv_sems.at[k - 1], device_id=(px, py, pc), device_id_type=MESH)
            rc.start()
            remote.append(rc)
        lc.wait()
        for rc in remote:
            rc.wait_send()
        for rc in remote:
            rc.wait_recv()

    return pl.pallas_call(
        body, in_specs=[_ANY], out_specs=_ANY,
        out_shape=jax.ShapeDtypeStruct((N_DEV, R, C), buf.dtype),
        scratch_shapes=[pltpu.SemaphoreType.DMA((N_DEV - 1,)), pltpu.SemaphoreType.DMA((N_DEV - 1,)),
                        pltpu.SemaphoreType.DMA],
        name=name)(buf)


def _w_in_to_z(w):
    pad = jnp.zeros(w.shape[:-1] + (64,), w.dtype)
    return jnp.concatenate([w[..., 0:512], w[..., 512:1024], w[..., 1344:1856], w[..., 1024:1280],
                            w[..., 1280:1344], pad], axis=-1)


def _z_to_w_in(g):
    return jnp.concatenate([g[..., 0:512], g[..., 512:1024], g[..., 1536:1792], g[..., 1792:1856],
                            g[..., 1024:1536]], axis=-1)


def _pad_heads(w, nh):
    w = w.reshape(w.shape[:-1] + (nh, QK))
    w = jnp.concatenate([w, jnp.zeros(w.shape[:-1] + (HEAD_PAD - QK,), w.dtype)], axis=-1)
    return w.reshape(w.shape[:-2] + (nh * HEAD_PAD,))


def _unpad_heads(g, nh):
    g = g.reshape(g.shape[:-1] + (nh, HEAD_PAD))[..., :QK]
    return g.reshape(g.shape[:-2] + (nh * QK,))


def _kv_split(w, nh):
    w = w.reshape(w.shape[:-1] + (nh, 2, 128))
    return jnp.swapaxes(w, -3, -2).reshape(w.shape[:-3] + (nh * 256,))


def _kv_join(g, nh):
    g = g.reshape(g.shape[:-1] + (2, nh, 128))
    return jnp.swapaxes(g, -3, -2).reshape(g.shape[:-3] + (nh * 256,))


def _pad_gain(g):
    return jnp.concatenate([g, jnp.zeros((1, HEAD_PAD - QK), g.dtype)], axis=1)


_SMALL = ("g_mix", "g_q_lat", "g_kv_lat", "g_q_mla", "g_k_mla", "w_pool", "pool_scale", "g_mem", "g_q_x",
          "g_k_x", "g_ffn", "conv_b", "conv_w")


def _pack(arrs, extra=0):
    flat = jnp.concatenate([a.reshape(-1) for a in arrs])
    n = flat.shape[0] + extra
    rows = -(-n // 1024) * 8
    return jnp.pad(flat, (0, rows * 128 - flat.shape[0])).reshape(rows, 128)


def _unpack(buf, shapes):
    flat = buf.reshape(-1)
    out, off = [], 0
    for s in shapes:
        n = int(np.prod(s))
        out.append(flat[off:off + n].reshape(s))
        off += n
    return out, off


def _tied(a, token):
    return a + token[:1, :1].astype(a.dtype)


def _local_step(x, mem, target, W, fetch=None, ship=None):
    fetch = fetch or (lambda group, after: None)
    ship = ship or (lambda group, G: jnp.zeros((8, 128), F32))
    S, D = x.shape
    F = W["conv_b"].shape[1]
    tabs = _rope_tables(S)
    tm = 512 if S % 512 == 0 else 128
    tl = 1024 if S % 1024 == 0 else tm
    tk = _pick(S, (1024, 512, 128))

    h = _rms_fwd(x, W["g_mix"], C=D, name="norm_mix")
    fetch("g1", h)
    z = mm_nn(h, W["w_in"], tm=tl, tn=Z_COLS, tk=D, name="z_proj")
    fetch("g2", z)
    y_pool = _pool_fwd(z, W["w_pool"], W["pool_scale"], name="pool_fwd")
    ql = _rms_fwd(z, W["g_q_lat"], C=Q_RANK, cb=Z_Q_CB, name="norm_qlat")
    kvl = _rms_fwd(z, W["g_kv_lat"], C=KV_RANK, cb=Z_KV_CB, name="norm_kvlat")
    qraw = mm_nn(ql, W["w_q_up"], nsh=N_CHIPS, tm=tl, tn=512, tk=Q_RANK, name="q_up")
    kvraw = mm_nn(kvl, W["w_kv_up"], nsh=N_CHIPS, tm=tl, tn=512, tk=KV_RANK, name="kv_up")
    q, k, v, vt = _qkrope_fwd(qraw, kvraw, z, W["g_q_mla"], W["g_k_mla"], tabs, name="qk_norm_rope")
    o, y_mla, lse = _flash_fwd(q, k, vt, name="mla_fwd")
    memn = _rms_fwd(mem, W["g_mem"], C=D, name="norm_mem")
    M = mem.shape[0]
    mkv = mm_nn(memn, W["w_mem_kv"], tm=M, tn=1024, tk=D, name="mem_kv")
    kx, vx = _memk_fwd(mkv, W["g_k_x"], name="memk_fwd")
    y_mem = _memattn_fwd(z, kx, vx, W["g_q_x"], name="memattn_fwd")
    cat = jnp.concatenate([y_pool, y_mla, y_mem], axis=1)
    x2 = mm_nn(cat, W["w_o"], tm=tm, tn=D, tk=D, add=x, name="o_proj")
    h2 = _rms_fwd(x2, W["g_ffn"], C=D, name="norm_ffn")
    fetch("g3", h2)
    fn = F // N_CHIPS
    g = mm_nn(h2, W["w_gate"], nsh=N_CHIPS, tm=tl, tn=fn, tk=D, out_dtypes=(BF16,), name="gate_proj")
    u = mm_nn(h2, W["w_up"], nsh=N_CHIPS, tm=tl, tn=fn, tk=D, out_dtypes=(BF16,), name="up_proj")
    a = _glu_fwd(g, u, W["conv_w"], W["conv_b"], name="glu_fwd")
    y = mm_nn(a, W["w_down"], tm=tm, tn=D, tk=fn, add=x2, name="down_proj")
    dy, dyb, loss_row = _loss_head(y, target, name="loss_head")

    G = {}
    d_a = mm_nt(dyb, W["w_down"], tm=tl, to=512, tc=D, name="d_a")
    G["w_down"] = mm_tn(a, dyb, to=fn, tn=1024, tk=tk, out_dtypes=(F32, BF16), name="grad_w_down")
    d_g, d_u, G["conv_w"], G["conv_b"] = _glu_bwd(g, u, d_a, W["conv_w"], W["conv_b"], name="glu_bwd")
    G["w_gate"] = mm_tn(h2, d_g, nsh=N_CHIPS, to=1024, tn=fn, tk=tk, out_dtypes=(F32, BF16), name="grad_w_gate")
    G["w_up"] = mm_tn(h2, d_u, nsh=N_CHIPS, to=1024, tn=fn, tk=tk, out_dtypes=(F32, BF16), name="grad_w_up")
    d_h2 = mm_nt(d_g, W["w_gate"], nsh=N_CHIPS, tm=tm, to=D, tc=fn, name="d_h2_gate")
    tok = ship("s1", G)
    d_h2 = mm_nt(d_u, W["w_up"], nsh=N_CHIPS, tm=tm, to=D, tc=fn, add=d_h2, name="d_h2_up")
    d_x2, d_x2b, G["g_ffn"] = _rms_bwd(x2, d_h2, _tied(W["g_ffn"], tok), C=D, res=dy, out_dtypes=(F32, BF16),
                                       name="norm_ffn_bwd")

    d_cat = mm_nt(d_x2b, W["w_o"], tm=tl, to=1024, tc=D, name="d_cat")
    G["w_o"] = mm_tn(cat, d_x2b, to=1024, tn=1024, tk=tk, out_dtypes=(F32, BF16), name="grad_w_o")
    tok = ship("s2", G)
    dz_pool, G["w_pool"], G["pool_scale"] = _pool_bwd(z, d_cat, W["w_pool"], _tied(W["pool_scale"], tok),
                                                      name="pool_bwd")
    dz_mq, dkx, dvx, G["g_q_x"] = _memattn_bwd(z, kx, vx, W["g_q_x"], d_cat, name="memattn_bwd")
    d_mkv, G["g_k_x"] = _memk_bwd(mkv, W["g_k_x"], dkx, dvx, name="memk_bwd")
    G["w_mem_kv"] = mm_tn(memn, d_mkv, to=1024, tn=1024, tk=M, out_dtypes=(F32, BF16), name="grad_w_mem_kv")
    d_memn = mm_nt(d_mkv, W["w_mem_kv"], tm=M, to=D, tc=1024, name="d_memn")
    _, G["g_mem"] = _rms_bwd(mem, d_memn, W["g_mem"], C=D, name="norm_mem_bwd")
    dob, delta = _attn_bwd_prep(o, d_cat, name="mla_bwd_prep")
    dq, dk, dv = _flash_bwd(q, k, v, dob, lse, delta, name="mla_bwd")
    d_qraw, d_kvraw, dz_kr, G["g_q_mla"], G["g_k_mla"] = _qkrope_bwd(
        qraw, kvraw, z, W["g_q_mla"], W["g_k_mla"], tabs, dq, dk, dv, name="qk_norm_rope_bwd")
    G["w_q_up"] = mm_tn(ql, d_qraw, nsh=N_CHIPS, to=Q_RANK, tn=512, tk=tk, out_dtypes=(F32, BF16), name="grad_w_q_up")
    d_ql = mm_nt(d_qraw, W["w_q_up"], nsh=N_CHIPS, tm=tl, to=Q_RANK, tc=512, name="d_ql")
    G["w_kv_up"] = mm_tn(kvl, d_kvraw, nsh=N_CHIPS, to=KV_RANK, tn=512, tk=tk, out_dtypes=(F32, BF16),
                         name="grad_w_kv_up")
    d_kvl = mm_nt(d_kvraw, W["w_kv_up"], nsh=N_CHIPS, tm=tl, to=KV_RANK, tc=512, name="d_kvl")
    dz_q, G["g_q_lat"] = _rms_bwd(z, d_ql, W["g_q_lat"], C=Q_RANK, cb=Z_Q_CB, out_dtypes=(BF16,), name="norm_qlat_bwd")
    dz_kv, G["g_kv_lat"] = _rms_bwd(z, d_kvl, W["g_kv_lat"], C=KV_RANK, cb=Z_KV_CB, out_dtypes=(BF16,),
                                    name="norm_kvlat_bwd")
    d_z = jnp.concatenate([dz_pool, dz_q, dz_mq, dz_kv, dz_kr], axis=1)
    G["w_in"] = mm_tn(h, d_z, to=512, tn=Z_COLS, tk=tk, out_dtypes=(F32, BF16), name="grad_w_in")
    tok = ship("s3", G)
    d_h = mm_nt(d_z, W["w_in"], tm=tl, to=1024, tc=Z_COLS, name="d_h")
    grad_x, G["g_mix"] = _rms_bwd(x, d_h, _tied(W["g_mix"], tok), C=D, res=d_x2, name="norm_mix_bwd")
    return loss_row, grad_x, G


_BIG = ("w_in", "w_q_up", "w_kv_up", "w_mem_kv", "w_o", "w_gate", "w_up", "w_down")
_WEIGHTS = ("g_mix", "w_in", "g_q_lat", "w_q_up", "g_kv_lat", "w_kv_up", "g_q_mla", "g_k_mla", "w_pool",
            "pool_scale", "g_mem", "w_mem_kv", "g_q_x", "g_k_x", "w_o", "g_ffn", "w_gate", "w_up", "conv_w",
            "conv_b", "w_down")


def _to_compute_layout(name, w):
    if name == "w_in":
        return _w_in_to_z(w)
    if name == "w_q_up":
        return _pad_heads(w, w.shape[-1] // QK)
    if name == "w_kv_up":
        return _kv_split(w, w.shape[-1] // 256)
    return w


def _from_compute_layout(name, g):
    if name == "w_in":
        return _z_to_w_in(g)
    if name == "w_q_up":
        return _unpad_heads(g, g.shape[-1] // HEAD_PAD)
    if name == "w_kv_up":
        return _kv_join(g, g.shape[-1] // 256)
    return g


def kernel(x, mem, g_mix, w_in, g_q_lat, w_q_up, g_kv_lat, w_kv_up, g_q_mla, g_k_mla, w_pool, pool_scale, g_mem, w_mem_kv, g_q_x, g_k_x, w_o, g_ffn, w_gate, w_up, conv_w, conv_b, w_down, loss_target, m_g_mix, m_w_in, m_g_q_lat, m_w_q_up, m_g_kv_lat, m_w_kv_up, m_g_q_mla, m_g_k_mla, m_w_pool, m_pool_scale, m_g_mem, m_w_mem_kv, m_g_q_x, m_g_k_x, m_w_o, m_g_ffn, m_w_gate, m_w_up, m_conv_w, m_conv_b, m_w_down, v_g_mix, v_w_in, v_g_q_lat, v_w_q_up, v_g_kv_lat, v_w_kv_up, v_g_q_mla, v_g_k_mla, v_w_pool, v_pool_scale, v_g_mem, v_w_mem_kv, v_g_q_x, v_g_k_x, v_w_o, v_g_ffn, v_w_gate, v_w_up, v_conv_w, v_conv_b, v_w_down):
    P = dict(g_mix=g_mix, w_in=w_in, g_q_lat=g_q_lat, w_q_up=w_q_up, g_kv_lat=g_kv_lat, w_kv_up=w_kv_up,
             g_q_mla=g_q_mla, g_k_mla=g_k_mla, w_pool=w_pool, pool_scale=pool_scale, g_mem=g_mem,
             w_mem_kv=w_mem_kv, g_q_x=g_q_x, g_k_x=g_k_x, w_o=w_o, g_ffn=g_ffn, w_gate=w_gate, w_up=w_up,
             conv_w=conv_w, conv_b=conv_b, w_down=w_down)
    Mo = dict(g_mix=m_g_mix, w_in=m_w_in, g_q_lat=m_g_q_lat, w_q_up=m_w_q_up, g_kv_lat=m_g_kv_lat,
              w_kv_up=m_w_kv_up, g_q_mla=m_g_q_mla, g_k_mla=m_g_k_mla, w_pool=m_w_pool,
              pool_scale=m_pool_scale, g_mem=m_g_mem, w_mem_kv=m_w_mem_kv, g_q_x=m_g_q_x, g_k_x=m_g_k_x,
              w_o=m_w_o, g_ffn=m_g_ffn, w_gate=m_w_gate, w_up=m_w_up, conv_w=m_conv_w, conv_b=m_conv_b,
              w_down=m_w_down)
    Vo = dict(g_mix=v_g_mix, w_in=v_w_in, g_q_lat=v_g_q_lat, w_q_up=v_w_q_up, g_kv_lat=v_g_kv_lat,
              w_kv_up=v_w_kv_up, g_q_mla=v_g_q_mla, g_k_mla=v_g_k_mla, w_pool=v_w_pool,
              pool_scale=v_pool_scale, g_mem=v_g_mem, w_mem_kv=v_w_mem_kv, g_q_x=v_g_q_x, g_k_x=v_g_k_x,
              w_o=v_w_o, g_ffn=v_g_ffn, w_gate=v_w_gate, w_up=v_w_up, conv_w=v_conv_w, conv_b=v_conv_b,
              w_down=v_w_down)
    xi, yi, ci = _place()
    me = (2 * xi + yi).astype(jnp.int32).reshape(1)

    shard = {n: _to_compute_layout(n, P[n][0]).astype(BF16) for n in _BIG}
    shard["conv_w"] = conv_w[0]
    gather_groups = {"g1": ("w_in",), "g2": ("w_q_up", "w_kv_up", "w_mem_kv", "w_o"),
                     "g3": ("w_gate", "w_up", "w_down", "conv_w")}
    gathers = {}

    def start_gather(grp, thru=()):
        names = gather_groups[grp]
        lands = [lax.dynamic_update_slice(lax.empty((N_CHIPS,) + shard[n].shape, shard[n].dtype), shard[n][None],
                                          (me[0], 0, 0)) for n in names]
        gathers[grp] = _copies_start([shard[n] for n in names], lands, _gather_copies, thru,
                                     name="gather_start_" + grp)
        return gathers[grp][5]

    start_gather("g1")
    W = {}
    W["g_q_mla"], W["g_k_mla"] = _pad_gain(g_q_mla), _pad_gain(g_k_mla)
    W["w_pool"] = w_pool[0].astype(BF16)
    for n in ("g_mix", "g_q_lat", "g_kv_lat", "pool_scale", "g_mem", "g_q_x", "g_k_x", "g_ffn", "conv_b"):
        W[n] = P[n]
    W["g_mix"] = _tied(W["g_mix"], gathers["g1"][4])

    def fetch(grp, after):
        stacks = _copies_wait(gathers[grp], after, _gather_copies, name="gather_wait_" + grp)
        if grp == "g1":
            stacks = start_gather("g3", start_gather("g2", stacks))
        for n, s in zip(gather_groups[grp], stacks):
            if n == "conv_w":
                W[n] = jnp.swapaxes(s, 0, 1).reshape(3, -1)
            else:
                W[n] = s.reshape(-1, s.shape[-1])

    shard_shape = {n: shard[n].shape for n in _BIG}
    scatter_groups = {"s1": ("w_down", "w_gate", "w_up"), "s2": ("w_o",),
                      "s3": ("w_mem_kv", "w_q_up", "w_kv_up", "w_in")}
    scatters = {}

    def ship(grp, G):
        names = scatter_groups[grp]
        srcs = [G[n][1].reshape((N_CHIPS,) + shard_shape[n]) for n in names]
        lands = [lax.empty((N_CHIPS - 1,) + shard_shape[n], BF16) for n in names]
        scatters[grp] = _copies_start(srcs, lands, _scatter_copies, name="scatter_start_" + grp)
        return scatters[grp][4]

    loss_row, grad_x, G = _local_step(x[0], mem[0], loss_target[0], W, fetch, ship)

    recv = {}
    for grp, names in scatter_groups.items():
        for n, r in zip(names, _copies_wait(scatters[grp], grad_x, _scatter_copies, name="scatter_wait_" + grp)):
            recv[n] = r
    part = [_sum4(G[n][0].reshape((N_CHIPS,) + shard_shape[n]), recv[n], me, name="sum4_" + n) for n in _BIG]
    part = [_from_compute_layout(n, p) for n, p in zip(_BIG, part)]
    sib = _swap_with_sibling(part, name="swap_grads")
    out = {}
    for n, p, s in zip(_BIG, part, sib):
        out[n] = [r[None] for r in _adamw(P[n][0], Mo[n][0], Vo[n][0], [p, s], name="adamw_" + n)]

    conv_w_full_grad = G["conv_w"]
    small_g = [G["g_mix"], G["g_q_lat"], G["g_kv_lat"], G["g_q_mla"][:, :QK], G["g_k_mla"][:, :QK], G["w_pool"],
               G["pool_scale"], G["g_mem"], G["g_q_x"], G["g_k_x"], G["g_ffn"], G["conv_b"], conv_w_full_grad]
    packed = _pack(small_g + [loss_row[:, :1]])
    total = _sum8(_gather_all(packed, name="gather_small"), name="sum_small")
    shapes = [a.shape for a in small_g] + [(1, 1)]
    (parts, _) = _unpack(total, shapes)
    loss = parts[-1].reshape(())
    F = conv_b.shape[1]
    fn = F // N_CHIPS
    col0 = (2 * xi + yi) * fn
    sg = dict(zip(_SMALL, parts[:-1]))
    sg["conv_w"] = lax.dynamic_slice(sg["conv_w"], (0, col0), (3, fn))
    sw = [P[n].reshape(sg[n].shape) for n in _SMALL]
    sm = [Mo[n].reshape(sg[n].shape) for n in _SMALL]
    sv = [Vo[n].reshape(sg[n].shape) for n in _SMALL]
    gp = _pack([sg[n] for n in _SMALL])
    res = _adamw(_pack(sw), _pack(sm), _pack(sv), [gp], name="adamw_small")
    sshapes = [sg[n].shape for n in _SMALL]
    for kind, buf in zip(range(4), res):
        vals, _ = _unpack(buf, sshapes)
        for n, val in zip(_SMALL, vals):
            out.setdefault(n, [None] * 4)[kind] = val.reshape(P[n].shape)

    return (loss, grad_x[None], *[out[n][0] for n in _WEIGHTS], *[out[n][1] for n in _WEIGHTS],
            *[out[n][2] for n in _WEIGHTS], *[out[n][3] for n in _WEIGHTS])
```

```python
import functools
import math

import numpy as np
import jax
import jax.numpy as jnp
from jax import lax
from jax.experimental import pallas as pl
from jax.experimental.pallas import tpu as pltpu

F32, BF16 = jnp.float32, jnp.bfloat16
NORM_EPS = 1e-6
ROPE_THETA = 10000.0
V7X_VMEM_LIMIT_BYTES = 48 * 1024 * 1024
N_CHIPS = 4
N_DEV = 8

POOL_W = 512
POOL_WINDOWS = (2, 4, 8, 16)
HEADS = 8
NOPE, ROPE, QK = 128, 64, 192
HEAD_PAD = 256
Q_RANK, KV_RANK = 512, 256
X_HEADS, X_DIM = 4, 128
Z_COLS = 1920
Z_POOL_CB, Z_Q_CB, Z_MQ_CB = 0, 1, 2
Z_KV_CB = 6
Z_KR_CB = 14

ADAM_LR, ADAM_B1, ADAM_B2, ADAM_EPS, ADAM_WD, ADAM_STEP = 0.001, 0.9, 0.999, 1e-08, 0.01, 10

MESH = pl.DeviceIdType.MESH


def _cp(sem):
    return pltpu.CompilerParams(dimension_semantics=sem, vmem_limit_bytes=V7X_VMEM_LIMIT_BYTES)


def _row_tile(S):
    return 256 if S % 256 == 0 and S >= 2048 else 128


def _big_row_tile(S):
    return 512 if S % 512 == 0 and S >= 2048 else _row_tile(S)


def _pick(dim, prefs):
    for p in prefs:
        if dim % p == 0:
            return p
    return dim


_DN = {"nn": (((1,), (0,)), ((), ())), "nt": (((1,), (1,)), ((), ())), "tn": (((0,), (0,)), ((), ()))}


def _mm(a, b, *, mode, grid, blocks, maps, out_shape, out_dtypes, add=None, name):
    nk = grid[2]
    dn = _DN[mode]
    n_out = len(out_dtypes)

    def body(*refs):
        a_ref, b_ref = refs[0], refs[1]
        add_ref = refs[2] if add is not None else None
        p = 2 + (add is not None)
        o_refs = refs[p:p + n_out]

        def finish(r):
            if add_ref is not None:
                r = r + add_ref[...]
            for o in o_refs:
                o[...] = r.astype(o.dtype)

        def product():
            return lax.dot_general(a_ref[...].astype(BF16), b_ref[...].astype(BF16), dn, preferred_element_type=F32)

        if nk == 1:
            finish(product())
            return
        acc = refs[p + n_out]
        k = pl.program_id(2)

        @pl.when(k == 0)
        def _():
            acc[...] = jnp.zeros_like(acc)

        acc[...] += product()

        @pl.when(k == nk - 1)
        def _():
            finish(acc[...])

    a_blk, b_blk, o_blk = blocks
    a_map, b_map, o_map = maps
    in_specs = [pl.BlockSpec(a_blk, a_map), pl.BlockSpec(b_blk, b_map)]
    args = [a, b]
    if add is not None:
        in_specs.append(pl.BlockSpec(o_blk, o_map))
        args.append(add)
    outs = pl.pallas_call(
        body, grid=grid, in_specs=in_specs,
        out_specs=[pl.BlockSpec(o_blk, o_map) for _ in out_dtypes],
        out_shape=[jax.ShapeDtypeStruct(out_shape, d) for d in out_dtypes],
        scratch_shapes=[pltpu.VMEM(o_blk, F32)] if nk > 1 else [],
        compiler_params=_cp(("parallel", "parallel", "arbitrary")), name=name)(*args)
    return outs[0] if n_out == 1 else outs


def mm_nn(a, w, *, nsh=1, tm, tn, tk, out_dtypes=(F32,), add=None, name):
    M, K = a.shape
    n = w.shape[1]
    N = nsh * n
    assert w.shape[0] == nsh * K and n % tn == 0 and K % tk == 0 and M % tm == 0
    npt, kt = n // tn, K // tk
    return _mm(a, w, mode="nn", grid=(M // tm, N // tn, kt),
               blocks=((tm, tk), (tk, tn), (tm, tn)),
               maps=(lambda i, j, k: (i, k), lambda i, j, k: ((j // npt) * kt + k, j % npt),
                     lambda i, j, k: (i, j)),
               out_shape=(M, N), out_dtypes=out_dtypes, add=add, name=name)


def mm_nt(d, w, *, nsh=1, tm, to, tc, out_dtypes=(F32,), add=None, name):
    M, N = d.shape
    n = w.shape[1]
    K = w.shape[0] // nsh
    assert nsh * n == N and n % tc == 0 and K % to == 0 and M % tm == 0
    cpt, ot = n // tc, K // to
    return _mm(d, w, mode="nt", grid=(M // tm, ot, N // tc),
               blocks=((tm, tc), (to, tc), (tm, to)),
               maps=(lambda i, j, c: (i, c), lambda i, j, c: ((c // cpt) * ot + j, c % cpt),
                     lambda i, j, c: (i, j)),
               out_shape=(M, K), out_dtypes=out_dtypes, add=add, name=name)


def mm_nt_pair(d1, w1, d2, w2, *, nsh, tm, to, tc, name):
    M, N = d1.shape
    n = w1.shape[1]
    K = w1.shape[0] // nsh
    assert d2.shape == d1.shape and w2.shape == w1.shape and nsh * n == N
    assert n % tc == 0 and K % to == 0 and M % tm == 0
    cpt, ot, nk = n // tc, K // to, N // tc

    def body(a1_ref, b1_ref, a2_ref, b2_ref, o_ref, acc):
        k = pl.program_id(2)

        @pl.when(k == 0)
        def _():
            acc[...] = jnp.zeros_like(acc)

        acc[...] += lax.dot_general(a1_ref[...], b1_ref[...], _DN["nt"], preferred_element_type=F32)
        acc[...] += lax.dot_general(a2_ref[...], b2_ref[...], _DN["nt"], preferred_element_type=F32)

        @pl.when(k == nk - 1)
        def _():
            o_ref[...] = acc[...]

    a_spec = pl.BlockSpec((tm, tc), lambda i, j, c: (i, c))
    b_spec = pl.BlockSpec((to, tc), lambda i, j, c: ((c // cpt) * ot + j, c % cpt))
    return pl.pallas_call(
        body, grid=(M // tm, ot, nk), in_specs=[a_spec, b_spec, a_spec, b_spec],
        out_specs=pl.BlockSpec((tm, to), lambda i, j, c: (i, j)),
        out_shape=jax.ShapeDtypeStruct((M, K), F32),
        scratch_shapes=[pltpu.VMEM((tm, to), F32)],
        compiler_params=_cp(("parallel", "parallel", "arbitrary")), name=name)(d1, w1, d2, w2)


def mm_tn(x, d, *, nsh=1, to, tn, tk, out_dtypes=(F32,), name):
    M, K = x.shape
    N = d.shape[1]
    n = N // nsh
    assert n % tn == 0 and K % to == 0 and M % tk == 0
    npt, ot = n // tn, K // to
    return _mm(x, d, mode="tn", grid=(ot, N // tn, M // tk),
               blocks=((tk, to), (tk, tn), (to, tn)),
               maps=(lambda i, j, k: (k, i), lambda i, j, k: (k, j),
                     lambda i, j, k: ((j // npt) * ot + i, j % npt)),
               out_shape=(nsh * K, n), out_dtypes=out_dtypes, name=name)


def _rms_fwd(x, g, *, C, cb=0, name):
    S = x.shape[0]
    tm = _big_row_tile(S) if S >= 128 else S

    def body(x_ref, g_ref, o_ref):
        xv = x_ref[...]
        r = lax.rsqrt(jnp.mean(xv * xv, axis=-1, keepdims=True) + NORM_EPS)
        o_ref[...] = ((xv * r) * g_ref[...]).astype(o_ref.dtype)

    return pl.pallas_call(
        body, grid=(S // tm,),
        in_specs=[pl.BlockSpec((tm, C), lambda i: (i, cb)), pl.BlockSpec((1, C), lambda i: (0, 0))],
        out_specs=pl.BlockSpec((tm, C), lambda i: (i, 0)),
        out_shape=jax.ShapeDtypeStruct((S, C), BF16),
        compiler_params=_cp(("parallel",)), name=name)(x, g)


def _rms_bwd(x, dh, g, *, C, cb=0, res=None, out_dtypes=(F32,), name):
    S = x.shape[0]
    tm = (_big_row_tile(S) if C <= 512 else _row_tile(S)) if S >= 128 else S
    n_out = len(out_dtypes)

    def body(*refs):
        x_ref, dh_ref, g_ref = refs[:3]
        res_ref = refs[3] if res is not None else None
        p = 3 + (res is not None)
        outs = refs[p:p + n_out]
        dg_ref = refs[p + n_out]
        i = pl.program_id(0)
        xv = x_ref[...]
        r = lax.rsqrt(jnp.mean(xv * xv, axis=-1, keepdims=True) + NORM_EPS)
        n = xv * r
        dhv = dh_ref[...].astype(F32)
        dn = dhv * g_ref[...]
        c = jnp.mean(dn * n, axis=-1, keepdims=True)
        dx = r * (dn - n * c)
        if res_ref is not None:
            dx = res_ref[...] + dx
        for o in outs:
            o[...] = dx.astype(o.dtype)

        @pl.when(i == 0)
        def _():
            dg_ref[...] = jnp.zeros_like(dg_ref)

        dg_ref[...] += jnp.sum(dhv * n, axis=0, keepdims=True)

    row = pl.BlockSpec((tm, C), lambda i: (i, 0))
    in_specs = [pl.BlockSpec((tm, C), lambda i: (i, cb)), row, pl.BlockSpec((1, C), lambda i: (0, 0))]
    args = [x, dh, g]
    if res is not None:
        in_specs.append(row)
        args.append(res)
    return pl.pallas_call(
        body, grid=(S // tm,), in_specs=in_specs,
        out_specs=[row] * n_out + [pl.BlockSpec((1, C), lambda i: (0, 0))],
        out_shape=[jax.ShapeDtypeStruct((S, C), d) for d in out_dtypes] + [jax.ShapeDtypeStruct((1, C), F32)],
        compiler_params=_cp(("arbitrary",)), name=name)(*args)


def _pool_cnt(t0, rows, w):
    t = t0 + lax.broadcasted_iota(jnp.int32, (rows, 1), 0)
    return jnp.minimum(t + 1, w).astype(F32)


def _pool_d(halo, tile, gi, t0, tm):
    s = jnp.concatenate([halo, tile], axis=0)
    for step in (1, 2, 4, 8)[:gi + 1]:
        s = s + pltpu.roll(s, step, 0)
    return s[16:] / _pool_cnt(t0, tm, POOL_WINDOWS[gi]) - tile


def _pool_fwd(z, w_pool, pool_scale, *, name):
    S = z.shape[0]
    tm = _row_tile(S)
    hb = tm // 16

    def body(z_ref, h_ref, w_ref, sc_ref, o_ref):
        i = pl.program_id(0)
        halo = h_ref[...] * (i > 0).astype(F32)
        for gi in range(4):
            cs = slice(gi * 128, (gi + 1) * 128)
            d = _pool_d(halo[:, cs], z_ref[:, cs], gi, i * tm, tm)
            yp = jnp.dot(d.astype(BF16), w_ref[gi], preferred_element_type=F32)
            o_ref[:, cs] = (yp * sc_ref[:, cs]).astype(o_ref.dtype)

    return pl.pallas_call(
        body, grid=(S // tm,),
        in_specs=[pl.BlockSpec((tm, POOL_W), lambda i: (i, Z_POOL_CB)),
                  pl.BlockSpec((16, POOL_W), lambda i: (jnp.maximum(i * hb - 1, 0), Z_POOL_CB)),
                  pl.BlockSpec((4, 128, 128), lambda i: (0, 0, 0)),
                  pl.BlockSpec((1, POOL_W), lambda i: (0, 0))],
        out_specs=pl.BlockSpec((tm, POOL_W), lambda i: (i, 0)),
        out_shape=jax.ShapeDtypeStruct((S, POOL_W), BF16),
        compiler_params=_cp(("parallel",)), name=name)(z, z, w_pool, pool_scale)


def _pool_bwd(z, d_cat, w_pool, pool_scale, *, name):
    S = z.shape[0]
    tm = _row_tile(S)
    hb = tm // 16
    nt = S // tm
    E = tm + 16

    def body(z_ref, h_ref, dy_ref, dyn_ref, w_ref, sc_ref, dz_ref, gw_ref, gs_ref):
        i = pl.program_id(0)

        @pl.when(i == 0)
        def _():
            gw_ref[...] = jnp.zeros_like(gw_ref)
            gs_ref[...] = jnp.zeros_like(gs_ref)

        halo = h_ref[...] * (i > 0).astype(F32)
        dy_next = dyn_ref[...].astype(F32) * (i < nt - 1).astype(F32)
        for gi in range(4):
            cs = slice(gi * 128, (gi + 1) * 128)
            w = w_ref[gi]
            d = _pool_d(halo[:, cs], z_ref[:, cs], gi, i * tm, tm)
            db = d.astype(BF16)
            dy = dy_ref[:, cs].astype(F32)
            yp = jnp.dot(db, w, preferred_element_type=F32)
            gs_ref[:, cs] += jnp.sum(dy * yp, axis=0, keepdims=True)
            sc = sc_ref[:, cs]
            dys = (dy * sc).astype(BF16)
            gw_ref[gi] += lax.dot_general(db, dys, _DN["tn"], preferred_element_type=F32)
            dys_ext = jnp.concatenate([dys, (dy_next[:, cs] * sc).astype(BF16)], axis=0)
            dd = lax.dot_general(dys_ext, w, _DN["nt"], preferred_element_type=F32)
            r = dd / _pool_cnt(i * tm, E, POOL_WINDOWS[gi])
            for step in (1, 2, 4, 8)[:gi + 1]:
                r = r + pltpu.roll(r, E - step, 0)
            dz_ref[:, cs] = (r[:tm] - dd[:tm]).astype(dz_ref.dtype)

    return pl.pallas_call(
        body, grid=(nt,),
        in_specs=[pl.BlockSpec((tm, POOL_W), lambda i: (i, Z_POOL_CB)),
                  pl.BlockSpec((16, POOL_W), lambda i: (jnp.maximum(i * hb - 1, 0), Z_POOL_CB)),
                  pl.BlockSpec((tm, POOL_W), lambda i: (i, 0)),
                  pl.BlockSpec((16, POOL_W), lambda i: (jnp.minimum((i + 1) * hb, S // 16 - 1), 0)),
                  pl.BlockSpec((4, 128, 128), lambda i: (0, 0, 0)),
                  pl.BlockSpec((1, POOL_W), lambda i: (0, 0))],
        out_specs=[pl.BlockSpec((tm, POOL_W), lambda i: (i, 0)),
                   pl.BlockSpec((4, 128, 128), lambda i: (0, 0, 0)),
                   pl.BlockSpec((1, POOL_W), lambda i: (0, 0))],
        out_shape=[jax.ShapeDtypeStruct((S, POOL_W), BF16),
                   jax.ShapeDtypeStruct((4, 128, 128), F32),
                   jax.ShapeDtypeStruct((1, POOL_W), F32)],
        compiler_params=_cp(("arbitrary",)), name=name)(z, z, d_cat, d_cat, w_pool, pool_scale)


def _rope_tables(S):
    half = ROPE // 2
    inv_freq = 1.0 / (ROPE_THETA ** (jnp.arange(half, dtype=F32) / half))
    ang = jnp.arange(S).astype(F32)[:, None] * inv_freq[None, :]
    cos, sin = jnp.cos(ang), jnp.sin(ang)
    zero = jnp.zeros((S, half), F32)
    cos_t = jnp.concatenate([cos, cos, zero, zero], axis=1)
    sa_t = jnp.concatenate([-sin, zero, zero, zero], axis=1)
    sb_t = jnp.concatenate([zero, sin, zero, zero], axis=1)
    return cos_t, sa_t, sb_t


def _head_fwd(xn, xr, gn, gr, cos, sa, sb):
    ms = (jnp.sum(xn * xn, axis=-1, keepdims=True) + jnp.sum(xr * xr, axis=-1, keepdims=True)) * (1.0 / QK)
    r = lax.rsqrt(ms + NORM_EPS)
    on = (xn * r) * gn
    yr = (xr * r) * gr
    orr = yr * cos + pltpu.roll(yr, 96, 1) * sa + pltpu.roll(yr, 32, 1) * sb
    return on, orr


def _head_bwd(xn, xr, gn, gr, don, dor, cos, sa, sb):
    ms = (jnp.sum(xn * xn, axis=-1, keepdims=True) + jnp.sum(xr * xr, axis=-1, keepdims=True)) * (1.0 / QK)
    r = lax.rsqrt(ms + NORM_EPS)
    nn, nr = xn * r, xr * r
    dyr = dor * cos + pltpu.roll(dor * sa, 32, 1) + pltpu.roll(dor * sb, 96, 1)
    ggn, ggr = don * nn, dyr * nr
    dnn, dnr = don * gn, dyr * gr
    c = (jnp.sum(dnn * nn, axis=-1, keepdims=True) + jnp.sum(dnr * nr, axis=-1, keepdims=True)) * (1.0 / QK)
    return r * (dnn - nn * c), r * (dnr - nr * c), ggn, ggr


def _kv_cols(h):
    base = (h // 2) * 512 + (h % 2) * 128
    return base, base + 256


def _qkrope_fwd(qraw, kvraw, z, gq, gk, tabs, *, name):
    S = qraw.shape[0]
    tm = _row_tile(S)

    def body(q_ref, kv_ref, zkr_ref, gq_ref, gk_ref, cos_ref, sa_ref, sb_ref, qo_ref, ko_ref, vo_ref, vt_ref):
        cos, sa, sb = cos_ref[...], sa_ref[...], sb_ref[...]
        zkr = zkr_ref[...]
        gqn, gqr, gkn, gkr = gq_ref[:, :128], gq_ref[:, 128:], gk_ref[:, :128], gk_ref[:, 128:]
        for h in range(HEADS):
            b = h * HEAD_PAD
            on, orr = _head_fwd(q_ref[:, b:b + 128], q_ref[:, b + 128:b + 256], gqn, gqr, cos, sa, sb)
            qo_ref[:, b:b + 128] = on.astype(BF16)
            qo_ref[:, b + 128:b + 256] = orr.astype(BF16)
            kc, vc = _kv_cols(h)
            on, orr = _head_fwd(kv_ref[:, kc:kc + 128], zkr, gkn, gkr, cos, sa, sb)
            ko_ref[:, b:b + 128] = on.astype(BF16)
            ko_ref[:, b + 128:b + 256] = orr.astype(BF16)
            vv = kv_ref[:, vc:vc + 128]
            vo_ref[:, h * 128:(h + 1) * 128] = vv.astype(BF16)
            vt_ref[h * 128:(h + 1) * 128, :] = jnp.transpose(vv).astype(BF16)

    W = HEADS * HEAD_PAD
    row = lambda c: pl.BlockSpec((tm, c), lambda i: (i, 0))
    vec = lambda c: pl.BlockSpec((1, c), lambda i: (0, 0))
    return pl.pallas_call(
        body, grid=(S // tm,),
        in_specs=[row(W), row(W), pl.BlockSpec((tm, 128), lambda i: (i, Z_KR_CB)), vec(256), vec(256),
                  row(128), row(128), row(128)],
        out_specs=[row(W), row(W), row(HEADS * 128), pl.BlockSpec((HEADS * 128, tm), lambda i: (0, i))],
        out_shape=[jax.ShapeDtypeStruct((S, W), BF16), jax.ShapeDtypeStruct((S, W), BF16),
                   jax.ShapeDtypeStruct((S, HEADS * 128), BF16), jax.ShapeDtypeStruct((HEADS * 128, S), BF16)],
        compiler_params=_cp(("parallel",)), name=name)(qraw, kvraw, z, gq, gk, *tabs)


def _qkrope_bwd(qraw, kvraw, z, gq, gk, tabs, dq, dk, dv, *, name):
    S = qraw.shape[0]
    tm = _row_tile(S)

    def body(q_ref, kv_ref, zkr_ref, gq_ref, gk_ref, cos_ref, sa_ref, sb_ref, dq_ref, dk_ref, dv_ref,
             dqo_ref, dkvo_ref, dkr_ref, ggq_ref, ggk_ref):
        i = pl.program_id(0)

        @pl.when(i == 0)
        def _():
            ggq_ref[...] = jnp.zeros_like(ggq_ref)
            ggk_ref[...] = jnp.zeros_like(ggk_ref)

        cos, sa, sb = cos_ref[...], sa_ref[...], sb_ref[...]
        zkr = zkr_ref[...]
        gqn, gqr, gkn, gkr = gq_ref[:, :128], gq_ref[:, 128:], gk_ref[:, :128], gk_ref[:, 128:]
        dkr = jnp.zeros((tm, 128), F32)
        sq_n = jnp.zeros((1, 128), F32)
        sq_r = jnp.zeros((1, 128), F32)
        sk_n = jnp.zeros((1, 128), F32)
        sk_r = jnp.zeros((1, 128), F32)
        for h in range(HEADS):
            b = h * HEAD_PAD
            dxn, dxr, ggn, ggr = _head_bwd(q_ref[:, b:b + 128], q_ref[:, b + 128:b + 256], gqn, gqr,
                                           dq_ref[:, b:b + 128], dq_ref[:, b + 128:b + 256], cos, sa, sb)
            dqo_ref[:, b:b + 128] = dxn.astype(BF16)
            dqo_ref[:, b + 128:b + 256] = dxr.astype(BF16)
            sq_n += jnp.sum(ggn, axis=0, keepdims=True)
            sq_r += jnp.sum(ggr, axis=0, keepdims=True)
            kc, vc = _kv_cols(h)
            dxn, dxr, ggn, ggr = _head_bwd(kv_ref[:, kc:kc + 128], zkr, gkn, gkr,
                                           dk_ref[:, b:b + 128], dk_ref[:, b + 128:b + 256], cos, sa, sb)
            dkvo_ref[:, kc:kc + 128] = dxn.astype(BF16)
            dkvo_ref[:, vc:vc + 128] = dv_ref[:, h * 128:(h + 1) * 128].astype(BF16)
            dkr += dxr
            sk_n += jnp.sum(ggn, axis=0, keepdims=True)
            sk_r += jnp.sum(ggr, axis=0, keepdims=True)
        dkr_ref[...] = dkr.astype(BF16)
        ggq_ref[:, :128] += sq_n
        ggq_ref[:, 128:] += sq_r
        ggk_ref[:, :128] += sk_n
        ggk_ref[:, 128:] += sk_r

    W = HEADS * HEAD_PAD
    row = lambda c: pl.BlockSpec((tm, c), lambda i: (i, 0))
    vec = lambda c: pl.BlockSpec((1, c), lambda i: (0, 0))
    return pl.pallas_call(
        body, grid=(S // tm,),
        in_specs=[row(W), row(W), pl.BlockSpec((tm, 128), lambda i: (i, Z_KR_CB)), vec(256), vec(256),
                  row(128), row(128), row(128), row(W), row(W), row(HEADS * 128)],
        out_specs=[row(W), row(W), row(128), vec(256), vec(256)],
        out_shape=[jax.ShapeDtypeStruct((S, W), BF16), jax.ShapeDtypeStruct((S, W), BF16),
                   jax.ShapeDtypeStruct((S, 128), BF16),
                   jax.ShapeDtypeStruct((1, 256), F32), jax.ShapeDtypeStruct((1, 256), F32)],
        compiler_params=_cp(("arbitrary",)), name=name)(qraw, kvraw, z, gq, gk, *tabs, dq, dk, dv)


LOG2E = 1.4426950408889634
SCORE_SCALE = 1.0 / math.sqrt(QK)
SCORE_SCALE_LOG2 = SCORE_SCALE * LOG2E


def _fa_tile(S):
    return 512 if S % 512 == 0 and S >= 2048 else 128


def _flash_fwd(q, k, vt, *, name):
    S = q.shape[0]
    ts = _fa_tile(S)

    def body(q_ref, k_ref, vt_ref, o_ref, ob_ref, lse_ref, m_sc, l_sc, acc_sc):
        qi = pl.program_id(1)
        m_sc[...] = jnp.full_like(m_sc, -jnp.inf)
        l_sc[...] = jnp.zeros_like(l_sc)
        acc_sc[...] = jnp.zeros_like(acc_sc)
        qb = q_ref[...]

        def scores(kidx, masked):
            k0 = pl.multiple_of(kidx * ts, ts)
            st = lax.dot_general(k_ref[pl.ds(k0, ts), :], qb, _DN["nt"], preferred_element_type=F32)
            if masked:
                st = jnp.where(lax.broadcasted_iota(jnp.int32, (ts, ts), 0) > lax.broadcasted_iota(jnp.int32, (ts, ts), 1),
                               -jnp.inf, st)
            return st

        def update(st, kidx):
            k0 = pl.multiple_of(kidx * ts, ts)
            m_prev = m_sc[...]
            m_new = jnp.maximum(m_prev, jnp.max(st, axis=0, keepdims=True))
            alpha = jnp.exp2((m_prev - m_new) * SCORE_SCALE_LOG2)
            pt = jnp.exp2((st - m_new[0:1, :]) * SCORE_SCALE_LOG2)
            l_sc[...] = alpha * l_sc[...] + jnp.sum(pt, axis=0, keepdims=True)
            acc_sc[...] = alpha[0:1, :] * acc_sc[...] + jnp.dot(vt_ref[:, pl.ds(k0, ts)], pt.astype(BF16),
                                                                preferred_element_type=F32)
            m_sc[...] = m_new

        def pair(t, carry):
            sa, sb = scores(2 * t, False), scores(2 * t + 1, False)
            update(sa, 2 * t)
            update(sb, 2 * t + 1)
            return carry

        lax.fori_loop(0, qi // 2, pair, 0)

        @pl.when(qi % 2 == 1)
        def _():
            update(scores(qi - 1, False), qi - 1)

        update(scores(qi, True), qi)
        ot = acc_sc[...] / l_sc[0:1, :]
        o = jnp.transpose(ot)
        o_ref[...] = o
        ob_ref[...] = o.astype(BF16)
        lse_ref[...] = m_sc[...] * SCORE_SCALE_LOG2 + jnp.log2(l_sc[...])

    return pl.pallas_call(
        body, grid=(HEADS, S // ts),
        in_specs=[pl.BlockSpec((ts, HEAD_PAD), lambda h, i: (i, h)),
                  pl.BlockSpec((S, HEAD_PAD), lambda h, i: (0, h)),
                  pl.BlockSpec((128, S), lambda h, i: (h, 0))],
        out_specs=[pl.BlockSpec((ts, 128), lambda h, i: (i, h)),
                   pl.BlockSpec((ts, 128), lambda h, i: (i, h)),
                   pl.BlockSpec((None, 8, ts), lambda h, i: (h, 0, i))],
        out_shape=[jax.ShapeDtypeStruct((S, HEADS * 128), F32), jax.ShapeDtypeStruct((S, HEADS * 128), BF16),
                   jax.ShapeDtypeStruct((HEADS, 8, S), F32)],
        scratch_shapes=[pltpu.VMEM((8, ts), F32), pltpu.VMEM((8, ts), F32), pltpu.VMEM((128, ts), F32)],
        compiler_params=_cp(("parallel", "arbitrary")), name=name)(q, k, vt)


def _attn_bwd_prep(o, d_cat, *, name):
    S = o.shape[0]
    tm = _row_tile(S)
    H = HEADS * 128
    half = H // 2

    def body(o_ref, da_ref, db_ref, delta_ref):
        for h in range(HEADS):
            src, c0 = (da_ref, h * 128) if h * 128 < half else (db_ref, h * 128 - half)
            do = src[:, c0:c0 + 128].astype(F32)
            prod = jnp.transpose(do * o_ref[:, h * 128:(h + 1) * 128])
            delta_ref[h] = jnp.broadcast_to(jnp.sum(prod, axis=0, keepdims=True), (8, tm))

    return pl.pallas_call(
        body, grid=(S // tm,),
        in_specs=[pl.BlockSpec((tm, H), lambda i: (i, 0)),
                  pl.BlockSpec((tm, half), lambda i: (i, 1)), pl.BlockSpec((tm, half), lambda i: (i, 2))],
        out_specs=pl.BlockSpec((HEADS, 8, tm), lambda i: (0, 0, i)),
        out_shape=jax.ShapeDtypeStruct((HEADS, 8, S), F32),
        compiler_params=_cp(("parallel",)), name=name)(o, d_cat, d_cat)


def _flash_bwd(q, k, v, d_cat, lse, delta, *, name):
    S = q.shape[0]
    ts = _fa_tile(S)
    nb = S // ts

    def body(q_ref, do_ref, lse_ref, delta_ref, k_ref, v_ref, dq_ref, dk_ref, dv_ref, dk_sc, dv_sc):
        j = pl.program_id(1)

        @pl.when(j == 0)
        def _():
            dq_ref[...] = jnp.zeros_like(dq_ref)

        dk_sc[...] = jnp.zeros_like(dk_sc)
        dv_sc[...] = jnp.zeros_like(dv_sc)
        kb, vb = k_ref[...], v_ref[...]

        def products(i):
            q0 = pl.multiple_of(i * ts, ts)
            qb = q_ref[pl.ds(q0, ts), :]
            dob_ = do_ref[pl.ds(q0, ts), :]
            st = lax.dot_general(kb, qb, _DN["nt"], preferred_element_type=F32)
            dpt = lax.dot_general(vb, dob_, _DN["nt"], preferred_element_type=F32)
            return q0, qb, dob_, st, dpt

        def accumulate(q0, qb, dob_, st, dpt, masked):
            pt = jnp.exp2(st * SCORE_SCALE_LOG2 - lse_ref[0:1, pl.ds(q0, ts)])
            if masked:
                pt = jnp.where(lax.broadcasted_iota(jnp.int32, (ts, ts), 0) > lax.broadcasted_iota(jnp.int32, (ts, ts), 1),
                               0.0, pt)
            dv_sc[...] += jnp.dot(pt.astype(BF16), dob_, preferred_element_type=F32)
            dst = (pt * (dpt - delta_ref[0:1, pl.ds(q0, ts)])).astype(BF16)
            dk_sc[...] += jnp.dot(dst, qb, preferred_element_type=F32) * SCORE_SCALE
            dq_ref[pl.ds(q0, ts), :] += lax.dot_general(dst, kb, _DN["tn"], preferred_element_type=F32) * SCORE_SCALE

        accumulate(*products(j), True)
        n_below = nb - 1 - j

        def pair(t, carry):
            a, b = products(j + 1 + 2 * t), products(j + 2 + 2 * t)
            accumulate(*a, False)
            accumulate(*b, False)
            return carry

        lax.fori_loop(0, n_below // 2, pair, 0)

        @pl.when(n_below % 2 == 1)
        def _():
            accumulate(*products(nb - 1), False)

        dk_ref[...] = dk_sc[...]
        dv_ref[...] = dv_sc[...]

    return pl.pallas_call(
        body, grid=(HEADS, nb),
        in_specs=[pl.BlockSpec((S, HEAD_PAD), lambda h, j: (0, h)),
                  pl.BlockSpec((S, 128), lambda h, j: (0, 4 + h)),
                  pl.BlockSpec((None, 8, S), lambda h, j: (h, 0, 0)),
                  pl.BlockSpec((None, 8, S), lambda h, j: (h, 0, 0)),
                  pl.BlockSpec((ts, HEAD_PAD), lambda h, j: (j, h)),
                  pl.BlockSpec((ts, 128), lambda h, j: (j, h))],
        out_specs=[pl.BlockSpec((S, HEAD_PAD), lambda h, j: (0, h)),
                   pl.BlockSpec((ts, HEAD_PAD), lambda h, j: (j, h)),
                   pl.BlockSpec((ts, 128), lambda h, j: (j, h))],
        out_shape=[jax.ShapeDtypeStruct((S, HEADS * HEAD_PAD), F32), jax.ShapeDtypeStruct((S, HEADS * HEAD_PAD), F32),
                   jax.ShapeDtypeStruct((S, HEADS * 128), F32)],
        scratch_shapes=[pltpu.VMEM((ts, HEAD_PAD), F32), pltpu.VMEM((ts, 128), F32)],
        compiler_params=_cp(("parallel", "arbitrary")), name=name)(q, d_cat, lse, delta, k, v)


def _memk_fwd(mkv, gkx, *, name):
    M = mkv.shape[0]
    XW = X_HEADS * X_DIM

    def body(mkv_ref, g_ref, k_ref, v_ref):
        for h in range(X_HEADS):
            cs = slice(h * X_DIM, (h + 1) * X_DIM)
            xv = mkv_ref[:, cs]
            r = lax.rsqrt(jnp.mean(xv * xv, axis=-1, keepdims=True) + NORM_EPS)
            k_ref[:, cs] = ((xv * r) * g_ref[...]).astype(BF16)
        v_ref[...] = mkv_ref[:, XW:].astype(BF16)

    return pl.pallas_call(
        body, grid=(1,),
        in_specs=[pl.BlockSpec((M, 2 * XW), lambda i: (0, 0)), pl.BlockSpec((1, X_DIM), lambda i: (0, 0))],
        out_specs=[pl.BlockSpec((M, XW), lambda i: (0, 0)), pl.BlockSpec((M, XW), lambda i: (0, 0))],
        out_shape=[jax.ShapeDtypeStruct((M, XW), BF16), jax.ShapeDtypeStruct((M, XW), BF16)],
        compiler_params=_cp(("arbitrary",)), name=name)(mkv, gkx)


def _memk_bwd(mkv, gkx, dk, dv, *, name):
    M = mkv.shape[0]
    XW = X_HEADS * X_DIM

    def body(mkv_ref, g_ref, dk_ref, dv_ref, o_ref, gg_ref):
        gg = jnp.zeros((1, X_DIM), F32)
        for h in range(X_HEADS):
            cs = slice(h * X_DIM, (h + 1) * X_DIM)
            xv = mkv_ref[:, cs]
            r = lax.rsqrt(jnp.mean(xv * xv, axis=-1, keepdims=True) + NORM_EPS)
            n = xv * r
            dkv = dk_ref[:, cs]
            gg += jnp.sum(dkv * n, axis=0, keepdims=True)
            dn = dkv * g_ref[...]
            c = jnp.mean(dn * n, axis=-1, keepdims=True)
            o_ref[:, cs] = (r * (dn - n * c)).astype(BF16)
        o_ref[:, XW:] = dv_ref[...].astype(BF16)
        gg_ref[...] = gg

    full = lambda c: pl.BlockSpec((M, c), lambda i: (0, 0))
    return pl.pallas_call(
        body, grid=(1,),
        in_specs=[full(2 * XW), pl.BlockSpec((1, X_DIM), lambda i: (0, 0)), full(XW), full(XW)],
        out_specs=[full(2 * XW), pl.BlockSpec((1, X_DIM), lambda i: (0, 0))],
        out_shape=[jax.ShapeDtypeStruct((M, 2 * XW), BF16), jax.ShapeDtypeStruct((1, X_DIM), F32)],
        compiler_params=_cp(("arbitrary",)), name=name)(mkv, gkx, dk, dv)


def _xq_norm(z_ref, g_ref, h):
    xv = z_ref[:, h * X_DIM:(h + 1) * X_DIM]
    r = lax.rsqrt(jnp.mean(xv * xv, axis=-1, keepdims=True) + NORM_EPS)
    n = xv * r
    return n, r, n * g_ref[...]


def _xprobs(qb, k_ref, h):
    s = lax.dot_general(qb, k_ref[:, h * X_DIM:(h + 1) * X_DIM], _DN["nt"],
                        preferred_element_type=F32) * (1.0 / math.sqrt(X_DIM))
    e = jnp.exp(s - jnp.max(s, axis=-1, keepdims=True))
    return e / jnp.sum(e, axis=-1, keepdims=True)


def _memattn_fwd(z, kx, vx, gqx, *, name):
    S = z.shape[0]
    M = kx.shape[0]
    tm = _row_tile(S)
    XW = X_HEADS * X_DIM

    def body(z_ref, k_ref, v_ref, g_ref, o_ref):
        for h in range(X_HEADS):
            cs = slice(h * X_DIM, (h + 1) * X_DIM)
            _, _, qn = _xq_norm(z_ref, g_ref, h)
            p = _xprobs(qn.astype(BF16), k_ref, h)
            o_ref[:, cs] = jnp.dot(p.astype(BF16), v_ref[:, cs], preferred_element_type=F32).astype(BF16)

    return pl.pallas_call(
        body, grid=(S // tm,),
        in_specs=[pl.BlockSpec((tm, XW), lambda i: (i, Z_MQ_CB)), pl.BlockSpec((M, XW), lambda i: (0, 0)),
                  pl.BlockSpec((M, XW), lambda i: (0, 0)), pl.BlockSpec((1, X_DIM), lambda i: (0, 0))],
        out_specs=pl.BlockSpec((tm, XW), lambda i: (i, 0)),
        out_shape=jax.ShapeDtypeStruct((S, XW), BF16),
        compiler_params=_cp(("parallel",)), name=name)(z, kx, vx, gqx)


def _memattn_bwd(z, kx, vx, gqx, d_cat, *, name):
    S = z.shape[0]
    M = kx.shape[0]
    tm = _row_tile(S)
    XW = X_HEADS * X_DIM
    scale = 1.0 / math.sqrt(X_DIM)

    def body(z_ref, k_ref, v_ref, g_ref, do_ref, dz_ref, dk_ref, dv_ref, gg_ref):
        i = pl.program_id(0)

        @pl.when(i == 0)
        def _():
            dk_ref[...] = jnp.zeros_like(dk_ref)
            dv_ref[...] = jnp.zeros_like(dv_ref)
            gg_ref[...] = jnp.zeros_like(gg_ref)

        gg = jnp.zeros((1, X_DIM), F32)
        for h in range(X_HEADS):
            cs = slice(h * X_DIM, (h + 1) * X_DIM)
            n, r, qn = _xq_norm(z_ref, g_ref, h)
            qb = qn.astype(BF16)
            p = _xprobs(qb, k_ref, h)
            pb = p.astype(BF16)
            dob = do_ref[:, cs].astype(BF16)
            dv_ref[:, cs] += lax.dot_general(pb, dob, _DN["tn"], preferred_element_type=F32)
            dp = lax.dot_general(dob, v_ref[:, cs], _DN["nt"], preferred_element_type=F32)
            ds = (p * (dp - jnp.sum(dp * p, axis=-1, keepdims=True))).astype(BF16)
            dk_ref[:, cs] += lax.dot_general(ds, qb, _DN["tn"], preferred_element_type=F32) * scale
            dqn = jnp.dot(ds, k_ref[:, cs], preferred_element_type=F32) * scale
            gg += jnp.sum(dqn * n, axis=0, keepdims=True)
            dn = dqn * g_ref[...]
            c = jnp.mean(dn * n, axis=-1, keepdims=True)
            dz_ref[:, cs] = (r * (dn - n * c)).astype(BF16)
        gg_ref[...] += gg

    full = pl.BlockSpec((M, XW), lambda i: (0, 0))
    vec = pl.BlockSpec((1, X_DIM), lambda i: (0, 0))
    return pl.pallas_call(
        body, grid=(S // tm,),
        in_specs=[pl.BlockSpec((tm, XW), lambda i: (i, Z_MQ_CB)), full, full, vec,
                  pl.BlockSpec((tm, XW), lambda i: (i, 3))],
        out_specs=[pl.BlockSpec((tm, XW), lambda i: (i, 0)), full, full, vec],
        out_shape=[jax.ShapeDtypeStruct((S, XW), BF16), jax.ShapeDtypeStruct((M, XW), F32),
                   jax.ShapeDtypeStruct((M, XW), F32), jax.ShapeDtypeStruct((1, X_DIM), F32)],
        compiler_params=_cp(("arbitrary",)), name=name)(z, kx, vx, gqx, d_cat)


def _silu_parts(x):
    h = 0.5 * x
    return h, jnp.tanh(h)


GLU_HALO = 16


def _glu_tiles(S, F):
    return _big_row_tile(S), _pick(F, (1408, 512, 256, 128))


def _glu_fwd(g, u, conv_w, conv_b, *, name):
    S, F = g.shape
    tm, tc = _glu_tiles(S, F)
    hb = tm // GLU_HALO

    def body(g_ref, gp_ref, u_ref, w_ref, b_ref, a_ref):
        i = pl.program_id(1)
        gt = g_ref[...].astype(F32)
        ext = jnp.concatenate([gp_ref[...].astype(F32) * (i > 0).astype(F32), gt], axis=0)
        gc = b_ref[...] + w_ref[0:1, :] * pltpu.roll(ext, 2, 0)[GLU_HALO:]
        gc = gc + w_ref[1:2, :] * pltpu.roll(ext, 1, 0)[GLU_HALO:]
        gc = gc + w_ref[2:3, :] * gt
        h, t = _silu_parts(gc)
        a_ref[...] = ((h * (1.0 + t)) * u_ref[...].astype(F32)).astype(BF16)

    return pl.pallas_call(
        body, grid=(F // tc, S // tm),
        in_specs=[pl.BlockSpec((tm, tc), lambda j, i: (i, j)),
                  pl.BlockSpec((GLU_HALO, tc), lambda j, i: (jnp.maximum(i * hb - 1, 0), j)),
                  pl.BlockSpec((tm, tc), lambda j, i: (i, j)),
                  pl.BlockSpec((3, tc), lambda j, i: (0, j)),
                  pl.BlockSpec((1, tc), lambda j, i: (0, j))],
        out_specs=pl.BlockSpec((tm, tc), lambda j, i: (i, j)),
        out_shape=jax.ShapeDtypeStruct((S, F), BF16),
        compiler_params=_cp(("parallel", "parallel")), name=name)(g, g, u, conv_w, conv_b)


def _glu_bwd(g, u, d_a, conv_w, conv_b, *, name):
    S, F = g.shape
    tm, tc = _glu_tiles(S, F)
    hb = tm // GLU_HALO
    nt = S // tm
    E = tm + GLU_HALO

    def body(g_ref, gp_ref, gn_ref, u_ref, un_ref, da_ref, dan_ref, w_ref, b_ref,
             dg_ref, du_ref, gw_ref, gb_ref):
        i = pl.program_id(1)

        @pl.when(i == 0)
        def _():
            gw_ref[...] = jnp.zeros_like(gw_ref)
            gb_ref[...] = jnp.zeros_like(gb_ref)

        w0, w1, w2 = w_ref[0:1, :], w_ref[1:2, :], w_ref[2:3, :]
        gext = jnp.concatenate([gp_ref[...].astype(F32) * (i > 0).astype(F32), g_ref[...].astype(F32),
                                gn_ref[...].astype(F32)], axis=0)
        g1 = pltpu.roll(gext, 1, 0)[GLU_HALO:]
        g2 = pltpu.roll(gext, 2, 0)[GLU_HALO:]
        g0 = gext[GLU_HALO:]
        gc = b_ref[...] + w0 * g2
        gc = gc + w1 * g1
        gc = gc + w2 * g0
        h, t = _silu_parts(gc)
        t1 = 1.0 + t
        da = jnp.concatenate([da_ref[...].astype(F32), dan_ref[...].astype(F32) * (i < nt - 1).astype(F32)], axis=0)
        uu = jnp.concatenate([u_ref[...].astype(F32), un_ref[...].astype(F32)], axis=0)
        du_ref[...] = (da[:tm] * (h[:tm] * t1[:tm])).astype(BF16)
        dgc = (da * uu) * (0.5 * (t1 + h * (1.0 - t * t)))
        dg = w2 * dgc[:tm] + w1 * pltpu.roll(dgc, E - 1, 0)[:tm] + w0 * pltpu.roll(dgc, E - 2, 0)[:tm]
        dg_ref[...] = dg.astype(BF16)
        dgt = dgc[:tm]
        gb_ref[...] += jnp.sum(dgt, axis=0, keepdims=True)
        gw_ref[0:1, :] += jnp.sum(dgt * g2[:tm], axis=0, keepdims=True)
        gw_ref[1:2, :] += jnp.sum(dgt * g1[:tm], axis=0, keepdims=True)
        gw_ref[2:3, :] += jnp.sum(dgt * g0[:tm], axis=0, keepdims=True)

    tile = pl.BlockSpec((tm, tc), lambda j, i: (i, j))
    nxt = pl.BlockSpec((GLU_HALO, tc), lambda j, i: (jnp.minimum((i + 1) * hb, S // GLU_HALO - 1), j))
    prv = pl.BlockSpec((GLU_HALO, tc), lambda j, i: (jnp.maximum(i * hb - 1, 0), j))
    return pl.pallas_call(
        body, grid=(F // tc, nt),
        in_specs=[tile, prv, nxt, tile, nxt, tile, nxt,
                  pl.BlockSpec((3, tc), lambda j, i: (0, j)), pl.BlockSpec((1, tc), lambda j, i: (0, j))],
        out_specs=[tile, tile, pl.BlockSpec((3, tc), lambda j, i: (0, j)), pl.BlockSpec((1, tc), lambda j, i: (0, j))],
        out_shape=[jax.ShapeDtypeStruct((S, F), BF16), jax.ShapeDtypeStruct((S, F), BF16),
                   jax.ShapeDtypeStruct((3, F), F32), jax.ShapeDtypeStruct((1, F), F32)],
        compiler_params=_cp(("parallel", "arbitrary")), name=name)(g, g, g, u, u, d_a, d_a, conv_w, conv_b)


def _loss_head(y, target, *, name):
    S, D = y.shape
    tm = _big_row_tile(S)
    nt = S // tm

    def body(y_ref, t_ref, dy_ref, dyb_ref, loss_ref, acc):
        i = pl.program_id(0)

        @pl.when(i == 0)
        def _():
            acc[...] = jnp.zeros_like(acc)

        e = y_ref[...] - t_ref[...]
        dy = e * (1.0 / D)
        dy_ref[...] = dy
        dyb_ref[...] = dy.astype(BF16)
        acc[...] += jnp.sum(e * e, axis=0, keepdims=True)

        @pl.when(i == nt - 1)
        def _():
            loss_ref[...] = jnp.broadcast_to(jnp.sum(acc[...], axis=1, keepdims=True) * (0.5 / D), (1, 128))

    row = pl.BlockSpec((tm, D), lambda i: (i, 0))
    return pl.pallas_call(
        body, grid=(nt,), in_specs=[row, row],
        out_specs=[row, row, pl.BlockSpec((1, 128), lambda i: (0, 0))],
        out_shape=[jax.ShapeDtypeStruct((S, D), F32), jax.ShapeDtypeStruct((S, D), BF16),
                   jax.ShapeDtypeStruct((1, 128), F32)],
        scratch_shapes=[pltpu.VMEM((1, D), F32)],
        compiler_params=_cp(("arbitrary",)), name=name)(y, target)


def _adamw_math(w, g, m, v):
    m = ADAM_B1 * m + (1.0 - ADAM_B1) * g
    v = ADAM_B2 * v + (1.0 - ADAM_B2) * (g * g)
    m_hat = m / (1.0 - ADAM_B1 ** ADAM_STEP)
    v_hat = v / (1.0 - ADAM_B2 ** ADAM_STEP)
    delta = -ADAM_LR * (m_hat / (jnp.sqrt(v_hat) + ADAM_EPS) + ADAM_WD * w)
    return delta, m, v


def _adamw(w, m, v, parts, *, name):
    R, C = w.shape
    tr = 128 if R % 128 == 0 else R
    n_parts = len(parts)

    def body(*refs):
        w_ref, m_ref, v_ref = refs[:3]
        p_refs = refs[3:3 + n_parts]
        g_ref, d_ref, mo_ref, vo_ref = refs[3 + n_parts:]
        g = p_refs[0][...]
        for p in p_refs[1:]:
            g = g + p[...]
        delta, mn, vn = _adamw_math(w_ref[...], g, m_ref[...], v_ref[...])
        g_ref[...] = g
        d_ref[...] = delta
        mo_ref[...] = mn
        vo_ref[...] = vn

    blk = pl.BlockSpec((tr, C), lambda i: (i, 0))
    return pl.pallas_call(
        body, grid=(R // tr,), in_specs=[blk] * (3 + n_parts), out_specs=[blk] * 4,
        out_shape=[jax.ShapeDtypeStruct((R, C), F32)] * 4,
        compiler_params=_cp(("parallel",)), name=name)(w, m, v, *parts)


def _sum4(g_stack, recv, me, *, name):
    _, R, C = g_stack.shape
    tr = 128 if R % 128 == 0 else R

    def body(me_ref, g_ref, r_ref, o_ref):
        acc = g_ref[...]
        for j in range(N_CHIPS - 1):
            acc = acc + r_ref[j].astype(F32)
        o_ref[...] = acc

    grid_spec = pltpu.PrefetchScalarGridSpec(
        num_scalar_prefetch=1, grid=(R // tr,),
        in_specs=[pl.BlockSpec((None, tr, C), lambda i, me_ref: (me_ref[0], i, 0)),
                  pl.BlockSpec((N_CHIPS - 1, tr, C), lambda i, me_ref: (0, i, 0))],
        out_specs=pl.BlockSpec((tr, C), lambda i, me_ref: (i, 0)))
    return pl.pallas_call(
        body, grid_spec=grid_spec, out_shape=jax.ShapeDtypeStruct((R, C), F32),
        compiler_params=_cp(("parallel",)), name=name)(me, g_stack, recv)


def _sum8(gathered, *, name):
    _, R, C = gathered.shape

    def body(g_ref, o_ref):
        acc = g_ref[0]
        for d in range(1, N_DEV):
            acc = acc + g_ref[d]
        o_ref[...] = acc

    return pl.pallas_call(
        body, grid=(1,), in_specs=[pl.BlockSpec((N_DEV, R, C), lambda i: (0, 0, 0))],
        out_specs=pl.BlockSpec((R, C), lambda i: (0, 0)),
        out_shape=jax.ShapeDtypeStruct((R, C), F32),
        compiler_params=_cp(("arbitrary",)), name=name)(gathered)


def _place():
    return lax.axis_index("x"), lax.axis_index("y"), lax.axis_index("c")


def _other_chips(x, y):
    return [(1 - x, y), (x, 1 - y), (1 - x, 1 - y)]


_ANY = pl.BlockSpec(memory_space=pl.ANY)


_HBM = pl.BlockSpec(memory_space=pltpu.HBM)
_SEM = pl.BlockSpec(memory_space=pltpu.SEMAPHORE)
_EFFECT = pltpu.SideEffectType.DATAFLOW_SIDE_EFFECTING


def _gather_copies(srcs, lands, send_sems, recv_sems):
    x, y, c = _place()
    me = 2 * x + y
    return [pltpu.make_async_remote_copy(
        src_ref=srcs[i], dst_ref=lands[i].at[me], send_sem=send_sems.at[3 * i + j],
        recv_sem=recv_sems.at[3 * i + j], device_id=(px, py, c), device_id_type=MESH)
        for i in range(len(srcs)) for j, (px, py) in enumerate(_other_chips(x, y))]


def _scatter_copies(srcs, lands, send_sems, recv_sems):
    x, y, c = _place()
    return [pltpu.make_async_remote_copy(
        src_ref=srcs[i].at[2 * px + py], dst_ref=lands[i].at[j], send_sem=send_sems.at[3 * i + j],
        recv_sem=recv_sems.at[3 * i + j], device_id=(px, py, c), device_id_type=MESH)
        for i in range(len(srcs)) for j, (px, py) in enumerate(_other_chips(x, y))]


def _copies_start(srcs, lands, make_copies, thru=(), *, name):
    n = len(srcs)
    n_ops = 2 * n + len(thru)

    def body(*refs):
        send_sems, recv_sems = refs[n_ops], refs[n_ops + 1]
        for cp in make_copies(refs[:n], refs[n:2 * n], send_sems, recv_sems):
            cp.start()
        refs[-1][...] = jnp.zeros_like(refs[-1])

    ops = list(srcs) + list(lands) + list(thru)
    outs = pl.pallas_call(
        body, name=name,
        out_shape=(pltpu.SemaphoreType.DMA((3 * n,)), pltpu.SemaphoreType.DMA((3 * n,)),
                   *[pltpu.HBM(a.shape, a.dtype) for a in ops], jax.ShapeDtypeStruct((8, 128), F32)),
        in_specs=[_HBM] * n_ops,
        out_specs=(_SEM, _SEM, *[_HBM] * n_ops, pl.BlockSpec(memory_space=pltpu.VMEM)),
        input_output_aliases={i: 2 + i for i in range(n_ops)},
        compiler_params=pltpu.CompilerParams(has_side_effects=_EFFECT),
    )(*[pltpu.with_memory_space_constraint(a, pltpu.HBM) for a in ops])
    return outs[0], outs[1], list(outs[2:2 + n]), list(outs[2 + n:2 + 2 * n]), outs[-1], list(outs[2 + 2 * n:-1])


def _copies_wait(handle, after, make_copies, *, name):
    send_sems, recv_sems, srcs, lands = handle[:4]
    n = len(srcs)

    def body(*refs):
        for cp in make_copies(refs[:n], refs[n:2 * n], refs[2 * n], refs[2 * n + 1]):
            cp.wait_send()
            cp.wait_recv()

    ops = list(srcs) + list(lands)
    outs = pl.pallas_call(
        body, name=name,
        out_shape=tuple(pltpu.HBM(a.shape, a.dtype) for a in ops),
        in_specs=[_HBM] * (2 * n) + [_SEM, _SEM, _ANY],
        out_specs=tuple([_HBM] * (2 * n)),
        input_output_aliases={i: i for i in range(2 * n)},
        compiler_params=pltpu.CompilerParams(has_side_effects=_EFFECT),
    )(*ops, send_sems, recv_sems, after)
    return list(outs[n:])


def _swap_with_sibling(arrs, *, name):
    n = len(arrs)

    def body(*refs):
        ins, outs = refs[:n], refs[n:2 * n]
        send_sems, recv_sems = refs[2 * n:]
        x, y, c = _place()
        remote = []
        for i in range(n):
            rc = pltpu.make_async_remote_copy(
                src_ref=ins[i], dst_ref=outs[i], send_sem=send_sems.at[i], recv_sem=recv_sems.at[i],
                device_id=(x, y, 1 - c), device_id_type=MESH)
            rc.start()
            remote.append(rc)
        for rc in remote:
            rc.wait_send()
        for rc in remote:
            rc.wait_recv()

    return pl.pallas_call(
        body, in_specs=[_ANY] * n, out_specs=[_ANY] * n,
        out_shape=[jax.ShapeDtypeStruct(a.shape, a.dtype) for a in arrs],
        scratch_shapes=[pltpu.SemaphoreType.DMA((n,)), pltpu.SemaphoreType.DMA((n,))],
        name=name)(*arrs)


def _gather_all(buf, *, name):
    R, C = buf.shape

    def body(in_ref, out_ref, send_sems, recv_sems, local_sem):
        x, y, c = _place()
        me = 4 * x + 2 * y + c
        lc = pltpu.make_async_copy(in_ref, out_ref.at[me], local_sem)
        lc.start()
        remote = []
        for k in range(1, N_DEV):
            px = 1 - x if (k >> 2) & 1 else x
            py = 1 - y if (k >> 1) & 1 else y
            pc = 1 - c if k & 1 else c
            rc = pltpu.make_async_remote_copy(
                src_ref=in_ref, dst_ref=out_ref.at[me], send_sem=send_sems.at[k - 1],
                recv_sem=recv_sems.at[k - 1], device_id=(px, py, pc), device_id_type=MESH)
            rc.start()
            remote.append(rc)
        lc.wait()
        for rc in remote:
            rc.wait_send()
        for rc in remote:
            rc.wait_recv()

    return pl.pallas_call(
        body, in_specs=[_ANY], out_specs=_ANY,
        out_shape=jax.ShapeDtypeStruct((N_DEV, R, C), buf.dtype),
        scratch_shapes=[pltpu.SemaphoreType.DMA((N_DEV - 1,)), pltpu.SemaphoreType.DMA((N_DEV - 1,)),
                        pltpu.SemaphoreType.DMA],
        name=name)(buf)


def _w_in_to_z(w):
    pad = jnp.zeros(w.shape[:-1] + (64,), w.dtype)
    return jnp.concatenate([w[..., 0:512], w[..., 512:1024], w[..., 1344:1856], w[..., 1024:1280],
                            w[..., 1280:1344], pad], axis=-1)


def _z_to_w_in(g):
    return jnp.concatenate([g[..., 0:512], g[..., 512:1024], g[..., 1536:1792], g[..., 1792:1856],
                            g[..., 1024:1536]], axis=-1)


def _pad_heads(w, nh):
    w = w.reshape(w.shape[:-1] + (nh, QK))
    w = jnp.concatenate([w, jnp.zeros(w.shape[:-1] + (HEAD_PAD - QK,), w.dtype)], axis=-1)
    return w.reshape(w.shape[:-2] + (nh * HEAD_PAD,))


def _unpad_heads(g, nh):
    g = g.reshape(g.shape[:-1] + (nh, HEAD_PAD))[..., :QK]
    return g.reshape(g.shape[:-2] + (nh * QK,))


def _kv_split(w, nh):
    w = w.reshape(w.shape[:-1] + (nh, 2, 128))
    return jnp.swapaxes(w, -3, -2).reshape(w.shape[:-3] + (nh * 256,))


def _kv_join(g, nh):
    g = g.reshape(g.shape[:-1] + (2, nh, 128))
    return jnp.swapaxes(g, -3, -2).reshape(g.shape[:-3] + (nh * 256,))


def _pad_gain(g):
    return jnp.concatenate([g, jnp.zeros((1, HEAD_PAD - QK), g.dtype)], axis=1)


_SMALL = ("g_mix", "g_q_lat", "g_kv_lat", "g_q_mla", "g_k_mla", "w_pool", "pool_scale", "g_mem", "g_q_x",
          "g_k_x", "g_ffn", "conv_b", "conv_w")


def _pack(arrs, extra=0):
    flat = jnp.concatenate([a.reshape(-1) for a in arrs])
    n = flat.shape[0] + extra
    rows = -(-n // 1024) * 8
    return jnp.pad(flat, (0, rows * 128 - flat.shape[0])).reshape(rows, 128)


def _unpack(buf, shapes):
    flat = buf.reshape(-1)
    out, off = [], 0
    for s in shapes:
        n = int(np.prod(s))
        out.append(flat[off:off + n].reshape(s))
        off += n
    return out, off


def _tied(a, token):
    return a + token[:1, :1].astype(a.dtype)


def _local_step(x, mem, target, W, fetch=None, ship=None):
    fetch = fetch or (lambda group, after: None)
    ship = ship or (lambda group, G: jnp.zeros((8, 128), F32))
    S, D = x.shape
    F = W["conv_b"].shape[1]
    tabs = _rope_tables(S)
    tm = 512 if S % 512 == 0 else 128
    tl = 1024 if S % 1024 == 0 else tm
    tk = _pick(S, (1024, 512, 128))

    h = _rms_fwd(x, W["g_mix"], C=D, name="norm_mix")
    fetch("g1", h)
    z = mm_nn(h, W["w_in"], tm=tl, tn=Z_COLS, tk=D, name="z_proj")
    fetch("g2", z)
    y_pool = _pool_fwd(z, W["w_pool"], W["pool_scale"], name="pool_fwd")
    ql = _rms_fwd(z, W["g_q_lat"], C=Q_RANK, cb=Z_Q_CB, name="norm_qlat")
    kvl = _rms_fwd(z, W["g_kv_lat"], C=KV_RANK, cb=Z_KV_CB, name="norm_kvlat")
    qraw = mm_nn(ql, W["w_q_up"], nsh=N_CHIPS, tm=tl, tn=512, tk=Q_RANK, name="q_up")
    kvraw = mm_nn(kvl, W["w_kv_up"], nsh=N_CHIPS, tm=tl, tn=512, tk=KV_RANK, name="kv_up")
    q, k, v, vt = _qkrope_fwd(qraw, kvraw, z, W["g_q_mla"], W["g_k_mla"], tabs, name="qk_norm_rope")
    o, y_mla, lse = _flash_fwd(q, k, vt, name="mla_fwd")
    memn = _rms_fwd(mem, W["g_mem"], C=D, name="norm_mem")
    M = mem.shape[0]
    mkv = mm_nn(memn, W["w_mem_kv"], tm=M, tn=1024, tk=D, name="mem_kv")
    kx, vx = _memk_fwd(mkv, W["g_k_x"], name="memk_fwd")
    y_mem = _memattn_fwd(z, kx, vx, W["g_q_x"], name="memattn_fwd")
    cat = jnp.concatenate([y_pool, y_mla, y_mem], axis=1)
    x2 = mm_nn(cat, W["w_o"], tm=tm, tn=D, tk=D, add=x, name="o_proj")
    h2 = _rms_fwd(x2, W["g_ffn"], C=D, name="norm_ffn")
    fetch("g3", h2)
    fn = F // N_CHIPS
    g = mm_nn(h2, W["w_gate"], nsh=N_CHIPS, tm=tl, tn=fn, tk=D, out_dtypes=(BF16,), name="gate_proj")
    u = mm_nn(h2, W["w_up"], nsh=N_CHIPS, tm=tl, tn=fn, tk=D, out_dtypes=(BF16,), name="up_proj")
    a = _glu_fwd(g, u, W["conv_w"], W["conv_b"], name="glu_fwd")
    y = mm_nn(a, W["w_down"], tm=tm, tn=1024, tk=F // 2, add=x2, name="down_proj")
    dy, dyb, loss_row = _loss_head(y, target, name="loss_head")

    G = {}
    d_a = mm_nt(dyb, W["w_down"], tm=tl, to=512, tc=D, out_dtypes=(BF16,), name="d_a")
    G["w_down"] = mm_tn(a, dyb, to=fn, tn=1024, tk=tk, out_dtypes=(F32, BF16), name="grad_w_down")
    d_g, d_u, G["conv_w"], G["conv_b"] = _glu_bwd(g, u, d_a, W["conv_w"], W["conv_b"], name="glu_bwd")
    G["w_gate"] = mm_tn(h2, d_g, nsh=N_CHIPS, to=1024, tn=fn, tk=tk, out_dtypes=(F32, BF16), name="grad_w_gate")
    G["w_up"] = mm_tn(h2, d_u, nsh=N_CHIPS, to=1024, tn=fn, tk=tk, out_dtypes=(F32, BF16), name="grad_w_up")
    tok = ship("s1", G)
    d_h2 = mm_nt_pair(d_g, W["w_gate"], d_u, W["w_up"], nsh=N_CHIPS, tm=tm, to=1024, tc=fn, name="d_h2")
    d_x2, d_x2b, G["g_ffn"] = _rms_bwd(x2, d_h2, _tied(W["g_ffn"], tok), C=D, res=dy, out_dtypes=(F32, BF16),
                                       name="norm_ffn_bwd")

    d_cat = mm_nt(d_x2b, W["w_o"], tm=tl, to=1024, tc=D, out_dtypes=(BF16,), name="d_cat")
    G["w_o"] = mm_tn(cat, d_x2b, to=1024, tn=1024, tk=tk, out_dtypes=(F32, BF16), name="grad_w_o")
    tok = ship("s2", G)
    dz_pool, G["w_pool"], G["pool_scale"] = _pool_bwd(z, d_cat, W["w_pool"], _tied(W["pool_scale"], tok),
                                                      name="pool_bwd")
    dz_mq, dkx, dvx, G["g_q_x"] = _memattn_bwd(z, kx, vx, W["g_q_x"], d_cat, name="memattn_bwd")
    d_mkv, G["g_k_x"] = _memk_bwd(mkv, W["g_k_x"], dkx, dvx, name="memk_bwd")
    G["w_mem_kv"] = mm_tn(memn, d_mkv, to=1024, tn=1024, tk=M, out_dtypes=(F32, BF16), name="grad_w_mem_kv")
    d_memn = mm_nt(d_mkv, W["w_mem_kv"], tm=M, to=D, tc=1024, name="d_memn")
    _, G["g_mem"] = _rms_bwd(mem, d_memn, W["g_mem"], C=D, name="norm_mem_bwd")
    delta = _attn_bwd_prep(o, d_cat, name="mla_bwd_prep")
    dq, dk, dv = _flash_bwd(q, k, v, d_cat, lse, delta, name="mla_bwd")
    d_qraw, d_kvraw, dz_kr, G["g_q_mla"], G["g_k_mla"] = _qkrope_bwd(
        qraw, kvraw, z, W["g_q_mla"], W["g_k_mla"], tabs, dq, dk, dv, name="qk_norm_rope_bwd")
    G["w_q_up"] = mm_tn(ql, d_qraw, nsh=N_CHIPS, to=Q_RANK, tn=512, tk=tk, out_dtypes=(F32, BF16), name="grad_w_q_up")
    d_ql = mm_nt(d_qraw, W["w_q_up"], nsh=N_CHIPS, tm=tl, to=Q_RANK, tc=512, name="d_ql")
    G["w_kv_up"] = mm_tn(kvl, d_kvraw, nsh=N_CHIPS, to=KV_RANK, tn=512, tk=tk, out_dtypes=(F32, BF16),
                         name="grad_w_kv_up")
    d_kvl = mm_nt(d_kvraw, W["w_kv_up"], nsh=N_CHIPS, tm=tl, to=KV_RANK, tc=512, name="d_kvl")
    dz_q, G["g_q_lat"] = _rms_bwd(z, d_ql, W["g_q_lat"], C=Q_RANK, cb=Z_Q_CB, out_dtypes=(BF16,), name="norm_qlat_bwd")
    dz_kv, G["g_kv_lat"] = _rms_bwd(z, d_kvl, W["g_kv_lat"], C=KV_RANK, cb=Z_KV_CB, out_dtypes=(BF16,),
                                    name="norm_kvlat_bwd")
    d_z = jnp.concatenate([dz_pool, dz_q, dz_mq, dz_kv, dz_kr], axis=1)
    G["w_in"] = mm_tn(h, d_z, to=512, tn=Z_COLS, tk=tk, out_dtypes=(F32, BF16), name="grad_w_in")
    tok = ship("s3", G)
    d_h = mm_nt(d_z, W["w_in"], tm=tl, to=1024, tc=Z_COLS, name="d_h")
    grad_x, G["g_mix"] = _rms_bwd(x, d_h, _tied(W["g_mix"], tok), C=D, res=d_x2, name="norm_mix_bwd")
    return loss_row, grad_x, G


_BIG = ("w_in", "w_q_up", "w_kv_up", "w_mem_kv", "w_o", "w_gate", "w_up", "w_down")
_WEIGHTS = ("g_mix", "w_in", "g_q_lat", "w_q_up", "g_kv_lat", "w_kv_up", "g_q_mla", "g_k_mla", "w_pool",
            "pool_scale", "g_mem", "w_mem_kv", "g_q_x", "g_k_x", "w_o", "g_ffn", "w_gate", "w_up", "conv_w",
            "conv_b", "w_down")


def _to_compute_layout(name, w):
    if name == "w_in":
        return _w_in_to_z(w)
    if name == "w_q_up":
        return _pad_heads(w, w.shape[-1] // QK)
    if name == "w_kv_up":
        return _kv_split(w, w.shape[-1] // 256)
    return w


def _from_compute_layout(name, g):
    if name == "w_in":
        return _z_to_w_in(g)
    if name == "w_q_up":
        return _unpad_heads(g, g.shape[-1] // HEAD_PAD)
    if name == "w_kv_up":
        return _kv_join(g, g.shape[-1] // 256)
    return g


def kernel(x, mem, g_mix, w_in, g_q_lat, w_q_up, g_kv_lat, w_kv_up, g_q_mla, g_k_mla, w_pool, pool_scale, g_mem, w_mem_kv, g_q_x, g_k_x, w_o, g_ffn, w_gate, w_up, conv_w, conv_b, w_down, loss_target, m_g_mix, m_w_in, m_g_q_lat, m_w_q_up, m_g_kv_lat, m_w_kv_up, m_g_q_mla, m_g_k_mla, m_w_pool, m_pool_scale, m_g_mem, m_w_mem_kv, m_g_q_x, m_g_k_x, m_w_o, m_g_ffn, m_w_gate, m_w_up, m_conv_w, m_conv_b, m_w_down, v_g_mix, v_w_in, v_g_q_lat, v_w_q_up, v_g_kv_lat, v_w_kv_up, v_g_q_mla, v_g_k_mla, v_w_pool, v_pool_scale, v_g_mem, v_w_mem_kv, v_g_q_x, v_g_k_x, v_w_o, v_g_ffn, v_w_gate, v_w_up, v_conv_w, v_conv_b, v_w_down):
    P = dict(g_mix=g_mix, w_in=w_in, g_q_lat=g_q_lat, w_q_up=w_q_up, g_kv_lat=g_kv_lat, w_kv_up=w_kv_up,
             g_q_mla=g_q_mla, g_k_mla=g_k_mla, w_pool=w_pool, pool_scale=pool_scale, g_mem=g_mem,
             w_mem_kv=w_mem_kv, g_q_x=g_q_x, g_k_x=g_k_x, w_o=w_o, g_ffn=g_ffn, w_gate=w_gate, w_up=w_up,
             conv_w=conv_w, conv_b=conv_b, w_down=w_down)
    Mo = dict(g_mix=m_g_mix, w_in=m_w_in, g_q_lat=m_g_q_lat, w_q_up=m_w_q_up, g_kv_lat=m_g_kv_lat,
              w_kv_up=m_w_kv_up, g_q_mla=m_g_q_mla, g_k_mla=m_g_k_mla, w_pool=m_w_pool,
              pool_scale=m_pool_scale, g_mem=m_g_mem, w_mem_kv=m_w_mem_kv, g_q_x=m_g_q_x, g_k_x=m_g_k_x,
              w_o=m_w_o, g_ffn=m_g_ffn, w_gate=m_w_gate, w_up=m_w_up, conv_w=m_conv_w, conv_b=m_conv_b,
              w_down=m_w_down)
    Vo = dict(g_mix=v_g_mix, w_in=v_w_in, g_q_lat=v_g_q_lat, w_q_up=v_w_q_up, g_kv_lat=v_g_kv_lat,
              w_kv_up=v_w_kv_up, g_q_mla=v_g_q_mla, g_k_mla=v_g_k_mla, w_pool=v_w_pool,
              pool_scale=v_pool_scale, g_mem=v_g_mem, w_mem_kv=v_w_mem_kv, g_q_x=v_g_q_x, g_k_x=v_g_k_x,
              w_o=v_w_o, g_ffn=v_g_ffn, w_gate=v_w_gate, w_up=v_w_up, conv_w=v_conv_w, conv_b=v_conv_b,
              w_down=v_w_down)
    xi, yi, ci = _place()
    me = (2 * xi + yi).astype(jnp.int32).reshape(1)

    shard = {n: _to_compute_layout(n, P[n][0]).astype(BF16) for n in _BIG}
    shard["conv_w"] = conv_w[0]
    gather_groups = {"g1": ("w_in",), "g2": ("w_q_up", "w_kv_up", "w_mem_kv", "w_o"),
                     "g3": ("w_gate", "w_up", "w_down", "conv_w")}
    gathers = {}

    def start_gather(grp, thru=()):
        names = gather_groups[grp]
        lands = [lax.dynamic_update_slice(lax.empty((N_CHIPS,) + shard[n].shape, shard[n].dtype), shard[n][None],
                                          (me[0], 0, 0)) for n in names]
        gathers[grp] = _copies_start([shard[n] for n in names], lands, _gather_copies, thru,
                                     name="gather_start_" + grp)
        return gathers[grp][5]

    start_gather("g1")
    W = {}
    W["g_q_mla"], W["g_k_mla"] = _pad_gain(g_q_mla), _pad_gain(g_k_mla)
    W["w_pool"] = w_pool[0].astype(BF16)
    for n in ("g_mix", "g_q_lat", "g_kv_lat", "pool_scale", "g_mem", "g_q_x", "g_k_x", "g_ffn", "conv_b"):
        W[n] = P[n]
    W["g_mix"] = _tied(W["g_mix"], gathers["g1"][4])

    def fetch(grp, after):
        stacks = _copies_wait(gathers[grp], after, _gather_copies, name="gather_wait_" + grp)
        if grp == "g1":
            stacks = start_gather("g3", start_gather("g2", stacks))
        for n, s in zip(gather_groups[grp], stacks):
            if n == "conv_w":
                W[n] = jnp.swapaxes(s, 0, 1).reshape(3, -1)
            else:
                W[n] = s.reshape(-1, s.shape[-1])

    shard_shape = {n: shard[n].shape for n in _BIG}
    scatter_groups = {"s1": ("w_down", "w_gate", "w_up"), "s2": ("w_o",),
                      "s3": ("w_mem_kv", "w_q_up", "w_kv_up", "w_in")}
    scatters = {}

    def ship(grp, G):
        names = scatter_groups[grp]
        srcs = [G[n][1].reshape((N_CHIPS,) + shard_shape[n]) for n in names]
        lands = [lax.empty((N_CHIPS - 1,) + shard_shape[n], BF16) for n in names]
        scatters[grp] = _copies_start(srcs, lands, _scatter_copies, name="scatter_start_" + grp)
        return scatters[grp][4]

    loss_row, grad_x, G = _local_step(x[0], mem[0], loss_target[0], W, fetch, ship)

    recv = {}
    for grp, names in scatter_groups.items():
        for n, r in zip(names, _copies_wait(scatters[grp], grad_x, _scatter_copies, name="scatter_wait_" + grp)):
            recv[n] = r
    part = [_sum4(G[n][0].reshape((N_CHIPS,) + shard_shape[n]), recv[n], me, name="sum4_" + n) for n in _BIG]
    part = [_from_compute_layout(n, p) for n, p in zip(_BIG, part)]
    sib = _swap_with_sibling(part, name="swap_grads")
    out = {}
    for n, p, s in zip(_BIG, part, sib):
        out[n] = [r[None] for r in _adamw(P[n][0], Mo[n][0], Vo[n][0], [p, s], name="adamw_" + n)]

    conv_w_full_grad = G["conv_w"]
    small_g = [G["g_mix"], G["g_q_lat"], G["g_kv_lat"], G["g_q_mla"][:, :QK], G["g_k_mla"][:, :QK], G["w_pool"],
               G["pool_scale"], G["g_mem"], G["g_q_x"], G["g_k_x"], G["g_ffn"], G["conv_b"], conv_w_full_grad]
    packed = _pack(small_g + [loss_row[:, :1]])
    total = _sum8(_gather_all(packed, name="gather_small"), name="sum_small")
    shapes = [a.shape for a in small_g] + [(1, 1)]
    (parts, _) = _unpack(total, shapes)
    loss = parts[-1].reshape(())
    F = conv_b.shape[1]
    fn = F // N_CHIPS
    col0 = (2 * xi + yi) * fn
    sg = dict(zip(_SMALL, parts[:-1]))
    sg["conv_w"] = lax.dynamic_slice(sg["conv_w"], (0, col0), (3, fn))
    sw = [P[n].reshape(sg[n].shape) for n in _SMALL]
    sm = [Mo[n].reshape(sg[n].shape) for n in _SMALL]
    sv = [Vo[n].reshape(sg[n].shape) for n in _SMALL]
    gp = _pack([sg[n] for n in _SMALL])
    res = _adamw(_pack(sw), _pack(sm), _pack(sv), [gp], name="adamw_small")
    sshapes = [sg[n].shape for n in _SMALL]
    for kind, buf in zip(range(4), res):
        vals, _ = _unpack(buf, sshapes)
        for n, val in zip(_SMALL, vals):
            out.setdefault(n, [None] * 4)[kind] = val.reshape(P[n].shape)

    return (loss, grad_x[None], *[out[n][0] for n in _WEIGHTS], *[out[n][1] for n in _WEIGHTS],
            *[out[n][2] for n in _WEIGHTS], *[out[n][3] for n in _WEIGHTS])
```

```python
import functools
import math

import numpy as np
import jax
import jax.numpy as jnp
from jax import lax
from jax.experimental import pallas as pl
from jax.experimental.pallas import tpu as pltpu

F32, BF16 = jnp.float32, jnp.bfloat16
NORM_EPS = 1e-6
ROPE_THETA = 10000.0
V7X_VMEM_LIMIT_BYTES = 48 * 1024 * 1024
N_CHIPS = 4
N_DEV = 8

POOL_W = 512
POOL_WINDOWS = (2, 4, 8, 16)
HEADS = 8
NOPE, ROPE, QK = 128, 64, 192
HEAD_PAD = 256
Q_RANK, KV_RANK = 512, 256
X_HEADS, X_DIM = 4, 128
Z_COLS = 1920
Z_POOL_CB, Z_Q_CB, Z_MQ_CB = 0, 1, 2
Z_KV_CB = 6
Z_KR_CB = 14

ADAM_LR, ADAM_B1, ADAM_B2, ADAM_EPS, ADAM_WD, ADAM_STEP = 0.001, 0.9, 0.999, 1e-08, 0.01, 10

MESH = pl.DeviceIdType.MESH


def _cp(sem):
    return pltpu.CompilerParams(dimension_semantics=sem, vmem_limit_bytes=V7X_VMEM_LIMIT_BYTES)


def _row_tile(S):
    return 256 if S % 256 == 0 and S >= 2048 else 128


def _big_row_tile(S):
    return 512 if S % 512 == 0 and S >= 2048 else _row_tile(S)


def _pick(dim, prefs):
    for p in prefs:
        if dim % p == 0:
            return p
    return dim


_DN = {"nn": (((1,), (0,)), ((), ())), "nt": (((1,), (1,)), ((), ())), "tn": (((0,), (0,)), ((), ()))}


def _mm(a, b, *, mode, grid, blocks, maps, out_shape, out_dtypes, add=None, name):
    nk = grid[2]
    dn = _DN[mode]
    n_out = len(out_dtypes)

    def body(*refs):
        a_ref, b_ref = refs[0], refs[1]
        add_ref = refs[2] if add is not None else None
        p = 2 + (add is not None)
        o_refs = refs[p:p + n_out]

        def finish(r):
            if add_ref is not None:
                r = r + add_ref[...]
            for o in o_refs:
                o[...] = r.astype(o.dtype)

        def product():
            return lax.dot_general(a_ref[...].astype(BF16), b_ref[...].astype(BF16), dn, preferred_element_type=F32)

        if nk == 1:
            finish(product())
            return
        acc = refs[p + n_out]
        k = pl.program_id(2)

        @pl.when(k == 0)
        def _():
            acc[...] = jnp.zeros_like(acc)

        acc[...] += product()

        @pl.when(k == nk - 1)
        def _():
            finish(acc[...])

    a_blk, b_blk, o_blk = blocks
    a_map, b_map, o_map = maps
    in_specs = [pl.BlockSpec(a_blk, a_map), pl.BlockSpec(b_blk, b_map)]
    args = [a, b]
    if add is not None:
        in_specs.append(pl.BlockSpec(o_blk, o_map))
        args.append(add)
    outs = pl.pallas_call(
        body, grid=grid, in_specs=in_specs,
        out_specs=[pl.BlockSpec(o_blk, o_map) for _ in out_dtypes],
        out_shape=[jax.ShapeDtypeStruct(out_shape, d) for d in out_dtypes],
        scratch_shapes=[pltpu.VMEM(o_blk, F32)] if nk > 1 else [],
        compiler_params=_cp(("parallel", "parallel", "arbitrary")), name=name)(*args)
    return outs[0] if n_out == 1 else outs


def mm_nn(a, w, *, nsh=1, tm, tn, tk, out_dtypes=(F32,), add=None, name):
    M, K = a.shape
    n = w.shape[1]
    N = nsh * n
    assert w.shape[0] == nsh * K and n % tn == 0 and K % tk == 0 and M % tm == 0
    npt, kt = n // tn, K // tk
    return _mm(a, w, mode="nn", grid=(M // tm, N // tn, kt),
               blocks=((tm, tk), (tk, tn), (tm, tn)),
               maps=(lambda i, j, k: (i, k), lambda i, j, k: ((j // npt) * kt + k, j % npt),
                     lambda i, j, k: (i, j)),
               out_shape=(M, N), out_dtypes=out_dtypes, add=add, name=name)


def mm_nt(d, w, *, nsh=1, tm, to, tc, out_dtypes=(F32,), add=None, name):
    M, N = d.shape
    n = w.shape[1]
    K = w.shape[0] // nsh
    assert nsh * n == N and n % tc == 0 and K % to == 0 and M % tm == 0
    cpt, ot = n // tc, K // to
    return _mm(d, w, mode="nt", grid=(M // tm, ot, N // tc),
               blocks=((tm, tc), (to, tc), (tm, to)),
               maps=(lambda i, j, c: (i, c), lambda i, j, c: ((c // cpt) * ot + j, c % cpt),
                     lambda i, j, c: (i, j)),
               out_shape=(M, K), out_dtypes=out_dtypes, add=add, name=name)


def mm_nt_pair(d1, w1, d2, w2, *, nsh, tm, to, tc, name):
    M, N = d1.shape
    n = w1.shape[1]
    K = w1.shape[0] // nsh
    assert d2.shape == d1.shape and w2.shape == w1.shape and nsh * n == N
    assert n % tc == 0 and K % to == 0 and M % tm == 0
    cpt, ot, nk = n // tc, K // to, N // tc

    def body(a1_ref, b1_ref, a2_ref, b2_ref, o_ref, acc):
        k = pl.program_id(2)

        @pl.when(k == 0)
        def _():
            acc[...] = jnp.zeros_like(acc)

        acc[...] += lax.dot_general(a1_ref[...], b1_ref[...], _DN["nt"], preferred_element_type=F32)
        acc[...] += lax.dot_general(a2_ref[...], b2_ref[...], _DN["nt"], preferred_element_type=F32)

        @pl.when(k == nk - 1)
        def _():
            o_ref[...] = acc[...]

    a_spec = pl.BlockSpec((tm, tc), lambda i, j, c: (i, c))
    b_spec = pl.BlockSpec((to, tc), lambda i, j, c: ((c // cpt) * ot + j, c % cpt))
    return pl.pallas_call(
        body, grid=(M // tm, ot, nk), in_specs=[a_spec, b_spec, a_spec, b_spec],
        out_specs=pl.BlockSpec((tm, to), lambda i, j, c: (i, j)),
        out_shape=jax.ShapeDtypeStruct((M, K), F32),
        scratch_shapes=[pltpu.VMEM((tm, to), F32)],
        compiler_params=_cp(("parallel", "parallel", "arbitrary")), name=name)(d1, w1, d2, w2)


def mm_tn(x, d, *, nsh=1, to, tn, tk, out_dtypes=(F32,), name):
    M, K = x.shape
    N = d.shape[1]
    n = N // nsh
    assert n % tn == 0 and K % to == 0 and M % tk == 0
    npt, ot = n // tn, K // to
    return _mm(x, d, mode="tn", grid=(ot, N // tn, M // tk),
               blocks=((tk, to), (tk, tn), (to, tn)),
               maps=(lambda i, j, k: (k, i), lambda i, j, k: (k, j),
                     lambda i, j, k: ((j // npt) * ot + i, j % npt)),
               out_shape=(nsh * K, n), out_dtypes=out_dtypes, name=name)


def _rms_fwd(x, g, *, C, cb=0, name):
    S = x.shape[0]
    tm = _big_row_tile(S) if S >= 128 else S

    def body(x_ref, g_ref, o_ref):
        xv = x_ref[...]
        r = lax.rsqrt(jnp.mean(xv * xv, axis=-1, keepdims=True) + NORM_EPS)
        o_ref[...] = ((xv * r) * g_ref[...]).astype(o_ref.dtype)

    return pl.pallas_call(
        body, grid=(S // tm,),
        in_specs=[pl.BlockSpec((tm, C), lambda i: (i, cb)), pl.BlockSpec((1, C), lambda i: (0, 0))],
        out_specs=pl.BlockSpec((tm, C), lambda i: (i, 0)),
        out_shape=jax.ShapeDtypeStruct((S, C), BF16),
        compiler_params=_cp(("parallel",)), name=name)(x, g)


def _rms_bwd(x, dh, g, *, C, cb=0, res=None, out_dtypes=(F32,), name):
    S = x.shape[0]
    tm = (_big_row_tile(S) if C <= 512 else _row_tile(S)) if S >= 128 else S
    n_out = len(out_dtypes)

    def body(*refs):
        x_ref, dh_ref, g_ref = refs[:3]
        res_ref = refs[3] if res is not None else None
        p = 3 + (res is not None)
        outs = refs[p:p + n_out]
        dg_ref = refs[p + n_out]
        i = pl.program_id(0)
        xv = x_ref[...]
        r = lax.rsqrt(jnp.mean(xv * xv, axis=-1, keepdims=True) + NORM_EPS)
        n = xv * r
        dhv = dh_ref[...].astype(F32)
        dn = dhv * g_ref[...]
        c = jnp.mean(dn * n, axis=-1, keepdims=True)
        dx = r * (dn - n * c)
        if res_ref is not None:
            dx = res_ref[...] + dx
        for o in outs:
            o[...] = dx.astype(o.dtype)

        @pl.when(i == 0)
        def _():
            dg_ref[...] = jnp.zeros_like(dg_ref)

        dg_ref[...] += jnp.sum(dhv * n, axis=0, keepdims=True)

    row = pl.BlockSpec((tm, C), lambda i: (i, 0))
    in_specs = [pl.BlockSpec((tm, C), lambda i: (i, cb)), row, pl.BlockSpec((1, C), lambda i: (0, 0))]
    args = [x, dh, g]
    if res is not None:
        in_specs.append(row)
        args.append(res)
    return pl.pallas_call(
        body, grid=(S // tm,), in_specs=in_specs,
        out_specs=[row] * n_out + [pl.BlockSpec((1, C), lambda i: (0, 0))],
        out_shape=[jax.ShapeDtypeStruct((S, C), d) for d in out_dtypes] + [jax.ShapeDtypeStruct((1, C), F32)],
        compiler_params=_cp(("arbitrary",)), name=name)(*args)


def _pool_cnt(t0, rows, w):
    t = t0 + lax.broadcasted_iota(jnp.int32, (rows, 1), 0)
    return jnp.minimum(t + 1, w).astype(F32)


def _pool_d(halo, tile, gi, t0, tm):
    s = jnp.concatenate([halo, tile], axis=0)
    for step in (1, 2, 4, 8)[:gi + 1]:
        s = s + pltpu.roll(s, step, 0)
    return s[16:] / _pool_cnt(t0, tm, POOL_WINDOWS[gi]) - tile


def _pool_fwd(z, w_pool, pool_scale, *, name):
    S = z.shape[0]
    tm = _row_tile(S)
    hb = tm // 16

    def body(z_ref, h_ref, w_ref, sc_ref, o_ref):
        i = pl.program_id(0)
        halo = h_ref[...] * (i > 0).astype(F32)
        for gi in range(4):
            cs = slice(gi * 128, (gi + 1) * 128)
            d = _pool_d(halo[:, cs], z_ref[:, cs], gi, i * tm, tm)
            yp = jnp.dot(d.astype(BF16), w_ref[gi], preferred_element_type=F32)
            o_ref[:, cs] = (yp * sc_ref[:, cs]).astype(o_ref.dtype)

    return pl.pallas_call(
        body, grid=(S // tm,),
        in_specs=[pl.BlockSpec((tm, POOL_W), lambda i: (i, Z_POOL_CB)),
                  pl.BlockSpec((16, POOL_W), lambda i: (jnp.maximum(i * hb - 1, 0), Z_POOL_CB)),
                  pl.BlockSpec((4, 128, 128), lambda i: (0, 0, 0)),
                  pl.BlockSpec((1, POOL_W), lambda i: (0, 0))],
        out_specs=pl.BlockSpec((tm, POOL_W), lambda i: (i, 0)),
        out_shape=jax.ShapeDtypeStruct((S, POOL_W), BF16),
        compiler_params=_cp(("parallel",)), name=name)(z, z, w_pool, pool_scale)


def _pool_bwd(z, d_cat, w_pool, pool_scale, *, name):
    S = z.shape[0]
    tm = _row_tile(S)
    hb = tm // 16
    nt = S // tm
    E = tm + 16

    def body(z_ref, h_ref, dy_ref, dyn_ref, w_ref, sc_ref, dz_ref, gw_ref, gs_ref):
        i = pl.program_id(0)

        @pl.when(i == 0)
        def _():
            gw_ref[...] = jnp.zeros_like(gw_ref)
            gs_ref[...] = jnp.zeros_like(gs_ref)

        halo = h_ref[...] * (i > 0).astype(F32)
        dy_next = dyn_ref[...].astype(F32) * (i < nt - 1).astype(F32)
        for gi in range(4):
            cs = slice(gi * 128, (gi + 1) * 128)
            w = w_ref[gi]
            d = _pool_d(halo[:, cs], z_ref[:, cs], gi, i * tm, tm)
            db = d.astype(BF16)
            dy = dy_ref[:, cs].astype(F32)
            yp = jnp.dot(db, w, preferred_element_type=F32)
            gs_ref[:, cs] += jnp.sum(dy * yp, axis=0, keepdims=True)
            sc = sc_ref[:, cs]
            dys = (dy * sc).astype(BF16)
            gw_ref[gi] += lax.dot_general(db, dys, _DN["tn"], preferred_element_type=F32)
            dys_ext = jnp.concatenate([dys, (dy_next[:, cs] * sc).astype(BF16)], axis=0)
            dd = lax.dot_general(dys_ext, w, _DN["nt"], preferred_element_type=F32)
            r = dd / _pool_cnt(i * tm, E, POOL_WINDOWS[gi])
            for step in (1, 2, 4, 8)[:gi + 1]:
                r = r + pltpu.roll(r, E - step, 0)
            dz_ref[:, cs] = (r[:tm] - dd[:tm]).astype(dz_ref.dtype)

    return pl.pallas_call(
        body, grid=(nt,),
        in_specs=[pl.BlockSpec((tm, POOL_W), lambda i: (i, Z_POOL_CB)),
                  pl.BlockSpec((16, POOL_W), lambda i: (jnp.maximum(i * hb - 1, 0), Z_POOL_CB)),
                  pl.BlockSpec((tm, POOL_W), lambda i: (i, 0)),
                  pl.BlockSpec((16, POOL_W), lambda i: (jnp.minimum((i + 1) * hb, S // 16 - 1), 0)),
                  pl.BlockSpec((4, 128, 128), lambda i: (0, 0, 0)),
                  pl.BlockSpec((1, POOL_W), lambda i: (0, 0))],
        out_specs=[pl.BlockSpec((tm, POOL_W), lambda i: (i, 0)),
                   pl.BlockSpec((4, 128, 128), lambda i: (0, 0, 0)),
                   pl.BlockSpec((1, POOL_W), lambda i: (0, 0))],
        out_shape=[jax.ShapeDtypeStruct((S, POOL_W), BF16),
                   jax.ShapeDtypeStruct((4, 128, 128), F32),
                   jax.ShapeDtypeStruct((1, POOL_W), F32)],
        compiler_params=_cp(("arbitrary",)), name=name)(z, z, d_cat, d_cat, w_pool, pool_scale)


def _rope_tables(S):
    half = ROPE // 2
    inv_freq = 1.0 / (ROPE_THETA ** (jnp.arange(half, dtype=F32) / half))
    ang = jnp.arange(S).astype(F32)[:, None] * inv_freq[None, :]
    cos, sin = jnp.cos(ang), jnp.sin(ang)
    zero = jnp.zeros((S, half), F32)
    cos_t = jnp.concatenate([cos, cos, zero, zero], axis=1)
    sa_t = jnp.concatenate([-sin, zero, zero, zero], axis=1)
    sb_t = jnp.concatenate([zero, sin, zero, zero], axis=1)
    return cos_t, sa_t, sb_t


def _head_fwd(xn, xr, gn, gr, cos, sa, sb):
    ms = (jnp.sum(xn * xn, axis=-1, keepdims=True) + jnp.sum(xr * xr, axis=-1, keepdims=True)) * (1.0 / QK)
    r = lax.rsqrt(ms + NORM_EPS)
    on = (xn * r) * gn
    yr = (xr * r) * gr
    orr = yr * cos + pltpu.roll(yr, 96, 1) * sa + pltpu.roll(yr, 32, 1) * sb
    return on, orr


def _head_bwd(xn, xr, gn, gr, don, dor, cos, sa, sb):
    ms = (jnp.sum(xn * xn, axis=-1, keepdims=True) + jnp.sum(xr * xr, axis=-1, keepdims=True)) * (1.0 / QK)
    r = lax.rsqrt(ms + NORM_EPS)
    nn, nr = xn * r, xr * r
    dyr = dor * cos + pltpu.roll(dor * sa, 32, 1) + pltpu.roll(dor * sb, 96, 1)
    ggn, ggr = don * nn, dyr * nr
    dnn, dnr = don * gn, dyr * gr
    c = (jnp.sum(dnn * nn, axis=-1, keepdims=True) + jnp.sum(dnr * nr, axis=-1, keepdims=True)) * (1.0 / QK)
    return r * (dnn - nn * c), r * (dnr - nr * c), ggn, ggr


def _kv_cols(h):
    base = (h // 2) * 512 + (h % 2) * 128
    return base, base + 256


def _qkrope_fwd(qraw, kvraw, z, gq, gk, tabs, *, name):
    S = qraw.shape[0]
    tm = _row_tile(S)

    def body(q_ref, kv_ref, zkr_ref, gq_ref, gk_ref, cos_ref, sa_ref, sb_ref, qo_ref, ko_ref, vo_ref, vt_ref):
        cos, sa, sb = cos_ref[...], sa_ref[...], sb_ref[...]
        zkr = zkr_ref[...]
        gqn, gqr, gkn, gkr = gq_ref[:, :128], gq_ref[:, 128:], gk_ref[:, :128], gk_ref[:, 128:]
        for h in range(HEADS):
            b = h * HEAD_PAD
            on, orr = _head_fwd(q_ref[:, b:b + 128], q_ref[:, b + 128:b + 256], gqn, gqr, cos, sa, sb)
            qo_ref[:, b:b + 128] = on.astype(BF16)
            qo_ref[:, b + 128:b + 256] = orr.astype(BF16)
            kc, vc = _kv_cols(h)
            on, orr = _head_fwd(kv_ref[:, kc:kc + 128], zkr, gkn, gkr, cos, sa, sb)
            ko_ref[:, b:b + 128] = on.astype(BF16)
            ko_ref[:, b + 128:b + 256] = orr.astype(BF16)
            vv = kv_ref[:, vc:vc + 128]
            vo_ref[:, h * 128:(h + 1) * 128] = vv.astype(BF16)
            vt_ref[h * 128:(h + 1) * 128, :] = jnp.transpose(vv).astype(BF16)

    W = HEADS * HEAD_PAD
    row = lambda c: pl.BlockSpec((tm, c), lambda i: (i, 0))
    vec = lambda c: pl.BlockSpec((1, c), lambda i: (0, 0))
    return pl.pallas_call(
        body, grid=(S // tm,),
        in_specs=[row(W), row(W), pl.BlockSpec((tm, 128), lambda i: (i, Z_KR_CB)), vec(256), vec(256),
                  row(128), row(128), row(128)],
        out_specs=[row(W), row(W), row(HEADS * 128), pl.BlockSpec((HEADS * 128, tm), lambda i: (0, i))],
        out_shape=[jax.ShapeDtypeStruct((S, W), BF16), jax.ShapeDtypeStruct((S, W), BF16),
                   jax.ShapeDtypeStruct((S, HEADS * 128), BF16), jax.ShapeDtypeStruct((HEADS * 128, S), BF16)],
        compiler_params=_cp(("parallel",)), name=name)(qraw, kvraw, z, gq, gk, *tabs)


def _qkrope_bwd(qraw, kvraw, z, gq, gk, tabs, dq, dk, dv, *, name):
    S = qraw.shape[0]
    tm = _row_tile(S)

    def body(q_ref, kv_ref, zkr_ref, gq_ref, gk_ref, cos_ref, sa_ref, sb_ref, dq_ref, dk_ref, dv_ref,
             dqo_ref, dkvo_ref, dkr_ref, ggq_ref, ggk_ref):
        i = pl.program_id(0)

        @pl.when(i == 0)
        def _():
            ggq_ref[...] = jnp.zeros_like(ggq_ref)
            ggk_ref[...] = jnp.zeros_like(ggk_ref)

        cos, sa, sb = cos_ref[...], sa_ref[...], sb_ref[...]
        zkr = zkr_ref[...]
        gqn, gqr, gkn, gkr = gq_ref[:, :128], gq_ref[:, 128:], gk_ref[:, :128], gk_ref[:, 128:]
        dkr = jnp.zeros((tm, 128), F32)
        sq_n = jnp.zeros((1, 128), F32)
        sq_r = jnp.zeros((1, 128), F32)
        sk_n = jnp.zeros((1, 128), F32)
        sk_r = jnp.zeros((1, 128), F32)
        for h in range(HEADS):
            b = h * HEAD_PAD
            dxn, dxr, ggn, ggr = _head_bwd(q_ref[:, b:b + 128], q_ref[:, b + 128:b + 256], gqn, gqr,
                                           dq_ref[:, b:b + 128], dq_ref[:, b + 128:b + 256], cos, sa, sb)
            dqo_ref[:, b:b + 128] = dxn.astype(BF16)
            dqo_ref[:, b + 128:b + 256] = dxr.astype(BF16)
            sq_n += jnp.sum(ggn, axis=0, keepdims=True)
            sq_r += jnp.sum(ggr, axis=0, keepdims=True)
            kc, vc = _kv_cols(h)
            dxn, dxr, ggn, ggr = _head_bwd(kv_ref[:, kc:kc + 128], zkr, gkn, gkr,
                                           dk_ref[:, b:b + 128], dk_ref[:, b + 128:b + 256], cos, sa, sb)
            dkvo_ref[:, kc:kc + 128] = dxn.astype(BF16)
            dkvo_ref[:, vc:vc + 128] = dv_ref[:, h * 128:(h + 1) * 128].astype(BF16)
            dkr += dxr
            sk_n += jnp.sum(ggn, axis=0, keepdims=True)
            sk_r += jnp.sum(ggr, axis=0, keepdims=True)
        dkr_ref[...] = dkr.astype(BF16)
        ggq_ref[:, :128] += sq_n
        ggq_ref[:, 128:] += sq_r
        ggk_ref[:, :128] += sk_n
        ggk_ref[:, 128:] += sk_r

    W = HEADS * HEAD_PAD
    row = lambda c: pl.BlockSpec((tm, c), lambda i: (i, 0))
    vec = lambda c: pl.BlockSpec((1, c), lambda i: (0, 0))
    return pl.pallas_call(
        body, grid=(S // tm,),
        in_specs=[row(W), row(W), pl.BlockSpec((tm, 128), lambda i: (i, Z_KR_CB)), vec(256), vec(256),
                  row(128), row(128), row(128), row(W), row(W), row(HEADS * 128)],
        out_specs=[row(W), row(W), row(128), vec(256), vec(256)],
        out_shape=[jax.ShapeDtypeStruct((S, W), BF16), jax.ShapeDtypeStruct((S, W), BF16),
                   jax.ShapeDtypeStruct((S, 128), BF16),
                   jax.ShapeDtypeStruct((1, 256), F32), jax.ShapeDtypeStruct((1, 256), F32)],
        compiler_params=_cp(("arbitrary",)), name=name)(qraw, kvraw, z, gq, gk, *tabs, dq, dk, dv)


LOG2E = 1.4426950408889634
SCORE_SCALE = 1.0 / math.sqrt(QK)
SCORE_SCALE_LOG2 = SCORE_SCALE * LOG2E


def _fa_tile(S):
    return 512 if S % 512 == 0 and S >= 2048 else 128


def _flash_fwd(q, k, vt, *, name):
    S = q.shape[0]
    ts = _fa_tile(S)

    def body(q_ref, k_ref, vt_ref, o_ref, ob_ref, lse_ref, m_sc, l_sc, acc_sc):
        qi = pl.program_id(1)
        m_sc[...] = jnp.full_like(m_sc, -jnp.inf)
        l_sc[...] = jnp.zeros_like(l_sc)
        acc_sc[...] = jnp.zeros_like(acc_sc)
        qb = q_ref[...]

        def scores(kidx, masked):
            k0 = pl.multiple_of(kidx * ts, ts)
            st = lax.dot_general(k_ref[pl.ds(k0, ts), :], qb, _DN["nt"], preferred_element_type=F32)
            if masked:
                st = jnp.where(lax.broadcasted_iota(jnp.int32, (ts, ts), 0) > lax.broadcasted_iota(jnp.int32, (ts, ts), 1),
                               -jnp.inf, st)
            return st

        def update(st, kidx):
            k0 = pl.multiple_of(kidx * ts, ts)
            m_prev = m_sc[...]
            m_new = jnp.maximum(m_prev, jnp.max(st, axis=0, keepdims=True))
            alpha = jnp.exp2((m_prev - m_new) * SCORE_SCALE_LOG2)
            pt = jnp.exp2((st - m_new[0:1, :]) * SCORE_SCALE_LOG2)
            l_sc[...] = alpha * l_sc[...] + jnp.sum(pt, axis=0, keepdims=True)
            acc_sc[...] = alpha[0:1, :] * acc_sc[...] + jnp.dot(vt_ref[:, pl.ds(k0, ts)], pt.astype(BF16),
                                                                preferred_element_type=F32)
            m_sc[...] = m_new

        def pair(t, carry):
            sa, sb = scores(2 * t, False), scores(2 * t + 1, False)
            update(sa, 2 * t)
            update(sb, 2 * t + 1)
            return carry

        lax.fori_loop(0, qi // 2, pair, 0)

        @pl.when(qi % 2 == 1)
        def _():
            update(scores(qi - 1, False), qi - 1)

        update(scores(qi, True), qi)
        ot = acc_sc[...] / l_sc[0:1, :]
        o = jnp.transpose(ot)
        o_ref[...] = o
        ob_ref[...] = o.astype(BF16)
        lse_ref[...] = m_sc[...] * SCORE_SCALE_LOG2 + jnp.log2(l_sc[...])

    return pl.pallas_call(
        body, grid=(HEADS, S // ts),
        in_specs=[pl.BlockSpec((ts, HEAD_PAD), lambda h, i: (i, h)),
                  pl.BlockSpec((S, HEAD_PAD), lambda h, i: (0, h)),
                  pl.BlockSpec((128, S), lambda h, i: (h, 0))],
        out_specs=[pl.BlockSpec((ts, 128), lambda h, i: (i, h)),
                   pl.BlockSpec((ts, 128), lambda h, i: (i, h)),
                   pl.BlockSpec((None, 8, ts), lambda h, i: (h, 0, i))],
        out_shape=[jax.ShapeDtypeStruct((S, HEADS * 128), F32), jax.ShapeDtypeStruct((S, HEADS * 128), BF16),
                   jax.ShapeDtypeStruct((HEADS, 8, S), F32)],
        scratch_shapes=[pltpu.VMEM((8, ts), F32), pltpu.VMEM((8, ts), F32), pltpu.VMEM((128, ts), F32)],
        compiler_params=_cp(("parallel", "arbitrary")), name=name)(q, k, vt)


def _attn_bwd_prep(o, d_cat, *, name):
    S = o.shape[0]
    tm = _row_tile(S)
    H = HEADS * 128
    half = H // 2

    def body(o_ref, da_ref, db_ref, delta_ref):
        for h in range(HEADS):
            src, c0 = (da_ref, h * 128) if h * 128 < half else (db_ref, h * 128 - half)
            do = src[:, c0:c0 + 128].astype(F32)
            prod = jnp.transpose(do * o_ref[:, h * 128:(h + 1) * 128])
            delta_ref[h] = jnp.broadcast_to(jnp.sum(prod, axis=0, keepdims=True), (8, tm))

    return pl.pallas_call(
        body, grid=(S // tm,),
        in_specs=[pl.BlockSpec((tm, H), lambda i: (i, 0)),
                  pl.BlockSpec((tm, half), lambda i: (i, 1)), pl.BlockSpec((tm, half), lambda i: (i, 2))],
        out_specs=pl.BlockSpec((HEADS, 8, tm), lambda i: (0, 0, i)),
        out_shape=jax.ShapeDtypeStruct((HEADS, 8, S), F32),
        compiler_params=_cp(("parallel",)), name=name)(o, d_cat, d_cat)


def _flash_bwd(q, k, v, d_cat, lse, delta, *, name):
    S = q.shape[0]
    ts = _fa_tile(S)
    nb = S // ts

    def body(q_ref, do_ref, lse_ref, delta_ref, k_ref, v_ref, dq_ref, dk_ref, dv_ref, dk_sc, dv_sc):
        j = pl.program_id(1)

        @pl.when(j == 0)
        def _():
            dq_ref[...] = jnp.zeros_like(dq_ref)

        dk_sc[...] = jnp.zeros_like(dk_sc)
        dv_sc[...] = jnp.zeros_like(dv_sc)
        kb, vb = k_ref[...], v_ref[...]

        def products(i):
            q0 = pl.multiple_of(i * ts, ts)
            qb = q_ref[pl.ds(q0, ts), :]
            dob_ = do_ref[pl.ds(q0, ts), :]
            st = lax.dot_general(kb, qb, _DN["nt"], preferred_element_type=F32)
            dpt = lax.dot_general(vb, dob_, _DN["nt"], preferred_element_type=F32)
            return q0, qb, dob_, st, dpt

        def accumulate(q0, qb, dob_, st, dpt, masked):
            pt = jnp.exp2(st * SCORE_SCALE_LOG2 - lse_ref[0:1, pl.ds(q0, ts)])
            if masked:
                pt = jnp.where(lax.broadcasted_iota(jnp.int32, (ts, ts), 0) > lax.broadcasted_iota(jnp.int32, (ts, ts), 1),
                               0.0, pt)
            dv_sc[...] += jnp.dot(pt.astype(BF16), dob_, preferred_element_type=F32)
            dst = (pt * (dpt - delta_ref[0:1, pl.ds(q0, ts)])).astype(BF16)
            dk_sc[...] += jnp.dot(dst, qb, preferred_element_type=F32) * SCORE_SCALE
            dq_ref[pl.ds(q0, ts), :] += lax.dot_general(dst, kb, _DN["tn"], preferred_element_type=F32) * SCORE_SCALE

        accumulate(*products(j), True)
        n_below = nb - 1 - j

        def pair(t, carry):
            a, b = products(j + 1 + 2 * t), products(j + 2 + 2 * t)
            accumulate(*a, False)
            accumulate(*b, False)
            return carry

        lax.fori_loop(0, n_below // 2, pair, 0)

        @pl.when(n_below % 2 == 1)
        def _():
            accumulate(*products(nb - 1), False)

        dk_ref[...] = dk_sc[...]
        dv_ref[...] = dv_sc[...]

    return pl.pallas_call(
        body, grid=(HEADS, nb),
        in_specs=[pl.BlockSpec((S, HEAD_PAD), lambda h, j: (0, h)),
                  pl.BlockSpec((S, 128), lambda h, j: (0, 4 + h)),
                  pl.BlockSpec((None, 8, S), lambda h, j: (h, 0, 0)),
                  pl.BlockSpec((None, 8, S), lambda h, j: (h, 0, 0)),
                  pl.BlockSpec((ts, HEAD_PAD), lambda h, j: (j, h)),
                  pl.BlockSpec((ts, 128), lambda h, j: (j, h))],
        out_specs=[pl.BlockSpec((S, HEAD_PAD), lambda h, j: (0, h)),
                   pl.BlockSpec((ts, HEAD_PAD), lambda h, j: (j, h)),
                   pl.BlockSpec((ts, 128), lambda h, j: (j, h))],
        out_shape=[jax.ShapeDtypeStruct((S, HEADS * HEAD_PAD), F32), jax.ShapeDtypeStruct((S, HEADS * HEAD_PAD), F32),
                   jax.ShapeDtypeStruct((S, HEADS * 128), F32)],
        scratch_shapes=[pltpu.VMEM((ts, HEAD_PAD), F32), pltpu.VMEM((ts, 128), F32)],
        compiler_params=_cp(("parallel", "arbitrary")), name=name)(q, d_cat, lse, delta, k, v)


def _memk_fwd(mkv, gkx, *, name):
    M = mkv.shape[0]
    XW = X_HEADS * X_DIM

    def body(mkv_ref, g_ref, k_ref, v_ref):
        for h in range(X_HEADS):
            cs = slice(h * X_DIM, (h + 1) * X_DIM)
            xv = mkv_ref[:, cs]
            r = lax.rsqrt(jnp.mean(xv * xv, axis=-1, keepdims=True) + NORM_EPS)
            k_ref[:, cs] = ((xv * r) * g_ref[...]).astype(BF16)
        v_ref[...] = mkv_ref[:, XW:].astype(BF16)

    return pl.pallas_call(
        body, grid=(1,),
        in_specs=[pl.BlockSpec((M, 2 * XW), lambda i: (0, 0)), pl.BlockSpec((1, X_DIM), lambda i: (0, 0))],
        out_specs=[pl.BlockSpec((M, XW), lambda i: (0, 0)), pl.BlockSpec((M, XW), lambda i: (0, 0))],
        out_shape=[jax.ShapeDtypeStruct((M, XW), BF16), jax.ShapeDtypeStruct((M, XW), BF16)],
        compiler_params=_cp(("arbitrary",)), name=name)(mkv, gkx)


def _memk_bwd(mkv, gkx, dk, dv, *, name):
    M = mkv.shape[0]
    XW = X_HEADS * X_DIM

    def body(mkv_ref, g_ref, dk_ref, dv_ref, o_ref, gg_ref):
        gg = jnp.zeros((1, X_DIM), F32)
        for h in range(X_HEADS):
            cs = slice(h * X_DIM, (h + 1) * X_DIM)
            xv = mkv_ref[:, cs]
            r = lax.rsqrt(jnp.mean(xv * xv, axis=-1, keepdims=True) + NORM_EPS)
            n = xv * r
            dkv = dk_ref[:, cs]
            gg += jnp.sum(dkv * n, axis=0, keepdims=True)
            dn = dkv * g_ref[...]
            c = jnp.mean(dn * n, axis=-1, keepdims=True)
            o_ref[:, cs] = (r * (dn - n * c)).astype(BF16)
        o_ref[:, XW:] = dv_ref[...].astype(BF16)
        gg_ref[...] = gg

    full = lambda c: pl.BlockSpec((M, c), lambda i: (0, 0))
    return pl.pallas_call(
        body, grid=(1,),
        in_specs=[full(2 * XW), pl.BlockSpec((1, X_DIM), lambda i: (0, 0)), full(XW), full(XW)],
        out_specs=[full(2 * XW), pl.BlockSpec((1, X_DIM), lambda i: (0, 0))],
        out_shape=[jax.ShapeDtypeStruct((M, 2 * XW), BF16), jax.ShapeDtypeStruct((1, X_DIM), F32)],
        compiler_params=_cp(("arbitrary",)), name=name)(mkv, gkx, dk, dv)


def _xq_norm(z_ref, g_ref, h):
    xv = z_ref[:, h * X_DIM:(h + 1) * X_DIM]
    r = lax.rsqrt(jnp.mean(xv * xv, axis=-1, keepdims=True) + NORM_EPS)
    n = xv * r
    return n, r, n * g_ref[...]


def _xprobs(qb, k_ref, h):
    s = lax.dot_general(qb, k_ref[:, h * X_DIM:(h + 1) * X_DIM], _DN["nt"],
                        preferred_element_type=F32) * (1.0 / math.sqrt(X_DIM))
    e = jnp.exp(s - jnp.max(s, axis=-1, keepdims=True))
    return e / jnp.sum(e, axis=-1, keepdims=True)


def _memattn_fwd(z, kx, vx, gqx, *, name):
    S = z.shape[0]
    M = kx.shape[0]
    tm = _row_tile(S)
    XW = X_HEADS * X_DIM

    def body(z_ref, k_ref, v_ref, g_ref, o_ref):
        for h in range(X_HEADS):
            cs = slice(h * X_DIM, (h + 1) * X_DIM)
            _, _, qn = _xq_norm(z_ref, g_ref, h)
            p = _xprobs(qn.astype(BF16), k_ref, h)
            o_ref[:, cs] = jnp.dot(p.astype(BF16), v_ref[:, cs], preferred_element_type=F32).astype(BF16)

    return pl.pallas_call(
        body, grid=(S // tm,),
        in_specs=[pl.BlockSpec((tm, XW), lambda i: (i, Z_MQ_CB)), pl.BlockSpec((M, XW), lambda i: (0, 0)),
                  pl.BlockSpec((M, XW), lambda i: (0, 0)), pl.BlockSpec((1, X_DIM), lambda i: (0, 0))],
        out_specs=pl.BlockSpec((tm, XW), lambda i: (i, 0)),
        out_shape=jax.ShapeDtypeStruct((S, XW), BF16),
        compiler_params=_cp(("parallel",)), name=name)(z, kx, vx, gqx)


def _memattn_bwd(z, kx, vx, gqx, d_cat, *, name):
    S = z.shape[0]
    M = kx.shape[0]
    tm = _row_tile(S)
    XW = X_HEADS * X_DIM
    scale = 1.0 / math.sqrt(X_DIM)

    def body(z_ref, k_ref, v_ref, g_ref, do_ref, dz_ref, dk_ref, dv_ref, gg_ref):
        i = pl.program_id(0)

        @pl.when(i == 0)
        def _():
            dk_ref[...] = jnp.zeros_like(dk_ref)
            dv_ref[...] = jnp.zeros_like(dv_ref)
            gg_ref[...] = jnp.zeros_like(gg_ref)

        gg = jnp.zeros((1, X_DIM), F32)
        for h in range(X_HEADS):
            cs = slice(h * X_DIM, (h + 1) * X_DIM)
            n, r, qn = _xq_norm(z_ref, g_ref, h)
            qb = qn.astype(BF16)
            p = _xprobs(qb, k_ref, h)
            pb = p.astype(BF16)
            dob = do_ref[:, cs].astype(BF16)
            dv_ref[:, cs] += lax.dot_general(pb, dob, _DN["tn"], preferred_element_type=F32)
            dp = lax.dot_general(dob, v_ref[:, cs], _DN["nt"], preferred_element_type=F32)
            ds = (p * (dp - jnp.sum(dp * p, axis=-1, keepdims=True))).astype(BF16)
            dk_ref[:, cs] += lax.dot_general(ds, qb, _DN["tn"], preferred_element_type=F32) * scale
            dqn = jnp.dot(ds, k_ref[:, cs], preferred_element_type=F32) * scale
            gg += jnp.sum(dqn * n, axis=0, keepdims=True)
            dn = dqn * g_ref[...]
            c = jnp.mean(dn * n, axis=-1, keepdims=True)
            dz_ref[:, cs] = (r * (dn - n * c)).astype(BF16)
        gg_ref[...] += gg

    full = pl.BlockSpec((M, XW), lambda i: (0, 0))
    vec = pl.BlockSpec((1, X_DIM), lambda i: (0, 0))
    return pl.pallas_call(
        body, grid=(S // tm,),
        in_specs=[pl.BlockSpec((tm, XW), lambda i: (i, Z_MQ_CB)), full, full, vec,
                  pl.BlockSpec((tm, XW), lambda i: (i, 3))],
        out_specs=[pl.BlockSpec((tm, XW), lambda i: (i, 0)), full, full, vec],
        out_shape=[jax.ShapeDtypeStruct((S, XW), BF16), jax.ShapeDtypeStruct((M, XW), F32),
                   jax.ShapeDtypeStruct((M, XW), F32), jax.ShapeDtypeStruct((1, X_DIM), F32)],
        compiler_params=_cp(("arbitrary",)), name=name)(z, kx, vx, gqx, d_cat)


def _silu_parts(x):
    h = 0.5 * x
    return h, jnp.tanh(h)


GLU_HALO = 16


def _glu_tiles(S, F):
    return _big_row_tile(S), _pick(F, (1408, 512, 256, 128))


def _glu_fwd(g, u, conv_w, conv_b, *, name):
    S, F = g.shape
    tm, tc = _glu_tiles(S, F)
    hb = tm // GLU_HALO

    def body(g_ref, gp_ref, u_ref, w_ref, b_ref, a_ref):
        i = pl.program_id(1)
        gt = g_ref[...].astype(F32)
        ext = jnp.concatenate([gp_ref[...].astype(F32) * (i > 0).astype(F32), gt], axis=0)
        gc = b_ref[...] + w_ref[0:1, :] * pltpu.roll(ext, 2, 0)[GLU_HALO:]
        gc = gc + w_ref[1:2, :] * pltpu.roll(ext, 1, 0)[GLU_HALO:]
        gc = gc + w_ref[2:3, :] * gt
        h, t = _silu_parts(gc)
        a_ref[...] = ((h * (1.0 + t)) * u_ref[...].astype(F32)).astype(BF16)

    return pl.pallas_call(
        body, grid=(F // tc, S // tm),
        in_specs=[pl.BlockSpec((tm, tc), lambda j, i: (i, j)),
                  pl.BlockSpec((GLU_HALO, tc), lambda j, i: (jnp.maximum(i * hb - 1, 0), j)),
                  pl.BlockSpec((tm, tc), lambda j, i: (i, j)),
                  pl.BlockSpec((3, tc), lambda j, i: (0, j)),
                  pl.BlockSpec((1, tc), lambda j, i: (0, j))],
        out_specs=pl.BlockSpec((tm, tc), lambda j, i: (i, j)),
        out_shape=jax.ShapeDtypeStruct((S, F), BF16),
        compiler_params=_cp(("parallel", "parallel")), name=name)(g, g, u, conv_w, conv_b)


def _glu_bwd(g, u, d_a, conv_w, conv_b, *, name):
    S, F = g.shape
    tm, tc = _glu_tiles(S, F)
    hb = tm // GLU_HALO
    nt = S // tm
    E = tm + GLU_HALO

    def body(g_ref, gp_ref, gn_ref, u_ref, un_ref, da_ref, dan_ref, w_ref, b_ref,
             dg_ref, du_ref, gw_ref, gb_ref):
        i = pl.program_id(1)

        @pl.when(i == 0)
        def _():
            gw_ref[...] = jnp.zeros_like(gw_ref)
            gb_ref[...] = jnp.zeros_like(gb_ref)

        w0, w1, w2 = w_ref[0:1, :], w_ref[1:2, :], w_ref[2:3, :]
        gext = jnp.concatenate([gp_ref[...].astype(F32) * (i > 0).astype(F32), g_ref[...].astype(F32),
                                gn_ref[...].astype(F32)], axis=0)
        g1 = pltpu.roll(gext, 1, 0)[GLU_HALO:]
        g2 = pltpu.roll(gext, 2, 0)[GLU_HALO:]
        g0 = gext[GLU_HALO:]
        gc = b_ref[...] + w0 * g2
        gc = gc + w1 * g1
        gc = gc + w2 * g0
        h, t = _silu_parts(gc)
        t1 = 1.0 + t
        da = jnp.concatenate([da_ref[...].astype(F32), dan_ref[...].astype(F32) * (i < nt - 1).astype(F32)], axis=0)
        uu = jnp.concatenate([u_ref[...].astype(F32), un_ref[...].astype(F32)], axis=0)
        du_ref[...] = (da[:tm] * (h[:tm] * t1[:tm])).astype(BF16)
        dgc = (da * uu) * (0.5 * (t1 + h * (1.0 - t * t)))
        dg = w2 * dgc[:tm] + w1 * pltpu.roll(dgc, E - 1, 0)[:tm] + w0 * pltpu.roll(dgc, E - 2, 0)[:tm]
        dg_ref[...] = dg.astype(BF16)
        dgt = dgc[:tm]
        gb_ref[...] += jnp.sum(dgt, axis=0, keepdims=True)
        gw_ref[0:1, :] += jnp.sum(dgt * g2[:tm], axis=0, keepdims=True)
        gw_ref[1:2, :] += jnp.sum(dgt * g1[:tm], axis=0, keepdims=True)
        gw_ref[2:3, :] += jnp.sum(dgt * g0[:tm], axis=0, keepdims=True)

    tile = pl.BlockSpec((tm, tc), lambda j, i: (i, j))
    nxt = pl.BlockSpec((GLU_HALO, tc), lambda j, i: (jnp.minimum((i + 1) * hb, S // GLU_HALO - 1), j))
    prv = pl.BlockSpec((GLU_HALO, tc), lambda j, i: (jnp.maximum(i * hb - 1, 0), j))
    return pl.pallas_call(
        body, grid=(F // tc, nt),
        in_specs=[tile, prv, nxt, tile, nxt, tile, nxt,
                  pl.BlockSpec((3, tc), lambda j, i: (0, j)), pl.BlockSpec((1, tc), lambda j, i: (0, j))],
        out_specs=[tile, tile, pl.BlockSpec((3, tc), lambda j, i: (0, j)), pl.BlockSpec((1, tc), lambda j, i: (0, j))],
        out_shape=[jax.ShapeDtypeStruct((S, F), BF16), jax.ShapeDtypeStruct((S, F), BF16),
                   jax.ShapeDtypeStruct((3, F), F32), jax.ShapeDtypeStruct((1, F), F32)],
        compiler_params=_cp(("parallel", "arbitrary")), name=name)(g, g, g, u, u, d_a, d_a, conv_w, conv_b)


def _loss_head(y, target, *, name):
    S, D = y.shape
    tm = _big_row_tile(S)
    nt = S // tm

    def body(y_ref, t_ref, dy_ref, dyb_ref, loss_ref, acc):
        i = pl.program_id(0)

        @pl.when(i == 0)
        def _():
            acc[...] = jnp.zeros_like(acc)

        e = y_ref[...] - t_ref[...]
        dy = e * (1.0 / D)
        dy_ref[...] = dy
        dyb_ref[...] = dy.astype(BF16)
        acc[...] += jnp.sum(e * e, axis=0, keepdims=True)

        @pl.when(i == nt - 1)
        def _():
            loss_ref[...] = jnp.broadcast_to(jnp.sum(acc[...], axis=1, keepdims=True) * (0.5 / D), (1, 128))

    row = pl.BlockSpec((tm, D), lambda i: (i, 0))
    return pl.pallas_call(
        body, grid=(nt,), in_specs=[row, row],
        out_specs=[row, row, pl.BlockSpec((1, 128), lambda i: (0, 0))],
        out_shape=[jax.ShapeDtypeStruct((S, D), F32), jax.ShapeDtypeStruct((S, D), BF16),
                   jax.ShapeDtypeStruct((1, 128), F32)],
        scratch_shapes=[pltpu.VMEM((1, D), F32)],
        compiler_params=_cp(("arbitrary",)), name=name)(y, target)


def _adamw_math(w, g, m, v):
    m = ADAM_B1 * m + (1.0 - ADAM_B1) * g
    v = ADAM_B2 * v + (1.0 - ADAM_B2) * (g * g)
    m_hat = m / (1.0 - ADAM_B1 ** ADAM_STEP)
    v_hat = v / (1.0 - ADAM_B2 ** ADAM_STEP)
    delta = -ADAM_LR * (m_hat / (jnp.sqrt(v_hat) + ADAM_EPS) + ADAM_WD * w)
    return delta, m, v


def _adamw(w, m, v, parts, *, name):
    R, C = w.shape
    tr = 128 if R % 128 == 0 else R
    n_parts = len(parts)

    def body(*refs):
        w_ref, m_ref, v_ref = refs[:3]
        p_refs = refs[3:3 + n_parts]
        g_ref, d_ref, mo_ref, vo_ref = refs[3 + n_parts:]
        g = p_refs[0][...]
        for p in p_refs[1:]:
            g = g + p[...]
        delta, mn, vn = _adamw_math(w_ref[...], g, m_ref[...], v_ref[...])
        g_ref[...] = g
        d_ref[...] = delta
        mo_ref[...] = mn
        vo_ref[...] = vn

    blk = pl.BlockSpec((tr, C), lambda i: (i, 0))
    return pl.pallas_call(
        body, grid=(R // tr,), in_specs=[blk] * (3 + n_parts), out_specs=[blk] * 4,
        out_shape=[jax.ShapeDtypeStruct((R, C), F32)] * 4,
        compiler_params=_cp(("parallel",)), name=name)(w, m, v, *parts)


def _sum4(g, recv, me, *, name):
    _, R, C = recv.shape
    tr = 128 if R % 128 == 0 else R

    def body(me_ref, g_ref, r_ref, o_ref):
        acc = g_ref[...]
        for j in range(N_CHIPS - 1):
            acc = acc + r_ref[j].astype(F32)
        o_ref[...] = acc

    if g.ndim == 3:
        own = pl.BlockSpec((None, tr, C), lambda i, me_ref: (me_ref[0], i, 0))
    else:
        own = pl.BlockSpec((tr, C), lambda i, me_ref: (i, me_ref[0]))
    grid_spec = pltpu.PrefetchScalarGridSpec(
        num_scalar_prefetch=1, grid=(R // tr,),
        in_specs=[own, pl.BlockSpec((N_CHIPS - 1, tr, C), lambda i, me_ref: (0, i, 0))],
        out_specs=pl.BlockSpec((tr, C), lambda i, me_ref: (i, 0)))
    return pl.pallas_call(
        body, grid_spec=grid_spec, out_shape=jax.ShapeDtypeStruct((R, C), F32),
        compiler_params=_cp(("parallel",)), name=name)(me, g, recv)


def _sum8(gathered, *, name):
    _, R, C = gathered.shape

    def body(g_ref, o_ref):
        acc = g_ref[0]
        for d in range(1, N_DEV):
            acc = acc + g_ref[d]
        o_ref[...] = acc

    return pl.pallas_call(
        body, grid=(1,), in_specs=[pl.BlockSpec((N_DEV, R, C), lambda i: (0, 0, 0))],
        out_specs=pl.BlockSpec((R, C), lambda i: (0, 0)),
        out_shape=jax.ShapeDtypeStruct((R, C), F32),
        compiler_params=_cp(("arbitrary",)), name=name)(gathered)


def _place():
    return lax.axis_index("x"), lax.axis_index("y"), lax.axis_index("c")


def _other_chips(x, y):
    return [(1 - x, y), (x, 1 - y), (1 - x, 1 - y)]


_ANY = pl.BlockSpec(memory_space=pl.ANY)


_HBM = pl.BlockSpec(memory_space=pltpu.HBM)
_SEM = pl.BlockSpec(memory_space=pltpu.SEMAPHORE)
_EFFECT = pltpu.SideEffectType.DATAFLOW_SIDE_EFFECTING


def _gather_copies(srcs, lands, send_sems, recv_sems):
    x, y, c = _place()
    me = 2 * x + y

    def place(i):
        if len(lands[i].shape) == 3:
            return lands[i].at[me]
        cols = srcs[i].shape[1]
        return lands[i].at[:, pl.ds(pl.multiple_of(me * cols, 128), cols)]

    return [pltpu.make_async_remote_copy(
        src_ref=srcs[i], dst_ref=place(i), send_sem=send_sems.at[3 * i + j],
        recv_sem=recv_sems.at[3 * i + j], device_id=(px, py, c), device_id_type=MESH)
        for i in range(len(srcs)) for j, (px, py) in enumerate(_other_chips(x, y))]


def _scatter_copies(srcs, lands, send_sems, recv_sems):
    x, y, c = _place()

    def part(i, owner):
        if len(srcs[i].shape) == 3:
            return srcs[i].at[owner]
        cols = lands[i].shape[2]
        return srcs[i].at[:, pl.ds(pl.multiple_of(owner * cols, 128), cols)]

    return [pltpu.make_async_remote_copy(
        src_ref=part(i, 2 * px + py), dst_ref=lands[i].at[j], send_sem=send_sems.at[3 * i + j],
        recv_sem=recv_sems.at[3 * i + j], device_id=(px, py, c), device_id_type=MESH)
        for i in range(len(srcs)) for j, (px, py) in enumerate(_other_chips(x, y))]


def _copies_start(srcs, lands, make_copies, thru=(), *, name):
    n = len(srcs)
    n_ops = 2 * n + len(thru)

    def body(*refs):
        send_sems, recv_sems = refs[n_ops], refs[n_ops + 1]
        for cp in make_copies(refs[:n], refs[n:2 * n], send_sems, recv_sems):
            cp.start()
        refs[-1][...] = jnp.zeros_like(refs[-1])

    ops = list(srcs) + list(lands) + list(thru)
    outs = pl.pallas_call(
        body, name=name,
        out_shape=(pltpu.SemaphoreType.DMA((3 * n,)), pltpu.SemaphoreType.DMA((3 * n,)),
                   *[pltpu.HBM(a.shape, a.dtype) for a in ops], jax.ShapeDtypeStruct((8, 128), F32)),
        in_specs=[_HBM] * n_ops,
        out_specs=(_SEM, _SEM, *[_HBM] * n_ops, pl.BlockSpec(memory_space=pltpu.VMEM)),
        input_output_aliases={i: 2 + i for i in range(n_ops)},
        compiler_params=pltpu.CompilerParams(has_side_effects=_EFFECT),
    )(*[pltpu.with_memory_space_constraint(a, pltpu.HBM) for a in ops])
    return outs[0], outs[1], list(outs[2:2 + n]), list(outs[2 + n:2 + 2 * n]), outs[-1], list(outs[2 + 2 * n:-1])


def _copies_wait(handle, after, make_copies, *, name):
    send_sems, recv_sems, srcs, lands = handle[:4]
    n = len(srcs)

    def body(*refs):
        for cp in make_copies(refs[:n], refs[n:2 * n], refs[2 * n], refs[2 * n + 1]):
            cp.wait_send()
            cp.wait_recv()

    ops = list(srcs) + list(lands)
    outs = pl.pallas_call(
        body, name=name,
        out_shape=tuple(pltpu.HBM(a.shape, a.dtype) for a in ops),
        in_specs=[_HBM] * (2 * n) + [_SEM, _SEM, _ANY],
        out_specs=tuple([_HBM] * (2 * n)),
        input_output_aliases={i: i for i in range(2 * n)},
        compiler_params=pltpu.CompilerParams(has_side_effects=_EFFECT),
    )(*ops, send_sems, recv_sems, after)
    return list(outs[n:])


def _swap_with_sibling(arrs, *, name):
    n = len(arrs)

    def body(*refs):
        ins, outs = refs[:n], refs[n:2 * n]
        send_sems, recv_sems = refs[2 * n:]
        x, y, c = _place()
        remote = []
        for i in range(n):
            rc = pltpu.make_async_remote_copy(
                src_ref=ins[i], dst_ref=outs[i], send_sem=send_sems.at[i], recv_sem=recv_sems.at[i],
                device_id=(x, y, 1 - c), device_id_type=MESH)
            rc.start()
            remote.append(rc)
        for rc in remote:
            rc.wait_send()
        for rc in remote:
            rc.wait_recv()

    return pl.pallas_call(
        body, in_specs=[_ANY] * n, out_specs=[_ANY] * n,
        out_shape=[jax.ShapeDtypeStruct(a.shape, a.dtype) for a in arrs],
        scratch_shapes=[pltpu.SemaphoreType.DMA((n,)), pltpu.SemaphoreType.DMA((n,))],
        name=name)(*arrs)


def _gather_all(buf, *, name):
    R, C = buf.shape

    def body(in_ref, out_ref, send_sems, recv_sems, local_sem):
        x, y, c = _place()
        me = 4 * x + 2 * y + c
        lc = pltpu.make_async_copy(in_ref, out_ref.at[me], local_sem)
        lc.start()
        remote = []
        for k in range(1, N_DEV):
            px = 1 - x if (k >> 2) & 1 else x
            py = 1 - y if (k >> 1) & 1 else y
            pc = 1 - c if k & 1 else c
            rc = pltpu.make_async_remote_copy(
                src_ref=in_ref, dst_ref=out_ref.at[me], send_sem=send_sems.at[k - 1],
                recv_sem=recv_sems.at[k - 1], device_id=(px, py, pc), device_id_type=MESH)
            rc.start()
            remote.append(rc)
        lc.wait()
        for rc in remote:
            rc.wait_send()
        for rc in remote:
            rc.wait_recv()

    return pl.pallas_call(
        body, in_specs=[_ANY], out_specs=_ANY,
        out_shape=jax.ShapeDtypeStruct((N_DEV, R, C), buf.dtype),
        scratch_shapes=[pltpu.SemaphoreType.DMA((N_DEV - 1,)), pltpu.SemaphoreType.DMA((N_DEV - 1,)),
                        pltpu.SemaphoreType.DMA],
        name=name)(buf)


def _w_in_to_z(w):
    pad = jnp.zeros(w.shape[:-1] + (64,), w.dtype)
    return jnp.concatenate([w[..., 0:512], w[..., 512:1024], w[..., 1344:1856], w[..., 1024:1280],
                            w[..., 1280:1344], pad], axis=-1)


def _z_to_w_in(g):
    return jnp.concatenate([g[..., 0:512], g[..., 512:1024], g[..., 1536:1792], g[..., 1792:1856],
                            g[..., 1024:1536]], axis=-1)


def _pad_heads(w, nh):
    w = w.reshape(w.shape[:-1] + (nh, QK))
    w = jnp.concatenate([w, jnp.zeros(w.shape[:-1] + (HEAD_PAD - QK,), w.dtype)], axis=-1)
    return w.reshape(w.shape[:-2] + (nh * HEAD_PAD,))


def _unpad_heads(g, nh):
    g = g.reshape(g.shape[:-1] + (nh, HEAD_PAD))[..., :QK]
    return g.reshape(g.shape[:-2] + (nh * QK,))


def _kv_split(w, nh):
    w = w.reshape(w.shape[:-1] + (nh, 2, 128))
    return jnp.swapaxes(w, -3, -2).reshape(w.shape[:-3] + (nh * 256,))


def _kv_join(g, nh):
    g = g.reshape(g.shape[:-1] + (2, nh, 128))
    return jnp.swapaxes(g, -3, -2).reshape(g.shape[:-3] + (nh * 256,))


def _pad_gain(g):
    return jnp.concatenate([g, jnp.zeros((1, HEAD_PAD - QK), g.dtype)], axis=1)


_SMALL = ("g_mix", "g_q_lat", "g_kv_lat", "g_q_mla", "g_k_mla", "w_pool", "pool_scale", "g_mem", "g_q_x",
          "g_k_x", "g_ffn", "conv_b", "conv_w")


def _pack(arrs, extra=0):
    flat = jnp.concatenate([a.reshape(-1) for a in arrs])
    n = flat.shape[0] + extra
    rows = -(-n // 1024) * 8
    return jnp.pad(flat, (0, rows * 128 - flat.shape[0])).reshape(rows, 128)


def _unpack(buf, shapes):
    flat = buf.reshape(-1)
    out, off = [], 0
    for s in shapes:
        n = int(np.prod(s))
        out.append(flat[off:off + n].reshape(s))
        off += n
    return out, off


def _tied(a, token):
    return a + token[:1, :1].astype(a.dtype)


def _local_step(x, mem, target, W, fetch=None, ship=None):
    fetch = fetch or (lambda group, after: None)
    ship = ship or (lambda group, G: jnp.zeros((8, 128), F32))
    S, D = x.shape
    F = W["conv_b"].shape[1]
    tabs = _rope_tables(S)
    tm = 512 if S % 512 == 0 else 128
    tl = 1024 if S % 1024 == 0 else tm
    tk = _pick(S, (1024, 512, 128))

    h = _rms_fwd(x, W["g_mix"], C=D, name="norm_mix")
    fetch("g1", h)
    z = mm_nn(h, W["w_in"], tm=tl, tn=Z_COLS, tk=D, name="z_proj")
    fetch("g2", z)
    y_pool = _pool_fwd(z, W["w_pool"], W["pool_scale"], name="pool_fwd")
    ql = _rms_fwd(z, W["g_q_lat"], C=Q_RANK, cb=Z_Q_CB, name="norm_qlat")
    kvl = _rms_fwd(z, W["g_kv_lat"], C=KV_RANK, cb=Z_KV_CB, name="norm_kvlat")
    qraw = mm_nn(ql, W["w_q_up"], nsh=N_CHIPS, tm=tl, tn=512, tk=Q_RANK, name="q_up")
    kvraw = mm_nn(kvl, W["w_kv_up"], nsh=N_CHIPS, tm=tl, tn=512, tk=KV_RANK, name="kv_up")
    q, k, v, vt = _qkrope_fwd(qraw, kvraw, z, W["g_q_mla"], W["g_k_mla"], tabs, name="qk_norm_rope")
    o, y_mla, lse = _flash_fwd(q, k, vt, name="mla_fwd")
    memn = _rms_fwd(mem, W["g_mem"], C=D, name="norm_mem")
    M = mem.shape[0]
    mkv = mm_nn(memn, W["w_mem_kv"], tm=M, tn=1024, tk=D, name="mem_kv")
    kx, vx = _memk_fwd(mkv, W["g_k_x"], name="memk_fwd")
    y_mem = _memattn_fwd(z, kx, vx, W["g_q_x"], name="memattn_fwd")
    cat = jnp.concatenate([y_pool, y_mla, y_mem], axis=1)
    fetch("g2b", cat)
    x2 = mm_nn(cat, W["w_o"], tm=tm, tn=D, tk=D, add=x, name="o_proj")
    h2 = _rms_fwd(x2, W["g_ffn"], C=D, name="norm_ffn")
    fetch("g3", h2)
    fn = F // N_CHIPS
    g = mm_nn(h2, W["w_gate"], tm=tl, tn=512, tk=D, out_dtypes=(BF16,), name="gate_proj")
    u = mm_nn(h2, W["w_up"], tm=tl, tn=512, tk=D, out_dtypes=(BF16,), name="up_proj")
    a = _glu_fwd(g, u, W["conv_w"], W["conv_b"], name="glu_fwd")
    y = mm_nn(a, W["w_down"], tm=tm, tn=1024, tk=F // 2, add=x2, name="down_proj")
    dy, dyb, loss_row = _loss_head(y, target, name="loss_head")

    G = {}
    d_a = mm_nt(dyb, W["w_down"], tm=tl, to=512, tc=D, out_dtypes=(BF16,), name="d_a")
    G["w_down"] = mm_tn(a, dyb, to=fn, tn=1024, tk=tk, out_dtypes=(F32, BF16), name="grad_w_down")
    d_g, d_u, G["conv_w"], G["conv_b"] = _glu_bwd(g, u, d_a, W["conv_w"], W["conv_b"], name="glu_bwd")
    G["w_gate"] = mm_tn(h2, d_g, to=D, tn=512, tk=tk, out_dtypes=(F32, BF16), name="grad_w_gate")
    G["w_up"] = mm_tn(h2, d_u, to=D, tn=512, tk=tk, out_dtypes=(F32, BF16), name="grad_w_up")
    tok = ship("s1", G)
    d_h2 = mm_nt_pair(d_g, W["w_gate"], d_u, W["w_up"], nsh=1, tm=tm, to=512, tc=F // 2, name="d_h2")
    d_x2, d_x2b, G["g_ffn"] = _rms_bwd(x2, d_h2, _tied(W["g_ffn"], tok), C=D, res=dy, out_dtypes=(F32, BF16),
                                       name="norm_ffn_bwd")

    d_cat = mm_nt(d_x2b, W["w_o"], tm=tl, to=1024, tc=D, out_dtypes=(BF16,), name="d_cat")
    G["w_o"] = mm_tn(cat, d_x2b, to=1024, tn=1024, tk=tk, out_dtypes=(F32, BF16), name="grad_w_o")
    tok = ship("s2", G)
    dz_pool, G["w_pool"], G["pool_scale"] = _pool_bwd(z, d_cat, W["w_pool"], _tied(W["pool_scale"], tok),
                                                      name="pool_bwd")
    dz_mq, dkx, dvx, G["g_q_x"] = _memattn_bwd(z, kx, vx, W["g_q_x"], d_cat, name="memattn_bwd")
    d_mkv, G["g_k_x"] = _memk_bwd(mkv, W["g_k_x"], dkx, dvx, name="memk_bwd")
    G["w_mem_kv"] = mm_tn(memn, d_mkv, to=1024, tn=1024, tk=M, out_dtypes=(F32, BF16), name="grad_w_mem_kv")
    d_memn = mm_nt(d_mkv, W["w_mem_kv"], tm=M, to=D, tc=1024, name="d_memn")
    _, G["g_mem"] = _rms_bwd(mem, d_memn, W["g_mem"], C=D, name="norm_mem_bwd")
    delta = _attn_bwd_prep(o, d_cat, name="mla_bwd_prep")
    dq, dk, dv = _flash_bwd(q, k, v, d_cat, lse, delta, name="mla_bwd")
    d_qraw, d_kvraw, dz_kr, G["g_q_mla"], G["g_k_mla"] = _qkrope_bwd(
        qraw, kvraw, z, W["g_q_mla"], W["g_k_mla"], tabs, dq, dk, dv, name="qk_norm_rope_bwd")
    G["w_q_up"] = mm_tn(ql, d_qraw, nsh=N_CHIPS, to=Q_RANK, tn=512, tk=tk, out_dtypes=(F32, BF16), name="grad_w_q_up")
    d_ql = mm_nt(d_qraw, W["w_q_up"], nsh=N_CHIPS, tm=tl, to=Q_RANK, tc=512, name="d_ql")
    G["w_kv_up"] = mm_tn(kvl, d_kvraw, nsh=N_CHIPS, to=KV_RANK, tn=512, tk=tk, out_dtypes=(F32, BF16),
                         name="grad_w_kv_up")
    d_kvl = mm_nt(d_kvraw, W["w_kv_up"], nsh=N_CHIPS, tm=tl, to=KV_RANK, tc=512, name="d_kvl")
    dz_q, G["g_q_lat"] = _rms_bwd(z, d_ql, W["g_q_lat"], C=Q_RANK, cb=Z_Q_CB, out_dtypes=(BF16,), name="norm_qlat_bwd")
    dz_kv, G["g_kv_lat"] = _rms_bwd(z, d_kvl, W["g_kv_lat"], C=KV_RANK, cb=Z_KV_CB, out_dtypes=(BF16,),
                                    name="norm_kvlat_bwd")
    d_z = jnp.concatenate([dz_pool, dz_q, dz_mq, dz_kv, dz_kr], axis=1)
    G["w_in"] = mm_tn(h, d_z, to=512, tn=Z_COLS, tk=tk, out_dtypes=(F32, BF16), name="grad_w_in")
    tok = ship("s3", G)
    d_h = mm_nt(d_z, W["w_in"], tm=tl, to=1024, tc=Z_COLS, name="d_h")
    grad_x, G["g_mix"] = _rms_bwd(x, d_h, _tied(W["g_mix"], tok), C=D, res=d_x2, name="norm_mix_bwd")
    return loss_row, grad_x, G


_BIG = ("w_in", "w_q_up", "w_kv_up", "w_mem_kv", "w_o", "w_gate", "w_up", "w_down")
_WEIGHTS = ("g_mix", "w_in", "g_q_lat", "w_q_up", "g_kv_lat", "w_kv_up", "g_q_mla", "g_k_mla", "w_pool",
            "pool_scale", "g_mem", "w_mem_kv", "g_q_x", "g_k_x", "w_o", "g_ffn", "w_gate", "w_up", "conv_w",
            "conv_b", "w_down")


def _to_compute_layout(name, w):
    if name == "w_in":
        return _w_in_to_z(w)
    if name == "w_q_up":
        return _pad_heads(w, w.shape[-1] // QK)
    if name == "w_kv_up":
        return _kv_split(w, w.shape[-1] // 256)
    return w


def _from_compute_layout(name, g):
    if name == "w_in":
        return _z_to_w_in(g)
    if name == "w_q_up":
        return _unpad_heads(g, g.shape[-1] // HEAD_PAD)
    if name == "w_kv_up":
        return _kv_join(g, g.shape[-1] // 256)
    return g


def kernel(x, mem, g_mix, w_in, g_q_lat, w_q_up, g_kv_lat, w_kv_up, g_q_mla, g_k_mla, w_pool, pool_scale, g_mem, w_mem_kv, g_q_x, g_k_x, w_o, g_ffn, w_gate, w_up, conv_w, conv_b, w_down, loss_target, m_g_mix, m_w_in, m_g_q_lat, m_w_q_up, m_g_kv_lat, m_w_kv_up, m_g_q_mla, m_g_k_mla, m_w_pool, m_pool_scale, m_g_mem, m_w_mem_kv, m_g_q_x, m_g_k_x, m_w_o, m_g_ffn, m_w_gate, m_w_up, m_conv_w, m_conv_b, m_w_down, v_g_mix, v_w_in, v_g_q_lat, v_w_q_up, v_g_kv_lat, v_w_kv_up, v_g_q_mla, v_g_k_mla, v_w_pool, v_pool_scale, v_g_mem, v_w_mem_kv, v_g_q_x, v_g_k_x, v_w_o, v_g_ffn, v_w_gate, v_w_up, v_conv_w, v_conv_b, v_w_down):
    P = dict(g_mix=g_mix, w_in=w_in, g_q_lat=g_q_lat, w_q_up=w_q_up, g_kv_lat=g_kv_lat, w_kv_up=w_kv_up,
             g_q_mla=g_q_mla, g_k_mla=g_k_mla, w_pool=w_pool, pool_scale=pool_scale, g_mem=g_mem,
             w_mem_kv=w_mem_kv, g_q_x=g_q_x, g_k_x=g_k_x, w_o=w_o, g_ffn=g_ffn, w_gate=w_gate, w_up=w_up,
             conv_w=conv_w, conv_b=conv_b, w_down=w_down)
    Mo = dict(g_mix=m_g_mix, w_in=m_w_in, g_q_lat=m_g_q_lat, w_q_up=m_w_q_up, g_kv_lat=m_g_kv_lat,
              w_kv_up=m_w_kv_up, g_q_mla=m_g_q_mla, g_k_mla=m_g_k_mla, w_pool=m_w_pool,
              pool_scale=m_pool_scale, g_mem=m_g_mem, w_mem_kv=m_w_mem_kv, g_q_x=m_g_q_x, g_k_x=m_g_k_x,
              w_o=m_w_o, g_ffn=m_g_ffn, w_gate=m_w_gate, w_up=m_w_up, conv_w=m_conv_w, conv_b=m_conv_b,
              w_down=m_w_down)
    Vo = dict(g_mix=v_g_mix, w_in=v_w_in, g_q_lat=v_g_q_lat, w_q_up=v_w_q_up, g_kv_lat=v_g_kv_lat,
              w_kv_up=v_w_kv_up, g_q_mla=v_g_q_mla, g_k_mla=v_g_k_mla, w_pool=v_w_pool,
              pool_scale=v_pool_scale, g_mem=v_g_mem, w_mem_kv=v_w_mem_kv, g_q_x=v_g_q_x, g_k_x=v_g_k_x,
              w_o=v_w_o, g_ffn=v_g_ffn, w_gate=v_w_gate, w_up=v_w_up, conv_w=v_conv_w, conv_b=v_conv_b,
              w_down=v_w_down)
    xi, yi, ci = _place()
    me = (2 * xi + yi).astype(jnp.int32).reshape(1)

    shard = {n: _to_compute_layout(n, P[n][0]).astype(BF16) for n in _BIG}
    shard["conv_w"] = conv_w[0]
    gather_groups = {"g1": ("w_in",), "g2": ("w_q_up", "w_kv_up", "w_mem_kv"), "g2b": ("w_o",),
                     "g3": ("w_gate", "w_up", "w_down", "conv_w")}
    whole = ("w_gate", "w_up")
    gathers = {}

    def landing(n):
        s = shard[n]
        if n in whole:
            return lax.dynamic_update_slice(lax.empty((s.shape[0], N_CHIPS * s.shape[1]), s.dtype), s,
                                            (0, me[0] * s.shape[1]))
        return lax.dynamic_update_slice(lax.empty((N_CHIPS,) + s.shape, s.dtype), s[None], (me[0], 0, 0))

    def start_gather(grp, thru=()):
        names = gather_groups[grp]
        gathers[grp] = _copies_start([shard[n] for n in names], [landing(n) for n in names], _gather_copies, thru,
                                     name="gather_start_" + grp)
        return gathers[grp][5]

    start_gather("g1")
    W = {}
    W["g_q_mla"], W["g_k_mla"] = _pad_gain(g_q_mla), _pad_gain(g_k_mla)
    W["w_pool"] = w_pool[0].astype(BF16)
    for n in ("g_mix", "g_q_lat", "g_kv_lat", "pool_scale", "g_mem", "g_q_x", "g_k_x", "g_ffn", "conv_b"):
        W[n] = P[n]
    W["g_mix"] = _tied(W["g_mix"], gathers["g1"][4])

    def fetch(grp, after):
        stacks = _copies_wait(gathers[grp], after, _gather_copies, name="gather_wait_" + grp)
        if grp == "g1":
            stacks = start_gather("g3", start_gather("g2b", start_gather("g2", stacks)))
        for n, s in zip(gather_groups[grp], stacks):
            if n == "conv_w":
                W[n] = jnp.swapaxes(s, 0, 1).reshape(3, -1)
            elif n in whole:
                W[n] = s
            else:
                W[n] = s.reshape(-1, s.shape[-1])

    shard_shape = {n: shard[n].shape for n in _BIG}
    scatter_groups = {"s1": ("w_down", "w_gate", "w_up"), "s2": ("w_o",),
                      "s3": ("w_mem_kv", "w_q_up", "w_kv_up", "w_in")}
    scatters = {}

    def parts_of(n, g):
        return g if n in whole else g.reshape((N_CHIPS,) + shard_shape[n])

    def ship(grp, G):
        names = scatter_groups[grp]
        srcs = [parts_of(n, G[n][1]) for n in names]
        lands = [lax.empty((N_CHIPS - 1,) + shard_shape[n], BF16) for n in names]
        scatters[grp] = _copies_start(srcs, lands, _scatter_copies, name="scatter_start_" + grp)
        return scatters[grp][4]

    loss_row, grad_x, G = _local_step(x[0], mem[0], loss_target[0], W, fetch, ship)

    recv = {}
    for grp, names in scatter_groups.items():
        for n, r in zip(names, _copies_wait(scatters[grp], grad_x, _scatter_copies, name="scatter_wait_" + grp)):
            recv[n] = r
    part = [_sum4(parts_of(n, G[n][0]), recv[n], me, name="sum4_" + n) for n in _BIG]
    part = [_from_compute_layout(n, p) for n, p in zip(_BIG, part)]
    sib = _swap_with_sibling(part, name="swap_grads")
    out = {}
    for n, p, s in zip(_BIG, part, sib):
        out[n] = [r[None] for r in _adamw(P[n][0], Mo[n][0], Vo[n][0], [p, s], name="adamw_" + n)]

    conv_w_full_grad = G["conv_w"]
    small_g = [G["g_mix"], G["g_q_lat"], G["g_kv_lat"], G["g_q_mla"][:, :QK], G["g_k_mla"][:, :QK], G["w_pool"],
               G["pool_scale"], G["g_mem"], G["g_q_x"], G["g_k_x"], G["g_ffn"], G["conv_b"], conv_w_full_grad]
    packed = _pack(small_g + [loss_row[:, :1]])
    total = _sum8(_gather_all(packed, name="gather_small"), name="sum_small")
    shapes = [a.shape for a in small_g] + [(1, 1)]
    (parts, _) = _unpack(total, shapes)
    loss = parts[-1].reshape(())
    F = conv_b.shape[1]
    fn = F // N_CHIPS
    col0 = (2 * xi + yi) * fn
    sg = dict(zip(_SMALL, parts[:-1]))
    sg["conv_w"] = lax.dynamic_slice(sg["conv_w"], (0, col0), (3, fn))
    sw = [P[n].reshape(sg[n].shape) for n in _SMALL]
    sm = [Mo[n].reshape(sg[n].shape) for n in _SMALL]
    sv = [Vo[n].reshape(sg[n].shape) for n in _SMALL]
    gp = _pack([sg[n] for n in _SMALL])
    res = _adamw(_pack(sw), _pack(sm), _pack(sv), [gp], name="adamw_small")
    sshapes = [sg[n].shape for n in _SMALL]
    for kind, buf in zip(range(4), res):
        vals, _ = _unpack(buf, sshapes)
        for n, val in zip(_SMALL, vals):
            out.setdefault(n, [None] * 4)[kind] = val.reshape(P[n].shape)

    return (loss, grad_x[None], *[out[n][0] for n in _WEIGHTS], *[out[n][1] for n in _WEIGHTS],
            *[out[n][2] for n in _WEIGHTS], *[out[n][3] for n in _WEIGHTS])
```

```python
import functools
import math

import numpy as np
import jax
import jax.numpy as jnp
from jax import lax
from jax.experimental import pallas as pl
from jax.experimental.pallas import tpu as pltpu

F32, BF16 = jnp.float32, jnp.bfloat16
NORM_EPS = 1e-6
ROPE_THETA = 10000.0
V7X_VMEM_LIMIT_BYTES = 48 * 1024 * 1024
N_CHIPS = 4
N_DEV = 8

POOL_W = 512
POOL_WINDOWS = (2, 4, 8, 16)
HEADS = 8
NOPE, ROPE, QK = 128, 64, 192
HEAD_PAD = 256
Q_RANK, KV_RANK = 512, 256
X_HEADS, X_DIM = 4, 128
Z_COLS = 1920
Z_POOL_CB, Z_Q_CB, Z_MQ_CB = 0, 1, 2
Z_KV_CB = 6
Z_KR_CB = 14

ADAM_LR, ADAM_B1, ADAM_B2, ADAM_EPS, ADAM_WD, ADAM_STEP = 0.001, 0.9, 0.999, 1e-08, 0.01, 10

MESH = pl.DeviceIdType.MESH


def _cp(sem):
    return pltpu.CompilerParams(dimension_semantics=sem, vmem_limit_bytes=V7X_VMEM_LIMIT_BYTES)


def _row_tile(S):
    return 256 if S % 256 == 0 and S >= 2048 else 128


def _big_row_tile(S):
    return 512 if S % 512 == 0 and S >= 2048 else _row_tile(S)


def _pick(dim, prefs):
    for p in prefs:
        if dim % p == 0:
            return p
    return dim


_DN = {"nn": (((1,), (0,)), ((), ())), "nt": (((1,), (1,)), ((), ())), "tn": (((0,), (0,)), ((), ()))}


def _mm(a, b, *, mode, grid, blocks, maps, out_shape, out_dtypes, add=None, name):
    nk = grid[2]
    dn = _DN[mode]
    n_out = len(out_dtypes)

    def body(*refs):
        a_ref, b_ref = refs[0], refs[1]
        add_ref = refs[2] if add is not None else None
        p = 2 + (add is not None)
        o_refs = refs[p:p + n_out]

        def finish(r):
            if add_ref is not None:
                r = r + add_ref[...]
            for o in o_refs:
                o[...] = r.astype(o.dtype)

        def product():
            return lax.dot_general(a_ref[...].astype(BF16), b_ref[...].astype(BF16), dn, preferred_element_type=F32)

        if nk == 1:
            finish(product())
            return
        acc = refs[p + n_out]
        k = pl.program_id(2)

        @pl.when(k == 0)
        def _():
            acc[...] = jnp.zeros_like(acc)

        acc[...] += product()

        @pl.when(k == nk - 1)
        def _():
            finish(acc[...])

    a_blk, b_blk, o_blk = blocks
    a_map, b_map, o_map = maps
    in_specs = [pl.BlockSpec(a_blk, a_map), pl.BlockSpec(b_blk, b_map)]
    args = [a, b]
    if add is not None:
        in_specs.append(pl.BlockSpec(o_blk, o_map))
        args.append(add)
    outs = pl.pallas_call(
        body, grid=grid, in_specs=in_specs,
        out_specs=[pl.BlockSpec(o_blk, o_map) for _ in out_dtypes],
        out_shape=[jax.ShapeDtypeStruct(out_shape, d) for d in out_dtypes],
        scratch_shapes=[pltpu.VMEM(o_blk, F32)] if nk > 1 else [],
        compiler_params=_cp(("parallel", "parallel", "arbitrary")), name=name)(*args)
    return outs[0] if n_out == 1 else outs


def mm_nn(a, w, *, nsh=1, tm, tn, tk, out_dtypes=(F32,), add=None, name):
    M, K = a.shape
    n = w.shape[1]
    N = nsh * n
    assert w.shape[0] == nsh * K and n % tn == 0 and K % tk == 0 and M % tm == 0
    npt, kt = n // tn, K // tk
    return _mm(a, w, mode="nn", grid=(M // tm, N // tn, kt),
               blocks=((tm, tk), (tk, tn), (tm, tn)),
               maps=(lambda i, j, k: (i, k), lambda i, j, k: ((j // npt) * kt + k, j % npt),
                     lambda i, j, k: (i, j)),
               out_shape=(M, N), out_dtypes=out_dtypes, add=add, name=name)


def mm_nt(d, w, *, nsh=1, tm, to, tc, out_dtypes=(F32,), add=None, name):
    M, N = d.shape
    n = w.shape[1]
    K = w.shape[0] // nsh
    assert nsh * n == N and n % tc == 0 and K % to == 0 and M % tm == 0
    cpt, ot = n // tc, K // to
    return _mm(d, w, mode="nt", grid=(M // tm, ot, N // tc),
               blocks=((tm, tc), (to, tc), (tm, to)),
               maps=(lambda i, j, c: (i, c), lambda i, j, c: ((c // cpt) * ot + j, c % cpt),
                     lambda i, j, c: (i, j)),
               out_shape=(M, K), out_dtypes=out_dtypes, add=add, name=name)


def mm_nt_pair(d1, w1, d2, w2, *, nsh, tm, to, tc, name):
    M, N = d1.shape
    n = w1.shape[1]
    K = w1.shape[0] // nsh
    assert d2.shape == d1.shape and w2.shape == w1.shape and nsh * n == N
    assert n % tc == 0 and K % to == 0 and M % tm == 0
    cpt, ot, nk = n // tc, K // to, N // tc

    def body(a1_ref, b1_ref, a2_ref, b2_ref, o_ref, acc):
        k = pl.program_id(2)

        @pl.when(k == 0)
        def _():
            acc[...] = jnp.zeros_like(acc)

        acc[...] += lax.dot_general(a1_ref[...], b1_ref[...], _DN["nt"], preferred_element_type=F32)
        acc[...] += lax.dot_general(a2_ref[...], b2_ref[...], _DN["nt"], preferred_element_type=F32)

        @pl.when(k == nk - 1)
        def _():
            o_ref[...] = acc[...]

    a_spec = pl.BlockSpec((tm, tc), lambda i, j, c: (i, c))
    b_spec = pl.BlockSpec((to, tc), lambda i, j, c: ((c // cpt) * ot + j, c % cpt))
    return pl.pallas_call(
        body, grid=(M // tm, ot, nk), in_specs=[a_spec, b_spec, a_spec, b_spec],
        out_specs=pl.BlockSpec((tm, to), lambda i, j, c: (i, j)),
        out_shape=jax.ShapeDtypeStruct((M, K), F32),
        scratch_shapes=[pltpu.VMEM((tm, to), F32)],
        compiler_params=_cp(("parallel", "parallel", "arbitrary")), name=name)(d1, w1, d2, w2)


def mm_tn(x, d, *, nsh=1, to, tn, tk, out_dtypes=(F32,), name):
    M, K = x.shape
    N = d.shape[1]
    n = N // nsh
    assert n % tn == 0 and K % to == 0 and M % tk == 0
    npt, ot = n // tn, K // to
    return _mm(x, d, mode="tn", grid=(ot, N // tn, M // tk),
               blocks=((tk, to), (tk, tn), (to, tn)),
               maps=(lambda i, j, k: (k, i), lambda i, j, k: (k, j),
                     lambda i, j, k: ((j // npt) * ot + i, j % npt)),
               out_shape=(nsh * K, n), out_dtypes=out_dtypes, name=name)


def _rms_fwd(x, g, *, C, cb=0, name):
    S = x.shape[0]
    tm = _big_row_tile(S) if S >= 128 else S

    def body(x_ref, g_ref, o_ref):
        xv = x_ref[...]
        r = lax.rsqrt(jnp.mean(xv * xv, axis=-1, keepdims=True) + NORM_EPS)
        o_ref[...] = ((xv * r) * g_ref[...]).astype(o_ref.dtype)

    return pl.pallas_call(
        body, grid=(S // tm,),
        in_specs=[pl.BlockSpec((tm, C), lambda i: (i, cb)), pl.BlockSpec((1, C), lambda i: (0, 0))],
        out_specs=pl.BlockSpec((tm, C), lambda i: (i, 0)),
        out_shape=jax.ShapeDtypeStruct((S, C), BF16),
        compiler_params=_cp(("parallel",)), name=name)(x, g)


def _rms_bwd(x, dh, g, *, C, cb=0, res=None, out_dtypes=(F32,), name):
    S = x.shape[0]
    tm = (_big_row_tile(S) if C <= 512 else _row_tile(S)) if S >= 128 else S
    n_out = len(out_dtypes)

    def body(*refs):
        x_ref, dh_ref, g_ref = refs[:3]
        res_ref = refs[3] if res is not None else None
        p = 3 + (res is not None)
        outs = refs[p:p + n_out]
        dg_ref = refs[p + n_out]
        i = pl.program_id(0)
        xv = x_ref[...]
        r = lax.rsqrt(jnp.mean(xv * xv, axis=-1, keepdims=True) + NORM_EPS)
        n = xv * r
        dhv = dh_ref[...].astype(F32)
        dn = dhv * g_ref[...]
        c = jnp.mean(dn * n, axis=-1, keepdims=True)
        dx = r * (dn - n * c)
        if res_ref is not None:
            dx = res_ref[...] + dx
        for o in outs:
            o[...] = dx.astype(o.dtype)

        @pl.when(i == 0)
        def _():
            dg_ref[...] = jnp.zeros_like(dg_ref)

        dg_ref[...] += jnp.sum(dhv * n, axis=0, keepdims=True)

    row = pl.BlockSpec((tm, C), lambda i: (i, 0))
    in_specs = [pl.BlockSpec((tm, C), lambda i: (i, cb)), row, pl.BlockSpec((1, C), lambda i: (0, 0))]
    args = [x, dh, g]
    if res is not None:
        in_specs.append(row)
        args.append(res)
    return pl.pallas_call(
        body, grid=(S // tm,), in_specs=in_specs,
        out_specs=[row] * n_out + [pl.BlockSpec((1, C), lambda i: (0, 0))],
        out_shape=[jax.ShapeDtypeStruct((S, C), d) for d in out_dtypes] + [jax.ShapeDtypeStruct((1, C), F32)],
        compiler_params=_cp(("arbitrary",)), name=name)(*args)


def _pool_cnt(t0, rows, w):
    t = t0 + lax.broadcasted_iota(jnp.int32, (rows, 1), 0)
    return jnp.minimum(t + 1, w).astype(F32)


def _pool_d(halo, tile, gi, t0, tm):
    s = jnp.concatenate([halo, tile], axis=0)
    for step in (1, 2, 4, 8)[:gi + 1]:
        s = s + pltpu.roll(s, step, 0)
    return s[16:] / _pool_cnt(t0, tm, POOL_WINDOWS[gi]) - tile


def _pool_fwd(z, w_pool, pool_scale, *, name):
    S = z.shape[0]
    tm = _row_tile(S)
    hb = tm // 16

    def body(z_ref, h_ref, w_ref, sc_ref, o_ref):
        i = pl.program_id(0)
        halo = h_ref[...] * (i > 0).astype(F32)
        for gi in range(4):
            cs = slice(gi * 128, (gi + 1) * 128)
            d = _pool_d(halo[:, cs], z_ref[:, cs], gi, i * tm, tm)
            yp = jnp.dot(d.astype(BF16), w_ref[gi], preferred_element_type=F32)
            o_ref[:, cs] = (yp * sc_ref[:, cs]).astype(o_ref.dtype)

    return pl.pallas_call(
        body, grid=(S // tm,),
        in_specs=[pl.BlockSpec((tm, POOL_W), lambda i: (i, Z_POOL_CB)),
                  pl.BlockSpec((16, POOL_W), lambda i: (jnp.maximum(i * hb - 1, 0), Z_POOL_CB)),
                  pl.BlockSpec((4, 128, 128), lambda i: (0, 0, 0)),
                  pl.BlockSpec((1, POOL_W), lambda i: (0, 0))],
        out_specs=pl.BlockSpec((tm, POOL_W), lambda i: (i, 0)),
        out_shape=jax.ShapeDtypeStruct((S, POOL_W), BF16),
        compiler_params=_cp(("parallel",)), name=name)(z, z, w_pool, pool_scale)


def _pool_bwd(z, d_cat, w_pool, pool_scale, *, name):
    S = z.shape[0]
    tm = _row_tile(S)
    hb = tm // 16
    nt = S // tm
    E = tm + 16

    def body(z_ref, h_ref, dy_ref, dyn_ref, w_ref, sc_ref, dz_ref, gw_ref, gs_ref):
        i = pl.program_id(0)

        @pl.when(i == 0)
        def _():
            gw_ref[...] = jnp.zeros_like(gw_ref)
            gs_ref[...] = jnp.zeros_like(gs_ref)

        halo = h_ref[...] * (i > 0).astype(F32)
        dy_next = dyn_ref[...].astype(F32) * (i < nt - 1).astype(F32)
        for gi in range(4):
            cs = slice(gi * 128, (gi + 1) * 128)
            w = w_ref[gi]
            d = _pool_d(halo[:, cs], z_ref[:, cs], gi, i * tm, tm)
            db = d.astype(BF16)
            dy = dy_ref[:, cs].astype(F32)
            yp = jnp.dot(db, w, preferred_element_type=F32)
            gs_ref[:, cs] += jnp.sum(dy * yp, axis=0, keepdims=True)
            sc = sc_ref[:, cs]
            dys = (dy * sc).astype(BF16)
            gw_ref[gi] += lax.dot_general(db, dys, _DN["tn"], preferred_element_type=F32)
            dys_ext = jnp.concatenate([dys, (dy_next[:, cs] * sc).astype(BF16)], axis=0)
            dd = lax.dot_general(dys_ext, w, _DN["nt"], preferred_element_type=F32)
            r = dd / _pool_cnt(i * tm, E, POOL_WINDOWS[gi])
            for step in (1, 2, 4, 8)[:gi + 1]:
                r = r + pltpu.roll(r, E - step, 0)
            dz_ref[:, cs] = (r[:tm] - dd[:tm]).astype(dz_ref.dtype)

    return pl.pallas_call(
        body, grid=(nt,),
        in_specs=[pl.BlockSpec((tm, POOL_W), lambda i: (i, Z_POOL_CB)),
                  pl.BlockSpec((16, POOL_W), lambda i: (jnp.maximum(i * hb - 1, 0), Z_POOL_CB)),
                  pl.BlockSpec((tm, POOL_W), lambda i: (i, 0)),
                  pl.BlockSpec((16, POOL_W), lambda i: (jnp.minimum((i + 1) * hb, S // 16 - 1), 0)),
                  pl.BlockSpec((4, 128, 128), lambda i: (0, 0, 0)),
                  pl.BlockSpec((1, POOL_W), lambda i: (0, 0))],
        out_specs=[pl.BlockSpec((tm, POOL_W), lambda i: (i, 0)),
                   pl.BlockSpec((4, 128, 128), lambda i: (0, 0, 0)),
                   pl.BlockSpec((1, POOL_W), lambda i: (0, 0))],
        out_shape=[jax.ShapeDtypeStruct((S, POOL_W), BF16),
                   jax.ShapeDtypeStruct((4, 128, 128), F32),
                   jax.ShapeDtypeStruct((1, POOL_W), F32)],
        compiler_params=_cp(("arbitrary",)), name=name)(z, z, d_cat, d_cat, w_pool, pool_scale)


def _rope_tables(S):
    half = ROPE // 2
    inv_freq = 1.0 / (ROPE_THETA ** (jnp.arange(half, dtype=F32) / half))
    ang = jnp.arange(S).astype(F32)[:, None] * inv_freq[None, :]
    cos, sin = jnp.cos(ang), jnp.sin(ang)
    zero = jnp.zeros((S, half), F32)
    cos_t = jnp.concatenate([cos, cos, zero, zero], axis=1)
    sa_t = jnp.concatenate([-sin, zero, zero, zero], axis=1)
    sb_t = jnp.concatenate([zero, sin, zero, zero], axis=1)
    return cos_t, sa_t, sb_t


def _head_fwd(xn, xr, gn, gr, cos, sa, sb):
    ms = (jnp.sum(xn * xn, axis=-1, keepdims=True) + jnp.sum(xr * xr, axis=-1, keepdims=True)) * (1.0 / QK)
    r = lax.rsqrt(ms + NORM_EPS)
    on = (xn * r) * gn
    yr = (xr * r) * gr
    orr = yr * cos + pltpu.roll(yr, 96, 1) * sa + pltpu.roll(yr, 32, 1) * sb
    return on, orr


def _head_bwd(xn, xr, gn, gr, don, dor, cos, sa, sb):
    ms = (jnp.sum(xn * xn, axis=-1, keepdims=True) + jnp.sum(xr * xr, axis=-1, keepdims=True)) * (1.0 / QK)
    r = lax.rsqrt(ms + NORM_EPS)
    nn, nr = xn * r, xr * r
    dyr = dor * cos + pltpu.roll(dor * sa, 32, 1) + pltpu.roll(dor * sb, 96, 1)
    ggn, ggr = don * nn, dyr * nr
    dnn, dnr = don * gn, dyr * gr
    c = (jnp.sum(dnn * nn, axis=-1, keepdims=True) + jnp.sum(dnr * nr, axis=-1, keepdims=True)) * (1.0 / QK)
    return r * (dnn - nn * c), r * (dnr - nr * c), ggn, ggr


def _kv_cols(h):
    base = (h // 2) * 512 + (h % 2) * 128
    return base, base + 256


def _qkrope_fwd(qraw, kvraw, z, gq, gk, tabs, *, name):
    S = qraw.shape[0]
    tm = _row_tile(S)

    def body(q_ref, kv_ref, zkr_ref, gq_ref, gk_ref, cos_ref, sa_ref, sb_ref, qo_ref, ko_ref, vo_ref, vt_ref):
        cos, sa, sb = cos_ref[...], sa_ref[...], sb_ref[...]
        zkr = zkr_ref[...]
        gqn, gqr, gkn, gkr = gq_ref[:, :128], gq_ref[:, 128:], gk_ref[:, :128], gk_ref[:, 128:]
        for h in range(HEADS):
            b = h * HEAD_PAD
            on, orr = _head_fwd(q_ref[:, b:b + 128], q_ref[:, b + 128:b + 256], gqn, gqr, cos, sa, sb)
            qo_ref[:, b:b + 128] = on.astype(BF16)
            qo_ref[:, b + 128:b + 256] = orr.astype(BF16)
            kc, vc = _kv_cols(h)
            on, orr = _head_fwd(kv_ref[:, kc:kc + 128], zkr, gkn, gkr, cos, sa, sb)
            ko_ref[:, b:b + 128] = on.astype(BF16)
            ko_ref[:, b + 128:b + 256] = orr.astype(BF16)
            vv = kv_ref[:, vc:vc + 128]
            vo_ref[:, h * 128:(h + 1) * 128] = vv.astype(BF16)
            vt_ref[h * 128:(h + 1) * 128, :] = jnp.transpose(vv).astype(BF16)

    W = HEADS * HEAD_PAD
    row = lambda c: pl.BlockSpec((tm, c), lambda i: (i, 0))
    vec = lambda c: pl.BlockSpec((1, c), lambda i: (0, 0))
    return pl.pallas_call(
        body, grid=(S // tm,),
        in_specs=[row(W), row(W), pl.BlockSpec((tm, 128), lambda i: (i, Z_KR_CB)), vec(256), vec(256),
                  row(128), row(128), row(128)],
        out_specs=[row(W), row(W), row(HEADS * 128), pl.BlockSpec((HEADS * 128, tm), lambda i: (0, i))],
        out_shape=[jax.ShapeDtypeStruct((S, W), BF16), jax.ShapeDtypeStruct((S, W), BF16),
                   jax.ShapeDtypeStruct((S, HEADS * 128), BF16), jax.ShapeDtypeStruct((HEADS * 128, S), BF16)],
        compiler_params=_cp(("parallel",)), name=name)(qraw, kvraw, z, gq, gk, *tabs)


def _qkrope_bwd(qraw, kvraw, z, gq, gk, tabs, dq, dk, dv, *, name):
    S = qraw.shape[0]
    tm = _row_tile(S)

    def body(q_ref, kv_ref, zkr_ref, gq_ref, gk_ref, cos_ref, sa_ref, sb_ref, dq_ref, dk_ref, dv_ref,
             dqo_ref, dkvo_ref, dkr_ref, ggq_ref, ggk_ref):
        i = pl.program_id(0)

        @pl.when(i == 0)
        def _():
            ggq_ref[...] = jnp.zeros_like(ggq_ref)
            ggk_ref[...] = jnp.zeros_like(ggk_ref)

        cos, sa, sb = cos_ref[...], sa_ref[...], sb_ref[...]
        zkr = zkr_ref[...]
        gqn, gqr, gkn, gkr = gq_ref[:, :128], gq_ref[:, 128:], gk_ref[:, :128], gk_ref[:, 128:]
        dkr = jnp.zeros((tm, 128), F32)
        sq_n = jnp.zeros((1, 128), F32)
        sq_r = jnp.zeros((1, 128), F32)
        sk_n = jnp.zeros((1, 128), F32)
        sk_r = jnp.zeros((1, 128), F32)
        for h in range(HEADS):
            b = h * HEAD_PAD
            dxn, dxr, ggn, ggr = _head_bwd(q_ref[:, b:b + 128], q_ref[:, b + 128:b + 256], gqn, gqr,
                                           dq_ref[:, b:b + 128], dq_ref[:, b + 128:b + 256], cos, sa, sb)
            dqo_ref[:, b:b + 128] = dxn.astype(BF16)
            dqo_ref[:, b + 128:b + 256] = dxr.astype(BF16)
            sq_n += jnp.sum(ggn, axis=0, keepdims=True)
            sq_r += jnp.sum(ggr, axis=0, keepdims=True)
            kc, vc = _kv_cols(h)
            dxn, dxr, ggn, ggr = _head_bwd(kv_ref[:, kc:kc + 128], zkr, gkn, gkr,
                                           dk_ref[:, b:b + 128], dk_ref[:, b + 128:b + 256], cos, sa, sb)
            dkvo_ref[:, kc:kc + 128] = dxn.astype(BF16)
            dkvo_ref[:, vc:vc + 128] = dv_ref[:, h * 128:(h + 1) * 128].astype(BF16)
            dkr += dxr
            sk_n += jnp.sum(ggn, axis=0, keepdims=True)
            sk_r += jnp.sum(ggr, axis=0, keepdims=True)
        dkr_ref[...] = dkr.astype(BF16)
        ggq_ref[:, :128] += sq_n
        ggq_ref[:, 128:] += sq_r
        ggk_ref[:, :128] += sk_n
        ggk_ref[:, 128:] += sk_r

    W = HEADS * HEAD_PAD
    row = lambda c: pl.BlockSpec((tm, c), lambda i: (i, 0))
    vec = lambda c: pl.BlockSpec((1, c), lambda i: (0, 0))
    return pl.pallas_call(
        body, grid=(S // tm,),
        in_specs=[row(W), row(W), pl.BlockSpec((tm, 128), lambda i: (i, Z_KR_CB)), vec(256), vec(256),
                  row(128), row(128), row(128), row(W), row(W), row(HEADS * 128)],
        out_specs=[row(W), row(W), row(128), vec(256), vec(256)],
        out_shape=[jax.ShapeDtypeStruct((S, W), BF16), jax.ShapeDtypeStruct((S, W), BF16),
                   jax.ShapeDtypeStruct((S, 128), BF16),
                   jax.ShapeDtypeStruct((1, 256), F32), jax.ShapeDtypeStruct((1, 256), F32)],
        compiler_params=_cp(("arbitrary",)), name=name)(qraw, kvraw, z, gq, gk, *tabs, dq, dk, dv)


LOG2E = 1.4426950408889634
SCORE_SCALE = 1.0 / math.sqrt(QK)
SCORE_SCALE_LOG2 = SCORE_SCALE * LOG2E


def _fa_tile(S):
    return 512 if S % 512 == 0 and S >= 2048 else 128


def _flash_fwd(q, k, vt, *, name):
    S = q.shape[0]
    ts = _fa_tile(S)

    def body(q_ref, k_ref, vt_ref, o_ref, ob_ref, lse_ref, m_sc, l_sc, acc_sc, s_buf):
        qi = pl.program_id(1)
        m_sc[...] = jnp.full_like(m_sc, -jnp.inf)
        l_sc[...] = jnp.zeros_like(l_sc)
        acc_sc[...] = jnp.zeros_like(acc_sc)
        qb = q_ref[...]

        def scores(kidx, masked):
            k0 = pl.multiple_of(kidx * ts, ts)
            st = lax.dot_general(k_ref[pl.ds(k0, ts), :], qb, _DN["nt"], preferred_element_type=F32)
            if masked:
                st = jnp.where(lax.broadcasted_iota(jnp.int32, (ts, ts), 0) > lax.broadcasted_iota(jnp.int32, (ts, ts), 1),
                               -jnp.inf, st)
            return st

        def update(st, kidx):
            k0 = pl.multiple_of(kidx * ts, ts)
            m_prev = m_sc[...]
            m_new = jnp.maximum(m_prev, jnp.max(st, axis=0, keepdims=True))
            alpha = jnp.exp2((m_prev - m_new) * SCORE_SCALE_LOG2)
            pt = jnp.exp2((st - m_new[0:1, :]) * SCORE_SCALE_LOG2)
            l_sc[...] = alpha * l_sc[...] + jnp.sum(pt, axis=0, keepdims=True)
            acc_sc[...] = alpha[0:1, :] * acc_sc[...] + jnp.dot(vt_ref[:, pl.ds(k0, ts)], pt.astype(BF16),
                                                                preferred_element_type=F32)
            m_sc[...] = m_new

        last = jnp.maximum(qi - 1, 0)
        s_buf[0] = scores(0, False)

        def trip(u, carry):
            s_buf[1] = scores(2 * u + 1, False)
            update(s_buf[0], 2 * u)
            s_buf[0] = scores(jnp.minimum(2 * u + 2, last), False)
            update(s_buf[1], 2 * u + 1)
            return carry

        lax.fori_loop(0, qi // 2, trip, 0)

        @pl.when(qi % 2 == 1)
        def _():
            update(s_buf[0], qi - 1)

        update(scores(qi, True), qi)
        ot = acc_sc[...] / l_sc[0:1, :]
        o = jnp.transpose(ot)
        o_ref[...] = o
        ob_ref[...] = o.astype(BF16)
        lse_ref[...] = m_sc[...] * SCORE_SCALE_LOG2 + jnp.log2(l_sc[...])

    return pl.pallas_call(
        body, grid=(HEADS, S // ts),
        in_specs=[pl.BlockSpec((ts, HEAD_PAD), lambda h, i: (i, h)),
                  pl.BlockSpec((S, HEAD_PAD), lambda h, i: (0, h)),
                  pl.BlockSpec((128, S), lambda h, i: (h, 0))],
        out_specs=[pl.BlockSpec((ts, 128), lambda h, i: (i, h)),
                   pl.BlockSpec((ts, 128), lambda h, i: (i, h)),
                   pl.BlockSpec((None, 8, ts), lambda h, i: (h, 0, i))],
        out_shape=[jax.ShapeDtypeStruct((S, HEADS * 128), F32), jax.ShapeDtypeStruct((S, HEADS * 128), BF16),
                   jax.ShapeDtypeStruct((HEADS, 8, S), F32)],
        scratch_shapes=[pltpu.VMEM((8, ts), F32), pltpu.VMEM((8, ts), F32), pltpu.VMEM((128, ts), F32),
                        pltpu.VMEM((2, ts, ts), F32)],
        compiler_params=_cp(("parallel", "arbitrary")), name=name)(q, k, vt)


def _attn_bwd_prep(o, d_cat, *, name):
    S = o.shape[0]
    tm = _row_tile(S)
    H = HEADS * 128
    half = H // 2

    def body(o_ref, da_ref, db_ref, delta_ref):
        for h in range(HEADS):
            src, c0 = (da_ref, h * 128) if h * 128 < half else (db_ref, h * 128 - half)
            do = src[:, c0:c0 + 128].astype(F32)
            prod = jnp.transpose(do * o_ref[:, h * 128:(h + 1) * 128])
            delta_ref[h] = jnp.broadcast_to(jnp.sum(prod, axis=0, keepdims=True), (8, tm))

    return pl.pallas_call(
        body, grid=(S // tm,),
        in_specs=[pl.BlockSpec((tm, H), lambda i: (i, 0)),
                  pl.BlockSpec((tm, half), lambda i: (i, 1)), pl.BlockSpec((tm, half), lambda i: (i, 2))],
        out_specs=pl.BlockSpec((HEADS, 8, tm), lambda i: (0, 0, i)),
        out_shape=jax.ShapeDtypeStruct((HEADS, 8, S), F32),
        compiler_params=_cp(("parallel",)), name=name)(o, d_cat, d_cat)


def _flash_bwd(q, k, v, d_cat, lse, delta, *, name):
    S = q.shape[0]
    ts = _fa_tile(S)
    nb = S // ts

    def body(q_ref, do_ref, lse_ref, delta_ref, k_ref, v_ref, dq_ref, dk_ref, dv_ref, dk_sc, dv_sc):
        j = pl.program_id(1)

        @pl.when(j == 0)
        def _():
            dq_ref[...] = jnp.zeros_like(dq_ref)

        dk_sc[...] = jnp.zeros_like(dk_sc)
        dv_sc[...] = jnp.zeros_like(dv_sc)
        kb, vb = k_ref[...], v_ref[...]

        def products(i):
            q0 = pl.multiple_of(i * ts, ts)
            qb = q_ref[pl.ds(q0, ts), :]
            dob_ = do_ref[pl.ds(q0, ts), :]
            st = lax.dot_general(kb, qb, _DN["nt"], preferred_element_type=F32)
            dpt = lax.dot_general(vb, dob_, _DN["nt"], preferred_element_type=F32)
            return q0, qb, dob_, st, dpt

        def accumulate(q0, qb, dob_, st, dpt, masked):
            pt = jnp.exp2(st * SCORE_SCALE_LOG2 - lse_ref[0:1, pl.ds(q0, ts)])
            if masked:
                pt = jnp.where(lax.broadcasted_iota(jnp.int32, (ts, ts), 0) > lax.broadcasted_iota(jnp.int32, (ts, ts), 1),
                               0.0, pt)
            dv_sc[...] += jnp.dot(pt.astype(BF16), dob_, preferred_element_type=F32)
            dst = (pt * (dpt - delta_ref[0:1, pl.ds(q0, ts)])).astype(BF16)
            dk_sc[...] += jnp.dot(dst, qb, preferred_element_type=F32) * SCORE_SCALE
            dq_ref[pl.ds(q0, ts), :] += lax.dot_general(dst, kb, _DN["tn"], preferred_element_type=F32) * SCORE_SCALE

        accumulate(*products(j), True)
        n_below = nb - 1 - j

        def pair(t, carry):
            a, b = products(j + 1 + 2 * t), products(j + 2 + 2 * t)
            accumulate(*a, False)
            accumulate(*b, False)
            return carry

        lax.fori_loop(0, n_below // 2, pair, 0)

        @pl.when(n_below % 2 == 1)
        def _():
            accumulate(*products(nb - 1), False)

        dk_ref[...] = dk_sc[...]
        dv_ref[...] = dv_sc[...]

    return pl.pallas_call(
        body, grid=(HEADS, nb),
        in_specs=[pl.BlockSpec((S, HEAD_PAD), lambda h, j: (0, h)),
                  pl.BlockSpec((S, 128), lambda h, j: (0, 4 + h)),
                  pl.BlockSpec((None, 8, S), lambda h, j: (h, 0, 0)),
                  pl.BlockSpec((None, 8, S), lambda h, j: (h, 0, 0)),
                  pl.BlockSpec((ts, HEAD_PAD), lambda h, j: (j, h)),
                  pl.BlockSpec((ts, 128), lambda h, j: (j, h))],
        out_specs=[pl.BlockSpec((S, HEAD_PAD), lambda h, j: (0, h)),
                   pl.BlockSpec((ts, HEAD_PAD), lambda h, j: (j, h)),
                   pl.BlockSpec((ts, 128), lambda h, j: (j, h))],
        out_shape=[jax.ShapeDtypeStruct((S, HEADS * HEAD_PAD), F32), jax.ShapeDtypeStruct((S, HEADS * HEAD_PAD), F32),
                   jax.ShapeDtypeStruct((S, HEADS * 128), F32)],
        scratch_shapes=[pltpu.VMEM((ts, HEAD_PAD), F32), pltpu.VMEM((ts, 128), F32)],
        compiler_params=_cp(("parallel", "arbitrary")), name=name)(q, d_cat, lse, delta, k, v)


def _memk_fwd(mkv, gkx, *, name):
    M = mkv.shape[0]
    XW = X_HEADS * X_DIM

    def body(mkv_ref, g_ref, k_ref, v_ref):
        for h in range(X_HEADS):
            cs = slice(h * X_DIM, (h + 1) * X_DIM)
            xv = mkv_ref[:, cs]
            r = lax.rsqrt(jnp.mean(xv * xv, axis=-1, keepdims=True) + NORM_EPS)
            k_ref[:, cs] = ((xv * r) * g_ref[...]).astype(BF16)
        v_ref[...] = mkv_ref[:, XW:].astype(BF16)

    return pl.pallas_call(
        body, grid=(1,),
        in_specs=[pl.BlockSpec((M, 2 * XW), lambda i: (0, 0)), pl.BlockSpec((1, X_DIM), lambda i: (0, 0))],
        out_specs=[pl.BlockSpec((M, XW), lambda i: (0, 0)), pl.BlockSpec((M, XW), lambda i: (0, 0))],
        out_shape=[jax.ShapeDtypeStruct((M, XW), BF16), jax.ShapeDtypeStruct((M, XW), BF16)],
        compiler_params=_cp(("arbitrary",)), name=name)(mkv, gkx)


def _memk_bwd(mkv, gkx, dk, dv, *, name):
    M = mkv.shape[0]
    XW = X_HEADS * X_DIM

    def body(mkv_ref, g_ref, dk_ref, dv_ref, o_ref, gg_ref):
        gg = jnp.zeros((1, X_DIM), F32)
        for h in range(X_HEADS):
            cs = slice(h * X_DIM, (h + 1) * X_DIM)
            xv = mkv_ref[:, cs]
            r = lax.rsqrt(jnp.mean(xv * xv, axis=-1, keepdims=True) + NORM_EPS)
            n = xv * r
            dkv = dk_ref[:, cs]
            gg += jnp.sum(dkv * n, axis=0, keepdims=True)
            dn = dkv * g_ref[...]
            c = jnp.mean(dn * n, axis=-1, keepdims=True)
            o_ref[:, cs] = (r * (dn - n * c)).astype(BF16)
        o_ref[:, XW:] = dv_ref[...].astype(BF16)
        gg_ref[...] = gg

    full = lambda c: pl.BlockSpec((M, c), lambda i: (0, 0))
    return pl.pallas_call(
        body, grid=(1,),
        in_specs=[full(2 * XW), pl.BlockSpec((1, X_DIM), lambda i: (0, 0)), full(XW), full(XW)],
        out_specs=[full(2 * XW), pl.BlockSpec((1, X_DIM), lambda i: (0, 0))],
        out_shape=[jax.ShapeDtypeStruct((M, 2 * XW), BF16), jax.ShapeDtypeStruct((1, X_DIM), F32)],
        compiler_params=_cp(("arbitrary",)), name=name)(mkv, gkx, dk, dv)


def _xq_norm(z_ref, g_ref, h):
    xv = z_ref[:, h * X_DIM:(h + 1) * X_DIM]
    r = lax.rsqrt(jnp.mean(xv * xv, axis=-1, keepdims=True) + NORM_EPS)
    n = xv * r
    return n, r, n * g_ref[...]


def _xprobs(qb, k_ref, h):
    s = lax.dot_general(qb, k_ref[:, h * X_DIM:(h + 1) * X_DIM], _DN["nt"],
                        preferred_element_type=F32) * (1.0 / math.sqrt(X_DIM))
    e = jnp.exp(s - jnp.max(s, axis=-1, keepdims=True))
    return e / jnp.sum(e, axis=-1, keepdims=True)


def _memattn_fwd(z, kx, vx, gqx, *, name):
    S = z.shape[0]
    M = kx.shape[0]
    tm = _big_row_tile(S)
    XW = X_HEADS * X_DIM

    def body(z_ref, k_ref, v_ref, g_ref, o_ref):
        for h in range(X_HEADS):
            cs = slice(h * X_DIM, (h + 1) * X_DIM)
            _, _, qn = _xq_norm(z_ref, g_ref, h)
            p = _xprobs(qn.astype(BF16), k_ref, h)
            o_ref[:, cs] = jnp.dot(p.astype(BF16), v_ref[:, cs], preferred_element_type=F32).astype(BF16)

    return pl.pallas_call(
        body, grid=(S // tm,),
        in_specs=[pl.BlockSpec((tm, XW), lambda i: (i, Z_MQ_CB)), pl.BlockSpec((M, XW), lambda i: (0, 0)),
                  pl.BlockSpec((M, XW), lambda i: (0, 0)), pl.BlockSpec((1, X_DIM), lambda i: (0, 0))],
        out_specs=pl.BlockSpec((tm, XW), lambda i: (i, 0)),
        out_shape=jax.ShapeDtypeStruct((S, XW), BF16),
        compiler_params=_cp(("parallel",)), name=name)(z, kx, vx, gqx)


def _memattn_bwd(z, kx, vx, gqx, d_cat, *, name):
    S = z.shape[0]
    M = kx.shape[0]
    tm = _big_row_tile(S)
    XW = X_HEADS * X_DIM
    scale = 1.0 / math.sqrt(X_DIM)

    def body(z_ref, k_ref, v_ref, g_ref, do_ref, dz_ref, dk_ref, dv_ref, gg_ref):
        i = pl.program_id(0)

        @pl.when(i == 0)
        def _():
            dk_ref[...] = jnp.zeros_like(dk_ref)
            dv_ref[...] = jnp.zeros_like(dv_ref)
            gg_ref[...] = jnp.zeros_like(gg_ref)

        gg = jnp.zeros((1, X_DIM), F32)
        for h in range(X_HEADS):
            cs = slice(h * X_DIM, (h + 1) * X_DIM)
            n, r, qn = _xq_norm(z_ref, g_ref, h)
            qb = qn.astype(BF16)
            p = _xprobs(qb, k_ref, h)
            pb = p.astype(BF16)
            dob = do_ref[:, cs].astype(BF16)
            dv_ref[:, cs] += lax.dot_general(pb, dob, _DN["tn"], preferred_element_type=F32)
            dp = lax.dot_general(dob, v_ref[:, cs], _DN["nt"], preferred_element_type=F32)
            ds = (p * (dp - jnp.sum(dp * p, axis=-1, keepdims=True))).astype(BF16)
            dk_ref[:, cs] += lax.dot_general(ds, qb, _DN["tn"], preferred_element_type=F32) * scale
            dqn = jnp.dot(ds, k_ref[:, cs], preferred_element_type=F32) * scale
            gg += jnp.sum(dqn * n, axis=0, keepdims=True)
            dn = dqn * g_ref[...]
            c = jnp.mean(dn * n, axis=-1, keepdims=True)
            dz_ref[:, cs] = (r * (dn - n * c)).astype(BF16)
        gg_ref[...] += gg

    full = pl.BlockSpec((M, XW), lambda i: (0, 0))
    vec = pl.BlockSpec((1, X_DIM), lambda i: (0, 0))
    return pl.pallas_call(
        body, grid=(S // tm,),
        in_specs=[pl.BlockSpec((tm, XW), lambda i: (i, Z_MQ_CB)), full, full, vec,
                  pl.BlockSpec((tm, XW), lambda i: (i, 3))],
        out_specs=[pl.BlockSpec((tm, XW), lambda i: (i, 0)), full, full, vec],
        out_shape=[jax.ShapeDtypeStruct((S, XW), BF16), jax.ShapeDtypeStruct((M, XW), F32),
                   jax.ShapeDtypeStruct((M, XW), F32), jax.ShapeDtypeStruct((1, X_DIM), F32)],
        compiler_params=_cp(("arbitrary",)), name=name)(z, kx, vx, gqx, d_cat)


def _silu_parts(x):
    h = 0.5 * x
    return h, jnp.tanh(h)


GLU_HALO = 16


def _glu_tiles(S, F):
    return _big_row_tile(S), _pick(F, (1408, 512, 256, 128))


def _glu_fwd(g, u, conv_w, conv_b, *, name):
    S, F = g.shape
    tm, tc = _glu_tiles(S, F)
    hb = tm // GLU_HALO

    def body(g_ref, gp_ref, u_ref, w_ref, b_ref, a_ref):
        i = pl.program_id(1)
        gt = g_ref[...].astype(F32)
        ext = jnp.concatenate([gp_ref[...].astype(F32) * (i > 0).astype(F32), gt], axis=0)
        gc = b_ref[...] + w_ref[0:1, :] * pltpu.roll(ext, 2, 0)[GLU_HALO:]
        gc = gc + w_ref[1:2, :] * pltpu.roll(ext, 1, 0)[GLU_HALO:]
        gc = gc + w_ref[2:3, :] * gt
        h, t = _silu_parts(gc)
        a_ref[...] = ((h * (1.0 + t)) * u_ref[...].astype(F32)).astype(BF16)

    return pl.pallas_call(
        body, grid=(F // tc, S // tm),
        in_specs=[pl.BlockSpec((tm, tc), lambda j, i: (i, j)),
                  pl.BlockSpec((GLU_HALO, tc), lambda j, i: (jnp.maximum(i * hb - 1, 0), j)),
                  pl.BlockSpec((tm, tc), lambda j, i: (i, j)),
                  pl.BlockSpec((3, tc), lambda j, i: (0, j)),
                  pl.BlockSpec((1, tc), lambda j, i: (0, j))],
        out_specs=pl.BlockSpec((tm, tc), lambda j, i: (i, j)),
        out_shape=jax.ShapeDtypeStruct((S, F), BF16),
        compiler_params=_cp(("parallel", "parallel")), name=name)(g, g, u, conv_w, conv_b)


def _glu_bwd(g, u, d_a, conv_w, conv_b, *, name):
    S, F = g.shape
    tm, tc = _glu_tiles(S, F)
    hb = tm // GLU_HALO
    nt = S // tm
    E = tm + GLU_HALO

    def body(g_ref, gp_ref, gn_ref, u_ref, un_ref, da_ref, dan_ref, w_ref, b_ref,
             dg_ref, du_ref, gw_ref, gb_ref):
        i = pl.program_id(1)

        @pl.when(i == 0)
        def _():
            gw_ref[...] = jnp.zeros_like(gw_ref)
            gb_ref[...] = jnp.zeros_like(gb_ref)

        w0, w1, w2 = w_ref[0:1, :], w_ref[1:2, :], w_ref[2:3, :]
        gext = jnp.concatenate([gp_ref[...].astype(F32) * (i > 0).astype(F32), g_ref[...].astype(F32),
                                gn_ref[...].astype(F32)], axis=0)
        g1 = pltpu.roll(gext, 1, 0)[GLU_HALO:]
        g2 = pltpu.roll(gext, 2, 0)[GLU_HALO:]
        g0 = gext[GLU_HALO:]
        gc = b_ref[...] + w0 * g2
        gc = gc + w1 * g1
        gc = gc + w2 * g0
        h, t = _silu_parts(gc)
        t1 = 1.0 + t
        da = jnp.concatenate([da_ref[...].astype(F32), dan_ref[...].astype(F32) * (i < nt - 1).astype(F32)], axis=0)
        uu = jnp.concatenate([u_ref[...].astype(F32), un_ref[...].astype(F32)], axis=0)
        du_ref[...] = (da[:tm] * (h[:tm] * t1[:tm])).astype(BF16)
        dgc = (da * uu) * (0.5 * (t1 + h * (1.0 - t * t)))
        dg = w2 * dgc[:tm] + w1 * pltpu.roll(dgc, E - 1, 0)[:tm] + w0 * pltpu.roll(dgc, E - 2, 0)[:tm]
        dg_ref[...] = dg.astype(BF16)
        dgt = dgc[:tm]
        gb_ref[...] += jnp.sum(dgt, axis=0, keepdims=True)
        gw_ref[0:1, :] += jnp.sum(dgt * g2[:tm], axis=0, keepdims=True)
        gw_ref[1:2, :] += jnp.sum(dgt * g1[:tm], axis=0, keepdims=True)
        gw_ref[2:3, :] += jnp.sum(dgt * g0[:tm], axis=0, keepdims=True)

    tile = pl.BlockSpec((tm, tc), lambda j, i: (i, j))
    nxt = pl.BlockSpec((GLU_HALO, tc), lambda j, i: (jnp.minimum((i + 1) * hb, S // GLU_HALO - 1), j))
    prv = pl.BlockSpec((GLU_HALO, tc), lambda j, i: (jnp.maximum(i * hb - 1, 0), j))
    return pl.pallas_call(
        body, grid=(F // tc, nt),
        in_specs=[tile, prv, nxt, tile, nxt, tile, nxt,
                  pl.BlockSpec((3, tc), lambda j, i: (0, j)), pl.BlockSpec((1, tc), lambda j, i: (0, j))],
        out_specs=[tile, tile, pl.BlockSpec((3, tc), lambda j, i: (0, j)), pl.BlockSpec((1, tc), lambda j, i: (0, j))],
        out_shape=[jax.ShapeDtypeStruct((S, F), BF16), jax.ShapeDtypeStruct((S, F), BF16),
                   jax.ShapeDtypeStruct((3, F), F32), jax.ShapeDtypeStruct((1, F), F32)],
        compiler_params=_cp(("parallel", "arbitrary")), name=name)(g, g, g, u, u, d_a, d_a, conv_w, conv_b)


def _loss_head(y, target, *, name):
    S, D = y.shape
    tm = _big_row_tile(S)
    nt = S // tm

    def body(y_ref, t_ref, dy_ref, dyb_ref, loss_ref, acc):
        i = pl.program_id(0)

        @pl.when(i == 0)
        def _():
            acc[...] = jnp.zeros_like(acc)

        e = y_ref[...] - t_ref[...]
        dy = e * (1.0 / D)
        dy_ref[...] = dy
        dyb_ref[...] = dy.astype(BF16)
        acc[...] += jnp.sum(e * e, axis=0, keepdims=True)

        @pl.when(i == nt - 1)
        def _():
            loss_ref[...] = jnp.broadcast_to(jnp.sum(acc[...], axis=1, keepdims=True) * (0.5 / D), (1, 128))

    row = pl.BlockSpec((tm, D), lambda i: (i, 0))
    return pl.pallas_call(
        body, grid=(nt,), in_specs=[row, row],
        out_specs=[row, row, pl.BlockSpec((1, 128), lambda i: (0, 0))],
        out_shape=[jax.ShapeDtypeStruct((S, D), F32), jax.ShapeDtypeStruct((S, D), BF16),
                   jax.ShapeDtypeStruct((1, 128), F32)],
        scratch_shapes=[pltpu.VMEM((1, D), F32)],
        compiler_params=_cp(("arbitrary",)), name=name)(y, target)


def _adamw_math(w, g, m, v):
    m = ADAM_B1 * m + (1.0 - ADAM_B1) * g
    v = ADAM_B2 * v + (1.0 - ADAM_B2) * (g * g)
    m_hat = m / (1.0 - ADAM_B1 ** ADAM_STEP)
    v_hat = v / (1.0 - ADAM_B2 ** ADAM_STEP)
    delta = -ADAM_LR * (m_hat / (jnp.sqrt(v_hat) + ADAM_EPS) + ADAM_WD * w)
    return delta, m, v


def _adamw(w, m, v, parts, *, name):
    R, C = w.shape
    tr = 128 if R % 128 == 0 else R
    n_parts = len(parts)

    def body(*refs):
        w_ref, m_ref, v_ref = refs[:3]
        p_refs = refs[3:3 + n_parts]
        g_ref, d_ref, mo_ref, vo_ref = refs[3 + n_parts:]
        g = p_refs[0][...]
        for p in p_refs[1:]:
            g = g + p[...]
        delta, mn, vn = _adamw_math(w_ref[...], g, m_ref[...], v_ref[...])
        g_ref[...] = g
        d_ref[...] = delta
        mo_ref[...] = mn
        vo_ref[...] = vn

    blk = pl.BlockSpec((tr, C), lambda i: (i, 0))
    return pl.pallas_call(
        body, grid=(R // tr,), in_specs=[blk] * (3 + n_parts), out_specs=[blk] * 4,
        out_shape=[jax.ShapeDtypeStruct((R, C), F32)] * 4,
        compiler_params=_cp(("parallel",)), name=name)(w, m, v, *parts)


def _sum4(g_stack, recv, me, *, name):
    _, R, C = g_stack.shape
    tr = 128 if R % 128 == 0 else R

    def body(me_ref, g_ref, r_ref, o_ref):
        acc = g_ref[...]
        for j in range(N_CHIPS - 1):
            acc = acc + r_ref[j].astype(F32)
        o_ref[...] = acc

    grid_spec = pltpu.PrefetchScalarGridSpec(
        num_scalar_prefetch=1, grid=(R // tr,),
        in_specs=[pl.BlockSpec((None, tr, C), lambda i, me_ref: (me_ref[0], i, 0)),
                  pl.BlockSpec((N_CHIPS - 1, tr, C), lambda i, me_ref: (0, i, 0))],
        out_specs=pl.BlockSpec((tr, C), lambda i, me_ref: (i, 0)))
    return pl.pallas_call(
        body, grid_spec=grid_spec, out_shape=jax.ShapeDtypeStruct((R, C), F32),
        compiler_params=_cp(("parallel",)), name=name)(me, g_stack, recv)


def _sum8(gathered, *, name):
    _, R, C = gathered.shape

    def body(g_ref, o_ref):
        acc = g_ref[0]
        for d in range(1, N_DEV):
            acc = acc + g_ref[d]
        o_ref[...] = acc

    return pl.pallas_call(
        body, grid=(1,), in_specs=[pl.BlockSpec((N_DEV, R, C), lambda i: (0, 0, 0))],
        out_specs=pl.BlockSpec((R, C), lambda i: (0, 0)),
        out_shape=jax.ShapeDtypeStruct((R, C), F32),
        compiler_params=_cp(("arbitrary",)), name=name)(gathered)


def _place():
    return lax.axis_index("x"), lax.axis_index("y"), lax.axis_index("c")


def _other_chips(x, y):
    return [(1 - x, y), (x, 1 - y), (1 - x, 1 - y)]


_ANY = pl.BlockSpec(memory_space=pl.ANY)


_HBM = pl.BlockSpec(memory_space=pltpu.HBM)
_SEM = pl.BlockSpec(memory_space=pltpu.SEMAPHORE)
_EFFECT = pltpu.SideEffectType.DATAFLOW_SIDE_EFFECTING


def _gather_copies(srcs, lands, send_sems, recv_sems):
    x, y, c = _place()
    me = 2 * x + y
    return [pltpu.make_async_remote_copy(
        src_ref=srcs[i], dst_ref=lands[i].at[me], send_sem=send_sems.at[3 * i + j],
        recv_sem=recv_sems.at[3 * i + j], device_id=(px, py, c), device_id_type=MESH)
        for i in range(len(srcs)) for j, (px, py) in enumerate(_other_chips(x, y))]


def _scatter_copies(srcs, lands, send_sems, recv_sems):
    x, y, c = _place()
    return [pltpu.make_async_remote_copy(
        src_ref=srcs[i].at[2 * px + py], dst_ref=lands[i].at[j], send_sem=send_sems.at[3 * i + j],
        recv_sem=recv_sems.at[3 * i + j], device_id=(px, py, c), device_id_type=MESH)
        for i in range(len(srcs)) for j, (px, py) in enumerate(_other_chips(x, y))]


def _copies_start(srcs, lands, make_copies, thru=(), *, name):
    n = len(srcs)
    n_ops = 2 * n + len(thru)

    def body(*refs):
        send_sems, recv_sems = refs[n_ops], refs[n_ops + 1]
        for cp in make_copies(refs[:n], refs[n:2 * n], send_sems, recv_sems):
            cp.start()
        refs[-1][...] = jnp.zeros_like(refs[-1])

    ops = list(srcs) + list(lands) + list(thru)
    outs = pl.pallas_call(
        body, name=name,
        out_shape=(pltpu.SemaphoreType.DMA((3 * n,)), pltpu.SemaphoreType.DMA((3 * n,)),
                   *[pltpu.HBM(a.shape, a.dtype) for a in ops], jax.ShapeDtypeStruct((8, 128), F32)),
        in_specs=[_HBM] * n_ops,
        out_specs=(_SEM, _SEM, *[_HBM] * n_ops, pl.BlockSpec(memory_space=pltpu.VMEM)),
        input_output_aliases={i: 2 + i for i in range(n_ops)},
        compiler_params=pltpu.CompilerParams(has_side_effects=_EFFECT),
    )(*[pltpu.with_memory_space_constraint(a, pltpu.HBM) for a in ops])
    return outs[0], outs[1], list(outs[2:2 + n]), list(outs[2 + n:2 + 2 * n]), outs[-1], list(outs[2 + 2 * n:-1])


def _copies_wait(handle, after, make_copies, *, name):
    send_sems, recv_sems, srcs, lands = handle[:4]
    n = len(srcs)

    def body(*refs):
        for cp in make_copies(refs[:n], refs[n:2 * n], refs[2 * n], refs[2 * n + 1]):
            cp.wait_send()
            cp.wait_recv()

    ops = list(srcs) + list(lands)
    outs = pl.pallas_call(
        body, name=name,
        out_shape=tuple(pltpu.HBM(a.shape, a.dtype) for a in ops),
        in_specs=[_HBM] * (2 * n) + [_SEM, _SEM, _ANY],
        out_specs=tuple([_HBM] * (2 * n)),
        input_output_aliases={i: i for i in range(2 * n)},
        compiler_params=pltpu.CompilerParams(has_side_effects=_EFFECT),
    )(*ops, send_sems, recv_sems, after)
    return list(outs[n:])


def _swap_with_sibling(arrs, *, name):
    n = len(arrs)

    def body(*refs):
        ins, outs = refs[:n], refs[n:2 * n]
        send_sems, recv_sems = refs[2 * n:]
        x, y, c = _place()
        remote = []
        for i in range(n):
            rc = pltpu.make_async_remote_copy(
                src_ref=ins[i], dst_ref=outs[i], send_sem=send_sems.at[i], recv_sem=recv_sems.at[i],
                device_id=(x, y, 1 - c), device_id_type=MESH)
            rc.start()
            remote.append(rc)
        for rc in remote:
            rc.wait_send()
        for rc in remote:
            rc.wait_recv()

    return pl.pallas_call(
        body, in_specs=[_ANY] * n, out_specs=[_ANY] * n,
        out_shape=[jax.ShapeDtypeStruct(a.shape, a.dtype) for a in arrs],
        scratch_shapes=[pltpu.SemaphoreType.DMA((n,)), pltpu.SemaphoreType.DMA((n,))],
        name=name)(*arrs)


def _gather_all(buf, *, name):
    R, C = buf.shape

    def body(in_ref, out_ref, send_sems, recv_sems, local_sem):
        x, y, c = _place()
        me = 4 * x + 2 * y + c
        lc = pltpu.make_async_copy(in_ref, out_ref.at[me], local_sem)
        lc.start()
        remote = []
        for k in range(1, N_DEV):
            px = 1 - x if (k >> 2) & 1 else x
            py = 1 - y if (k >> 1) & 1 else y
            pc = 1 - c if k & 1 else c
            rc = pltpu.make_async_remote_copy(
                src_ref=in_ref, dst_ref=out_ref.at[me], send_sem=send_sems.at[k - 1],
                recv_sem=recv_sems.at[k - 1], device_id=(px, py, pc), device_id_type=MESH)
            rc.start()
            remote.append(rc)
        lc.wait()
        for rc in remote:
            rc.wait_send()
        for rc in remote:
            rc.wait_recv()

    return pl.pallas_call(
        body, in_specs=[_ANY], out_specs=_ANY,
        out_shape=jax.ShapeDtypeStruct((N_DEV, R, C), buf.dtype),
        scratch_shapes=[pltpu.SemaphoreType.DMA((N_DEV - 1,)), pltpu.SemaphoreType.DMA((N_DEV - 1,)),
                        pltpu.SemaphoreType.DMA],
        name=name)(buf)


def _w_in_to_z(w):
    pad = jnp.zeros(w.shape[:-1] + (64,), w.dtype)
    return jnp.concatenate([w[..., 0:512], w[..., 512:1024], w[..., 1344:1856], w[..., 1024:1280],
                            w[..., 1280:1344], pad], axis=-1)


def _z_to_w_in(g):
    return jnp.concatenate([g[..., 0:512], g[..., 512:1024], g[..., 1536:1792], g[..., 1792:1856],
                            g[..., 1024:1536]], axis=-1)


def _pad_heads(w, nh):
    w = w.reshape(w.shape[:-1] + (nh, QK))
    w = jnp.concatenate([w, jnp.zeros(w.shape[:-1] + (HEAD_PAD - QK,), w.dtype)], axis=-1)
    return w.reshape(w.shape[:-2] + (nh * HEAD_PAD,))


def _unpad_heads(g, nh):
    g = g.reshape(g.shape[:-1] + (nh, HEAD_PAD))[..., :QK]
    return g.reshape(g.shape[:-2] + (nh * QK,))


def _kv_split(w, nh):
    w = w.reshape(w.shape[:-1] + (nh, 2, 128))
    return jnp.swapaxes(w, -3, -2).reshape(w.shape[:-3] + (nh * 256,))


def _kv_join(g, nh):
    g = g.reshape(g.shape[:-1] + (2, nh, 128))
    return jnp.swapaxes(g, -3, -2).reshape(g.shape[:-3] + (nh * 256,))


def _pad_gain(g):
    return jnp.concatenate([g, jnp.zeros((1, HEAD_PAD - QK), g.dtype)], axis=1)


_SMALL = ("g_mix", "g_q_lat", "g_kv_lat", "g_q_mla", "g_k_mla", "w_pool", "pool_scale", "g_mem", "g_q_x",
          "g_k_x", "g_ffn", "conv_b", "conv_w")


def _pack(arrs, extra=0):
    flat = jnp.concatenate([a.reshape(-1) for a in arrs])
    n = flat.shape[0] + extra
    rows = -(-n // 1024) * 8
    return jnp.pad(flat, (0, rows * 128 - flat.shape[0])).reshape(rows, 128)


def _unpack(buf, shapes):
    flat = buf.reshape(-1)
    out, off = [], 0
    for s in shapes:
        n = int(np.prod(s))
        out.append(flat[off:off + n].reshape(s))
        off += n
    return out, off


def _tied(a, token):
    return a + token[:1, :1].astype(a.dtype)


def _local_step(x, mem, target, W, fetch=None, ship=None):
    fetch = fetch or (lambda group, after: None)
    ship = ship or (lambda group, G: jnp.zeros((8, 128), F32))
    S, D = x.shape
    F = W["conv_b"].shape[1]
    tabs = _rope_tables(S)
    tm = 512 if S % 512 == 0 else 128
    tl = 1024 if S % 1024 == 0 else tm
    tk = _pick(S, (1024, 512, 128))

    h = _rms_fwd(x, W["g_mix"], C=D, name="norm_mix")
    fetch("g1", h)
    z = mm_nn(h, W["w_in"], tm=tl, tn=Z_COLS, tk=D, name="z_proj")
    fetch("g2", z)
    y_pool = _pool_fwd(z, W["w_pool"], W["pool_scale"], name="pool_fwd")
    ql = _rms_fwd(z, W["g_q_lat"], C=Q_RANK, cb=Z_Q_CB, name="norm_qlat")
    kvl = _rms_fwd(z, W["g_kv_lat"], C=KV_RANK, cb=Z_KV_CB, name="norm_kvlat")
    qraw = mm_nn(ql, W["w_q_up"], nsh=N_CHIPS, tm=tl, tn=512, tk=Q_RANK, name="q_up")
    kvraw = mm_nn(kvl, W["w_kv_up"], nsh=N_CHIPS, tm=tl, tn=512, tk=KV_RANK, name="kv_up")
    q, k, v, vt = _qkrope_fwd(qraw, kvraw, z, W["g_q_mla"], W["g_k_mla"], tabs, name="qk_norm_rope")
    o, y_mla, lse = _flash_fwd(q, k, vt, name="mla_fwd")
    memn = _rms_fwd(mem, W["g_mem"], C=D, name="norm_mem")
    M = mem.shape[0]
    mkv = mm_nn(memn, W["w_mem_kv"], tm=M, tn=1024, tk=D, name="mem_kv")
    kx, vx = _memk_fwd(mkv, W["g_k_x"], name="memk_fwd")
    y_mem = _memattn_fwd(z, kx, vx, W["g_q_x"], name="memattn_fwd")
    cat = jnp.concatenate([y_pool, y_mla, y_mem], axis=1)
    fetch("g2b", cat)
    x2 = mm_nn(cat, W["w_o"], tm=tm, tn=D, tk=D, add=x, name="o_proj")
    h2 = _rms_fwd(x2, W["g_ffn"], C=D, name="norm_ffn")
    fetch("g3", h2)
    fn = F // N_CHIPS
    g = mm_nn(h2, W["w_gate"], nsh=N_CHIPS, tm=tl, tn=fn, tk=D, out_dtypes=(BF16,), name="gate_proj")
    u = mm_nn(h2, W["w_up"], nsh=N_CHIPS, tm=tl, tn=fn, tk=D, out_dtypes=(BF16,), name="up_proj")
    a = _glu_fwd(g, u, W["conv_w"], W["conv_b"], name="glu_fwd")
    y = mm_nn(a, W["w_down"], tm=tm, tn=1024, tk=F // 2, add=x2, name="down_proj")
    dy, dyb, loss_row = _loss_head(y, target, name="loss_head")

    G = {}
    d_a = mm_nt(dyb, W["w_down"], tm=tl, to=512, tc=D, out_dtypes=(BF16,), name="d_a")
    G["w_down"] = mm_tn(a, dyb, to=fn, tn=1024, tk=tk, out_dtypes=(F32, BF16), name="grad_w_down")
    d_g, d_u, G["conv_w"], G["conv_b"] = _glu_bwd(g, u, d_a, W["conv_w"], W["conv_b"], name="glu_bwd")
    G["w_gate"] = mm_tn(h2, d_g, nsh=N_CHIPS, to=1024, tn=fn, tk=tk, out_dtypes=(F32, BF16), name="grad_w_gate")
    G["w_up"] = mm_tn(h2, d_u, nsh=N_CHIPS, to=1024, tn=fn, tk=tk, out_dtypes=(F32, BF16), name="grad_w_up")
    tok = ship("s1", G)
    d_h2 = mm_nt_pair(d_g, W["w_gate"], d_u, W["w_up"], nsh=N_CHIPS, tm=tm, to=1024, tc=fn, name="d_h2")
    d_x2, d_x2b, G["g_ffn"] = _rms_bwd(x2, d_h2, _tied(W["g_ffn"], tok), C=D, res=dy, out_dtypes=(F32, BF16),
                                       name="norm_ffn_bwd")

    d_cat = mm_nt(d_x2b, W["w_o"], tm=tl, to=1024, tc=D, out_dtypes=(BF16,), name="d_cat")
    G["w_o"] = mm_tn(cat, d_x2b, to=1024, tn=1024, tk=tk, out_dtypes=(F32, BF16), name="grad_w_o")
    tok = ship("s2", G)
    dz_pool, G["w_pool"], G["pool_scale"] = _pool_bwd(z, d_cat, W["w_pool"], _tied(W["pool_scale"], tok),
                                                      name="pool_bwd")
    dz_mq, dkx, dvx, G["g_q_x"] = _memattn_bwd(z, kx, vx, W["g_q_x"], d_cat, name="memattn_bwd")
    d_mkv, G["g_k_x"] = _memk_bwd(mkv, W["g_k_x"], dkx, dvx, name="memk_bwd")
    G["w_mem_kv"] = mm_tn(memn, d_mkv, to=1024, tn=1024, tk=M, out_dtypes=(F32, BF16), name="grad_w_mem_kv")
    d_memn = mm_nt(d_mkv, W["w_mem_kv"], tm=M, to=D, tc=1024, name="d_memn")
    _, G["g_mem"] = _rms_bwd(mem, d_memn, W["g_mem"], C=D, name="norm_mem_bwd")
    delta = _attn_bwd_prep(o, d_cat, name="mla_bwd_prep")
    dq, dk, dv = _flash_bwd(q, k, v, d_cat, lse, delta, name="mla_bwd")
    d_qraw, d_kvraw, dz_kr, G["g_q_mla"], G["g_k_mla"] = _qkrope_bwd(
        qraw, kvraw, z, W["g_q_mla"], W["g_k_mla"], tabs, dq, dk, dv, name="qk_norm_rope_bwd")
    G["w_q_up"] = mm_tn(ql, d_qraw, nsh=N_CHIPS, to=Q_RANK, tn=512, tk=tk, out_dtypes=(F32, BF16), name="grad_w_q_up")
    d_ql = mm_nt(d_qraw, W["w_q_up"], nsh=N_CHIPS, tm=tl, to=Q_RANK, tc=512, name="d_ql")
    G["w_kv_up"] = mm_tn(kvl, d_kvraw, nsh=N_CHIPS, to=KV_RANK, tn=512, tk=tk, out_dtypes=(F32, BF16),
                         name="grad_w_kv_up")
    d_kvl = mm_nt(d_kvraw, W["w_kv_up"], nsh=N_CHIPS, tm=tl, to=KV_RANK, tc=512, name="d_kvl")
    dz_q, G["g_q_lat"] = _rms_bwd(z, d_ql, W["g_q_lat"], C=Q_RANK, cb=Z_Q_CB, out_dtypes=(BF16,), name="norm_qlat_bwd")
    dz_kv, G["g_kv_lat"] = _rms_bwd(z, d_kvl, W["g_kv_lat"], C=KV_RANK, cb=Z_KV_CB, out_dtypes=(BF16,),
                                    name="norm_kvlat_bwd")
    d_z = jnp.concatenate([dz_pool, dz_q, dz_mq, dz_kv, dz_kr], axis=1)
    G["w_in"] = mm_tn(h, d_z, to=512, tn=Z_COLS, tk=tk, out_dtypes=(F32, BF16), name="grad_w_in")
    tok = ship("s3", G)
    d_h = mm_nt(d_z, W["w_in"], tm=tl, to=1024, tc=Z_COLS, name="d_h")
    grad_x, G["g_mix"] = _rms_bwd(x, d_h, _tied(W["g_mix"], tok), C=D, res=d_x2, name="norm_mix_bwd")
    return loss_row, grad_x, G


_BIG = ("w_in", "w_q_up", "w_kv_up", "w_mem_kv", "w_o", "w_gate", "w_up", "w_down")
_WEIGHTS = ("g_mix", "w_in", "g_q_lat", "w_q_up", "g_kv_lat", "w_kv_up", "g_q_mla", "g_k_mla", "w_pool",
            "pool_scale", "g_mem", "w_mem_kv", "g_q_x", "g_k_x", "w_o", "g_ffn", "w_gate", "w_up", "conv_w",
            "conv_b", "w_down")


def _to_compute_layout(name, w):
    if name == "w_in":
        return _w_in_to_z(w)
    if name == "w_q_up":
        return _pad_heads(w, w.shape[-1] // QK)
    if name == "w_kv_up":
        return _kv_split(w, w.shape[-1] // 256)
    return w


def _from_compute_layout(name, g):
    if name == "w_in":
        return _z_to_w_in(g)
    if name == "w_q_up":
        return _unpad_heads(g, g.shape[-1] // HEAD_PAD)
    if name == "w_kv_up":
        return _kv_join(g, g.shape[-1] // 256)
    return g


def kernel(x, mem, g_mix, w_in, g_q_lat, w_q_up, g_kv_lat, w_kv_up, g_q_mla, g_k_mla, w_pool, pool_scale, g_mem, w_mem_kv, g_q_x, g_k_x, w_o, g_ffn, w_gate, w_up, conv_w, conv_b, w_down, loss_target, m_g_mix, m_w_in, m_g_q_lat, m_w_q_up, m_g_kv_lat, m_w_kv_up, m_g_q_mla, m_g_k_mla, m_w_pool, m_pool_scale, m_g_mem, m_w_mem_kv, m_g_q_x, m_g_k_x, m_w_o, m_g_ffn, m_w_gate, m_w_up, m_conv_w, m_conv_b, m_w_down, v_g_mix, v_w_in, v_g_q_lat, v_w_q_up, v_g_kv_lat, v_w_kv_up, v_g_q_mla, v_g_k_mla, v_w_pool, v_pool_scale, v_g_mem, v_w_mem_kv, v_g_q_x, v_g_k_x, v_w_o, v_g_ffn, v_w_gate, v_w_up, v_conv_w, v_conv_b, v_w_down):
    P = dict(g_mix=g_mix, w_in=w_in, g_q_lat=g_q_lat, w_q_up=w_q_up, g_kv_lat=g_kv_lat, w_kv_up=w_kv_up,
             g_q_mla=g_q_mla, g_k_mla=g_k_mla, w_pool=w_pool, pool_scale=pool_scale, g_mem=g_mem,
             w_mem_kv=w_mem_kv, g_q_x=g_q_x, g_k_x=g_k_x, w_o=w_o, g_ffn=g_ffn, w_gate=w_gate, w_up=w_up,
             conv_w=conv_w, conv_b=conv_b, w_down=w_down)
    Mo = dict(g_mix=m_g_mix, w_in=m_w_in, g_q_lat=m_g_q_lat, w_q_up=m_w_q_up, g_kv_lat=m_g_kv_lat,
              w_kv_up=m_w_kv_up, g_q_mla=m_g_q_mla, g_k_mla=m_g_k_mla, w_pool=m_w_pool,
              pool_scale=m_pool_scale, g_mem=m_g_mem, w_mem_kv=m_w_mem_kv, g_q_x=m_g_q_x, g_k_x=m_g_k_x,
              w_o=m_w_o, g_ffn=m_g_ffn, w_gate=m_w_gate, w_up=m_w_up, conv_w=m_conv_w, conv_b=m_conv_b,
              w_down=m_w_down)
    Vo = dict(g_mix=v_g_mix, w_in=v_w_in, g_q_lat=v_g_q_lat, w_q_up=v_w_q_up, g_kv_lat=v_g_kv_lat,
              w_kv_up=v_w_kv_up, g_q_mla=v_g_q_mla, g_k_mla=v_g_k_mla, w_pool=v_w_pool,
              pool_scale=v_pool_scale, g_mem=v_g_mem, w_mem_kv=v_w_mem_kv, g_q_x=v_g_q_x, g_k_x=v_g_k_x,
              w_o=v_w_o, g_ffn=v_g_ffn, w_gate=v_w_gate, w_up=v_w_up, conv_w=v_conv_w, conv_b=v_conv_b,
              w_down=v_w_down)
    xi, yi, ci = _place()
    me = (2 * xi + yi).astype(jnp.int32).reshape(1)

    shard = {n: _to_compute_layout(n, P[n][0]).astype(BF16) for n in _BIG}
    shard["conv_w"] = conv_w[0]
    gather_groups = {"g1": ("w_in",), "g2": ("w_q_up", "w_kv_up", "w_mem_kv"), "g2b": ("w_o",),
                     "g3": ("w_gate", "w_up", "w_down", "conv_w")}
    gathers = {}

    def landing(n):
        s = shard[n]
        return lax.dynamic_update_slice(lax.empty((N_CHIPS,) + s.shape, s.dtype), s[None], (me[0], 0, 0))

    def start_gather(grp, thru=()):
        names = gather_groups[grp]
        gathers[grp] = _copies_start([shard[n] for n in names], [landing(n) for n in names], _gather_copies, thru,
                                     name="gather_start_" + grp)
        return gathers[grp][5]

    start_gather("g1")
    W = {}
    W["g_q_mla"], W["g_k_mla"] = _pad_gain(g_q_mla), _pad_gain(g_k_mla)
    W["w_pool"] = w_pool[0].astype(BF16)
    for n in ("g_mix", "g_q_lat", "g_kv_lat", "pool_scale", "g_mem", "g_q_x", "g_k_x", "g_ffn", "conv_b"):
        W[n] = P[n]
    W["g_mix"] = _tied(W["g_mix"], gathers["g1"][4])

    def fetch(grp, after):
        stacks = _copies_wait(gathers[grp], after, _gather_copies, name="gather_wait_" + grp)
        if grp == "g1":
            stacks = start_gather("g3", start_gather("g2b", start_gather("g2", stacks)))
        for n, s in zip(gather_groups[grp], stacks):
            if n == "conv_w":
                W[n] = jnp.swapaxes(s, 0, 1).reshape(3, -1)
            else:
                W[n] = s.reshape(-1, s.shape[-1])

    shard_shape = {n: shard[n].shape for n in _BIG}
    scatter_groups = {"s1": ("w_down", "w_gate", "w_up"), "s2": ("w_o",),
                      "s3": ("w_mem_kv", "w_q_up", "w_kv_up", "w_in")}
    scatters = {}

    def parts_of(n, g):
        return g.reshape((N_CHIPS,) + shard_shape[n])

    def ship(grp, G):
        names = scatter_groups[grp]
        srcs = [parts_of(n, G[n][1]) for n in names]
        lands = [lax.empty((N_CHIPS - 1,) + shard_shape[n], BF16) for n in names]
        scatters[grp] = _copies_start(srcs, lands, _scatter_copies, name="scatter_start_" + grp)
        return scatters[grp][4]

    loss_row, grad_x, G = _local_step(x[0], mem[0], loss_target[0], W, fetch, ship)

    recv = {}
    for grp, names in scatter_groups.items():
        for n, r in zip(names, _copies_wait(scatters[grp], grad_x, _scatter_copies, name="scatter_wait_" + grp)):
            recv[n] = r
    part = [_sum4(parts_of(n, G[n][0]), recv[n], me, name="sum4_" + n) for n in _BIG]
    part = [_from_compute_layout(n, p) for n, p in zip(_BIG, part)]
    sib = _swap_with_sibling(part, name="swap_grads")
    out = {}
    for n, p, s in zip(_BIG, part, sib):
        out[n] = [r[None] for r in _adamw(P[n][0], Mo[n][0], Vo[n][0], [p, s], name="adamw_" + n)]

    conv_w_full_grad = G["conv_w"]
    small_g = [G["g_mix"], G["g_q_lat"], G["g_kv_lat"], G["g_q_mla"][:, :QK], G["g_k_mla"][:, :QK], G["w_pool"],
               G["pool_scale"], G["g_mem"], G["g_q_x"], G["g_k_x"], G["g_ffn"], G["conv_b"], conv_w_full_grad]
    packed = _pack(small_g + [loss_row[:, :1]])
    total = _sum8(_gather_all(packed, name="gather_small"), name="sum_small")
    shapes = [a.shape for a in small_g] + [(1, 1)]
    (parts, _) = _unpack(total, shapes)
    loss = parts[-1].reshape(())
    F = conv_b.shape[1]
    fn = F // N_CHIPS
    col0 = (2 * xi + yi) * fn
    sg = dict(zip(_SMALL, parts[:-1]))
    sg["conv_w"] = lax.dynamic_slice(sg["conv_w"], (0, col0), (3, fn))
    sw = [P[n].reshape(sg[n].shape) for n in _SMALL]
    sm = [Mo[n].reshape(sg[n].shape) for n in _SMALL]
    sv = [Vo[n].reshape(sg[n].shape) for n in _SMALL]
    gp = _pack([sg[n] for n in _SMALL])
    res = _adamw(_pack(sw), _pack(sm), _pack(sv), [gp], name="adamw_small")
    sshapes = [sg[n].shape for n in _SMALL]
    for kind, buf in zip(range(4), res):
        vals, _ = _unpack(buf, sshapes)
        for n, val in zip(_SMALL, vals):
            out.setdefault(n, [None] * 4)[kind] = val.reshape(P[n].shape)

    return (loss, grad_x[None], *[out[n][0] for n in _WEIGHTS], *[out[n][1] for n in _WEIGHTS],
            *[out[n][2] for n in _WEIGHTS], *[out[n][3] for n in _WEIGHTS])
```

```python
import functools
import math

import numpy as np
import jax
import jax.numpy as jnp
from jax import lax
from jax.experimental import pallas as pl
from jax.experimental.pallas import tpu as pltpu

F32, BF16 = jnp.float32, jnp.bfloat16
NORM_EPS = 1e-6
ROPE_THETA = 10000.0
V7X_VMEM_LIMIT_BYTES = 48 * 1024 * 1024
N_CHIPS = 4
N_DEV = 8

POOL_W = 512
POOL_WINDOWS = (2, 4, 8, 16)
HEADS = 8
NOPE, ROPE, QK = 128, 64, 192
HEAD_PAD = 256
Q_RANK, KV_RANK = 512, 256
X_HEADS, X_DIM = 4, 128
Z_COLS = 1920
Z_POOL_CB, Z_Q_CB, Z_MQ_CB = 0, 1, 2
Z_KV_CB = 6
Z_KR_CB = 14

ADAM_LR, ADAM_B1, ADAM_B2, ADAM_EPS, ADAM_WD, ADAM_STEP = 0.001, 0.9, 0.999, 1e-08, 0.01, 10

MESH = pl.DeviceIdType.MESH


def _cp(sem):
    return pltpu.CompilerParams(dimension_semantics=sem, vmem_limit_bytes=V7X_VMEM_LIMIT_BYTES)


def _row_tile(S):
    return 256 if S % 256 == 0 and S >= 2048 else 128


def _big_row_tile(S):
    return 512 if S % 512 == 0 and S >= 2048 else _row_tile(S)


def _pick(dim, prefs):
    for p in prefs:
        if dim % p == 0:
            return p
    return dim


_DN = {"nn": (((1,), (0,)), ((), ())), "nt": (((1,), (1,)), ((), ())), "tn": (((0,), (0,)), ((), ()))}


def _mm(a, b, *, mode, grid, blocks, maps, out_shape, out_dtypes, add=None, name):
    nk = grid[2]
    dn = _DN[mode]
    n_out = len(out_dtypes)

    def body(*refs):
        a_ref, b_ref = refs[0], refs[1]
        add_ref = refs[2] if add is not None else None
        p = 2 + (add is not None)
        o_refs = refs[p:p + n_out]

        def finish(r):
            if add_ref is not None:
                r = r + add_ref[...]
            for o in o_refs:
                o[...] = r.astype(o.dtype)

        def product():
            return lax.dot_general(a_ref[...].astype(BF16), b_ref[...].astype(BF16), dn, preferred_element_type=F32)

        if nk == 1:
            finish(product())
            return
        acc = refs[p + n_out]
        k = pl.program_id(2)

        @pl.when(k == 0)
        def _():
            acc[...] = jnp.zeros_like(acc)

        acc[...] += product()

        @pl.when(k == nk - 1)
        def _():
            finish(acc[...])

    a_blk, b_blk, o_blk = blocks
    a_map, b_map, o_map = maps
    in_specs = [pl.BlockSpec(a_blk, a_map), pl.BlockSpec(b_blk, b_map)]
    args = [a, b]
    if add is not None:
        in_specs.append(pl.BlockSpec(o_blk, o_map))
        args.append(add)
    outs = pl.pallas_call(
        body, grid=grid, in_specs=in_specs,
        out_specs=[pl.BlockSpec(o_blk, o_map) for _ in out_dtypes],
        out_shape=[jax.ShapeDtypeStruct(out_shape, d) for d in out_dtypes],
        scratch_shapes=[pltpu.VMEM(o_blk, F32)] if nk > 1 else [],
        compiler_params=_cp(("parallel", "parallel", "arbitrary")), name=name)(*args)
    return outs[0] if n_out == 1 else outs


def mm_nn(a, w, *, nsh=1, tm, tn, tk, out_dtypes=(F32,), add=None, name):
    M, K = a.shape
    n = w.shape[1]
    N = nsh * n
    assert w.shape[0] == nsh * K and n % tn == 0 and K % tk == 0 and M % tm == 0
    npt, kt = n // tn, K // tk
    return _mm(a, w, mode="nn", grid=(M // tm, N // tn, kt),
               blocks=((tm, tk), (tk, tn), (tm, tn)),
               maps=(lambda i, j, k: (i, k), lambda i, j, k: ((j // npt) * kt + k, j % npt),
                     lambda i, j, k: (i, j)),
               out_shape=(M, N), out_dtypes=out_dtypes, add=add, name=name)


def mm_nt(d, w, *, nsh=1, tm, to, tc, out_dtypes=(F32,), add=None, name):
    M, N = d.shape
    n = w.shape[1]
    K = w.shape[0] // nsh
    assert nsh * n == N and n % tc == 0 and K % to == 0 and M % tm == 0
    cpt, ot = n // tc, K // to
    return _mm(d, w, mode="nt", grid=(M // tm, ot, N // tc),
               blocks=((tm, tc), (to, tc), (tm, to)),
               maps=(lambda i, j, c: (i, c), lambda i, j, c: ((c // cpt) * ot + j, c % cpt),
                     lambda i, j, c: (i, j)),
               out_shape=(M, K), out_dtypes=out_dtypes, add=add, name=name)


def mm_nt_pair(d1, w1, d2, w2, *, nsh, tm, to, tc, out_dtype, name):
    M, N = d1.shape
    n = w1.shape[1]
    K = w1.shape[0] // nsh
    assert d2.shape == d1.shape and w2.shape == w1.shape and nsh * n == N
    assert n % tc == 0 and K % to == 0 and M % tm == 0
    cpt, ot, nk = n // tc, K // to, N // tc

    def body(a1_ref, b1_ref, a2_ref, b2_ref, o_ref, acc):
        k = pl.program_id(2)

        @pl.when(k == 0)
        def _():
            acc[...] = jnp.zeros_like(acc)

        acc[...] += lax.dot_general(a1_ref[...], b1_ref[...], _DN["nt"], preferred_element_type=F32)
        acc[...] += lax.dot_general(a2_ref[...], b2_ref[...], _DN["nt"], preferred_element_type=F32)

        @pl.when(k == nk - 1)
        def _():
            o_ref[...] = acc[...].astype(o_ref.dtype)

    a_spec = pl.BlockSpec((tm, tc), lambda i, j, c: (i, c))
    b_spec = pl.BlockSpec((to, tc), lambda i, j, c: ((c // cpt) * ot + j, c % cpt))
    return pl.pallas_call(
        body, grid=(M // tm, ot, nk), in_specs=[a_spec, b_spec, a_spec, b_spec],
        out_specs=pl.BlockSpec((tm, to), lambda i, j, c: (i, j)),
        out_shape=jax.ShapeDtypeStruct((M, K), out_dtype),
        scratch_shapes=[pltpu.VMEM((tm, to), F32)],
        compiler_params=_cp(("parallel", "parallel", "arbitrary")), name=name)(d1, w1, d2, w2)


def mm_tn(x, d, *, nsh=1, to, tn, tk, out_dtypes=(F32,), name):
    M, K = x.shape
    N = d.shape[1]
    n = N // nsh
    assert n % tn == 0 and K % to == 0 and M % tk == 0
    npt, ot = n // tn, K // to
    return _mm(x, d, mode="tn", grid=(ot, N // tn, M // tk),
               blocks=((tk, to), (tk, tn), (to, tn)),
               maps=(lambda i, j, k: (k, i), lambda i, j, k: (k, j),
                     lambda i, j, k: ((j // npt) * ot + i, j % npt)),
               out_shape=(nsh * K, n), out_dtypes=out_dtypes, name=name)


def _rms_fwd(x, g, *, C, cb=0, name):
    S = x.shape[0]
    tm = _big_row_tile(S) if S >= 128 else S

    def body(x_ref, g_ref, o_ref):
        xv = x_ref[...]
        r = lax.rsqrt(jnp.mean(xv * xv, axis=-1, keepdims=True) + NORM_EPS)
        o_ref[...] = ((xv * r) * g_ref[...]).astype(o_ref.dtype)

    return pl.pallas_call(
        body, grid=(S // tm,),
        in_specs=[pl.BlockSpec((tm, C), lambda i: (i, cb)), pl.BlockSpec((1, C), lambda i: (0, 0))],
        out_specs=pl.BlockSpec((tm, C), lambda i: (i, 0)),
        out_shape=jax.ShapeDtypeStruct((S, C), BF16),
        compiler_params=_cp(("parallel",)), name=name)(x, g)


def _rms_bwd(x, dh, g, *, C, cb=0, res=None, out_dtypes=(F32,), name):
    S = x.shape[0]
    tm = (_big_row_tile(S) if C <= 512 else _row_tile(S)) if S >= 128 else S
    n_out = len(out_dtypes)

    def body(*refs):
        x_ref, dh_ref, g_ref = refs[:3]
        res_ref = refs[3] if res is not None else None
        p = 3 + (res is not None)
        outs = refs[p:p + n_out]
        dg_ref = refs[p + n_out]
        i = pl.program_id(0)
        xv = x_ref[...]
        r = lax.rsqrt(jnp.mean(xv * xv, axis=-1, keepdims=True) + NORM_EPS)
        n = xv * r
        dhv = dh_ref[...].astype(F32)
        dn = dhv * g_ref[...]
        c = jnp.mean(dn * n, axis=-1, keepdims=True)
        dx = r * (dn - n * c)
        if res_ref is not None:
            dx = res_ref[...] + dx
        for o in outs:
            o[...] = dx.astype(o.dtype)

        @pl.when(i == 0)
        def _():
            dg_ref[...] = jnp.zeros_like(dg_ref)

        dg_ref[...] += jnp.sum(dhv * n, axis=0, keepdims=True)

    row = pl.BlockSpec((tm, C), lambda i: (i, 0))
    in_specs = [pl.BlockSpec((tm, C), lambda i: (i, cb)), row, pl.BlockSpec((1, C), lambda i: (0, 0))]
    args = [x, dh, g]
    if res is not None:
        in_specs.append(row)
        args.append(res)
    return pl.pallas_call(
        body, grid=(S // tm,), in_specs=in_specs,
        out_specs=[row] * n_out + [pl.BlockSpec((1, C), lambda i: (0, 0))],
        out_shape=[jax.ShapeDtypeStruct((S, C), d) for d in out_dtypes] + [jax.ShapeDtypeStruct((1, C), F32)],
        compiler_params=_cp(("arbitrary",)), name=name)(*args)


def _pool_cnt(t0, rows, w):
    t = t0 + lax.broadcasted_iota(jnp.int32, (rows, 1), 0)
    return jnp.minimum(t + 1, w).astype(F32)


def _pool_d(halo, tile, gi, t0, tm):
    s = jnp.concatenate([halo, tile], axis=0)
    for step in (1, 2, 4, 8)[:gi + 1]:
        s = s + pltpu.roll(s, step, 0)
    return s[16:] / _pool_cnt(t0, tm, POOL_WINDOWS[gi]) - tile


def _pool_fwd(z, w_pool, pool_scale, *, name):
    S = z.shape[0]
    tm = _row_tile(S)
    hb = tm // 16

    def body(z_ref, h_ref, w_ref, sc_ref, o_ref):
        i = pl.program_id(0)
        halo = h_ref[...] * (i > 0).astype(F32)
        for gi in range(4):
            cs = slice(gi * 128, (gi + 1) * 128)
            d = _pool_d(halo[:, cs], z_ref[:, cs], gi, i * tm, tm)
            yp = jnp.dot(d.astype(BF16), w_ref[gi], preferred_element_type=F32)
            o_ref[:, cs] = (yp * sc_ref[:, cs]).astype(o_ref.dtype)

    return pl.pallas_call(
        body, grid=(S // tm,),
        in_specs=[pl.BlockSpec((tm, POOL_W), lambda i: (i, Z_POOL_CB)),
                  pl.BlockSpec((16, POOL_W), lambda i: (jnp.maximum(i * hb - 1, 0), Z_POOL_CB)),
                  pl.BlockSpec((4, 128, 128), lambda i: (0, 0, 0)),
                  pl.BlockSpec((1, POOL_W), lambda i: (0, 0))],
        out_specs=pl.BlockSpec((tm, POOL_W), lambda i: (i, 0)),
        out_shape=jax.ShapeDtypeStruct((S, POOL_W), BF16),
        compiler_params=_cp(("parallel",)), name=name)(z, z, w_pool, pool_scale)


def _pool_bwd(z, d_cat, w_pool, pool_scale, *, name):
    S = z.shape[0]
    tm = _row_tile(S)
    hb = tm // 16
    nt = S // tm
    E = tm + 16

    def body(z_ref, h_ref, dy_ref, dyn_ref, w_ref, sc_ref, dz_ref, gw_ref, gs_ref):
        i = pl.program_id(0)

        @pl.when(i == 0)
        def _():
            gw_ref[...] = jnp.zeros_like(gw_ref)
            gs_ref[...] = jnp.zeros_like(gs_ref)

        halo = h_ref[...] * (i > 0).astype(F32)
        dy_next = dyn_ref[...].astype(F32) * (i < nt - 1).astype(F32)
        for gi in range(4):
            cs = slice(gi * 128, (gi + 1) * 128)
            w = w_ref[gi]
            d = _pool_d(halo[:, cs], z_ref[:, cs], gi, i * tm, tm)
            db = d.astype(BF16)
            dy = dy_ref[:, cs].astype(F32)
            yp = jnp.dot(db, w, preferred_element_type=F32)
            gs_ref[:, cs] += jnp.sum(dy * yp, axis=0, keepdims=True)
            sc = sc_ref[:, cs]
            dys = (dy * sc).astype(BF16)
            gw_ref[gi] += lax.dot_general(db, dys, _DN["tn"], preferred_element_type=F32)
            dys_ext = jnp.concatenate([dys, (dy_next[:, cs] * sc).astype(BF16)], axis=0)
            dd = lax.dot_general(dys_ext, w, _DN["nt"], preferred_element_type=F32)
            r = dd / _pool_cnt(i * tm, E, POOL_WINDOWS[gi])
            for step in (1, 2, 4, 8)[:gi + 1]:
                r = r + pltpu.roll(r, E - step, 0)
            dz_ref[:, cs] = (r[:tm] - dd[:tm]).astype(dz_ref.dtype)

    return pl.pallas_call(
        body, grid=(nt,),
        in_specs=[pl.BlockSpec((tm, POOL_W), lambda i: (i, Z_POOL_CB)),
                  pl.BlockSpec((16, POOL_W), lambda i: (jnp.maximum(i * hb - 1, 0), Z_POOL_CB)),
                  pl.BlockSpec((tm, POOL_W), lambda i: (i, 0)),
                  pl.BlockSpec((16, POOL_W), lambda i: (jnp.minimum((i + 1) * hb, S // 16 - 1), 0)),
                  pl.BlockSpec((4, 128, 128), lambda i: (0, 0, 0)),
                  pl.BlockSpec((1, POOL_W), lambda i: (0, 0))],
        out_specs=[pl.BlockSpec((tm, POOL_W), lambda i: (i, 0)),
                   pl.BlockSpec((4, 128, 128), lambda i: (0, 0, 0)),
                   pl.BlockSpec((1, POOL_W), lambda i: (0, 0))],
        out_shape=[jax.ShapeDtypeStruct((S, POOL_W), BF16),
                   jax.ShapeDtypeStruct((4, 128, 128), F32),
                   jax.ShapeDtypeStruct((1, POOL_W), F32)],
        compiler_params=_cp(("arbitrary",)), name=name)(z, z, d_cat, d_cat, w_pool, pool_scale)


def _rope_tables(S):
    half = ROPE // 2
    inv_freq = 1.0 / (ROPE_THETA ** (jnp.arange(half, dtype=F32) / half))
    ang = jnp.arange(S).astype(F32)[:, None] * inv_freq[None, :]
    cos, sin = jnp.cos(ang), jnp.sin(ang)
    zero = jnp.zeros((S, half), F32)
    cos_t = jnp.concatenate([cos, cos, zero, zero], axis=1)
    sa_t = jnp.concatenate([-sin, zero, zero, zero], axis=1)
    sb_t = jnp.concatenate([zero, sin, zero, zero], axis=1)
    return cos_t, sa_t, sb_t


def _head_fwd(xn, xr, gn, gr, cos, sa, sb):
    ms = (jnp.sum(xn * xn, axis=-1, keepdims=True) + jnp.sum(xr * xr, axis=-1, keepdims=True)) * (1.0 / QK)
    r = lax.rsqrt(ms + NORM_EPS)
    on = (xn * r) * gn
    yr = (xr * r) * gr
    orr = yr * cos + pltpu.roll(yr, 96, 1) * sa + pltpu.roll(yr, 32, 1) * sb
    return on, orr


def _head_bwd(xn, xr, gn, gr, don, dor, cos, sa, sb):
    ms = (jnp.sum(xn * xn, axis=-1, keepdims=True) + jnp.sum(xr * xr, axis=-1, keepdims=True)) * (1.0 / QK)
    r = lax.rsqrt(ms + NORM_EPS)
    nn, nr = xn * r, xr * r
    dyr = dor * cos + pltpu.roll(dor * sa, 32, 1) + pltpu.roll(dor * sb, 96, 1)
    ggn, ggr = don * nn, dyr * nr
    dnn, dnr = don * gn, dyr * gr
    c = (jnp.sum(dnn * nn, axis=-1, keepdims=True) + jnp.sum(dnr * nr, axis=-1, keepdims=True)) * (1.0 / QK)
    return r * (dnn - nn * c), r * (dnr - nr * c), ggn, ggr


def _kv_cols(h):
    base = (h // 2) * 512 + (h % 2) * 128
    return base, base + 256


def _qkrope_fwd(qraw, kvraw, z, gq, gk, tabs, *, name):
    S = qraw.shape[0]
    tm = _row_tile(S)

    def body(q_ref, kv_ref, zkr_ref, gq_ref, gk_ref, cos_ref, sa_ref, sb_ref, qo_ref, ko_ref, vo_ref, vt_ref):
        rope = (cos_ref[...], sa_ref[...], sb_ref[...])
        zkr = zkr_ref[...]
        gqn, gqr, gkn, gkr = gq_ref[:, :128], gq_ref[:, 128:], gk_ref[:, :128], gk_ref[:, 128:]
        for h in range(HEADS):
            b = h * HEAD_PAD
            on, orr = _head_fwd(q_ref[:, b:b + 128], q_ref[:, b + 128:b + 256], gqn, gqr, *rope)
            qo_ref[:, b:b + 128] = on.astype(BF16)
            qo_ref[:, b + 128:b + 256] = orr.astype(BF16)
            kc, vc = _kv_cols(h)
            on, orr = _head_fwd(kv_ref[:, kc:kc + 128], zkr, gkn, gkr, *rope)
            ko_ref[:, b:b + 128] = on.astype(BF16)
            ko_ref[:, b + 128:b + 256] = orr.astype(BF16)
            vv = kv_ref[:, vc:vc + 128]
            vo_ref[:, h * 128:(h + 1) * 128] = vv.astype(BF16)
            vt_ref[h * 128:(h + 1) * 128, :] = jnp.transpose(vv).astype(BF16)

    W = HEADS * HEAD_PAD
    row = lambda c: pl.BlockSpec((tm, c), lambda i: (i, 0))
    vec = lambda c: pl.BlockSpec((1, c), lambda i: (0, 0))
    return pl.pallas_call(
        body, grid=(S // tm,),
        in_specs=[row(W), row(W), pl.BlockSpec((tm, 128), lambda i: (i, Z_KR_CB)), vec(256), vec(256),
                  row(128), row(128), row(128)],
        out_specs=[row(W), row(W), row(HEADS * 128), pl.BlockSpec((HEADS * 128, tm), lambda i: (0, i))],
        out_shape=[jax.ShapeDtypeStruct((S, W), BF16), jax.ShapeDtypeStruct((S, W), BF16),
                   jax.ShapeDtypeStruct((S, HEADS * 128), BF16), jax.ShapeDtypeStruct((HEADS * 128, S), BF16)],
        compiler_params=_cp(("parallel",)), name=name)(qraw, kvraw, z, gq, gk, *tabs)


def _qkrope_bwd(qraw, kvraw, z, gq, gk, tabs, dq, dk, dv, *, name):
    S = qraw.shape[0]
    tm = _row_tile(S)

    def body(q_ref, kv_ref, zkr_ref, gq_ref, gk_ref, cos_ref, sa_ref, sb_ref, dq_ref, dk_ref, dv_ref,
             dqo_ref, dkvo_ref, dkr_ref, ggq_ref, ggk_ref):
        i = pl.program_id(0)

        @pl.when(i == 0)
        def _():
            ggq_ref[...] = jnp.zeros_like(ggq_ref)
            ggk_ref[...] = jnp.zeros_like(ggk_ref)

        rope = (cos_ref[...], sa_ref[...], sb_ref[...])
        zkr = zkr_ref[...]
        gqn, gqr, gkn, gkr = gq_ref[:, :128], gq_ref[:, 128:], gk_ref[:, :128], gk_ref[:, 128:]
        dkr = jnp.zeros((tm, 128), F32)
        sq_n = jnp.zeros((1, 128), F32)
        sq_r = jnp.zeros((1, 128), F32)
        sk_n = jnp.zeros((1, 128), F32)
        sk_r = jnp.zeros((1, 128), F32)
        for h in range(HEADS):
            b = h * HEAD_PAD
            dxn, dxr, ggn, ggr = _head_bwd(q_ref[:, b:b + 128], q_ref[:, b + 128:b + 256], gqn, gqr,
                                           dq_ref[:, b:b + 128], dq_ref[:, b + 128:b + 256], *rope)
            dqo_ref[:, b:b + 128] = dxn.astype(BF16)
            dqo_ref[:, b + 128:b + 256] = dxr.astype(BF16)
            sq_n += jnp.sum(ggn, axis=0, keepdims=True)
            sq_r += jnp.sum(ggr, axis=0, keepdims=True)
            kc, vc = _kv_cols(h)
            dxn, dxr, ggn, ggr = _head_bwd(kv_ref[:, kc:kc + 128], zkr, gkn, gkr,
                                           dk_ref[:, b:b + 128], dk_ref[:, b + 128:b + 256], *rope)
            dkvo_ref[:, kc:kc + 128] = dxn.astype(BF16)
            dkvo_ref[:, vc:vc + 128] = dv_ref[:, h * 128:(h + 1) * 128].astype(BF16)
            dkr += dxr
            sk_n += jnp.sum(ggn, axis=0, keepdims=True)
            sk_r += jnp.sum(ggr, axis=0, keepdims=True)
        dkr_ref[...] = dkr.astype(BF16)
        ggq_ref[:, :128] += sq_n
        ggq_ref[:, 128:] += sq_r
        ggk_ref[:, :128] += sk_n
        ggk_ref[:, 128:] += sk_r

    W = HEADS * HEAD_PAD
    row = lambda c: pl.BlockSpec((tm, c), lambda i: (i, 0))
    vec = lambda c: pl.BlockSpec((1, c), lambda i: (0, 0))
    return pl.pallas_call(
        body, grid=(S // tm,),
        in_specs=[row(W), row(W), pl.BlockSpec((tm, 128), lambda i: (i, Z_KR_CB)), vec(256), vec(256),
                  row(128), row(128), row(128), row(W), row(W), row(HEADS * 128)],
        out_specs=[row(W), row(W), row(128), vec(256), vec(256)],
        out_shape=[jax.ShapeDtypeStruct((S, W), BF16), jax.ShapeDtypeStruct((S, W), BF16),
                   jax.ShapeDtypeStruct((S, 128), BF16),
                   jax.ShapeDtypeStruct((1, 256), F32), jax.ShapeDtypeStruct((1, 256), F32)],
        compiler_params=_cp(("arbitrary",)), name=name)(qraw, kvraw, z, gq, gk, *tabs, dq, dk, dv)


LOG2E = 1.4426950408889634
SCORE_SCALE = 1.0 / math.sqrt(QK)
SCORE_SCALE_LOG2 = SCORE_SCALE * LOG2E


def _fa_tile(S):
    return 512 if S % 512 == 0 and S >= 2048 else 128


def _flash_fwd(q, k, vt, *, name):
    S = q.shape[0]
    ts = _fa_tile(S)
    tq = 2 * ts

    def body(q_ref, k_ref, vt_ref, o_ref, ob_ref, lse_ref, m_sc, l_sc, acc_sc, s_buf):
        qi = pl.program_id(1)
        m_sc[...] = jnp.full_like(m_sc, -jnp.inf)
        l_sc[...] = jnp.zeros_like(l_sc)
        acc_sc[...] = jnp.zeros_like(acc_sc)
        qb = q_ref[...]

        def scores(kidx):
            k0 = pl.multiple_of(kidx * ts, ts)
            return lax.dot_general(k_ref[pl.ds(k0, ts), :], qb, _DN["nt"], preferred_element_type=F32)

        def causal(st, j):
            key = lax.broadcasted_iota(jnp.int32, (ts, tq), 0) + j * ts
            return jnp.where(key > lax.broadcasted_iota(jnp.int32, (ts, tq), 1), -jnp.inf, st)

        def update(st, kidx):
            k0 = pl.multiple_of(kidx * ts, ts)
            m_prev = m_sc[...]
            m_new = jnp.maximum(m_prev, jnp.max(st, axis=0, keepdims=True))
            alpha = jnp.exp2((m_prev - m_new) * SCORE_SCALE_LOG2)
            pt = jnp.exp2((st - m_new[0:1, :]) * SCORE_SCALE_LOG2)
            l_sc[...] = alpha * l_sc[...] + jnp.sum(pt, axis=0, keepdims=True)
            acc_sc[...] = alpha[0:1, :] * acc_sc[...] + jnp.dot(vt_ref[:, pl.ds(k0, ts)], pt.astype(BF16),
                                                                preferred_element_type=F32)
            m_sc[...] = m_new

        s_buf[0] = scores(0)

        def trip(u, carry):
            s_buf[1] = scores(2 * u + 1)
            update(s_buf[0], 2 * u)
            s_buf[0] = scores(2 * u + 2)
            update(s_buf[1], 2 * u + 1)
            return carry

        lax.fori_loop(0, qi, trip, 0)
        s_buf[1] = scores(2 * qi + 1)
        update(causal(s_buf[0], 0), 2 * qi)
        update(causal(s_buf[1], 1), 2 * qi + 1)
        ot = acc_sc[...] / l_sc[0:1, :]
        o = jnp.transpose(ot)
        o_ref[...] = o
        ob_ref[...] = o.astype(BF16)
        lse_ref[...] = m_sc[...] * SCORE_SCALE_LOG2 + jnp.log2(l_sc[...])

    return pl.pallas_call(
        body, grid=(HEADS, S // tq),
        in_specs=[pl.BlockSpec((tq, HEAD_PAD), lambda h, i: (i, h)),
                  pl.BlockSpec((S, HEAD_PAD), lambda h, i: (0, h)),
                  pl.BlockSpec((128, S), lambda h, i: (h, 0))],
        out_specs=[pl.BlockSpec((tq, 128), lambda h, i: (i, h)),
                   pl.BlockSpec((tq, 128), lambda h, i: (i, h)),
                   pl.BlockSpec((None, 8, tq), lambda h, i: (h, 0, i))],
        out_shape=[jax.ShapeDtypeStruct((S, HEADS * 128), F32), jax.ShapeDtypeStruct((S, HEADS * 128), BF16),
                   jax.ShapeDtypeStruct((HEADS, 8, S), F32)],
        scratch_shapes=[pltpu.VMEM((8, tq), F32), pltpu.VMEM((8, tq), F32), pltpu.VMEM((128, tq), F32),
                        pltpu.VMEM((2, ts, tq), F32)],
        compiler_params=_cp(("parallel", "arbitrary")), name=name)(q, k, vt)


def _attn_bwd_prep(o, d_cat, *, name):
    S = o.shape[0]
    tm = _row_tile(S)
    H = HEADS * 128
    half = H // 2

    def body(o_ref, da_ref, db_ref, delta_ref):
        for h in range(HEADS):
            src, c0 = (da_ref, h * 128) if h * 128 < half else (db_ref, h * 128 - half)
            do = src[:, c0:c0 + 128].astype(F32)
            prod = jnp.transpose(do * o_ref[:, h * 128:(h + 1) * 128])
            delta_ref[h] = jnp.broadcast_to(jnp.sum(prod, axis=0, keepdims=True), (8, tm))

    return pl.pallas_call(
        body, grid=(S // tm,),
        in_specs=[pl.BlockSpec((tm, H), lambda i: (i, 0)),
                  pl.BlockSpec((tm, half), lambda i: (i, 1)), pl.BlockSpec((tm, half), lambda i: (i, 2))],
        out_specs=pl.BlockSpec((HEADS, 8, tm), lambda i: (0, 0, i)),
        out_shape=jax.ShapeDtypeStruct((HEADS, 8, S), F32),
        compiler_params=_cp(("parallel",)), name=name)(o, d_cat, d_cat)


def _flash_bwd(q, k, v, d_cat, lse, delta, *, name):
    S = q.shape[0]
    ts = _fa_tile(S)
    nb = S // ts

    def body(q_ref, do_ref, lse_ref, delta_ref, k_ref, v_ref, dq_ref, dk_ref, dv_ref, dk_sc, dv_sc):
        j = pl.program_id(1)

        @pl.when(j == 0)
        def _():
            dq_ref[...] = jnp.zeros_like(dq_ref)

        dk_sc[...] = jnp.zeros_like(dk_sc)
        dv_sc[...] = jnp.zeros_like(dv_sc)
        kb, vb = k_ref[...], v_ref[...]

        def products(i):
            q0 = pl.multiple_of(i * ts, ts)
            qb = q_ref[pl.ds(q0, ts), :]
            dob_ = do_ref[pl.ds(q0, ts), :]
            st = lax.dot_general(kb, qb, _DN["nt"], preferred_element_type=F32)
            dpt = lax.dot_general(vb, dob_, _DN["nt"], preferred_element_type=F32)
            return q0, qb, dob_, st, dpt

        def accumulate(q0, qb, dob_, st, dpt, masked):
            pt = jnp.exp2(st * SCORE_SCALE_LOG2 - lse_ref[0:1, pl.ds(q0, ts)])
            if masked:
                pt = jnp.where(lax.broadcasted_iota(jnp.int32, (ts, ts), 0) > lax.broadcasted_iota(jnp.int32, (ts, ts), 1),
                               0.0, pt)
            dv_sc[...] += jnp.dot(pt.astype(BF16), dob_, preferred_element_type=F32)
            dst = (pt * (dpt - delta_ref[0:1, pl.ds(q0, ts)])).astype(BF16)
            dk_sc[...] += jnp.dot(dst, qb, preferred_element_type=F32) * SCORE_SCALE
            dq_ref[pl.ds(q0, ts), :] += lax.dot_general(dst, kb, _DN["tn"], preferred_element_type=F32) * SCORE_SCALE

        accumulate(*products(j), True)
        n_below = nb - 1 - j

        def pair(t, carry):
            a, b = products(j + 1 + 2 * t), products(j + 2 + 2 * t)
            accumulate(*a, False)
            accumulate(*b, False)
            return carry

        lax.fori_loop(0, n_below // 2, pair, 0)

        @pl.when(n_below % 2 == 1)
        def _():
            accumulate(*products(nb - 1), False)

        dk_ref[...] = dk_sc[...]
        dv_ref[...] = dv_sc[...]

    return pl.pallas_call(
        body, grid=(HEADS, nb),
        in_specs=[pl.BlockSpec((S, HEAD_PAD), lambda h, j: (0, h)),
                  pl.BlockSpec((S, 128), lambda h, j: (0, 4 + h)),
                  pl.BlockSpec((None, 8, S), lambda h, j: (h, 0, 0)),
                  pl.BlockSpec((None, 8, S), lambda h, j: (h, 0, 0)),
                  pl.BlockSpec((ts, HEAD_PAD), lambda h, j: (j, h)),
                  pl.BlockSpec((ts, 128), lambda h, j: (j, h))],
        out_specs=[pl.BlockSpec((S, HEAD_PAD), lambda h, j: (0, h)),
                   pl.BlockSpec((ts, HEAD_PAD), lambda h, j: (j, h)),
                   pl.BlockSpec((ts, 128), lambda h, j: (j, h))],
        out_shape=[jax.ShapeDtypeStruct((S, HEADS * HEAD_PAD), F32), jax.ShapeDtypeStruct((S, HEADS * HEAD_PAD), F32),
                   jax.ShapeDtypeStruct((S, HEADS * 128), F32)],
        scratch_shapes=[pltpu.VMEM((ts, HEAD_PAD), F32), pltpu.VMEM((ts, 128), F32)],
        compiler_params=_cp(("parallel", "arbitrary")), name=name)(q, d_cat, lse, delta, k, v)


def _memk_fwd(mkv, gkx, *, name):
    M = mkv.shape[0]
    XW = X_HEADS * X_DIM

    def body(mkv_ref, g_ref, k_ref, v_ref):
        for h in range(X_HEADS):
            cs = slice(h * X_DIM, (h + 1) * X_DIM)
            xv = mkv_ref[:, cs]
            r = lax.rsqrt(jnp.mean(xv * xv, axis=-1, keepdims=True) + NORM_EPS)
            k_ref[:, cs] = ((xv * r) * g_ref[...]).astype(BF16)
        v_ref[...] = mkv_ref[:, XW:].astype(BF16)

    return pl.pallas_call(
        body, grid=(1,),
        in_specs=[pl.BlockSpec((M, 2 * XW), lambda i: (0, 0)), pl.BlockSpec((1, X_DIM), lambda i: (0, 0))],
        out_specs=[pl.BlockSpec((M, XW), lambda i: (0, 0)), pl.BlockSpec((M, XW), lambda i: (0, 0))],
        out_shape=[jax.ShapeDtypeStruct((M, XW), BF16), jax.ShapeDtypeStruct((M, XW), BF16)],
        compiler_params=_cp(("arbitrary",)), name=name)(mkv, gkx)


def _memk_bwd(mkv, gkx, dk, dv, *, name):
    M = mkv.shape[0]
    XW = X_HEADS * X_DIM

    def body(mkv_ref, g_ref, dk_ref, dv_ref, o_ref, gg_ref):
        gg = jnp.zeros((1, X_DIM), F32)
        for h in range(X_HEADS):
            cs = slice(h * X_DIM, (h + 1) * X_DIM)
            xv = mkv_ref[:, cs]
            r = lax.rsqrt(jnp.mean(xv * xv, axis=-1, keepdims=True) + NORM_EPS)
            n = xv * r
            dkv = dk_ref[:, cs]
            gg += jnp.sum(dkv * n, axis=0, keepdims=True)
            dn = dkv * g_ref[...]
            c = jnp.mean(dn * n, axis=-1, keepdims=True)
            o_ref[:, cs] = (r * (dn - n * c)).astype(BF16)
        o_ref[:, XW:] = dv_ref[...].astype(BF16)
        gg_ref[...] = gg

    full = lambda c: pl.BlockSpec((M, c), lambda i: (0, 0))
    return pl.pallas_call(
        body, grid=(1,),
        in_specs=[full(2 * XW), pl.BlockSpec((1, X_DIM), lambda i: (0, 0)), full(XW), full(XW)],
        out_specs=[full(2 * XW), pl.BlockSpec((1, X_DIM), lambda i: (0, 0))],
        out_shape=[jax.ShapeDtypeStruct((M, 2 * XW), BF16), jax.ShapeDtypeStruct((1, X_DIM), F32)],
        compiler_params=_cp(("arbitrary",)), name=name)(mkv, gkx, dk, dv)


def _xq_norm(z_ref, g_ref, h):
    xv = z_ref[:, h * X_DIM:(h + 1) * X_DIM]
    r = lax.rsqrt(jnp.mean(xv * xv, axis=-1, keepdims=True) + NORM_EPS)
    n = xv * r
    return n, r, n * g_ref[...]


def _xprobs(qb, k_ref, h):
    s = lax.dot_general(qb, k_ref[:, h * X_DIM:(h + 1) * X_DIM], _DN["nt"],
                        preferred_element_type=F32) * (1.0 / math.sqrt(X_DIM))
    e = jnp.exp(s - jnp.max(s, axis=-1, keepdims=True))
    return e / jnp.sum(e, axis=-1, keepdims=True)


def _memattn_fwd(z, kx, vx, gqx, *, name):
    S = z.shape[0]
    M = kx.shape[0]
    tm = _big_row_tile(S)
    XW = X_HEADS * X_DIM

    def body(z_ref, k_ref, v_ref, g_ref, o_ref):
        for h in range(X_HEADS):
            cs = slice(h * X_DIM, (h + 1) * X_DIM)
            _, _, qn = _xq_norm(z_ref, g_ref, h)
            p = _xprobs(qn.astype(BF16), k_ref, h)
            o_ref[:, cs] = jnp.dot(p.astype(BF16), v_ref[:, cs], preferred_element_type=F32).astype(BF16)

    return pl.pallas_call(
        body, grid=(S // tm,),
        in_specs=[pl.BlockSpec((tm, XW), lambda i: (i, Z_MQ_CB)), pl.BlockSpec((M, XW), lambda i: (0, 0)),
                  pl.BlockSpec((M, XW), lambda i: (0, 0)), pl.BlockSpec((1, X_DIM), lambda i: (0, 0))],
        out_specs=pl.BlockSpec((tm, XW), lambda i: (i, 0)),
        out_shape=jax.ShapeDtypeStruct((S, XW), BF16),
        compiler_params=_cp(("parallel",)), name=name)(z, kx, vx, gqx)


def _memattn_bwd(z, kx, vx, gqx, d_cat, *, name):
    S = z.shape[0]
    M = kx.shape[0]
    tm = _big_row_tile(S)
    XW = X_HEADS * X_DIM
    scale = 1.0 / math.sqrt(X_DIM)

    def body(z_ref, k_ref, v_ref, g_ref, do_ref, dz_ref, dk_ref, dv_ref, gg_ref):
        i = pl.program_id(0)

        @pl.when(i == 0)
        def _():
            dk_ref[...] = jnp.zeros_like(dk_ref)
            dv_ref[...] = jnp.zeros_like(dv_ref)
            gg_ref[...] = jnp.zeros_like(gg_ref)

        gg = jnp.zeros((1, X_DIM), F32)
        for h in range(X_HEADS):
            cs = slice(h * X_DIM, (h + 1) * X_DIM)
            n, r, qn = _xq_norm(z_ref, g_ref, h)
            qb = qn.astype(BF16)
            p = _xprobs(qb, k_ref, h)
            pb = p.astype(BF16)
            dob = do_ref[:, cs].astype(BF16)
            dv_ref[:, cs] += lax.dot_general(pb, dob, _DN["tn"], preferred_element_type=F32)
            dp = lax.dot_general(dob, v_ref[:, cs], _DN["nt"], preferred_element_type=F32)
            ds = (p * (dp - jnp.sum(dp * p, axis=-1, keepdims=True))).astype(BF16)
            dk_ref[:, cs] += lax.dot_general(ds, qb, _DN["tn"], preferred_element_type=F32) * scale
            dqn = jnp.dot(ds, k_ref[:, cs], preferred_element_type=F32) * scale
            gg += jnp.sum(dqn * n, axis=0, keepdims=True)
            dn = dqn * g_ref[...]
            c = jnp.mean(dn * n, axis=-1, keepdims=True)
            dz_ref[:, cs] = (r * (dn - n * c)).astype(BF16)
        gg_ref[...] += gg

    full = pl.BlockSpec((M, XW), lambda i: (0, 0))
    vec = pl.BlockSpec((1, X_DIM), lambda i: (0, 0))
    return pl.pallas_call(
        body, grid=(S // tm,),
        in_specs=[pl.BlockSpec((tm, XW), lambda i: (i, Z_MQ_CB)), full, full, vec,
                  pl.BlockSpec((tm, XW), lambda i: (i, 3))],
        out_specs=[pl.BlockSpec((tm, XW), lambda i: (i, 0)), full, full, vec],
        out_shape=[jax.ShapeDtypeStruct((S, XW), BF16), jax.ShapeDtypeStruct((M, XW), F32),
                   jax.ShapeDtypeStruct((M, XW), F32), jax.ShapeDtypeStruct((1, X_DIM), F32)],
        compiler_params=_cp(("arbitrary",)), name=name)(z, kx, vx, gqx, d_cat)


def _silu_parts(x):
    h = 0.5 * x
    return h, jnp.tanh(h)


GLU_HALO = 16


def _glu_tiles(S, F):
    return _big_row_tile(S), _pick(F, (1408, 512, 256, 128))


def _glu_fwd(g, u, conv_w, conv_b, *, name):
    S, F = g.shape
    tm, tc = _glu_tiles(S, F)
    hb = tm // GLU_HALO

    def body(g_ref, gp_ref, u_ref, w_ref, b_ref, a_ref):
        i = pl.program_id(1)
        gt = g_ref[...].astype(F32)
        ext = jnp.concatenate([gp_ref[...].astype(F32) * (i > 0).astype(F32), gt], axis=0)
        gc = b_ref[...] + w_ref[0:1, :] * pltpu.roll(ext, 2, 0)[GLU_HALO:]
        gc = gc + w_ref[1:2, :] * pltpu.roll(ext, 1, 0)[GLU_HALO:]
        gc = gc + w_ref[2:3, :] * gt
        h, t = _silu_parts(gc)
        a_ref[...] = ((h * (1.0 + t)) * u_ref[...].astype(F32)).astype(BF16)

    return pl.pallas_call(
        body, grid=(F // tc, S // tm),
        in_specs=[pl.BlockSpec((tm, tc), lambda j, i: (i, j)),
                  pl.BlockSpec((GLU_HALO, tc), lambda j, i: (jnp.maximum(i * hb - 1, 0), j)),
                  pl.BlockSpec((tm, tc), lambda j, i: (i, j)),
                  pl.BlockSpec((3, tc), lambda j, i: (0, j)),
                  pl.BlockSpec((1, tc), lambda j, i: (0, j))],
        out_specs=pl.BlockSpec((tm, tc), lambda j, i: (i, j)),
        out_shape=jax.ShapeDtypeStruct((S, F), BF16),
        compiler_params=_cp(("parallel", "parallel")), name=name)(g, g, u, conv_w, conv_b)


def _glu_bwd(g, u, d_a, conv_w, conv_b, *, name):
    S, F = g.shape
    tm, tc = _glu_tiles(S, F)
    hb = tm // GLU_HALO
    nt = S // tm
    E = tm + GLU_HALO

    def body(g_ref, gp_ref, gn_ref, u_ref, un_ref, da_ref, dan_ref, w_ref, b_ref,
             dg_ref, du_ref, gw_ref, gb_ref):
        i = pl.program_id(1)

        @pl.when(i == 0)
        def _():
            gw_ref[...] = jnp.zeros_like(gw_ref)
            gb_ref[...] = jnp.zeros_like(gb_ref)

        w0, w1, w2 = w_ref[0:1, :], w_ref[1:2, :], w_ref[2:3, :]
        gext = jnp.concatenate([gp_ref[...].astype(F32) * (i > 0).astype(F32), g_ref[...].astype(F32),
                                gn_ref[...].astype(F32)], axis=0)
        g1 = pltpu.roll(gext, 1, 0)[GLU_HALO:]
        g2 = pltpu.roll(gext, 2, 0)[GLU_HALO:]
        g0 = gext[GLU_HALO:]
        gc = b_ref[...] + w0 * g2
        gc = gc + w1 * g1
        gc = gc + w2 * g0
        h, t = _silu_parts(gc)
        t1 = 1.0 + t
        da = jnp.concatenate([da_ref[...].astype(F32), dan_ref[...].astype(F32) * (i < nt - 1).astype(F32)], axis=0)
        uu = jnp.concatenate([u_ref[...].astype(F32), un_ref[...].astype(F32)], axis=0)
        du_ref[...] = (da[:tm] * (h[:tm] * t1[:tm])).astype(BF16)
        dgc = (da * uu) * (0.5 * (t1 + h * (1.0 - t * t)))
        dg = w2 * dgc[:tm] + w1 * pltpu.roll(dgc, E - 1, 0)[:tm] + w0 * pltpu.roll(dgc, E - 2, 0)[:tm]
        dg_ref[...] = dg.astype(BF16)
        dgt = dgc[:tm]
        gb_ref[...] += jnp.sum(dgt, axis=0, keepdims=True)
        gw_ref[0:1, :] += jnp.sum(dgt * g2[:tm], axis=0, keepdims=True)
        gw_ref[1:2, :] += jnp.sum(dgt * g1[:tm], axis=0, keepdims=True)
        gw_ref[2:3, :] += jnp.sum(dgt * g0[:tm], axis=0, keepdims=True)

    tile = pl.BlockSpec((tm, tc), lambda j, i: (i, j))
    nxt = pl.BlockSpec((GLU_HALO, tc), lambda j, i: (jnp.minimum((i + 1) * hb, S // GLU_HALO - 1), j))
    prv = pl.BlockSpec((GLU_HALO, tc), lambda j, i: (jnp.maximum(i * hb - 1, 0), j))
    return pl.pallas_call(
        body, grid=(F // tc, nt),
        in_specs=[tile, prv, nxt, tile, nxt, tile, nxt,
                  pl.BlockSpec((3, tc), lambda j, i: (0, j)), pl.BlockSpec((1, tc), lambda j, i: (0, j))],
        out_specs=[tile, tile, pl.BlockSpec((3, tc), lambda j, i: (0, j)), pl.BlockSpec((1, tc), lambda j, i: (0, j))],
        out_shape=[jax.ShapeDtypeStruct((S, F), BF16), jax.ShapeDtypeStruct((S, F), BF16),
                   jax.ShapeDtypeStruct((3, F), F32), jax.ShapeDtypeStruct((1, F), F32)],
        compiler_params=_cp(("parallel", "arbitrary")), name=name)(g, g, g, u, u, d_a, d_a, conv_w, conv_b)


def _loss_head(y, target, *, name):
    S, D = y.shape
    tm = _big_row_tile(S)
    nt = S // tm

    def body(y_ref, t_ref, dy_ref, dyb_ref, loss_ref, acc):
        i = pl.program_id(0)

        @pl.when(i == 0)
        def _():
            acc[...] = jnp.zeros_like(acc)

        e = y_ref[...] - t_ref[...]
        dy = e * (1.0 / D)
        dy_ref[...] = dy
        dyb_ref[...] = dy.astype(BF16)
        acc[...] += jnp.sum(e * e, axis=0, keepdims=True)

        @pl.when(i == nt - 1)
        def _():
            loss_ref[...] = jnp.broadcast_to(jnp.sum(acc[...], axis=1, keepdims=True) * (0.5 / D), (1, 128))

    row = pl.BlockSpec((tm, D), lambda i: (i, 0))
    return pl.pallas_call(
        body, grid=(nt,), in_specs=[row, row],
        out_specs=[row, row, pl.BlockSpec((1, 128), lambda i: (0, 0))],
        out_shape=[jax.ShapeDtypeStruct((S, D), F32), jax.ShapeDtypeStruct((S, D), BF16),
                   jax.ShapeDtypeStruct((1, 128), F32)],
        scratch_shapes=[pltpu.VMEM((1, D), F32)],
        compiler_params=_cp(("arbitrary",)), name=name)(y, target)


def _adamw_math(w, g, m, v):
    m = ADAM_B1 * m + (1.0 - ADAM_B1) * g
    v = ADAM_B2 * v + (1.0 - ADAM_B2) * (g * g)
    m_hat = m / (1.0 - ADAM_B1 ** ADAM_STEP)
    v_hat = v / (1.0 - ADAM_B2 ** ADAM_STEP)
    delta = -ADAM_LR * (m_hat / (jnp.sqrt(v_hat) + ADAM_EPS) + ADAM_WD * w)
    return delta, m, v


def _adamw(w, m, v, parts, *, name):
    R, C = w.shape
    tr = 128 if R % 128 == 0 else R
    n_parts = len(parts)

    def body(*refs):
        w_ref, m_ref, v_ref = refs[:3]
        p_refs = refs[3:3 + n_parts]
        g_ref, d_ref, mo_ref, vo_ref = refs[3 + n_parts:]
        g = p_refs[0][...]
        for p in p_refs[1:]:
            g = g + p[...]
        delta, mn, vn = _adamw_math(w_ref[...], g, m_ref[...], v_ref[...])
        g_ref[...] = g
        d_ref[...] = delta
        mo_ref[...] = mn
        vo_ref[...] = vn

    blk = pl.BlockSpec((tr, C), lambda i: (i, 0))
    return pl.pallas_call(
        body, grid=(R // tr,), in_specs=[blk] * (3 + n_parts), out_specs=[blk] * 4,
        out_shape=[jax.ShapeDtypeStruct((R, C), F32)] * 4,
        compiler_params=_cp(("parallel",)), name=name)(w, m, v, *parts)


def _sum4(g_stack, recv, me, *, name):
    _, R, C = g_stack.shape
    tr = 128 if R % 128 == 0 else R

    def body(me_ref, g_ref, r_ref, o_ref):
        acc = g_ref[...]
        for j in range(N_CHIPS - 1):
            acc = acc + r_ref[j].astype(F32)
        o_ref[...] = acc

    grid_spec = pltpu.PrefetchScalarGridSpec(
        num_scalar_prefetch=1, grid=(R // tr,),
        in_specs=[pl.BlockSpec((None, tr, C), lambda i, me_ref: (me_ref[0], i, 0)),
                  pl.BlockSpec((N_CHIPS - 1, tr, C), lambda i, me_ref: (0, i, 0))],
        out_specs=pl.BlockSpec((tr, C), lambda i, me_ref: (i, 0)))
    return pl.pallas_call(
        body, grid_spec=grid_spec, out_shape=jax.ShapeDtypeStruct((R, C), F32),
        compiler_params=_cp(("parallel",)), name=name)(me, g_stack, recv)


def _sum8(gathered, *, name):
    _, R, C = gathered.shape

    def body(g_ref, o_ref):
        acc = g_ref[0]
        for d in range(1, N_DEV):
            acc = acc + g_ref[d]
        o_ref[...] = acc

    return pl.pallas_call(
        body, grid=(1,), in_specs=[pl.BlockSpec((N_DEV, R, C), lambda i: (0, 0, 0))],
        out_specs=pl.BlockSpec((R, C), lambda i: (0, 0)),
        out_shape=jax.ShapeDtypeStruct((R, C), F32),
        compiler_params=_cp(("arbitrary",)), name=name)(gathered)


def _place():
    return lax.axis_index("x"), lax.axis_index("y"), lax.axis_index("c")


def _other_chips(x, y):
    return [(1 - x, y), (x, 1 - y), (1 - x, 1 - y)]


_ANY = pl.BlockSpec(memory_space=pl.ANY)


_HBM = pl.BlockSpec(memory_space=pltpu.HBM)
_SEM = pl.BlockSpec(memory_space=pltpu.SEMAPHORE)
_EFFECT = pltpu.SideEffectType.DATAFLOW_SIDE_EFFECTING


def _gather_copies(srcs, lands, send_sems, recv_sems):
    x, y, c = _place()
    me = 2 * x + y
    return [pltpu.make_async_remote_copy(
        src_ref=srcs[i], dst_ref=lands[i].at[me], send_sem=send_sems.at[3 * i + j],
        recv_sem=recv_sems.at[3 * i + j], device_id=(px, py, c), device_id_type=MESH)
        for i in range(len(srcs)) for j, (px, py) in enumerate(_other_chips(x, y))]


def _scatter_copies(srcs, lands, send_sems, recv_sems):
    x, y, c = _place()
    return [pltpu.make_async_remote_copy(
        src_ref=srcs[i].at[2 * px + py], dst_ref=lands[i].at[j], send_sem=send_sems.at[3 * i + j],
        recv_sem=recv_sems.at[3 * i + j], device_id=(px, py, c), device_id_type=MESH)
        for i in range(len(srcs)) for j, (px, py) in enumerate(_other_chips(x, y))]


def _copies_start(srcs, lands, make_copies, thru=(), *, name):
    n = len(srcs)
    n_ops = 2 * n + len(thru)

    def body(*refs):
        send_sems, recv_sems = refs[n_ops], refs[n_ops + 1]
        for cp in make_copies(refs[:n], refs[n:2 * n], send_sems, recv_sems):
            cp.start()
        refs[-1][...] = jnp.zeros_like(refs[-1])

    ops = list(srcs) + list(lands) + list(thru)
    outs = pl.pallas_call(
        body, name=name,
        out_shape=(pltpu.SemaphoreType.DMA((3 * n,)), pltpu.SemaphoreType.DMA((3 * n,)),
                   *[pltpu.HBM(a.shape, a.dtype) for a in ops], jax.ShapeDtypeStruct((8, 128), F32)),
        in_specs=[_HBM] * n_ops,
        out_specs=(_SEM, _SEM, *[_HBM] * n_ops, pl.BlockSpec(memory_space=pltpu.VMEM)),
        input_output_aliases={i: 2 + i for i in range(n_ops)},
        compiler_params=pltpu.CompilerParams(has_side_effects=_EFFECT),
    )(*[pltpu.with_memory_space_constraint(a, pltpu.HBM) for a in ops])
    return outs[0], outs[1], list(outs[2:2 + n]), list(outs[2 + n:2 + 2 * n]), outs[-1], list(outs[2 + 2 * n:-1])


def _copies_wait(handle, after, make_copies, *, name):
    send_sems, recv_sems, srcs, lands = handle[:4]
    n = len(srcs)

    def body(*refs):
        for cp in make_copies(refs[:n], refs[n:2 * n], refs[2 * n], refs[2 * n + 1]):
            cp.wait_send()
            cp.wait_recv()

    ops = list(srcs) + list(lands)
    outs = pl.pallas_call(
        body, name=name,
        out_shape=tuple(pltpu.HBM(a.shape, a.dtype) for a in ops),
        in_specs=[_HBM] * (2 * n) + [_SEM, _SEM, _ANY],
        out_specs=tuple([_HBM] * (2 * n)),
        input_output_aliases={i: i for i in range(2 * n)},
        compiler_params=pltpu.CompilerParams(has_side_effects=_EFFECT),
    )(*ops, send_sems, recv_sems, after)
    return list(outs[n:])


def _swap_with_sibling(arrs, *, name):
    n = len(arrs)

    def body(*refs):
        ins, outs = refs[:n], refs[n:2 * n]
        send_sems, recv_sems = refs[2 * n:]
        x, y, c = _place()
        remote = []
        for i in range(n):
            rc = pltpu.make_async_remote_copy(
                src_ref=ins[i], dst_ref=outs[i], send_sem=send_sems.at[i], recv_sem=recv_sems.at[i],
                device_id=(x, y, 1 - c), device_id_type=MESH)
            rc.start()
            remote.append(rc)
        for rc in remote:
            rc.wait_send()
        for rc in remote:
            rc.wait_recv()

    return pl.pallas_call(
        body, in_specs=[_ANY] * n, out_specs=[_ANY] * n,
        out_shape=[jax.ShapeDtypeStruct(a.shape, a.dtype) for a in arrs],
        scratch_shapes=[pltpu.SemaphoreType.DMA((n,)), pltpu.SemaphoreType.DMA((n,))],
        name=name)(*arrs)


def _gather_all(buf, *, name):
    R, C = buf.shape

    def body(in_ref, out_ref, send_sems, recv_sems, local_sem):
        x, y, c = _place()
        me = 4 * x + 2 * y + c
        lc = pltpu.make_async_copy(in_ref, out_ref.at[me], local_sem)
        lc.start()
        remote = []
        for k in range(1, N_DEV):
            px = 1 - x if (k >> 2) & 1 else x
            py = 1 - y if (k >> 1) & 1 else y
            pc = 1 - c if k & 1 else c
            rc = pltpu.make_async_remote_copy(
                src_ref=in_ref, dst_ref=out_ref.at[me], send_sem=send_sems.at[k - 1],
                recv_sem=recv_sems.at[k - 1], device_id=(px, py, pc), device_id_type=MESH)
            rc.start()
            remote.append(rc)
        lc.wait()
        for rc in remote:
            rc.wait_send()
        for rc in remote:
            rc.wait_recv()

    return pl.pallas_call(
        body, in_specs=[_ANY], out_specs=_ANY,
        out_shape=jax.ShapeDtypeStruct((N_DEV, R, C), buf.dtype),
        scratch_shapes=[pltpu.SemaphoreType.DMA((N_DEV - 1,)), pltpu.SemaphoreType.DMA((N_DEV - 1,)),
                        pltpu.SemaphoreType.DMA],
        name=name)(buf)


def _w_in_to_z(w):
    pad = jnp.zeros(w.shape[:-1] + (64,), w.dtype)
    return jnp.concatenate([w[..., 0:512], w[..., 512:1024], w[..., 1344:1856], w[..., 1024:1280],
                            w[..., 1280:1344], pad], axis=-1)


def _z_to_w_in(g):
    return jnp.concatenate([g[..., 0:512], g[..., 512:1024], g[..., 1536:1792], g[..., 1792:1856],
                            g[..., 1024:1536]], axis=-1)


def _pad_heads(w, nh):
    w = w.reshape(w.shape[:-1] + (nh, QK))
    w = jnp.concatenate([w, jnp.zeros(w.shape[:-1] + (HEAD_PAD - QK,), w.dtype)], axis=-1)
    return w.reshape(w.shape[:-2] + (nh * HEAD_PAD,))


def _unpad_heads(g, nh):
    g = g.reshape(g.shape[:-1] + (nh, HEAD_PAD))[..., :QK]
    return g.reshape(g.shape[:-2] + (nh * QK,))


def _kv_split(w, nh):
    w = w.reshape(w.shape[:-1] + (nh, 2, 128))
    return jnp.swapaxes(w, -3, -2).reshape(w.shape[:-3] + (nh * 256,))


def _kv_join(g, nh):
    g = g.reshape(g.shape[:-1] + (2, nh, 128))
    return jnp.swapaxes(g, -3, -2).reshape(g.shape[:-3] + (nh * 256,))


def _pad_gain(g):
    return jnp.concatenate([g, jnp.zeros((1, HEAD_PAD - QK), g.dtype)], axis=1)


_SMALL = ("g_mix", "g_q_lat", "g_kv_lat", "g_q_mla", "g_k_mla", "w_pool", "pool_scale", "g_mem", "g_q_x",
          "g_k_x", "g_ffn", "conv_b", "conv_w")


def _pack(arrs, extra=0):
    flat = jnp.concatenate([a.reshape(-1) for a in arrs])
    n = flat.shape[0] + extra
    rows = -(-n // 1024) * 8
    return jnp.pad(flat, (0, rows * 128 - flat.shape[0])).reshape(rows, 128)


def _unpack(buf, shapes):
    flat = buf.reshape(-1)
    out, off = [], 0
    for s in shapes:
        n = int(np.prod(s))
        out.append(flat[off:off + n].reshape(s))
        off += n
    return out, off


def _tied(a, token):
    return a + token[:1, :1].astype(a.dtype)


def _local_step(x, mem, target, W, fetch=None, ship=None):
    fetch = fetch or (lambda group, after: None)
    ship = ship or (lambda group, G: jnp.zeros((8, 128), F32))
    S, D = x.shape
    F = W["conv_b"].shape[1]
    tabs = _rope_tables(S)
    tm = 512 if S % 512 == 0 else 128
    tl = 1024 if S % 1024 == 0 else tm
    tk = _pick(S, (1024, 512, 128))

    h = _rms_fwd(x, W["g_mix"], C=D, name="norm_mix")
    fetch("g1", h)
    z = mm_nn(h, W["w_in"], tm=tl, tn=Z_COLS, tk=D, name="z_proj")
    fetch("g2", z)
    y_pool = _pool_fwd(z, W["w_pool"], W["pool_scale"], name="pool_fwd")
    ql = _rms_fwd(z, W["g_q_lat"], C=Q_RANK, cb=Z_Q_CB, name="norm_qlat")
    kvl = _rms_fwd(z, W["g_kv_lat"], C=KV_RANK, cb=Z_KV_CB, name="norm_kvlat")
    qraw = mm_nn(ql, W["w_q_up"], nsh=N_CHIPS, tm=tl, tn=512, tk=Q_RANK, name="q_up")
    kvraw = mm_nn(kvl, W["w_kv_up"], nsh=N_CHIPS, tm=tl, tn=512, tk=KV_RANK, name="kv_up")
    q, k, v, vt = _qkrope_fwd(qraw, kvraw, z, W["g_q_mla"], W["g_k_mla"], tabs, name="qk_norm_rope")
    o, y_mla, lse = _flash_fwd(q, k, vt, name="mla_fwd")
    memn = _rms_fwd(mem, W["g_mem"], C=D, name="norm_mem")
    M = mem.shape[0]
    mkv = mm_nn(memn, W["w_mem_kv"], tm=M, tn=1024, tk=D, name="mem_kv")
    kx, vx = _memk_fwd(mkv, W["g_k_x"], name="memk_fwd")
    y_mem = _memattn_fwd(z, kx, vx, W["g_q_x"], name="memattn_fwd")
    cat = jnp.concatenate([y_pool, y_mla, y_mem], axis=1)
    fetch("g2b", cat)
    x2 = mm_nn(cat, W["w_o"], tm=tm, tn=D, tk=D, add=x, name="o_proj")
    h2 = _rms_fwd(x2, W["g_ffn"], C=D, name="norm_ffn")
    fetch("g3", h2)
    fn = F // N_CHIPS
    g = mm_nn(h2, W["w_gate"], nsh=N_CHIPS, tm=tl, tn=fn, tk=D, out_dtypes=(BF16,), name="gate_proj")
    u = mm_nn(h2, W["w_up"], nsh=N_CHIPS, tm=tl, tn=fn, tk=D, out_dtypes=(BF16,), name="up_proj")
    a = _glu_fwd(g, u, W["conv_w"], W["conv_b"], name="glu_fwd")
    y = mm_nn(a, W["w_down"], tm=tm, tn=1024, tk=F // 2, add=x2, name="down_proj")
    dy, dyb, loss_row = _loss_head(y, target, name="loss_head")

    G = {}
    d_a = mm_nt(dyb, W["w_down"], tm=tl, to=512, tc=D, out_dtypes=(BF16,), name="d_a")
    G["w_down"] = mm_tn(a, dyb, to=fn, tn=1024, tk=tk, out_dtypes=(F32, BF16), name="grad_w_down")
    d_g, d_u, G["conv_w"], G["conv_b"] = _glu_bwd(g, u, d_a, W["conv_w"], W["conv_b"], name="glu_bwd")
    G["w_gate"] = mm_tn(h2, d_g, nsh=N_CHIPS, to=1024, tn=fn, tk=tk, out_dtypes=(F32, BF16), name="grad_w_gate")
    G["w_up"] = mm_tn(h2, d_u, nsh=N_CHIPS, to=1024, tn=fn, tk=tk, out_dtypes=(F32, BF16), name="grad_w_up")
    tok = ship("s1", G)
    d_h2 = mm_nt_pair(d_g, W["w_gate"], d_u, W["w_up"], nsh=N_CHIPS, tm=tm, to=1024, tc=fn, out_dtype=BF16,
                      name="d_h2")
    d_x2, d_x2b, G["g_ffn"] = _rms_bwd(x2, d_h2, _tied(W["g_ffn"], tok), C=D, res=dy, out_dtypes=(F32, BF16),
                                       name="norm_ffn_bwd")

    d_cat = mm_nt(d_x2b, W["w_o"], tm=tl, to=1024, tc=D, out_dtypes=(BF16,), name="d_cat")
    G["w_o"] = mm_tn(cat, d_x2b, to=1024, tn=1024, tk=tk, out_dtypes=(F32, BF16), name="grad_w_o")
    tok = ship("s2", G)
    dz_pool, G["w_pool"], G["pool_scale"] = _pool_bwd(z, d_cat, W["w_pool"], _tied(W["pool_scale"], tok),
                                                      name="pool_bwd")
    dz_mq, dkx, dvx, G["g_q_x"] = _memattn_bwd(z, kx, vx, W["g_q_x"], d_cat, name="memattn_bwd")
    d_mkv, G["g_k_x"] = _memk_bwd(mkv, W["g_k_x"], dkx, dvx, name="memk_bwd")
    G["w_mem_kv"] = mm_tn(memn, d_mkv, to=1024, tn=1024, tk=M, out_dtypes=(F32, BF16), name="grad_w_mem_kv")
    d_memn = mm_nt(d_mkv, W["w_mem_kv"], tm=M, to=D, tc=1024, name="d_memn")
    _, G["g_mem"] = _rms_bwd(mem, d_memn, W["g_mem"], C=D, name="norm_mem_bwd")
    delta = _attn_bwd_prep(o, d_cat, name="mla_bwd_prep")
    dq, dk, dv = _flash_bwd(q, k, v, d_cat, lse, delta, name="mla_bwd")
    d_qraw, d_kvraw, dz_kr, G["g_q_mla"], G["g_k_mla"] = _qkrope_bwd(
        qraw, kvraw, z, W["g_q_mla"], W["g_k_mla"], tabs, dq, dk, dv, name="qk_norm_rope_bwd")
    G["w_q_up"] = mm_tn(ql, d_qraw, nsh=N_CHIPS, to=Q_RANK, tn=512, tk=tk, out_dtypes=(F32, BF16), name="grad_w_q_up")
    d_ql = mm_nt(d_qraw, W["w_q_up"], nsh=N_CHIPS, tm=tl, to=Q_RANK, tc=512, name="d_ql")
    G["w_kv_up"] = mm_tn(kvl, d_kvraw, nsh=N_CHIPS, to=KV_RANK, tn=512, tk=tk, out_dtypes=(F32, BF16),
                         name="grad_w_kv_up")
    d_kvl = mm_nt(d_kvraw, W["w_kv_up"], nsh=N_CHIPS, tm=tl, to=KV_RANK, tc=512, name="d_kvl")
    dz_q, G["g_q_lat"] = _rms_bwd(z, d_ql, W["g_q_lat"], C=Q_RANK, cb=Z_Q_CB, out_dtypes=(BF16,), name="norm_qlat_bwd")
    dz_kv, G["g_kv_lat"] = _rms_bwd(z, d_kvl, W["g_kv_lat"], C=KV_RANK, cb=Z_KV_CB, out_dtypes=(BF16,),
                                    name="norm_kvlat_bwd")
    d_z = jnp.concatenate([dz_pool, dz_q, dz_mq, dz_kv, dz_kr], axis=1)
    G["w_in"] = mm_tn(h, d_z, to=512, tn=Z_COLS, tk=tk, out_dtypes=(F32, BF16), name="grad_w_in")
    tok = ship("s3", G)
    d_h = mm_nt(d_z, W["w_in"], tm=tl, to=1024, tc=Z_COLS, out_dtypes=(BF16,), name="d_h")
    grad_x, G["g_mix"] = _rms_bwd(x, d_h, _tied(W["g_mix"], tok), C=D, res=d_x2, name="norm_mix_bwd")
    return loss_row, grad_x, G


_BIG = ("w_in", "w_q_up", "w_kv_up", "w_mem_kv", "w_o", "w_gate", "w_up", "w_down")
_WEIGHTS = ("g_mix", "w_in", "g_q_lat", "w_q_up", "g_kv_lat", "w_kv_up", "g_q_mla", "g_k_mla", "w_pool",
            "pool_scale", "g_mem", "w_mem_kv", "g_q_x", "g_k_x", "w_o", "g_ffn", "w_gate", "w_up", "conv_w",
            "conv_b", "w_down")


def _to_compute_layout(name, w):
    if name == "w_in":
        return _w_in_to_z(w)
    if name == "w_q_up":
        return _pad_heads(w, w.shape[-1] // QK)
    if name == "w_kv_up":
        return _kv_split(w, w.shape[-1] // 256)
    return w


def _from_compute_layout(name, g):
    if name == "w_in":
        return _z_to_w_in(g)
    if name == "w_q_up":
        return _unpad_heads(g, g.shape[-1] // HEAD_PAD)
    if name == "w_kv_up":
        return _kv_join(g, g.shape[-1] // 256)
    return g


def kernel(x, mem, g_mix, w_in, g_q_lat, w_q_up, g_kv_lat, w_kv_up, g_q_mla, g_k_mla, w_pool, pool_scale, g_mem, w_mem_kv, g_q_x, g_k_x, w_o, g_ffn, w_gate, w_up, conv_w, conv_b, w_down, loss_target, m_g_mix, m_w_in, m_g_q_lat, m_w_q_up, m_g_kv_lat, m_w_kv_up, m_g_q_mla, m_g_k_mla, m_w_pool, m_pool_scale, m_g_mem, m_w_mem_kv, m_g_q_x, m_g_k_x, m_w_o, m_g_ffn, m_w_gate, m_w_up, m_conv_w, m_conv_b, m_w_down, v_g_mix, v_w_in, v_g_q_lat, v_w_q_up, v_g_kv_lat, v_w_kv_up, v_g_q_mla, v_g_k_mla, v_w_pool, v_pool_scale, v_g_mem, v_w_mem_kv, v_g_q_x, v_g_k_x, v_w_o, v_g_ffn, v_w_gate, v_w_up, v_conv_w, v_conv_b, v_w_down):
    P = dict(g_mix=g_mix, w_in=w_in, g_q_lat=g_q_lat, w_q_up=w_q_up, g_kv_lat=g_kv_lat, w_kv_up=w_kv_up,
             g_q_mla=g_q_mla, g_k_mla=g_k_mla, w_pool=w_pool, pool_scale=pool_scale, g_mem=g_mem,
             w_mem_kv=w_mem_kv, g_q_x=g_q_x, g_k_x=g_k_x, w_o=w_o, g_ffn=g_ffn, w_gate=w_gate, w_up=w_up,
             conv_w=conv_w, conv_b=conv_b, w_down=w_down)
    Mo = dict(g_mix=m_g_mix, w_in=m_w_in, g_q_lat=m_g_q_lat, w_q_up=m_w_q_up, g_kv_lat=m_g_kv_lat,
              w_kv_up=m_w_kv_up, g_q_mla=m_g_q_mla, g_k_mla=m_g_k_mla, w_pool=m_w_pool,
              pool_scale=m_pool_scale, g_mem=m_g_mem, w_mem_kv=m_w_mem_kv, g_q_x=m_g_q_x, g_k_x=m_g_k_x,
              w_o=m_w_o, g_ffn=m_g_ffn, w_gate=m_w_gate, w_up=m_w_up, conv_w=m_conv_w, conv_b=m_conv_b,
              w_down=m_w_down)
    Vo = dict(g_mix=v_g_mix, w_in=v_w_in, g_q_lat=v_g_q_lat, w_q_up=v_w_q_up, g_kv_lat=v_g_kv_lat,
              w_kv_up=v_w_kv_up, g_q_mla=v_g_q_mla, g_k_mla=v_g_k_mla, w_pool=v_w_pool,
              pool_scale=v_pool_scale, g_mem=v_g_mem, w_mem_kv=v_w_mem_kv, g_q_x=v_g_q_x, g_k_x=v_g_k_x,
              w_o=v_w_o, g_ffn=v_g_ffn, w_gate=v_w_gate, w_up=v_w_up, conv_w=v_conv_w, conv_b=v_conv_b,
              w_down=v_w_down)
    xi, yi, ci = _place()
    me = (2 * xi + yi).astype(jnp.int32).reshape(1)

    shard = {n: _to_compute_layout(n, P[n][0]).astype(BF16) for n in _BIG}
    shard["conv_w"] = conv_w[0]
    gather_groups = {"g1": ("w_in",), "g2": ("w_q_up", "w_kv_up", "w_mem_kv"), "g2b": ("w_o",),
                     "g3": ("w_gate", "w_up", "w_down", "conv_w")}
    gathers = {}

    def landing(n):
        s = shard[n]
        return lax.dynamic_update_slice(lax.empty((N_CHIPS,) + s.shape, s.dtype), s[None], (me[0], 0, 0))

    def start_gather(grp, thru=()):
        names = gather_groups[grp]
        gathers[grp] = _copies_start([shard[n] for n in names], [landing(n) for n in names], _gather_copies, thru,
                                     name="gather_start_" + grp)
        return gathers[grp][5]

    start_gather("g1")
    W = {}
    W["g_q_mla"], W["g_k_mla"] = _pad_gain(g_q_mla), _pad_gain(g_k_mla)
    W["w_pool"] = w_pool[0].astype(BF16)
    for n in ("g_mix", "g_q_lat", "g_kv_lat", "pool_scale", "g_mem", "g_q_x", "g_k_x", "g_ffn", "conv_b"):
        W[n] = P[n]
    W["g_mix"] = _tied(W["g_mix"], gathers["g1"][4])

    def fetch(grp, after):
        stacks = _copies_wait(gathers[grp], after, _gather_copies, name="gather_wait_" + grp)
        if grp == "g1":
            stacks = start_gather("g3", start_gather("g2b", start_gather("g2", stacks)))
        for n, s in zip(gather_groups[grp], stacks):
            if n == "conv_w":
                W[n] = jnp.swapaxes(s, 0, 1).reshape(3, -1)
            else:
                W[n] = s.reshape(-1, s.shape[-1])

    shard_shape = {n: shard[n].shape for n in _BIG}
    scatter_groups = {"s1": ("w_down", "w_gate", "w_up"), "s2": ("w_o",),
                      "s3": ("w_mem_kv", "w_q_up", "w_kv_up", "w_in")}
    scatters = {}

    def parts_of(n, g):
        return g.reshape((N_CHIPS,) + shard_shape[n])

    def ship(grp, G):
        names = scatter_groups[grp]
        srcs = [parts_of(n, G[n][1]) for n in names]
        lands = [lax.empty((N_CHIPS - 1,) + shard_shape[n], BF16) for n in names]
        scatters[grp] = _copies_start(srcs, lands, _scatter_copies, name="scatter_start_" + grp)
        return scatters[grp][4]

    loss_row, grad_x, G = _local_step(x[0], mem[0], loss_target[0], W, fetch, ship)

    recv = {}
    for grp, names in scatter_groups.items():
        for n, r in zip(names, _copies_wait(scatters[grp], grad_x, _scatter_copies, name="scatter_wait_" + grp)):
            recv[n] = r
    part = [_sum4(parts_of(n, G[n][0]), recv[n], me, name="sum4_" + n) for n in _BIG]
    part = [_from_compute_layout(n, p) for n, p in zip(_BIG, part)]
    sib = _swap_with_sibling(part, name="swap_grads")
    out = {}
    for n, p, s in zip(_BIG, part, sib):
        out[n] = [r[None] for r in _adamw(P[n][0], Mo[n][0], Vo[n][0], [p, s], name="adamw_" + n)]

    conv_w_full_grad = G["conv_w"]
    small_g = [G["g_mix"], G["g_q_lat"], G["g_kv_lat"], G["g_q_mla"][:, :QK], G["g_k_mla"][:, :QK], G["w_pool"],
               G["pool_scale"], G["g_mem"], G["g_q_x"], G["g_k_x"], G["g_ffn"], G["conv_b"], conv_w_full_grad]
    packed = _pack(small_g + [loss_row[:, :1]])
    total = _sum8(_gather_all(packed, name="gather_small"), name="sum_small")
    shapes = [a.shape for a in small_g] + [(1, 1)]
    (parts, _) = _unpack(total, shapes)
    loss = parts[-1].reshape(())
    F = conv_b.shape[1]
    fn = F // N_CHIPS
    col0 = (2 * xi + yi) * fn
    sg = dict(zip(_SMALL, parts[:-1]))
    sg["conv_w"] = lax.dynamic_slice(sg["conv_w"], (0, col0), (3, fn))
    sw = [P[n].reshape(sg[n].shape) for n in _SMALL]
    sm = [Mo[n].reshape(sg[n].shape) for n in _SMALL]
    sv = [Vo[n].reshape(sg[n].shape) for n in _SMALL]
    gp = _pack([sg[n] for n in _SMALL])
    res = _adamw(_pack(sw), _pack(sm), _pack(sv), [gp], name="adamw_small")
    sshapes = [sg[n].shape for n in _SMALL]
    for kind, buf in zip(range(4), res):
        vals, _ = _unpack(buf, sshapes)
        for n, val in zip(_SMALL, vals):
            out.setdefault(n, [None] * 4)[kind] = val.reshape(P[n].shape)

    return (loss, grad_x[None], *[out[n][0] for n in _WEIGHTS], *[out[n][1] for n in _WEIGHTS],
            *[out[n][2] for n in _WEIGHTS], *[out[n][3] for n in _WEIGHTS])
```

```python
import functools
import math

import numpy as np
import jax
import jax.numpy as jnp
from jax import lax
from jax.experimental import pallas as pl
from jax.experimental.pallas import tpu as pltpu

F32, BF16 = jnp.float32, jnp.bfloat16
NORM_EPS = 1e-6
ROPE_THETA = 10000.0
V7X_VMEM_LIMIT_BYTES = 48 * 1024 * 1024
N_CHIPS = 4
N_DEV = 8

POOL_W = 512
POOL_WINDOWS = (2, 4, 8, 16)
HEADS = 8
NOPE, ROPE, QK = 128, 64, 192
HEAD_PAD = 256
Q_RANK, KV_RANK = 512, 256
X_HEADS, X_DIM = 4, 128
Z_COLS = 1920
Z_POOL_CB, Z_Q_CB, Z_MQ_CB = 0, 1, 2
Z_KV_CB = 6
Z_KR_CB = 14

ADAM_LR, ADAM_B1, ADAM_B2, ADAM_EPS, ADAM_WD, ADAM_STEP = 0.001, 0.9, 0.999, 1e-08, 0.01, 10

MESH = pl.DeviceIdType.MESH


def _cp(sem):
    return pltpu.CompilerParams(dimension_semantics=sem, vmem_limit_bytes=V7X_VMEM_LIMIT_BYTES)


def _row_tile(S):
    return 256 if S % 256 == 0 and S >= 2048 else 128


def _big_row_tile(S):
    return 512 if S % 512 == 0 and S >= 2048 else _row_tile(S)


def _pick(dim, prefs):
    for p in prefs:
        if dim % p == 0:
            return p
    return dim


_DN = {"nn": (((1,), (0,)), ((), ())), "nt": (((1,), (1,)), ((), ())), "tn": (((0,), (0,)), ((), ()))}


def _mm(a, b, *, mode, grid, blocks, maps, out_shape, out_dtypes, add=None, name):
    nk = grid[2]
    dn = _DN[mode]
    n_out = len(out_dtypes)

    def body(*refs):
        a_ref, b_ref = refs[0], refs[1]
        add_ref = refs[2] if add is not None else None
        p = 2 + (add is not None)
        o_refs = refs[p:p + n_out]

        def finish(r):
            if add_ref is not None:
                r = r + add_ref[...]
            for o in o_refs:
                o[...] = r.astype(o.dtype)

        def product():
            return lax.dot_general(a_ref[...].astype(BF16), b_ref[...].astype(BF16), dn, preferred_element_type=F32)

        if nk == 1:
            finish(product())
            return
        acc = refs[p + n_out]
        k = pl.program_id(2)

        @pl.when(k == 0)
        def _():
            acc[...] = jnp.zeros_like(acc)

        acc[...] += product()

        @pl.when(k == nk - 1)
        def _():
            finish(acc[...])

    a_blk, b_blk, o_blk = blocks
    a_map, b_map, o_map = maps
    in_specs = [pl.BlockSpec(a_blk, a_map), pl.BlockSpec(b_blk, b_map)]
    args = [a, b]
    if add is not None:
        in_specs.append(pl.BlockSpec(o_blk, o_map))
        args.append(add)
    outs = pl.pallas_call(
        body, grid=grid, in_specs=in_specs,
        out_specs=[pl.BlockSpec(o_blk, o_map) for _ in out_dtypes],
        out_shape=[jax.ShapeDtypeStruct(out_shape, d) for d in out_dtypes],
        scratch_shapes=[pltpu.VMEM(o_blk, F32)] if nk > 1 else [],
        compiler_params=_cp(("parallel", "parallel", "arbitrary")), name=name)(*args)
    return outs[0] if n_out == 1 else outs


def mm_nn(a, w, *, nsh=1, tm, tn, tk, out_dtypes=(F32,), add=None, name):
    M, K = a.shape
    n = w.shape[1]
    N = nsh * n
    assert w.shape[0] == nsh * K and n % tn == 0 and K % tk == 0 and M % tm == 0
    npt, kt = n // tn, K // tk
    return _mm(a, w, mode="nn", grid=(M // tm, N // tn, kt),
               blocks=((tm, tk), (tk, tn), (tm, tn)),
               maps=(lambda i, j, k: (i, k), lambda i, j, k: ((j // npt) * kt + k, j % npt),
                     lambda i, j, k: (i, j)),
               out_shape=(M, N), out_dtypes=out_dtypes, add=add, name=name)


def mm_nt(d, w, *, nsh=1, tm, to, tc, out_dtypes=(F32,), add=None, name):
    M, N = d.shape
    n = w.shape[1]
    K = w.shape[0] // nsh
    assert nsh * n == N and n % tc == 0 and K % to == 0 and M % tm == 0
    cpt, ot = n // tc, K // to
    return _mm(d, w, mode="nt", grid=(M // tm, ot, N // tc),
               blocks=((tm, tc), (to, tc), (tm, to)),
               maps=(lambda i, j, c: (i, c), lambda i, j, c: ((c // cpt) * ot + j, c % cpt),
                     lambda i, j, c: (i, j)),
               out_shape=(M, K), out_dtypes=out_dtypes, add=add, name=name)


def mm_nt_pair(d1, w1, d2, w2, *, nsh, tm, to, tc, out_dtype, name):
    M, N = d1.shape
    n = w1.shape[1]
    K = w1.shape[0] // nsh
    assert d2.shape == d1.shape and w2.shape == w1.shape and nsh * n == N
    assert n % tc == 0 and K % to == 0 and M % tm == 0
    cpt, ot, nk = n // tc, K // to, N // tc

    def body(a1_ref, b1_ref, a2_ref, b2_ref, o_ref, acc):
        k = pl.program_id(2)

        @pl.when(k == 0)
        def _():
            acc[...] = jnp.zeros_like(acc)

        acc[...] += lax.dot_general(a1_ref[...], b1_ref[...], _DN["nt"], preferred_element_type=F32)
        acc[...] += lax.dot_general(a2_ref[...], b2_ref[...], _DN["nt"], preferred_element_type=F32)

        @pl.when(k == nk - 1)
        def _():
            o_ref[...] = acc[...].astype(o_ref.dtype)

    a_spec = pl.BlockSpec((tm, tc), lambda i, j, c: (i, c))
    b_spec = pl.BlockSpec((to, tc), lambda i, j, c: ((c // cpt) * ot + j, c % cpt))
    return pl.pallas_call(
        body, grid=(M // tm, ot, nk), in_specs=[a_spec, b_spec, a_spec, b_spec],
        out_specs=pl.BlockSpec((tm, to), lambda i, j, c: (i, j)),
        out_shape=jax.ShapeDtypeStruct((M, K), out_dtype),
        scratch_shapes=[pltpu.VMEM((tm, to), F32)],
        compiler_params=_cp(("parallel", "parallel", "arbitrary")), name=name)(d1, w1, d2, w2)


def mm_tn(x, d, *, nsh=1, to, tn, tk, out_dtypes=(F32,), name):
    M, K = x.shape
    N = d.shape[1]
    n = N // nsh
    assert n % tn == 0 and K % to == 0 and M % tk == 0
    npt, ot = n // tn, K // to
    return _mm(x, d, mode="tn", grid=(ot, N // tn, M // tk),
               blocks=((tk, to), (tk, tn), (to, tn)),
               maps=(lambda i, j, k: (k, i), lambda i, j, k: (k, j),
                     lambda i, j, k: ((j // npt) * ot + i, j % npt)),
               out_shape=(nsh * K, n), out_dtypes=out_dtypes, name=name)


def _rms_fwd(x, g, *, C, cb=0, name):
    S = x.shape[0]
    tm = _big_row_tile(S) if S >= 128 else S

    def body(x_ref, g_ref, o_ref):
        xv = x_ref[...]
        r = lax.rsqrt(jnp.mean(xv * xv, axis=-1, keepdims=True) + NORM_EPS)
        o_ref[...] = ((xv * r) * g_ref[...]).astype(o_ref.dtype)

    return pl.pallas_call(
        body, grid=(S // tm,),
        in_specs=[pl.BlockSpec((tm, C), lambda i: (i, cb)), pl.BlockSpec((1, C), lambda i: (0, 0))],
        out_specs=pl.BlockSpec((tm, C), lambda i: (i, 0)),
        out_shape=jax.ShapeDtypeStruct((S, C), BF16),
        compiler_params=_cp(("parallel",)), name=name)(x, g)


def _rms_bwd(x, dh, g, *, C, cb=0, res=None, out_dtypes=(F32,), name):
    S = x.shape[0]
    tm = (_big_row_tile(S) if C <= 512 else _row_tile(S)) if S >= 128 else S
    n_out = len(out_dtypes)

    def body(*refs):
        x_ref, dh_ref, g_ref = refs[:3]
        res_ref = refs[3] if res is not None else None
        p = 3 + (res is not None)
        outs = refs[p:p + n_out]
        dg_ref = refs[p + n_out]
        i = pl.program_id(0)
        xv = x_ref[...]
        r = lax.rsqrt(jnp.mean(xv * xv, axis=-1, keepdims=True) + NORM_EPS)
        n = xv * r
        dhv = dh_ref[...].astype(F32)
        dn = dhv * g_ref[...]
        c = jnp.mean(dn * n, axis=-1, keepdims=True)
        dx = r * (dn - n * c)
        if res_ref is not None:
            dx = res_ref[...] + dx
        for o in outs:
            o[...] = dx.astype(o.dtype)

        @pl.when(i == 0)
        def _():
            dg_ref[...] = jnp.zeros_like(dg_ref)

        dg_ref[...] += jnp.sum(dhv * n, axis=0, keepdims=True)

    row = pl.BlockSpec((tm, C), lambda i: (i, 0))
    in_specs = [pl.BlockSpec((tm, C), lambda i: (i, cb)), row, pl.BlockSpec((1, C), lambda i: (0, 0))]
    args = [x, dh, g]
    if res is not None:
        in_specs.append(row)
        args.append(res)
    return pl.pallas_call(
        body, grid=(S // tm,), in_specs=in_specs,
        out_specs=[row] * n_out + [pl.BlockSpec((1, C), lambda i: (0, 0))],
        out_shape=[jax.ShapeDtypeStruct((S, C), d) for d in out_dtypes] + [jax.ShapeDtypeStruct((1, C), F32)],
        compiler_params=_cp(("arbitrary",)), name=name)(*args)


def _pool_cnt(t0, rows, w):
    t = t0 + lax.broadcasted_iota(jnp.int32, (rows, 1), 0)
    return jnp.minimum(t + 1, w).astype(F32)


def _pool_d(halo, tile, gi, t0, tm):
    s = jnp.concatenate([halo, tile], axis=0)
    for step in (1, 2, 4, 8)[:gi + 1]:
        s = s + pltpu.roll(s, step, 0)
    return s[16:] / _pool_cnt(t0, tm, POOL_WINDOWS[gi]) - tile


def _pool_fwd(z, w_pool, pool_scale, *, name):
    S = z.shape[0]
    tm = _row_tile(S)
    hb = tm // 16

    def body(z_ref, h_ref, w_ref, sc_ref, o_ref):
        i = pl.program_id(0)
        halo = h_ref[...] * (i > 0).astype(F32)
        for gi in range(4):
            cs = slice(gi * 128, (gi + 1) * 128)
            d = _pool_d(halo[:, cs], z_ref[:, cs], gi, i * tm, tm)
            yp = jnp.dot(d.astype(BF16), w_ref[gi], preferred_element_type=F32)
            o_ref[:, cs] = (yp * sc_ref[:, cs]).astype(o_ref.dtype)

    return pl.pallas_call(
        body, grid=(S // tm,),
        in_specs=[pl.BlockSpec((tm, POOL_W), lambda i: (i, Z_POOL_CB)),
                  pl.BlockSpec((16, POOL_W), lambda i: (jnp.maximum(i * hb - 1, 0), Z_POOL_CB)),
                  pl.BlockSpec((4, 128, 128), lambda i: (0, 0, 0)),
                  pl.BlockSpec((1, POOL_W), lambda i: (0, 0))],
        out_specs=pl.BlockSpec((tm, POOL_W), lambda i: (i, 0)),
        out_shape=jax.ShapeDtypeStruct((S, POOL_W), BF16),
        compiler_params=_cp(("parallel",)), name=name)(z, z, w_pool, pool_scale)


def _pool_bwd(z, d_cat, w_pool, pool_scale, *, name):
    S = z.shape[0]
    tm = _row_tile(S)
    hb = tm // 16
    nt = S // tm
    E = tm + 16

    def body(z_ref, h_ref, dy_ref, dyn_ref, w_ref, sc_ref, dz_ref, gw_ref, gs_ref):
        i = pl.program_id(0)

        @pl.when(i == 0)
        def _():
            gw_ref[...] = jnp.zeros_like(gw_ref)
            gs_ref[...] = jnp.zeros_like(gs_ref)

        halo = h_ref[...] * (i > 0).astype(F32)
        dy_next = dyn_ref[...].astype(F32) * (i < nt - 1).astype(F32)
        for gi in range(4):
            cs = slice(gi * 128, (gi + 1) * 128)
            w = w_ref[gi]
            d = _pool_d(halo[:, cs], z_ref[:, cs], gi, i * tm, tm)
            db = d.astype(BF16)
            dy = dy_ref[:, cs].astype(F32)
            yp = jnp.dot(db, w, preferred_element_type=F32)
            gs_ref[:, cs] += jnp.sum(dy * yp, axis=0, keepdims=True)
            sc = sc_ref[:, cs]
            dys = (dy * sc).astype(BF16)
            gw_ref[gi] += lax.dot_general(db, dys, _DN["tn"], preferred_element_type=F32)
            dys_ext = jnp.concatenate([dys, (dy_next[:, cs] * sc).astype(BF16)], axis=0)
            dd = lax.dot_general(dys_ext, w, _DN["nt"], preferred_element_type=F32)
            r = dd / _pool_cnt(i * tm, E, POOL_WINDOWS[gi])
            for step in (1, 2, 4, 8)[:gi + 1]:
                r = r + pltpu.roll(r, E - step, 0)
            dz_ref[:, cs] = (r[:tm] - dd[:tm]).astype(dz_ref.dtype)

    return pl.pallas_call(
        body, grid=(nt,),
        in_specs=[pl.BlockSpec((tm, POOL_W), lambda i: (i, Z_POOL_CB)),
                  pl.BlockSpec((16, POOL_W), lambda i: (jnp.maximum(i * hb - 1, 0), Z_POOL_CB)),
                  pl.BlockSpec((tm, POOL_W), lambda i: (i, 0)),
                  pl.BlockSpec((16, POOL_W), lambda i: (jnp.minimum((i + 1) * hb, S // 16 - 1), 0)),
                  pl.BlockSpec((4, 128, 128), lambda i: (0, 0, 0)),
                  pl.BlockSpec((1, POOL_W), lambda i: (0, 0))],
        out_specs=[pl.BlockSpec((tm, POOL_W), lambda i: (i, 0)),
                   pl.BlockSpec((4, 128, 128), lambda i: (0, 0, 0)),
                   pl.BlockSpec((1, POOL_W), lambda i: (0, 0))],
        out_shape=[jax.ShapeDtypeStruct((S, POOL_W), BF16),
                   jax.ShapeDtypeStruct((4, 128, 128), F32),
                   jax.ShapeDtypeStruct((1, POOL_W), F32)],
        compiler_params=_cp(("arbitrary",)), name=name)(z, z, d_cat, d_cat, w_pool, pool_scale)


def _rope_tables(S):
    half = ROPE // 2
    inv_freq = 1.0 / (ROPE_THETA ** (jnp.arange(half, dtype=F32) / half))
    ang = jnp.arange(S).astype(F32)[:, None] * inv_freq[None, :]
    cos, sin = jnp.cos(ang), jnp.sin(ang)
    zero = jnp.zeros((S, half), F32)
    cos_t = jnp.concatenate([cos, cos, zero, zero], axis=1)
    sa_t = jnp.concatenate([-sin, zero, zero, zero], axis=1)
    sb_t = jnp.concatenate([zero, sin, zero, zero], axis=1)
    return cos_t, sa_t, sb_t


def _head_fwd(xn, xr, gn, gr, cos, sa, sb):
    ms = (jnp.sum(xn * xn, axis=-1, keepdims=True) + jnp.sum(xr * xr, axis=-1, keepdims=True)) * (1.0 / QK)
    r = lax.rsqrt(ms + NORM_EPS)
    on = (xn * r) * gn
    yr = (xr * r) * gr
    orr = yr * cos + pltpu.roll(yr, 96, 1) * sa + pltpu.roll(yr, 32, 1) * sb
    return on, orr


def _head_bwd(xn, xr, gn, gr, don, dor, cos, sa, sb):
    ms = (jnp.sum(xn * xn, axis=-1, keepdims=True) + jnp.sum(xr * xr, axis=-1, keepdims=True)) * (1.0 / QK)
    r = lax.rsqrt(ms + NORM_EPS)
    nn, nr = xn * r, xr * r
    dyr = dor * cos + pltpu.roll(dor * sa, 32, 1) + pltpu.roll(dor * sb, 96, 1)
    ggn, ggr = don * nn, dyr * nr
    dnn, dnr = don * gn, dyr * gr
    c = (jnp.sum(dnn * nn, axis=-1, keepdims=True) + jnp.sum(dnr * nr, axis=-1, keepdims=True)) * (1.0 / QK)
    return r * (dnn - nn * c), r * (dnr - nr * c), ggn, ggr


def _kv_cols(h):
    base = (h // 2) * 512 + (h % 2) * 128
    return base, base + 256


def _qkrope_fwd(qraw, kvraw, z, gq, gk, tabs, *, name):
    S = qraw.shape[0]
    tm = _row_tile(S)

    def body(q_ref, kv_ref, zkr_ref, gq_ref, gk_ref, cos_ref, sa_ref, sb_ref, qo_ref, ko_ref, vo_ref, vt_ref):
        rope = (cos_ref[...], sa_ref[...], sb_ref[...])
        zkr = zkr_ref[...]
        gqn, gqr, gkn, gkr = gq_ref[:, :128], gq_ref[:, 128:], gk_ref[:, :128], gk_ref[:, 128:]
        for h in range(HEADS):
            b = h * HEAD_PAD
            on, orr = _head_fwd(q_ref[:, b:b + 128], q_ref[:, b + 128:b + 256], gqn, gqr, *rope)
            qo_ref[:, b:b + 128] = on.astype(BF16)
            qo_ref[:, b + 128:b + 256] = orr.astype(BF16)
            kc, vc = _kv_cols(h)
            on, orr = _head_fwd(kv_ref[:, kc:kc + 128], zkr, gkn, gkr, *rope)
            ko_ref[:, b:b + 128] = on.astype(BF16)
            ko_ref[:, b + 128:b + 256] = orr.astype(BF16)
            vv = kv_ref[:, vc:vc + 128]
            vo_ref[:, h * 128:(h + 1) * 128] = vv.astype(BF16)
            vt_ref[h * 128:(h + 1) * 128, :] = jnp.transpose(vv).astype(BF16)

    W = HEADS * HEAD_PAD
    row = lambda c: pl.BlockSpec((tm, c), lambda i: (i, 0))
    vec = lambda c: pl.BlockSpec((1, c), lambda i: (0, 0))
    return pl.pallas_call(
        body, grid=(S // tm,),
        in_specs=[row(W), row(W), pl.BlockSpec((tm, 128), lambda i: (i, Z_KR_CB)), vec(256), vec(256),
                  row(128), row(128), row(128)],
        out_specs=[row(W), row(W), row(HEADS * 128), pl.BlockSpec((HEADS * 128, tm), lambda i: (0, i))],
        out_shape=[jax.ShapeDtypeStruct((S, W), BF16), jax.ShapeDtypeStruct((S, W), BF16),
                   jax.ShapeDtypeStruct((S, HEADS * 128), BF16), jax.ShapeDtypeStruct((HEADS * 128, S), BF16)],
        compiler_params=_cp(("parallel",)), name=name)(qraw, kvraw, z, gq, gk, *tabs)


def _qkrope_bwd(qraw, kvraw, z, gq, gk, tabs, dq, dk, dv, *, name):
    S = qraw.shape[0]
    tm = _row_tile(S)

    def body(q_ref, kv_ref, zkr_ref, gq_ref, gk_ref, cos_ref, sa_ref, sb_ref, dq_ref, dk_ref, dv_ref,
             dqo_ref, dkvo_ref, dkr_ref, ggq_ref, ggk_ref):
        i = pl.program_id(0)

        @pl.when(i == 0)
        def _():
            ggq_ref[...] = jnp.zeros_like(ggq_ref)
            ggk_ref[...] = jnp.zeros_like(ggk_ref)

        rope = (cos_ref[...], sa_ref[...], sb_ref[...])
        zkr = zkr_ref[...]
        gqn, gqr, gkn, gkr = gq_ref[:, :128], gq_ref[:, 128:], gk_ref[:, :128], gk_ref[:, 128:]
        dkr = jnp.zeros((tm, 128), F32)
        sq_n = jnp.zeros((1, 128), F32)
        sq_r = jnp.zeros((1, 128), F32)
        sk_n = jnp.zeros((1, 128), F32)
        sk_r = jnp.zeros((1, 128), F32)
        for h in range(HEADS):
            b = h * HEAD_PAD
            dxn, dxr, ggn, ggr = _head_bwd(q_ref[:, b:b + 128], q_ref[:, b + 128:b + 256], gqn, gqr,
                                           dq_ref[:, b:b + 128], dq_ref[:, b + 128:b + 256], *rope)
            dqo_ref[:, b:b + 128] = dxn.astype(BF16)
            dqo_ref[:, b + 128:b + 256] = dxr.astype(BF16)
            sq_n += jnp.sum(ggn, axis=0, keepdims=True)
            sq_r += jnp.sum(ggr, axis=0, keepdims=True)
            kc, vc = _kv_cols(h)
            dxn, dxr, ggn, ggr = _head_bwd(kv_ref[:, kc:kc + 128], zkr, gkn, gkr,
                                           dk_ref[:, b:b + 128], dk_ref[:, b + 128:b + 256], *rope)
            dkvo_ref[:, kc:kc + 128] = dxn.astype(BF16)
            dkvo_ref[:, vc:vc + 128] = dv_ref[:, h * 128:(h + 1) * 128].astype(BF16)
            dkr += dxr
            sk_n += jnp.sum(ggn, axis=0, keepdims=True)
            sk_r += jnp.sum(ggr, axis=0, keepdims=True)
        dkr_ref[...] = dkr.astype(BF16)
        ggq_ref[:, :128] += sq_n
        ggq_ref[:, 128:] += sq_r
        ggk_ref[:, :128] += sk_n
        ggk_ref[:, 128:] += sk_r

    W = HEADS * HEAD_PAD
    row = lambda c: pl.BlockSpec((tm, c), lambda i: (i, 0))
    vec = lambda c: pl.BlockSpec((1, c), lambda i: (0, 0))
    return pl.pallas_call(
        body, grid=(S // tm,),
        in_specs=[row(W), row(W), pl.BlockSpec((tm, 128), lambda i: (i, Z_KR_CB)), vec(256), vec(256),
                  row(128), row(128), row(128), row(W), row(W), row(HEADS * 128)],
        out_specs=[row(W), row(W), row(128), vec(256), vec(256)],
        out_shape=[jax.ShapeDtypeStruct((S, W), BF16), jax.ShapeDtypeStruct((S, W), BF16),
                   jax.ShapeDtypeStruct((S, 128), BF16),
                   jax.ShapeDtypeStruct((1, 256), F32), jax.ShapeDtypeStruct((1, 256), F32)],
        compiler_params=_cp(("arbitrary",)), name=name)(qraw, kvraw, z, gq, gk, *tabs, dq, dk, dv)


LOG2E = 1.4426950408889634
SCORE_SCALE = 1.0 / math.sqrt(QK)
SCORE_SCALE_LOG2 = SCORE_SCALE * LOG2E


def _fa_tile(S):
    return 512 if S % 512 == 0 and S >= 2048 else 128


def _flash_fwd(q, k, vt, *, name):
    S = q.shape[0]
    ts = _fa_tile(S)
    tq = 2 * ts

    def body(q_ref, k_ref, vt_ref, o_ref, ob_ref, lse_ref, m_sc, l_sc, acc_sc, s_buf):
        qi = pl.program_id(1)
        m_sc[...] = jnp.full_like(m_sc, -jnp.inf)
        l_sc[...] = jnp.zeros_like(l_sc)
        acc_sc[...] = jnp.zeros_like(acc_sc)
        qb = q_ref[...]

        def scores(kidx):
            k0 = pl.multiple_of(kidx * ts, ts)
            return lax.dot_general(k_ref[pl.ds(k0, ts), :], qb, _DN["nt"], preferred_element_type=F32)

        def causal(st, j):
            key = lax.broadcasted_iota(jnp.int32, (ts, tq), 0) + j * ts
            return jnp.where(key > lax.broadcasted_iota(jnp.int32, (ts, tq), 1), -jnp.inf, st)

        def update(st, kidx):
            k0 = pl.multiple_of(kidx * ts, ts)
            m_prev = m_sc[...]
            m_new = jnp.maximum(m_prev, jnp.max(st, axis=0, keepdims=True))
            alpha = jnp.exp2((m_prev - m_new) * SCORE_SCALE_LOG2)
            pt = jnp.exp2((st - m_new[0:1, :]) * SCORE_SCALE_LOG2)
            l_sc[...] = alpha * l_sc[...] + jnp.sum(pt, axis=0, keepdims=True)
            acc_sc[...] = alpha[0:1, :] * acc_sc[...] + jnp.dot(vt_ref[:, pl.ds(k0, ts)], pt.astype(BF16),
                                                                preferred_element_type=F32)
            m_sc[...] = m_new

        s_buf[0] = scores(0)

        def trip(u, carry):
            s_buf[1] = scores(2 * u + 1)
            update(s_buf[0], 2 * u)
            s_buf[0] = scores(2 * u + 2)
            update(s_buf[1], 2 * u + 1)
            return carry

        lax.fori_loop(0, qi, trip, 0)
        s_buf[1] = scores(2 * qi + 1)
        update(causal(s_buf[0], 0), 2 * qi)
        update(causal(s_buf[1], 1), 2 * qi + 1)
        ot = acc_sc[...] / l_sc[0:1, :]
        o = jnp.transpose(ot)
        o_ref[...] = o
        ob_ref[...] = o.astype(BF16)
        lse_ref[...] = m_sc[...] * SCORE_SCALE_LOG2 + jnp.log2(l_sc[...])

    return pl.pallas_call(
        body, grid=(HEADS, S // tq),
        in_specs=[pl.BlockSpec((tq, HEAD_PAD), lambda h, i: (i, h)),
                  pl.BlockSpec((S, HEAD_PAD), lambda h, i: (0, h)),
                  pl.BlockSpec((128, S), lambda h, i: (h, 0))],
        out_specs=[pl.BlockSpec((tq, 128), lambda h, i: (i, h)),
                   pl.BlockSpec((tq, 128), lambda h, i: (i, h)),
                   pl.BlockSpec((None, 8, tq), lambda h, i: (h, 0, i))],
        out_shape=[jax.ShapeDtypeStruct((S, HEADS * 128), F32), jax.ShapeDtypeStruct((S, HEADS * 128), BF16),
                   jax.ShapeDtypeStruct((HEADS, 8, S), F32)],
        scratch_shapes=[pltpu.VMEM((8, tq), F32), pltpu.VMEM((8, tq), F32), pltpu.VMEM((128, tq), F32),
                        pltpu.VMEM((2, ts, tq), F32)],
        compiler_params=_cp(("parallel", "arbitrary")), name=name)(q, k, vt)


def _attn_bwd_prep(o, d_cat, *, name):
    S = o.shape[0]
    tm = _row_tile(S)
    H = HEADS * 128
    half = H // 2

    def body(o_ref, da_ref, db_ref, delta_ref):
        for h in range(HEADS):
            src, c0 = (da_ref, h * 128) if h * 128 < half else (db_ref, h * 128 - half)
            do = src[:, c0:c0 + 128].astype(F32)
            prod = jnp.transpose(do * o_ref[:, h * 128:(h + 1) * 128])
            delta_ref[h] = jnp.broadcast_to(jnp.sum(prod, axis=0, keepdims=True), (8, tm))

    return pl.pallas_call(
        body, grid=(S // tm,),
        in_specs=[pl.BlockSpec((tm, H), lambda i: (i, 0)),
                  pl.BlockSpec((tm, half), lambda i: (i, 1)), pl.BlockSpec((tm, half), lambda i: (i, 2))],
        out_specs=pl.BlockSpec((HEADS, 8, tm), lambda i: (0, 0, i)),
        out_shape=jax.ShapeDtypeStruct((HEADS, 8, S), F32),
        compiler_params=_cp(("parallel",)), name=name)(o, d_cat, d_cat)


def _flash_bwd(q, k, v, d_cat, lse, delta, *, name):
    S = q.shape[0]
    ts = _fa_tile(S)
    nb = S // ts

    def body(q_ref, do_ref, lse_ref, delta_ref, k_ref, v_ref, dq_ref, dk_ref, dv_ref, dk_sc, dv_sc):
        j = pl.program_id(1)

        @pl.when(j == 0)
        def _():
            dq_ref[...] = jnp.zeros_like(dq_ref)

        dk_sc[...] = jnp.zeros_like(dk_sc)
        dv_sc[...] = jnp.zeros_like(dv_sc)
        kb, vb = k_ref[...], v_ref[...]

        def products(i):
            q0 = pl.multiple_of(i * ts, ts)
            qb = q_ref[pl.ds(q0, ts), :]
            dob_ = do_ref[pl.ds(q0, ts), :]
            st = lax.dot_general(kb, qb, _DN["nt"], preferred_element_type=F32)
            dpt = lax.dot_general(vb, dob_, _DN["nt"], preferred_element_type=F32)
            return q0, qb, dob_, st, dpt

        def accumulate(q0, qb, dob_, st, dpt, masked):
            pt = jnp.exp2(st * SCORE_SCALE_LOG2 - lse_ref[0:1, pl.ds(q0, ts)])
            if masked:
                pt = jnp.where(lax.broadcasted_iota(jnp.int32, (ts, ts), 0) > lax.broadcasted_iota(jnp.int32, (ts, ts), 1),
                               0.0, pt)
            dv_sc[...] += jnp.dot(pt.astype(BF16), dob_, preferred_element_type=F32)
            dst = (pt * (dpt - delta_ref[0:1, pl.ds(q0, ts)])).astype(BF16)
            dk_sc[...] += jnp.dot(dst, qb, preferred_element_type=F32) * SCORE_SCALE
            dq_ref[pl.ds(q0, ts), :] += lax.dot_general(dst, kb, _DN["tn"], preferred_element_type=F32) * SCORE_SCALE

        accumulate(*products(j), True)
        n_below = nb - 1 - j

        def pair(t, carry):
            a, b = products(j + 1 + 2 * t), products(j + 2 + 2 * t)
            accumulate(*a, False)
            accumulate(*b, False)
            return carry

        lax.fori_loop(0, n_below // 2, pair, 0)

        @pl.when(n_below % 2 == 1)
        def _():
            accumulate(*products(nb - 1), False)

        dk_ref[...] = dk_sc[...]
        dv_ref[...] = dv_sc[...]

    return pl.pallas_call(
        body, grid=(HEADS, nb),
        in_specs=[pl.BlockSpec((S, HEAD_PAD), lambda h, j: (0, h)),
                  pl.BlockSpec((S, 128), lambda h, j: (0, 4 + h)),
                  pl.BlockSpec((None, 8, S), lambda h, j: (h, 0, 0)),
                  pl.BlockSpec((None, 8, S), lambda h, j: (h, 0, 0)),
                  pl.BlockSpec((ts, HEAD_PAD), lambda h, j: (j, h)),
                  pl.BlockSpec((ts, 128), lambda h, j: (j, h))],
        out_specs=[pl.BlockSpec((S, HEAD_PAD), lambda h, j: (0, h)),
                   pl.BlockSpec((ts, HEAD_PAD), lambda h, j: (j, h)),
                   pl.BlockSpec((ts, 128), lambda h, j: (j, h))],
        out_shape=[jax.ShapeDtypeStruct((S, HEADS * HEAD_PAD), F32), jax.ShapeDtypeStruct((S, HEADS * HEAD_PAD), F32),
                   jax.ShapeDtypeStruct((S, HEADS * 128), F32)],
        scratch_shapes=[pltpu.VMEM((ts, HEAD_PAD), F32), pltpu.VMEM((ts, 128), F32)],
        compiler_params=_cp(("parallel", "arbitrary")), name=name)(q, d_cat, lse, delta, k, v)


def _memk_fwd(mkv, gkx, *, name):
    M = mkv.shape[0]
    XW = X_HEADS * X_DIM

    def body(mkv_ref, g_ref, k_ref, v_ref):
        for h in range(X_HEADS):
            cs = slice(h * X_DIM, (h + 1) * X_DIM)
            xv = mkv_ref[:, cs]
            r = lax.rsqrt(jnp.mean(xv * xv, axis=-1, keepdims=True) + NORM_EPS)
            k_ref[:, cs] = ((xv * r) * g_ref[...]).astype(BF16)
        v_ref[...] = mkv_ref[:, XW:].astype(BF16)

    return pl.pallas_call(
        body, grid=(1,),
        in_specs=[pl.BlockSpec((M, 2 * XW), lambda i: (0, 0)), pl.BlockSpec((1, X_DIM), lambda i: (0, 0))],
        out_specs=[pl.BlockSpec((M, XW), lambda i: (0, 0)), pl.BlockSpec((M, XW), lambda i: (0, 0))],
        out_shape=[jax.ShapeDtypeStruct((M, XW), BF16), jax.ShapeDtypeStruct((M, XW), BF16)],
        compiler_params=_cp(("arbitrary",)), name=name)(mkv, gkx)


def _memk_bwd(mkv, gkx, dk, dv, *, name):
    M = mkv.shape[0]
    XW = X_HEADS * X_DIM

    def body(mkv_ref, g_ref, dk_ref, dv_ref, o_ref, gg_ref):
        gg = jnp.zeros((1, X_DIM), F32)
        for h in range(X_HEADS):
            cs = slice(h * X_DIM, (h + 1) * X_DIM)
            xv = mkv_ref[:, cs]
            r = lax.rsqrt(jnp.mean(xv * xv, axis=-1, keepdims=True) + NORM_EPS)
            n = xv * r
            dkv = dk_ref[:, cs]
            gg += jnp.sum(dkv * n, axis=0, keepdims=True)
            dn = dkv * g_ref[...]
            c = jnp.mean(dn * n, axis=-1, keepdims=True)
            o_ref[:, cs] = (r * (dn - n * c)).astype(BF16)
        o_ref[:, XW:] = dv_ref[...].astype(BF16)
        gg_ref[...] = gg

    full = lambda c: pl.BlockSpec((M, c), lambda i: (0, 0))
    return pl.pallas_call(
        body, grid=(1,),
        in_specs=[full(2 * XW), pl.BlockSpec((1, X_DIM), lambda i: (0, 0)), full(XW), full(XW)],
        out_specs=[full(2 * XW), pl.BlockSpec((1, X_DIM), lambda i: (0, 0))],
        out_shape=[jax.ShapeDtypeStruct((M, 2 * XW), BF16), jax.ShapeDtypeStruct((1, X_DIM), F32)],
        compiler_params=_cp(("arbitrary",)), name=name)(mkv, gkx, dk, dv)


def _xq_norm(z_ref, g_ref, h):
    xv = z_ref[:, h * X_DIM:(h + 1) * X_DIM]
    r = lax.rsqrt(jnp.mean(xv * xv, axis=-1, keepdims=True) + NORM_EPS)
    n = xv * r
    return n, r, n * g_ref[...]


def _xprobs(qb, k_ref, h):
    s = lax.dot_general(qb, k_ref[:, h * X_DIM:(h + 1) * X_DIM], _DN["nt"],
                        preferred_element_type=F32) * (1.0 / math.sqrt(X_DIM))
    e = jnp.exp(s - jnp.max(s, axis=-1, keepdims=True))
    return e / jnp.sum(e, axis=-1, keepdims=True)


def _memattn_fwd(z, kx, vx, gqx, *, name):
    S = z.shape[0]
    M = kx.shape[0]
    tm = _big_row_tile(S)
    XW = X_HEADS * X_DIM

    def body(z_ref, k_ref, v_ref, g_ref, o_ref):
        for h in range(X_HEADS):
            cs = slice(h * X_DIM, (h + 1) * X_DIM)
            _, _, qn = _xq_norm(z_ref, g_ref, h)
            p = _xprobs(qn.astype(BF16), k_ref, h)
            o_ref[:, cs] = jnp.dot(p.astype(BF16), v_ref[:, cs], preferred_element_type=F32).astype(BF16)

    return pl.pallas_call(
        body, grid=(S // tm,),
        in_specs=[pl.BlockSpec((tm, XW), lambda i: (i, Z_MQ_CB)), pl.BlockSpec((M, XW), lambda i: (0, 0)),
                  pl.BlockSpec((M, XW), lambda i: (0, 0)), pl.BlockSpec((1, X_DIM), lambda i: (0, 0))],
        out_specs=pl.BlockSpec((tm, XW), lambda i: (i, 0)),
        out_shape=jax.ShapeDtypeStruct((S, XW), BF16),
        compiler_params=_cp(("parallel",)), name=name)(z, kx, vx, gqx)


def _memattn_bwd(z, kx, vx, gqx, d_cat, *, name):
    S = z.shape[0]
    M = kx.shape[0]
    tm = _big_row_tile(S)
    XW = X_HEADS * X_DIM
    scale = 1.0 / math.sqrt(X_DIM)

    def body(z_ref, k_ref, v_ref, g_ref, do_ref, dz_ref, dk_ref, dv_ref, gg_ref):
        i = pl.program_id(0)

        @pl.when(i == 0)
        def _():
            dk_ref[...] = jnp.zeros_like(dk_ref)
            dv_ref[...] = jnp.zeros_like(dv_ref)
            gg_ref[...] = jnp.zeros_like(gg_ref)

        gg = jnp.zeros((1, X_DIM), F32)
        for h in range(X_HEADS):
            cs = slice(h * X_DIM, (h + 1) * X_DIM)
            n, r, qn = _xq_norm(z_ref, g_ref, h)
            qb = qn.astype(BF16)
            p = _xprobs(qb, k_ref, h)
            pb = p.astype(BF16)
            dob = do_ref[:, cs].astype(BF16)
            dv_ref[:, cs] += lax.dot_general(pb, dob, _DN["tn"], preferred_element_type=F32)
            dp = lax.dot_general(dob, v_ref[:, cs], _DN["nt"], preferred_element_type=F32)
            ds = (p * (dp - jnp.sum(dp * p, axis=-1, keepdims=True))).astype(BF16)
            dk_ref[:, cs] += lax.dot_general(ds, qb, _DN["tn"], preferred_element_type=F32) * scale
            dqn = jnp.dot(ds, k_ref[:, cs], preferred_element_type=F32) * scale
            gg += jnp.sum(dqn * n, axis=0, keepdims=True)
            dn = dqn * g_ref[...]
            c = jnp.mean(dn * n, axis=-1, keepdims=True)
            dz_ref[:, cs] = (r * (dn - n * c)).astype(BF16)
        gg_ref[...] += gg

    full = pl.BlockSpec((M, XW), lambda i: (0, 0))
    vec = pl.BlockSpec((1, X_DIM), lambda i: (0, 0))
    return pl.pallas_call(
        body, grid=(S // tm,),
        in_specs=[pl.BlockSpec((tm, XW), lambda i: (i, Z_MQ_CB)), full, full, vec,
                  pl.BlockSpec((tm, XW), lambda i: (i, 3))],
        out_specs=[pl.BlockSpec((tm, XW), lambda i: (i, 0)), full, full, vec],
        out_shape=[jax.ShapeDtypeStruct((S, XW), BF16), jax.ShapeDtypeStruct((M, XW), F32),
                   jax.ShapeDtypeStruct((M, XW), F32), jax.ShapeDtypeStruct((1, X_DIM), F32)],
        compiler_params=_cp(("arbitrary",)), name=name)(z, kx, vx, gqx, d_cat)


def _silu_parts(x):
    h = 0.5 * x
    return h, jnp.tanh(h)


GLU_HALO = 16


def _glu_tiles(S, F):
    return _big_row_tile(S), _pick(F, (1408, 512, 256, 128))


def _glu_fwd(g, u, conv_w, conv_b, *, name):
    S, F = g.shape
    tm, tc = _glu_tiles(S, F)
    hb = tm // GLU_HALO

    def body(g_ref, gp_ref, u_ref, w_ref, b_ref, a_ref):
        i = pl.program_id(1)
        gt = g_ref[...].astype(F32)
        ext = jnp.concatenate([gp_ref[...].astype(F32) * (i > 0).astype(F32), gt], axis=0)
        gc = b_ref[...] + w_ref[0:1, :] * pltpu.roll(ext, 2, 0)[GLU_HALO:]
        gc = gc + w_ref[1:2, :] * pltpu.roll(ext, 1, 0)[GLU_HALO:]
        gc = gc + w_ref[2:3, :] * gt
        h, t = _silu_parts(gc)
        a_ref[...] = ((h * (1.0 + t)) * u_ref[...].astype(F32)).astype(BF16)

    return pl.pallas_call(
        body, grid=(F // tc, S // tm),
        in_specs=[pl.BlockSpec((tm, tc), lambda j, i: (i, j)),
                  pl.BlockSpec((GLU_HALO, tc), lambda j, i: (jnp.maximum(i * hb - 1, 0), j)),
                  pl.BlockSpec((tm, tc), lambda j, i: (i, j)),
                  pl.BlockSpec((3, tc), lambda j, i: (0, j)),
                  pl.BlockSpec((1, tc), lambda j, i: (0, j))],
        out_specs=pl.BlockSpec((tm, tc), lambda j, i: (i, j)),
        out_shape=jax.ShapeDtypeStruct((S, F), BF16),
        compiler_params=_cp(("parallel", "parallel")), name=name)(g, g, u, conv_w, conv_b)


def _glu_bwd(g, u, d_a, conv_w, conv_b, *, name):
    S, F = g.shape
    tm, tc = _glu_tiles(S, F)
    hb = tm // GLU_HALO
    nt = S // tm
    E = tm + GLU_HALO

    def body(g_ref, gp_ref, gn_ref, u_ref, un_ref, da_ref, dan_ref, w_ref, b_ref,
             dg_ref, du_ref, gw_ref, gb_ref):
        i = pl.program_id(1)

        @pl.when(i == 0)
        def _():
            gw_ref[...] = jnp.zeros_like(gw_ref)
            gb_ref[...] = jnp.zeros_like(gb_ref)

        w0, w1, w2 = w_ref[0:1, :], w_ref[1:2, :], w_ref[2:3, :]
        gext = jnp.concatenate([gp_ref[...].astype(F32) * (i > 0).astype(F32), g_ref[...].astype(F32),
                                gn_ref[...].astype(F32)], axis=0)
        g1 = pltpu.roll(gext, 1, 0)[GLU_HALO:]
        g2 = pltpu.roll(gext, 2, 0)[GLU_HALO:]
        g0 = gext[GLU_HALO:]
        gc = b_ref[...] + w0 * g2
        gc = gc + w1 * g1
        gc = gc + w2 * g0
        h, t = _silu_parts(gc)
        t1 = 1.0 + t
        da = jnp.concatenate([da_ref[...].astype(F32), dan_ref[...].astype(F32) * (i < nt - 1).astype(F32)], axis=0)
        uu = jnp.concatenate([u_ref[...].astype(F32), un_ref[...].astype(F32)], axis=0)
        du_ref[...] = (da[:tm] * (h[:tm] * t1[:tm])).astype(BF16)
        dgc = (da * uu) * (0.5 * (t1 + h * (1.0 - t * t)))
        dg = w2 * dgc[:tm] + w1 * pltpu.roll(dgc, E - 1, 0)[:tm] + w0 * pltpu.roll(dgc, E - 2, 0)[:tm]
        dg_ref[...] = dg.astype(BF16)
        dgt = dgc[:tm]
        gb_ref[...] += jnp.sum(dgt, axis=0, keepdims=True)
        gw_ref[0:1, :] += jnp.sum(dgt * g2[:tm], axis=0, keepdims=True)
        gw_ref[1:2, :] += jnp.sum(dgt * g1[:tm], axis=0, keepdims=True)
        gw_ref[2:3, :] += jnp.sum(dgt * g0[:tm], axis=0, keepdims=True)

    tile = pl.BlockSpec((tm, tc), lambda j, i: (i, j))
    nxt = pl.BlockSpec((GLU_HALO, tc), lambda j, i: (jnp.minimum((i + 1) * hb, S // GLU_HALO - 1), j))
    prv = pl.BlockSpec((GLU_HALO, tc), lambda j, i: (jnp.maximum(i * hb - 1, 0), j))
    return pl.pallas_call(
        body, grid=(F // tc, nt),
        in_specs=[tile, prv, nxt, tile, nxt, tile, nxt,
                  pl.BlockSpec((3, tc), lambda j, i: (0, j)), pl.BlockSpec((1, tc), lambda j, i: (0, j))],
        out_specs=[tile, tile, pl.BlockSpec((3, tc), lambda j, i: (0, j)), pl.BlockSpec((1, tc), lambda j, i: (0, j))],
        out_shape=[jax.ShapeDtypeStruct((S, F), BF16), jax.ShapeDtypeStruct((S, F), BF16),
                   jax.ShapeDtypeStruct((3, F), F32), jax.ShapeDtypeStruct((1, F), F32)],
        compiler_params=_cp(("parallel", "arbitrary")), name=name)(g, g, g, u, u, d_a, d_a, conv_w, conv_b)


def _loss_head(y, target, *, name):
    S, D = y.shape
    tm = _big_row_tile(S)
    nt = S // tm

    def body(y_ref, t_ref, dy_ref, dyb_ref, loss_ref, acc):
        i = pl.program_id(0)

        @pl.when(i == 0)
        def _():
            acc[...] = jnp.zeros_like(acc)

        e = y_ref[...] - t_ref[...]
        dy = e * (1.0 / D)
        dy_ref[...] = dy
        dyb_ref[...] = dy.astype(BF16)
        acc[...] += jnp.sum(e * e, axis=0, keepdims=True)

        @pl.when(i == nt - 1)
        def _():
            loss_ref[...] = jnp.broadcast_to(jnp.sum(acc[...], axis=1, keepdims=True) * (0.5 / D), (1, 128))

    row = pl.BlockSpec((tm, D), lambda i: (i, 0))
    return pl.pallas_call(
        body, grid=(nt,), in_specs=[row, row],
        out_specs=[row, row, pl.BlockSpec((1, 128), lambda i: (0, 0))],
        out_shape=[jax.ShapeDtypeStruct((S, D), F32), jax.ShapeDtypeStruct((S, D), BF16),
                   jax.ShapeDtypeStruct((1, 128), F32)],
        scratch_shapes=[pltpu.VMEM((1, D), F32)],
        compiler_params=_cp(("arbitrary",)), name=name)(y, target)


def _adamw_math(w, g, m, v):
    m = ADAM_B1 * m + (1.0 - ADAM_B1) * g
    v = ADAM_B2 * v + (1.0 - ADAM_B2) * (g * g)
    m_hat = m / (1.0 - ADAM_B1 ** ADAM_STEP)
    v_hat = v / (1.0 - ADAM_B2 ** ADAM_STEP)
    delta = -ADAM_LR * (m_hat / (jnp.sqrt(v_hat) + ADAM_EPS) + ADAM_WD * w)
    return delta, m, v


def _adamw(w, m, v, parts, *, name):
    R, C = w.shape
    tr = 128 if R % 128 == 0 else R
    n_parts = len(parts)

    def body(*refs):
        w_ref, m_ref, v_ref = refs[:3]
        p_refs = refs[3:3 + n_parts]
        g_ref, d_ref, mo_ref, vo_ref = refs[3 + n_parts:]
        g = p_refs[0][...]
        for p in p_refs[1:]:
            g = g + p[...]
        delta, mn, vn = _adamw_math(w_ref[...], g, m_ref[...], v_ref[...])
        g_ref[...] = g
        d_ref[...] = delta
        mo_ref[...] = mn
        vo_ref[...] = vn

    blk = pl.BlockSpec((tr, C), lambda i: (i, 0))
    return pl.pallas_call(
        body, grid=(R // tr,), in_specs=[blk] * (3 + n_parts), out_specs=[blk] * 4,
        out_shape=[jax.ShapeDtypeStruct((R, C), F32)] * 4,
        compiler_params=_cp(("parallel",)), name=name)(w, m, v, *parts)


def _sum4(g_stack, recv, me, *, name):
    _, R, C = g_stack.shape
    tr = 128 if R % 128 == 0 else R

    def body(me_ref, g_ref, r_ref, o_ref):
        acc = g_ref[...]
        for j in range(N_CHIPS - 1):
            acc = acc + r_ref[j].astype(F32)
        o_ref[...] = acc

    grid_spec = pltpu.PrefetchScalarGridSpec(
        num_scalar_prefetch=1, grid=(R // tr,),
        in_specs=[pl.BlockSpec((None, tr, C), lambda i, me_ref: (me_ref[0], i, 0)),
                  pl.BlockSpec((N_CHIPS - 1, tr, C), lambda i, me_ref: (0, i, 0))],
        out_specs=pl.BlockSpec((tr, C), lambda i, me_ref: (i, 0)))
    return pl.pallas_call(
        body, grid_spec=grid_spec, out_shape=jax.ShapeDtypeStruct((R, C), F32),
        compiler_params=_cp(("parallel",)), name=name)(me, g_stack, recv)


def _sum8(gathered, *, name):
    _, R, C = gathered.shape

    def body(g_ref, o_ref):
        acc = g_ref[0]
        for d in range(1, N_DEV):
            acc = acc + g_ref[d]
        o_ref[...] = acc

    return pl.pallas_call(
        body, grid=(1,), in_specs=[pl.BlockSpec((N_DEV, R, C), lambda i: (0, 0, 0))],
        out_specs=pl.BlockSpec((R, C), lambda i: (0, 0)),
        out_shape=jax.ShapeDtypeStruct((R, C), F32),
        compiler_params=_cp(("arbitrary",)), name=name)(gathered)


def _place():
    return lax.axis_index("x"), lax.axis_index("y"), lax.axis_index("c")


def _other_chips(x, y):
    return [(1 - x, y), (x, 1 - y), (1 - x, 1 - y)]


_ANY = pl.BlockSpec(memory_space=pl.ANY)


_HBM = pl.BlockSpec(memory_space=pltpu.HBM)
_SEM = pl.BlockSpec(memory_space=pltpu.SEMAPHORE)
_EFFECT = pltpu.SideEffectType.DATAFLOW_SIDE_EFFECTING


def _gather_copies(srcs, lands, send_sems, recv_sems):
    x, y, c = _place()
    me = 2 * x + y
    return [pltpu.make_async_remote_copy(
        src_ref=srcs[i], dst_ref=lands[i].at[me], send_sem=send_sems.at[3 * i + j],
        recv_sem=recv_sems.at[3 * i + j], device_id=(px, py, c), device_id_type=MESH)
        for i in range(len(srcs)) for j, (px, py) in enumerate(_other_chips(x, y))]


def _scatter_copies(srcs, lands, send_sems, recv_sems):
    x, y, c = _place()
    return [pltpu.make_async_remote_copy(
        src_ref=srcs[i].at[2 * px + py], dst_ref=lands[i].at[j], send_sem=send_sems.at[3 * i + j],
        recv_sem=recv_sems.at[3 * i + j], device_id=(px, py, c), device_id_type=MESH)
        for i in range(len(srcs)) for j, (px, py) in enumerate(_other_chips(x, y))]


def _sibling_copies(srcs, lands, send_sems, recv_sems):
    x, y, c = _place()
    return [pltpu.make_async_remote_copy(
        src_ref=srcs[i], dst_ref=lands[i], send_sem=send_sems.at[i], recv_sem=recv_sems.at[i],
        device_id=(x, y, 1 - c), device_id_type=MESH) for i in range(len(srcs))]


def _copies_start(srcs, lands, make_copies, thru=(), *, per=N_CHIPS - 1, name):
    n = len(srcs)
    n_ops = 2 * n + len(thru)

    def body(*refs):
        send_sems, recv_sems = refs[n_ops], refs[n_ops + 1]
        for cp in make_copies(refs[:n], refs[n:2 * n], send_sems, recv_sems):
            cp.start()
        refs[-1][...] = jnp.zeros_like(refs[-1])

    ops = list(srcs) + list(lands) + list(thru)
    outs = pl.pallas_call(
        body, name=name,
        out_shape=(pltpu.SemaphoreType.DMA((per * n,)), pltpu.SemaphoreType.DMA((per * n,)),
                   *[pltpu.HBM(a.shape, a.dtype) for a in ops], jax.ShapeDtypeStruct((8, 128), F32)),
        in_specs=[_HBM] * n_ops,
        out_specs=(_SEM, _SEM, *[_HBM] * n_ops, pl.BlockSpec(memory_space=pltpu.VMEM)),
        input_output_aliases={i: 2 + i for i in range(n_ops)},
        compiler_params=pltpu.CompilerParams(has_side_effects=_EFFECT),
    )(*[pltpu.with_memory_space_constraint(a, pltpu.HBM) for a in ops])
    return outs[0], outs[1], list(outs[2:2 + n]), list(outs[2 + n:2 + 2 * n]), outs[-1], list(outs[2 + 2 * n:-1])


def _copies_wait(handle, after, make_copies, *, with_srcs=False, name):
    send_sems, recv_sems, srcs, lands = handle[:4]
    n = len(srcs)

    def body(*refs):
        for cp in make_copies(refs[:n], refs[n:2 * n], refs[2 * n], refs[2 * n + 1]):
            cp.wait_send()
            cp.wait_recv()

    ops = list(srcs) + list(lands)
    outs = pl.pallas_call(
        body, name=name,
        out_shape=tuple(pltpu.HBM(a.shape, a.dtype) for a in ops),
        in_specs=[_HBM] * (2 * n) + [_SEM, _SEM, _ANY],
        out_specs=tuple([_HBM] * (2 * n)),
        input_output_aliases={i: i for i in range(2 * n)},
        compiler_params=pltpu.CompilerParams(has_side_effects=_EFFECT),
    )(*ops, send_sems, recv_sems, after)
    return (list(outs[:n]), list(outs[n:])) if with_srcs else list(outs[n:])


def _gather_all(buf, *, name):
    R, C = buf.shape

    def body(in_ref, out_ref, send_sems, recv_sems, local_sem):
        x, y, c = _place()
        me = 4 * x + 2 * y + c
        lc = pltpu.make_async_copy(in_ref, out_ref.at[me], local_sem)
        lc.start()
        remote = []
        for k in range(1, N_DEV):
            px = 1 - x if (k >> 2) & 1 else x
            py = 1 - y if (k >> 1) & 1 else y
            pc = 1 - c if k & 1 else c
            rc = pltpu.make_async_remote_copy(
                src_ref=in_ref, dst_ref=out_ref.at[me], send_sem=send_sems.at[k - 1],
                recv_sem=recv_sems.at[k - 1], device_id=(px, py, pc), device_id_type=MESH)
            rc.start()
            remote.append(rc)
        lc.wait()
        for rc in remote:
            rc.wait_send()
        for rc in remote:
            rc.wait_recv()

    return pl.pallas_call(
        body, in_specs=[_ANY], out_specs=_ANY,
        out_shape=jax.ShapeDtypeStruct((N_DEV, R, C), buf.dtype),
        scratch_shapes=[pltpu.SemaphoreType.DMA((N_DEV - 1,)), pltpu.SemaphoreType.DMA((N_DEV - 1,)),
                        pltpu.SemaphoreType.DMA],
        name=name)(buf)


def _w_in_to_z(w):
    pad = jnp.zeros(w.shape[:-1] + (64,), w.dtype)
    return jnp.concatenate([w[..., 0:512], w[..., 512:1024], w[..., 1344:1856], w[..., 1024:1280],
                            w[..., 1280:1344], pad], axis=-1)


def _z_to_w_in(g):
    return jnp.concatenate([g[..., 0:512], g[..., 512:1024], g[..., 1536:1792], g[..., 1792:1856],
                            g[..., 1024:1536]], axis=-1)


def _pad_heads(w, nh):
    w = w.reshape(w.shape[:-1] + (nh, QK))
    w = jnp.concatenate([w, jnp.zeros(w.shape[:-1] + (HEAD_PAD - QK,), w.dtype)], axis=-1)
    return w.reshape(w.shape[:-2] + (nh * HEAD_PAD,))


def _unpad_heads(g, nh):
    g = g.reshape(g.shape[:-1] + (nh, HEAD_PAD))[..., :QK]
    return g.reshape(g.shape[:-2] + (nh * QK,))


def _kv_split(w, nh):
    w = w.reshape(w.shape[:-1] + (nh, 2, 128))
    return jnp.swapaxes(w, -3, -2).reshape(w.shape[:-3] + (nh * 256,))


def _kv_join(g, nh):
    g = g.reshape(g.shape[:-1] + (2, nh, 128))
    return jnp.swapaxes(g, -3, -2).reshape(g.shape[:-3] + (nh * 256,))


def _pad_gain(g):
    return jnp.concatenate([g, jnp.zeros((1, HEAD_PAD - QK), g.dtype)], axis=1)


_SMALL = ("g_mix", "g_q_lat", "g_kv_lat", "g_q_mla", "g_k_mla", "w_pool", "pool_scale", "g_mem", "g_q_x",
          "g_k_x", "g_ffn", "conv_b", "conv_w")


def _pack(arrs, extra=0):
    flat = jnp.concatenate([a.reshape(-1) for a in arrs])
    n = flat.shape[0] + extra
    rows = -(-n // 1024) * 8
    return jnp.pad(flat, (0, rows * 128 - flat.shape[0])).reshape(rows, 128)


def _unpack(buf, shapes):
    flat = buf.reshape(-1)
    out, off = [], 0
    for s in shapes:
        n = int(np.prod(s))
        out.append(flat[off:off + n].reshape(s))
        off += n
    return out, off


def _tied(a, token):
    return a + token[:1, :1].astype(a.dtype)


def _local_step(x, mem, target, W, fetch=None, ship=None):
    fetch = fetch or (lambda group, after: None)
    ship = ship or (lambda group, G: jnp.zeros((8, 128), F32))
    S, D = x.shape
    F = W["conv_b"].shape[1]
    tabs = _rope_tables(S)
    tm = 512 if S % 512 == 0 else 128
    tl = 1024 if S % 1024 == 0 else tm
    tk = _pick(S, (1024, 512, 128))

    h = _rms_fwd(x, W["g_mix"], C=D, name="norm_mix")
    fetch("g1", h)
    z = mm_nn(h, W["w_in"], tm=tl, tn=Z_COLS, tk=D, name="z_proj")
    fetch("g2", z)
    y_pool = _pool_fwd(z, W["w_pool"], W["pool_scale"], name="pool_fwd")
    ql = _rms_fwd(z, W["g_q_lat"], C=Q_RANK, cb=Z_Q_CB, name="norm_qlat")
    kvl = _rms_fwd(z, W["g_kv_lat"], C=KV_RANK, cb=Z_KV_CB, name="norm_kvlat")
    qraw = mm_nn(ql, W["w_q_up"], nsh=N_CHIPS, tm=tl, tn=512, tk=Q_RANK, name="q_up")
    kvraw = mm_nn(kvl, W["w_kv_up"], nsh=N_CHIPS, tm=tl, tn=512, tk=KV_RANK, name="kv_up")
    q, k, v, vt = _qkrope_fwd(qraw, kvraw, z, W["g_q_mla"], W["g_k_mla"], tabs, name="qk_norm_rope")
    o, y_mla, lse = _flash_fwd(q, k, vt, name="mla_fwd")
    memn = _rms_fwd(mem, W["g_mem"], C=D, name="norm_mem")
    M = mem.shape[0]
    mkv = mm_nn(memn, W["w_mem_kv"], tm=M, tn=1024, tk=D, name="mem_kv")
    kx, vx = _memk_fwd(mkv, W["g_k_x"], name="memk_fwd")
    y_mem = _memattn_fwd(z, kx, vx, W["g_q_x"], name="memattn_fwd")
    cat = jnp.concatenate([y_pool, y_mla, y_mem], axis=1)
    fetch("g2b", cat)
    x2 = mm_nn(cat, W["w_o"], tm=tm, tn=D, tk=D, add=x, name="o_proj")
    h2 = _rms_fwd(x2, W["g_ffn"], C=D, name="norm_ffn")
    fetch("g3", h2)
    fn = F // N_CHIPS
    g = mm_nn(h2, W["w_gate"], nsh=N_CHIPS, tm=tl, tn=fn, tk=D, out_dtypes=(BF16,), name="gate_proj")
    u = mm_nn(h2, W["w_up"], nsh=N_CHIPS, tm=tl, tn=fn, tk=D, out_dtypes=(BF16,), name="up_proj")
    a = _glu_fwd(g, u, W["conv_w"], W["conv_b"], name="glu_fwd")
    y = mm_nn(a, W["w_down"], tm=tm, tn=1024, tk=F // 2, add=x2, name="down_proj")
    dy, dyb, loss_row = _loss_head(y, target, name="loss_head")

    G = {}
    d_a = mm_nt(dyb, W["w_down"], tm=tl, to=512, tc=D, out_dtypes=(BF16,), name="d_a")
    G["w_down"] = mm_tn(a, dyb, to=fn, tn=1024, tk=tk, out_dtypes=(F32, BF16), name="grad_w_down")
    d_g, d_u, G["conv_w"], G["conv_b"] = _glu_bwd(g, u, d_a, W["conv_w"], W["conv_b"], name="glu_bwd")
    G["w_gate"] = mm_tn(h2, d_g, nsh=N_CHIPS, to=1024, tn=fn, tk=tk, out_dtypes=(F32, BF16), name="grad_w_gate")
    G["w_up"] = mm_tn(h2, d_u, nsh=N_CHIPS, to=1024, tn=fn, tk=tk, out_dtypes=(F32, BF16), name="grad_w_up")
    tok = ship("s1", G)
    d_h2 = mm_nt_pair(d_g, W["w_gate"], d_u, W["w_up"], nsh=N_CHIPS, tm=tm, to=1024, tc=fn, out_dtype=BF16,
                      name="d_h2")
    d_x2, d_x2b, G["g_ffn"] = _rms_bwd(x2, d_h2, _tied(W["g_ffn"], tok), C=D, res=dy, out_dtypes=(F32, BF16),
                                       name="norm_ffn_bwd")

    d_cat = mm_nt(d_x2b, W["w_o"], tm=tl, to=1024, tc=D, out_dtypes=(BF16,), name="d_cat")
    G["w_o"] = mm_tn(cat, d_x2b, to=1024, tn=1024, tk=tk, out_dtypes=(F32, BF16), name="grad_w_o")
    tok = ship("s2", G)
    dz_pool, G["w_pool"], G["pool_scale"] = _pool_bwd(z, d_cat, W["w_pool"], _tied(W["pool_scale"], tok),
                                                      name="pool_bwd")
    dz_mq, dkx, dvx, G["g_q_x"] = _memattn_bwd(z, kx, vx, W["g_q_x"], d_cat, name="memattn_bwd")
    d_mkv, G["g_k_x"] = _memk_bwd(mkv, W["g_k_x"], dkx, dvx, name="memk_bwd")
    G["w_mem_kv"] = mm_tn(memn, d_mkv, to=1024, tn=1024, tk=M, out_dtypes=(F32, BF16), name="grad_w_mem_kv")
    d_memn = mm_nt(d_mkv, W["w_mem_kv"], tm=M, to=D, tc=1024, name="d_memn")
    _, G["g_mem"] = _rms_bwd(mem, d_memn, W["g_mem"], C=D, name="norm_mem_bwd")
    delta = _attn_bwd_prep(o, d_cat, name="mla_bwd_prep")
    dq, dk, dv = _flash_bwd(q, k, v, d_cat, lse, delta, name="mla_bwd")
    d_qraw, d_kvraw, dz_kr, G["g_q_mla"], G["g_k_mla"] = _qkrope_bwd(
        qraw, kvraw, z, W["g_q_mla"], W["g_k_mla"], tabs, dq, dk, dv, name="qk_norm_rope_bwd")
    G["w_q_up"] = mm_tn(ql, d_qraw, nsh=N_CHIPS, to=Q_RANK, tn=512, tk=tk, out_dtypes=(F32, BF16), name="grad_w_q_up")
    d_ql = mm_nt(d_qraw, W["w_q_up"], nsh=N_CHIPS, tm=tl, to=Q_RANK, tc=512, name="d_ql")
    G["w_kv_up"] = mm_tn(kvl, d_kvraw, nsh=N_CHIPS, to=KV_RANK, tn=512, tk=tk, out_dtypes=(F32, BF16),
                         name="grad_w_kv_up")
    d_kvl = mm_nt(d_kvraw, W["w_kv_up"], nsh=N_CHIPS, tm=tl, to=KV_RANK, tc=512, name="d_kvl")
    dz_q, G["g_q_lat"] = _rms_bwd(z, d_ql, W["g_q_lat"], C=Q_RANK, cb=Z_Q_CB, out_dtypes=(BF16,), name="norm_qlat_bwd")
    dz_kv, G["g_kv_lat"] = _rms_bwd(z, d_kvl, W["g_kv_lat"], C=KV_RANK, cb=Z_KV_CB, out_dtypes=(BF16,),
                                    name="norm_kvlat_bwd")
    d_z = jnp.concatenate([dz_pool, dz_q, dz_mq, dz_kv, dz_kr], axis=1)
    G["w_in"] = mm_tn(h, d_z, to=512, tn=Z_COLS, tk=tk, out_dtypes=(F32, BF16), name="grad_w_in")
    tok = ship("s3", G)
    d_h = mm_nt(d_z, W["w_in"], tm=tl, to=1024, tc=Z_COLS, out_dtypes=(BF16,), name="d_h")
    grad_x, G["g_mix"] = _rms_bwd(x, d_h, _tied(W["g_mix"], tok), C=D, res=d_x2, name="norm_mix_bwd")
    return loss_row, grad_x, G


_BIG = ("w_in", "w_q_up", "w_kv_up", "w_mem_kv", "w_o", "w_gate", "w_up", "w_down")
_WEIGHTS = ("g_mix", "w_in", "g_q_lat", "w_q_up", "g_kv_lat", "w_kv_up", "g_q_mla", "g_k_mla", "w_pool",
            "pool_scale", "g_mem", "w_mem_kv", "g_q_x", "g_k_x", "w_o", "g_ffn", "w_gate", "w_up", "conv_w",
            "conv_b", "w_down")


def _to_compute_layout(name, w):
    if name == "w_in":
        return _w_in_to_z(w)
    if name == "w_q_up":
        return _pad_heads(w, w.shape[-1] // QK)
    if name == "w_kv_up":
        return _kv_split(w, w.shape[-1] // 256)
    return w


def _from_compute_layout(name, g):
    if name == "w_in":
        return _z_to_w_in(g)
    if name == "w_q_up":
        return _unpad_heads(g, g.shape[-1] // HEAD_PAD)
    if name == "w_kv_up":
        return _kv_join(g, g.shape[-1] // 256)
    return g


def kernel(x, mem, g_mix, w_in, g_q_lat, w_q_up, g_kv_lat, w_kv_up, g_q_mla, g_k_mla, w_pool, pool_scale, g_mem, w_mem_kv, g_q_x, g_k_x, w_o, g_ffn, w_gate, w_up, conv_w, conv_b, w_down, loss_target, m_g_mix, m_w_in, m_g_q_lat, m_w_q_up, m_g_kv_lat, m_w_kv_up, m_g_q_mla, m_g_k_mla, m_w_pool, m_pool_scale, m_g_mem, m_w_mem_kv, m_g_q_x, m_g_k_x, m_w_o, m_g_ffn, m_w_gate, m_w_up, m_conv_w, m_conv_b, m_w_down, v_g_mix, v_w_in, v_g_q_lat, v_w_q_up, v_g_kv_lat, v_w_kv_up, v_g_q_mla, v_g_k_mla, v_w_pool, v_pool_scale, v_g_mem, v_w_mem_kv, v_g_q_x, v_g_k_x, v_w_o, v_g_ffn, v_w_gate, v_w_up, v_conv_w, v_conv_b, v_w_down):
    P = dict(g_mix=g_mix, w_in=w_in, g_q_lat=g_q_lat, w_q_up=w_q_up, g_kv_lat=g_kv_lat, w_kv_up=w_kv_up,
             g_q_mla=g_q_mla, g_k_mla=g_k_mla, w_pool=w_pool, pool_scale=pool_scale, g_mem=g_mem,
             w_mem_kv=w_mem_kv, g_q_x=g_q_x, g_k_x=g_k_x, w_o=w_o, g_ffn=g_ffn, w_gate=w_gate, w_up=w_up,
             conv_w=conv_w, conv_b=conv_b, w_down=w_down)
    Mo = dict(g_mix=m_g_mix, w_in=m_w_in, g_q_lat=m_g_q_lat, w_q_up=m_w_q_up, g_kv_lat=m_g_kv_lat,
              w_kv_up=m_w_kv_up, g_q_mla=m_g_q_mla, g_k_mla=m_g_k_mla, w_pool=m_w_pool,
              pool_scale=m_pool_scale, g_mem=m_g_mem, w_mem_kv=m_w_mem_kv, g_q_x=m_g_q_x, g_k_x=m_g_k_x,
              w_o=m_w_o, g_ffn=m_g_ffn, w_gate=m_w_gate, w_up=m_w_up, conv_w=m_conv_w, conv_b=m_conv_b,
              w_down=m_w_down)
    Vo = dict(g_mix=v_g_mix, w_in=v_w_in, g_q_lat=v_g_q_lat, w_q_up=v_w_q_up, g_kv_lat=v_g_kv_lat,
              w_kv_up=v_w_kv_up, g_q_mla=v_g_q_mla, g_k_mla=v_g_k_mla, w_pool=v_w_pool,
              pool_scale=v_pool_scale, g_mem=v_g_mem, w_mem_kv=v_w_mem_kv, g_q_x=v_g_q_x, g_k_x=v_g_k_x,
              w_o=v_w_o, g_ffn=v_g_ffn, w_gate=v_w_gate, w_up=v_w_up, conv_w=v_conv_w, conv_b=v_conv_b,
              w_down=v_w_down)
    xi, yi, ci = _place()
    me = (2 * xi + yi).astype(jnp.int32).reshape(1)

    shard = {n: _to_compute_layout(n, P[n][0]).astype(BF16) for n in _BIG}
    shard["conv_w"] = conv_w[0]
    gather_groups = {"g1": ("w_in",), "g2": ("w_q_up", "w_kv_up", "w_mem_kv"), "g2b": ("w_o",),
                     "g3": ("w_gate", "w_up", "w_down", "conv_w")}
    gathers = {}

    def landing(n):
        s = shard[n]
        return lax.dynamic_update_slice(lax.empty((N_CHIPS,) + s.shape, s.dtype), s[None], (me[0], 0, 0))

    def start_gather(grp, thru=()):
        names = gather_groups[grp]
        gathers[grp] = _copies_start([shard[n] for n in names], [landing(n) for n in names], _gather_copies, thru,
                                     name="gather_start_" + grp)
        return gathers[grp][5]

    start_gather("g1")
    W = {}
    W["g_q_mla"], W["g_k_mla"] = _pad_gain(g_q_mla), _pad_gain(g_k_mla)
    W["w_pool"] = w_pool[0].astype(BF16)
    for n in ("g_mix", "g_q_lat", "g_kv_lat", "pool_scale", "g_mem", "g_q_x", "g_k_x", "g_ffn", "conv_b"):
        W[n] = P[n]
    W["g_mix"] = _tied(W["g_mix"], gathers["g1"][4])

    def fetch(grp, after):
        stacks = _copies_wait(gathers[grp], after, _gather_copies, name="gather_wait_" + grp)
        if grp == "g1":
            stacks = start_gather("g3", start_gather("g2b", start_gather("g2", stacks)))
        for n, s in zip(gather_groups[grp], stacks):
            if n == "conv_w":
                W[n] = jnp.swapaxes(s, 0, 1).reshape(3, -1)
            else:
                W[n] = s.reshape(-1, s.shape[-1])

    shard_shape = {n: shard[n].shape for n in _BIG}
    scatter_groups = {"s1": ("w_down", "w_gate", "w_up"), "s2": ("w_o",),
                      "s3": ("w_mem_kv", "w_q_up", "w_kv_up", "w_in")}
    scatters = {}

    def parts_of(n, g):
        return g.reshape((N_CHIPS,) + shard_shape[n])

    def ship(grp, G):
        names = scatter_groups[grp]
        srcs = [parts_of(n, G[n][1]) for n in names]
        lands = [lax.empty((N_CHIPS - 1,) + shard_shape[n], BF16) for n in names]
        scatters[grp] = _copies_start(srcs, lands, _scatter_copies, name="scatter_start_" + grp)
        return scatters[grp][4]

    loss_row, grad_x, G = _local_step(x[0], mem[0], loss_target[0], W, fetch, ship)

    swaps, part = {}, {}
    after = grad_x
    for grp, names in (("s1", scatter_groups["s1"]), ("s23", scatter_groups["s2"] + scatter_groups["s3"])):
        recv = {}
        for sg_ in (("s1",) if grp == "s1" else ("s2", "s3")):
            for n, r in zip(scatter_groups[sg_],
                            _copies_wait(scatters[sg_], after, _scatter_copies, name="scatter_wait_" + sg_)):
                recv[n] = r
        for n in names:
            part[n] = _from_compute_layout(n, _sum4(parts_of(n, G[n][0]), recv[n], me, name="sum4_" + n))
        swaps[grp] = _copies_start([part[n] for n in names], [lax.empty(part[n].shape, F32) for n in names],
                                   _sibling_copies, per=1, name="swap_start_" + grp)
        after = swaps[grp][4]

    conv_w_full_grad = G["conv_w"]
    small_g = [G["g_mix"], G["g_q_lat"], G["g_kv_lat"], G["g_q_mla"][:, :QK], G["g_k_mla"][:, :QK], G["w_pool"],
               G["pool_scale"], G["g_mem"], G["g_q_x"], G["g_k_x"], G["g_ffn"], G["conv_b"], conv_w_full_grad]
    packed = _tied(_pack(small_g + [loss_row[:, :1]]), after)
    total = _sum8(_gather_all(packed, name="gather_small"), name="sum_small")
    out = {}
    after = total
    for grp, names in (("s1", scatter_groups["s1"]), ("s23", scatter_groups["s2"] + scatter_groups["s3"])):
        mine, sib = _copies_wait(swaps[grp], after, _sibling_copies, with_srcs=True, name="swap_wait_" + grp)
        for n, p, s in zip(names, mine, sib):
            out[n] = [r[None] for r in _adamw(P[n][0], Mo[n][0], Vo[n][0], [p, s], name="adamw_" + n)]
        after = out[names[-1]][0]
    shapes = [a.shape for a in small_g] + [(1, 1)]
    (parts, _) = _unpack(total, shapes)
    loss = parts[-1].reshape(())
    F = conv_b.shape[1]
    fn = F // N_CHIPS
    col0 = (2 * xi + yi) * fn
    sg = dict(zip(_SMALL, parts[:-1]))
    sg["conv_w"] = lax.dynamic_slice(sg["conv_w"], (0, col0), (3, fn))
    sw = [P[n].reshape(sg[n].shape) for n in _SMALL]
    sm = [Mo[n].reshape(sg[n].shape) for n in _SMALL]
    sv = [Vo[n].reshape(sg[n].shape) for n in _SMALL]
    gp = _pack([sg[n] for n in _SMALL])
    res = _adamw(_pack(sw), _pack(sm), _pack(sv), [gp], name="adamw_small")
    sshapes = [sg[n].shape for n in _SMALL]
    for kind, buf in zip(range(4), res):
        vals, _ = _unpack(buf, sshapes)
        for n, val in zip(_SMALL, vals):
            out.setdefault(n, [None] * 4)[kind] = val.reshape(P[n].shape)

    return (loss, grad_x[None], *[out[n][0] for n in _WEIGHTS], *[out[n][1] for n in _WEIGHTS],
            *[out[n][2] for n in _WEIGHTS], *[out[n][3] for n in _WEIGHTS])
```

```python
import functools
import math

import numpy as np
import jax
import jax.numpy as jnp
from jax import lax
from jax.experimental import pallas as pl
from jax.experimental.pallas import tpu as pltpu

F32, BF16 = jnp.float32, jnp.bfloat16
NORM_EPS = 1e-6
ROPE_THETA = 10000.0
V7X_VMEM_LIMIT_BYTES = 48 * 1024 * 1024
N_CHIPS = 4
N_DEV = 8

POOL_W = 512
POOL_WINDOWS = (2, 4, 8, 16)
HEADS = 8
NOPE, ROPE, QK = 128, 64, 192
HEAD_PAD = 256
Q_RANK, KV_RANK = 512, 256
X_HEADS, X_DIM = 4, 128
Z_COLS = 1920
Z_POOL_CB, Z_Q_CB, Z_MQ_CB = 0, 1, 2
Z_KV_CB = 6
Z_KR_CB = 14

ADAM_LR, ADAM_B1, ADAM_B2, ADAM_EPS, ADAM_WD, ADAM_STEP = 0.001, 0.9, 0.999, 1e-08, 0.01, 10

MESH = pl.DeviceIdType.MESH


def _cp(sem):
    return pltpu.CompilerParams(dimension_semantics=sem, vmem_limit_bytes=V7X_VMEM_LIMIT_BYTES)


def _row_tile(S):
    return 256 if S % 256 == 0 and S >= 2048 else 128


def _big_row_tile(S):
    return 512 if S % 512 == 0 and S >= 2048 else _row_tile(S)


def _pick(dim, prefs):
    for p in prefs:
        if dim % p == 0:
            return p
    return dim


_DN = {"nn": (((1,), (0,)), ((), ())), "nt": (((1,), (1,)), ((), ())), "tn": (((0,), (0,)), ((), ()))}


def _mm(a, b, *, mode, grid, blocks, maps, out_shape, out_dtypes, add=None, name):
    nk = grid[2]
    dn = _DN[mode]
    n_out = len(out_dtypes)

    def body(*refs):
        a_ref, b_ref = refs[0], refs[1]
        add_ref = refs[2] if add is not None else None
        p = 2 + (add is not None)
        o_refs = refs[p:p + n_out]

        def finish(r):
            if add_ref is not None:
                r = r + add_ref[...]
            for o in o_refs:
                o[...] = r.astype(o.dtype)

        def product():
            return lax.dot_general(a_ref[...].astype(BF16), b_ref[...].astype(BF16), dn, preferred_element_type=F32)

        if nk == 1:
            finish(product())
            return
        acc = refs[p + n_out]
        k = pl.program_id(2)

        @pl.when(k == 0)
        def _():
            acc[...] = jnp.zeros_like(acc)

        acc[...] += product()

        @pl.when(k == nk - 1)
        def _():
            finish(acc[...])

    a_blk, b_blk, o_blk = blocks
    a_map, b_map, o_map = maps
    in_specs = [pl.BlockSpec(a_blk, a_map), pl.BlockSpec(b_blk, b_map)]
    args = [a, b]
    if add is not None:
        in_specs.append(pl.BlockSpec(o_blk, o_map))
        args.append(add)
    outs = pl.pallas_call(
        body, grid=grid, in_specs=in_specs,
        out_specs=[pl.BlockSpec(o_blk, o_map) for _ in out_dtypes],
        out_shape=[jax.ShapeDtypeStruct(out_shape, d) for d in out_dtypes],
        scratch_shapes=[pltpu.VMEM(o_blk, F32)] if nk > 1 else [],
        compiler_params=_cp(("parallel", "parallel", "arbitrary")), name=name)(*args)
    return outs[0] if n_out == 1 else outs


def mm_nn(a, w, *, nsh=1, tm, tn, tk, out_dtypes=(F32,), add=None, name):
    M, K = a.shape
    n = w.shape[1]
    N = nsh * n
    assert w.shape[0] == nsh * K and n % tn == 0 and K % tk == 0 and M % tm == 0
    npt, kt = n // tn, K // tk
    return _mm(a, w, mode="nn", grid=(M // tm, N // tn, kt),
               blocks=((tm, tk), (tk, tn), (tm, tn)),
               maps=(lambda i, j, k: (i, k), lambda i, j, k: ((j // npt) * kt + k, j % npt),
                     lambda i, j, k: (i, j)),
               out_shape=(M, N), out_dtypes=out_dtypes, add=add, name=name)


def mm_nt(d, w, *, nsh=1, tm, to, tc, out_dtypes=(F32,), add=None, name):
    M, N = d.shape
    n = w.shape[1]
    K = w.shape[0] // nsh
    assert nsh * n == N and n % tc == 0 and K % to == 0 and M % tm == 0
    cpt, ot = n // tc, K // to
    return _mm(d, w, mode="nt", grid=(M // tm, ot, N // tc),
               blocks=((tm, tc), (to, tc), (tm, to)),
               maps=(lambda i, j, c: (i, c), lambda i, j, c: ((c // cpt) * ot + j, c % cpt),
                     lambda i, j, c: (i, j)),
               out_shape=(M, K), out_dtypes=out_dtypes, add=add, name=name)


def mm_nt_pair(d1, w1, d2, w2, *, nsh, tm, to, tc, out_dtype, name):
    M, N = d1.shape
    n = w1.shape[1]
    K = w1.shape[0] // nsh
    assert d2.shape == d1.shape and w2.shape == w1.shape and nsh * n == N
    assert n % tc == 0 and K % to == 0 and M % tm == 0
    cpt, ot, nk = n // tc, K // to, N // tc

    def body(a1_ref, b1_ref, a2_ref, b2_ref, o_ref, acc):
        k = pl.program_id(2)

        @pl.when(k == 0)
        def _():
            acc[...] = jnp.zeros_like(acc)

        acc[...] += lax.dot_general(a1_ref[...], b1_ref[...], _DN["nt"], preferred_element_type=F32)
        acc[...] += lax.dot_general(a2_ref[...], b2_ref[...], _DN["nt"], preferred_element_type=F32)

        @pl.when(k == nk - 1)
        def _():
            o_ref[...] = acc[...].astype(o_ref.dtype)

    a_spec = pl.BlockSpec((tm, tc), lambda i, j, c: (i, c))
    b_spec = pl.BlockSpec((to, tc), lambda i, j, c: ((c // cpt) * ot + j, c % cpt))
    return pl.pallas_call(
        body, grid=(M // tm, ot, nk), in_specs=[a_spec, b_spec, a_spec, b_spec],
        out_specs=pl.BlockSpec((tm, to), lambda i, j, c: (i, j)),
        out_shape=jax.ShapeDtypeStruct((M, K), out_dtype),
        scratch_shapes=[pltpu.VMEM((tm, to), F32)],
        compiler_params=_cp(("parallel", "parallel", "arbitrary")), name=name)(d1, w1, d2, w2)


def mm_tn(x, d, *, nsh=1, to, tn, tk, out_dtypes=(F32,), name):
    M, K = x.shape
    N = d.shape[1]
    n = N // nsh
    assert n % tn == 0 and K % to == 0 and M % tk == 0
    npt, ot = n // tn, K // to
    return _mm(x, d, mode="tn", grid=(ot, N // tn, M // tk),
               blocks=((tk, to), (tk, tn), (to, tn)),
               maps=(lambda i, j, k: (k, i), lambda i, j, k: (k, j),
                     lambda i, j, k: ((j // npt) * ot + i, j % npt)),
               out_shape=(nsh * K, n), out_dtypes=out_dtypes, name=name)


def _rms_fwd(x, g, *, C, cb=0, name):
    S = x.shape[0]
    tm = _big_row_tile(S) if S >= 128 else S

    def body(x_ref, g_ref, o_ref):
        xv = x_ref[...]
        r = lax.rsqrt(jnp.mean(xv * xv, axis=-1, keepdims=True) + NORM_EPS)
        o_ref[...] = ((xv * r) * g_ref[...]).astype(o_ref.dtype)

    return pl.pallas_call(
        body, grid=(S // tm,),
        in_specs=[pl.BlockSpec((tm, C), lambda i: (i, cb)), pl.BlockSpec((1, C), lambda i: (0, 0))],
        out_specs=pl.BlockSpec((tm, C), lambda i: (i, 0)),
        out_shape=jax.ShapeDtypeStruct((S, C), BF16),
        compiler_params=_cp(("parallel",)), name=name)(x, g)


def _rms_bwd(x, dh, g, *, C, cb=0, res=None, out_dtypes=(F32,), name):
    S = x.shape[0]
    tm = (_big_row_tile(S) if C <= 512 else _row_tile(S)) if S >= 128 else S
    n_out = len(out_dtypes)

    def body(*refs):
        x_ref, dh_ref, g_ref = refs[:3]
        res_ref = refs[3] if res is not None else None
        p = 3 + (res is not None)
        outs = refs[p:p + n_out]
        dg_ref = refs[p + n_out]
        i = pl.program_id(0)
        xv = x_ref[...]
        r = lax.rsqrt(jnp.mean(xv * xv, axis=-1, keepdims=True) + NORM_EPS)
        n = xv * r
        dhv = dh_ref[...].astype(F32)
        dn = dhv * g_ref[...]
        c = jnp.mean(dn * n, axis=-1, keepdims=True)
        dx = r * (dn - n * c)
        if res_ref is not None:
            dx = res_ref[...] + dx
        for o in outs:
            o[...] = dx.astype(o.dtype)

        @pl.when(i == 0)
        def _():
            dg_ref[...] = jnp.zeros_like(dg_ref)

        dg_ref[...] += jnp.sum(dhv * n, axis=0, keepdims=True)

    row = pl.BlockSpec((tm, C), lambda i: (i, 0))
    in_specs = [pl.BlockSpec((tm, C), lambda i: (i, cb)), row, pl.BlockSpec((1, C), lambda i: (0, 0))]
    args = [x, dh, g]
    if res is not None:
        in_specs.append(row)
        args.append(res)
    return pl.pallas_call(
        body, grid=(S // tm,), in_specs=in_specs,
        out_specs=[row] * n_out + [pl.BlockSpec((1, C), lambda i: (0, 0))],
        out_shape=[jax.ShapeDtypeStruct((S, C), d) for d in out_dtypes] + [jax.ShapeDtypeStruct((1, C), F32)],
        compiler_params=_cp(("arbitrary",)), name=name)(*args)


def _pool_cnt(t0, rows, w):
    t = t0 + lax.broadcasted_iota(jnp.int32, (rows, 1), 0)
    return jnp.minimum(t + 1, w).astype(F32)


def _pool_d(halo, tile, gi, t0, tm):
    s = jnp.concatenate([halo, tile], axis=0)
    for step in (1, 2, 4, 8)[:gi + 1]:
        s = s + pltpu.roll(s, step, 0)
    return s[16:] / _pool_cnt(t0, tm, POOL_WINDOWS[gi]) - tile


def _pool_fwd(z, w_pool, pool_scale, *, name):
    S = z.shape[0]
    tm = _row_tile(S)
    hb = tm // 16

    def body(z_ref, h_ref, w_ref, sc_ref, o_ref):
        i = pl.program_id(0)
        halo = h_ref[...] * (i > 0).astype(F32)
        for gi in range(4):
            cs = slice(gi * 128, (gi + 1) * 128)
            d = _pool_d(halo[:, cs], z_ref[:, cs], gi, i * tm, tm)
            yp = jnp.dot(d.astype(BF16), w_ref[gi], preferred_element_type=F32)
            o_ref[:, cs] = (yp * sc_ref[:, cs]).astype(o_ref.dtype)

    return pl.pallas_call(
        body, grid=(S // tm,),
        in_specs=[pl.BlockSpec((tm, POOL_W), lambda i: (i, Z_POOL_CB)),
                  pl.BlockSpec((16, POOL_W), lambda i: (jnp.maximum(i * hb - 1, 0), Z_POOL_CB)),
                  pl.BlockSpec((4, 128, 128), lambda i: (0, 0, 0)),
                  pl.BlockSpec((1, POOL_W), lambda i: (0, 0))],
        out_specs=pl.BlockSpec((tm, POOL_W), lambda i: (i, 0)),
        out_shape=jax.ShapeDtypeStruct((S, POOL_W), BF16),
        compiler_params=_cp(("parallel",)), name=name)(z, z, w_pool, pool_scale)


def _pool_bwd(z, d_cat, w_pool, pool_scale, *, name):
    S = z.shape[0]
    tm = _row_tile(S)
    hb = tm // 16
    nt = S // tm
    E = tm + 16

    def body(z_ref, h_ref, dy_ref, dyn_ref, w_ref, sc_ref, dz_ref, gw_ref, gs_ref):
        i = pl.program_id(0)

        @pl.when(i == 0)
        def _():
            gw_ref[...] = jnp.zeros_like(gw_ref)
            gs_ref[...] = jnp.zeros_like(gs_ref)

        halo = h_ref[...] * (i > 0).astype(F32)
        dy_next = dyn_ref[...].astype(F32) * (i < nt - 1).astype(F32)
        for gi in range(4):
            cs = slice(gi * 128, (gi + 1) * 128)
            w = w_ref[gi]
            d = _pool_d(halo[:, cs], z_ref[:, cs], gi, i * tm, tm)
            db = d.astype(BF16)
            dy = dy_ref[:, cs].astype(F32)
            yp = jnp.dot(db, w, preferred_element_type=F32)
            gs_ref[:, cs] += jnp.sum(dy * yp, axis=0, keepdims=True)
            sc = sc_ref[:, cs]
            dys = (dy * sc).astype(BF16)
            gw_ref[gi] += lax.dot_general(db, dys, _DN["tn"], preferred_element_type=F32)
            dys_ext = jnp.concatenate([dys, (dy_next[:, cs] * sc).astype(BF16)], axis=0)
            dd = lax.dot_general(dys_ext, w, _DN["nt"], preferred_element_type=F32)
            r = dd / _pool_cnt(i * tm, E, POOL_WINDOWS[gi])
            for step in (1, 2, 4, 8)[:gi + 1]:
                r = r + pltpu.roll(r, E - step, 0)
            dz_ref[:, cs] = (r[:tm] - dd[:tm]).astype(dz_ref.dtype)

    return pl.pallas_call(
        body, grid=(nt,),
        in_specs=[pl.BlockSpec((tm, POOL_W), lambda i: (i, Z_POOL_CB)),
                  pl.BlockSpec((16, POOL_W), lambda i: (jnp.maximum(i * hb - 1, 0), Z_POOL_CB)),
                  pl.BlockSpec((tm, POOL_W), lambda i: (i, 0)),
                  pl.BlockSpec((16, POOL_W), lambda i: (jnp.minimum((i + 1) * hb, S // 16 - 1), 0)),
                  pl.BlockSpec((4, 128, 128), lambda i: (0, 0, 0)),
                  pl.BlockSpec((1, POOL_W), lambda i: (0, 0))],
        out_specs=[pl.BlockSpec((tm, POOL_W), lambda i: (i, 0)),
                   pl.BlockSpec((4, 128, 128), lambda i: (0, 0, 0)),
                   pl.BlockSpec((1, POOL_W), lambda i: (0, 0))],
        out_shape=[jax.ShapeDtypeStruct((S, POOL_W), BF16),
                   jax.ShapeDtypeStruct((4, 128, 128), F32),
                   jax.ShapeDtypeStruct((1, POOL_W), F32)],
        compiler_params=_cp(("arbitrary",)), name=name)(z, z, d_cat, d_cat, w_pool, pool_scale)


def _rope_tables(S):
    half = ROPE // 2
    inv_freq = 1.0 / (ROPE_THETA ** (jnp.arange(half, dtype=F32) / half))
    ang = jnp.arange(S).astype(F32)[:, None] * inv_freq[None, :]
    cos, sin = jnp.cos(ang), jnp.sin(ang)
    zero = jnp.zeros((S, half), F32)
    cos_t = jnp.concatenate([cos, cos, zero, zero], axis=1)
    sa_t = jnp.concatenate([-sin, zero, zero, zero], axis=1)
    sb_t = jnp.concatenate([zero, sin, zero, zero], axis=1)
    return cos_t, sa_t, sb_t


def _head_fwd(xn, xr, gn, gr, cos, sa, sb):
    ms = (jnp.sum(xn * xn, axis=-1, keepdims=True) + jnp.sum(xr * xr, axis=-1, keepdims=True)) * (1.0 / QK)
    r = lax.rsqrt(ms + NORM_EPS)
    on = (xn * r) * gn
    yr = (xr * r) * gr
    orr = yr * cos + pltpu.roll(yr, 96, 1) * sa + pltpu.roll(yr, 32, 1) * sb
    return on, orr


def _head_bwd(xn, xr, gn, gr, don, dor, cos, sa, sb):
    ms = (jnp.sum(xn * xn, axis=-1, keepdims=True) + jnp.sum(xr * xr, axis=-1, keepdims=True)) * (1.0 / QK)
    r = lax.rsqrt(ms + NORM_EPS)
    nn, nr = xn * r, xr * r
    dyr = dor * cos + pltpu.roll(dor * sa, 32, 1) + pltpu.roll(dor * sb, 96, 1)
    ggn, ggr = don * nn, dyr * nr
    dnn, dnr = don * gn, dyr * gr
    c = (jnp.sum(dnn * nn, axis=-1, keepdims=True) + jnp.sum(dnr * nr, axis=-1, keepdims=True)) * (1.0 / QK)
    return r * (dnn - nn * c), r * (dnr - nr * c), ggn, ggr


def _kv_cols(h):
    base = (h // 2) * 512 + (h % 2) * 128
    return base, base + 256


def _qkrope_fwd(qraw, kvraw, z, gq, gk, tabs, *, name):
    S = qraw.shape[0]
    tm = _row_tile(S)

    def body(q_ref, kv_ref, zkr_ref, gq_ref, gk_ref, cos_ref, sa_ref, sb_ref, qo_ref, ko_ref, vo_ref, vt_ref):
        rope = (cos_ref[...], sa_ref[...], sb_ref[...])
        zkr = zkr_ref[...]
        gqn, gqr, gkn, gkr = gq_ref[:, :128], gq_ref[:, 128:], gk_ref[:, :128], gk_ref[:, 128:]
        for h in range(HEADS):
            b = h * HEAD_PAD
            on, orr = _head_fwd(q_ref[:, b:b + 128], q_ref[:, b + 128:b + 256], gqn, gqr, *rope)
            qo_ref[:, b:b + 128] = on.astype(BF16)
            qo_ref[:, b + 128:b + 256] = orr.astype(BF16)
            kc, vc = _kv_cols(h)
            on, orr = _head_fwd(kv_ref[:, kc:kc + 128], zkr, gkn, gkr, *rope)
            ko_ref[:, b:b + 128] = on.astype(BF16)
            ko_ref[:, b + 128:b + 256] = orr.astype(BF16)
            vv = kv_ref[:, vc:vc + 128]
            vo_ref[:, h * 128:(h + 1) * 128] = vv.astype(BF16)
            vt_ref[h * 128:(h + 1) * 128, :] = jnp.transpose(vv).astype(BF16)

    W = HEADS * HEAD_PAD
    row = lambda c: pl.BlockSpec((tm, c), lambda i: (i, 0))
    vec = lambda c: pl.BlockSpec((1, c), lambda i: (0, 0))
    return pl.pallas_call(
        body, grid=(S // tm,),
        in_specs=[row(W), row(W), pl.BlockSpec((tm, 128), lambda i: (i, Z_KR_CB)), vec(256), vec(256),
                  row(128), row(128), row(128)],
        out_specs=[row(W), row(W), row(HEADS * 128), pl.BlockSpec((HEADS * 128, tm), lambda i: (0, i))],
        out_shape=[jax.ShapeDtypeStruct((S, W), BF16), jax.ShapeDtypeStruct((S, W), BF16),
                   jax.ShapeDtypeStruct((S, HEADS * 128), BF16), jax.ShapeDtypeStruct((HEADS * 128, S), BF16)],
        compiler_params=_cp(("parallel",)), name=name)(qraw, kvraw, z, gq, gk, *tabs)


def _qkrope_bwd(qraw, kvraw, z, gq, gk, tabs, dq, dk, dv, *, name):
    S = qraw.shape[0]
    tm = _row_tile(S)

    def body(q_ref, kv_ref, zkr_ref, gq_ref, gk_ref, cos_ref, sa_ref, sb_ref, dq_ref, dk_ref, dv_ref,
             dqo_ref, dkvo_ref, dkr_ref, ggq_ref, ggk_ref):
        i = pl.program_id(0)

        @pl.when(i == 0)
        def _():
            ggq_ref[...] = jnp.zeros_like(ggq_ref)
            ggk_ref[...] = jnp.zeros_like(ggk_ref)

        rope = (cos_ref[...], sa_ref[...], sb_ref[...])
        zkr = zkr_ref[...]
        gqn, gqr, gkn, gkr = gq_ref[:, :128], gq_ref[:, 128:], gk_ref[:, :128], gk_ref[:, 128:]
        dkr = jnp.zeros((tm, 128), F32)
        sq_n = jnp.zeros((1, 128), F32)
        sq_r = jnp.zeros((1, 128), F32)
        sk_n = jnp.zeros((1, 128), F32)
        sk_r = jnp.zeros((1, 128), F32)
        for h in range(HEADS):
            b = h * HEAD_PAD
            dxn, dxr, ggn, ggr = _head_bwd(q_ref[:, b:b + 128], q_ref[:, b + 128:b + 256], gqn, gqr,
                                           dq_ref[:, b:b + 128], dq_ref[:, b + 128:b + 256], *rope)
            dqo_ref[:, b:b + 128] = dxn.astype(BF16)
            dqo_ref[:, b + 128:b + 256] = dxr.astype(BF16)
            sq_n += jnp.sum(ggn, axis=0, keepdims=True)
            sq_r += jnp.sum(ggr, axis=0, keepdims=True)
            kc, vc = _kv_cols(h)
            dxn, dxr, ggn, ggr = _head_bwd(kv_ref[:, kc:kc + 128], zkr, gkn, gkr,
                                           dk_ref[:, b:b + 128], dk_ref[:, b + 128:b + 256], *rope)
            dkvo_ref[:, kc:kc + 128] = dxn.astype(BF16)
            dkvo_ref[:, vc:vc + 128] = dv_ref[:, h * 128:(h + 1) * 128].astype(BF16)
            dkr += dxr
            sk_n += jnp.sum(ggn, axis=0, keepdims=True)
            sk_r += jnp.sum(ggr, axis=0, keepdims=True)
        dkr_ref[...] = dkr.astype(BF16)
        ggq_ref[:, :128] += sq_n
        ggq_ref[:, 128:] += sq_r
        ggk_ref[:, :128] += sk_n
        ggk_ref[:, 128:] += sk_r

    W = HEADS * HEAD_PAD
    row = lambda c: pl.BlockSpec((tm, c), lambda i: (i, 0))
    vec = lambda c: pl.BlockSpec((1, c), lambda i: (0, 0))
    return pl.pallas_call(
        body, grid=(S // tm,),
        in_specs=[row(W), row(W), pl.BlockSpec((tm, 128), lambda i: (i, Z_KR_CB)), vec(256), vec(256),
                  row(128), row(128), row(128), row(W), row(W), row(HEADS * 128)],
        out_specs=[row(W), row(W), row(128), vec(256), vec(256)],
        out_shape=[jax.ShapeDtypeStruct((S, W), BF16), jax.ShapeDtypeStruct((S, W), BF16),
                   jax.ShapeDtypeStruct((S, 128), BF16),
                   jax.ShapeDtypeStruct((1, 256), F32), jax.ShapeDtypeStruct((1, 256), F32)],
        compiler_params=_cp(("arbitrary",)), name=name)(qraw, kvraw, z, gq, gk, *tabs, dq, dk, dv)


LOG2E = 1.4426950408889634
SCORE_SCALE = 1.0 / math.sqrt(QK)
SCORE_SCALE_LOG2 = SCORE_SCALE * LOG2E


def _fa_tile(S):
    return 512 if S % 512 == 0 and S >= 2048 else 128


def _flash_fwd(q, k, vt, *, name):
    S = q.shape[0]
    ts = _fa_tile(S)
    tq = 2 * ts

    def body(q_ref, k_ref, vt_ref, o_ref, ob_ref, lse_ref, m_sc, l_sc, acc_sc, s_buf):
        qi = pl.program_id(1)
        m_sc[...] = jnp.full_like(m_sc, -jnp.inf)
        l_sc[...] = jnp.zeros_like(l_sc)
        acc_sc[...] = jnp.zeros_like(acc_sc)
        qb = q_ref[...]

        def scores(kidx):
            k0 = pl.multiple_of(kidx * ts, ts)
            return lax.dot_general(k_ref[pl.ds(k0, ts), :], qb, _DN["nt"], preferred_element_type=F32)

        def causal(st, j):
            key = lax.broadcasted_iota(jnp.int32, (ts, tq), 0) + j * ts
            return jnp.where(key > lax.broadcasted_iota(jnp.int32, (ts, tq), 1), -jnp.inf, st)

        def update(st, kidx):
            k0 = pl.multiple_of(kidx * ts, ts)
            m_prev = m_sc[...]
            m_new = jnp.maximum(m_prev, jnp.max(st, axis=0, keepdims=True))
            alpha = jnp.exp2((m_prev - m_new) * SCORE_SCALE_LOG2)
            pt = jnp.exp2((st - m_new[0:1, :]) * SCORE_SCALE_LOG2)
            l_sc[...] = alpha * l_sc[...] + jnp.sum(pt, axis=0, keepdims=True)
            acc_sc[...] = alpha[0:1, :] * acc_sc[...] + jnp.dot(vt_ref[:, pl.ds(k0, ts)], pt.astype(BF16),
                                                                preferred_element_type=F32)
            m_sc[...] = m_new

        s_buf[0] = scores(0)

        def trip(u, carry):
            s_buf[1] = scores(2 * u + 1)
            update(s_buf[0], 2 * u)
            s_buf[0] = scores(2 * u + 2)
            update(s_buf[1], 2 * u + 1)
            return carry

        lax.fori_loop(0, qi, trip, 0)
        s_buf[1] = scores(2 * qi + 1)
        update(causal(s_buf[0], 0), 2 * qi)
        update(causal(s_buf[1], 1), 2 * qi + 1)
        ot = acc_sc[...] / l_sc[0:1, :]
        o = jnp.transpose(ot)
        o_ref[...] = o
        ob_ref[...] = o.astype(BF16)
        lse_ref[...] = m_sc[...] * SCORE_SCALE_LOG2 + jnp.log2(l_sc[...])

    return pl.pallas_call(
        body, grid=(HEADS, S // tq),
        in_specs=[pl.BlockSpec((tq, HEAD_PAD), lambda h, i: (i, h)),
                  pl.BlockSpec((S, HEAD_PAD), lambda h, i: (0, h)),
                  pl.BlockSpec((128, S), lambda h, i: (h, 0))],
        out_specs=[pl.BlockSpec((tq, 128), lambda h, i: (i, h)),
                   pl.BlockSpec((tq, 128), lambda h, i: (i, h)),
                   pl.BlockSpec((None, 8, tq), lambda h, i: (h, 0, i))],
        out_shape=[jax.ShapeDtypeStruct((S, HEADS * 128), F32), jax.ShapeDtypeStruct((S, HEADS * 128), BF16),
                   jax.ShapeDtypeStruct((HEADS, 8, S), F32)],
        scratch_shapes=[pltpu.VMEM((8, tq), F32), pltpu.VMEM((8, tq), F32), pltpu.VMEM((128, tq), F32),
                        pltpu.VMEM((2, ts, tq), F32)],
        compiler_params=_cp(("parallel", "arbitrary")), name=name)(q, k, vt)


def _attn_bwd_prep(o, d_cat, *, name):
    S = o.shape[0]
    tm = _row_tile(S)
    H = HEADS * 128
    half = H // 2

    def body(o_ref, da_ref, db_ref, delta_ref):
        for h in range(HEADS):
            src, c0 = (da_ref, h * 128) if h * 128 < half else (db_ref, h * 128 - half)
            do = src[:, c0:c0 + 128].astype(F32)
            prod = jnp.transpose(do * o_ref[:, h * 128:(h + 1) * 128])
            delta_ref[h] = jnp.broadcast_to(jnp.sum(prod, axis=0, keepdims=True), (8, tm))

    return pl.pallas_call(
        body, grid=(S // tm,),
        in_specs=[pl.BlockSpec((tm, H), lambda i: (i, 0)),
                  pl.BlockSpec((tm, half), lambda i: (i, 1)), pl.BlockSpec((tm, half), lambda i: (i, 2))],
        out_specs=pl.BlockSpec((HEADS, 8, tm), lambda i: (0, 0, i)),
        out_shape=jax.ShapeDtypeStruct((HEADS, 8, S), F32),
        compiler_params=_cp(("parallel",)), name=name)(o, d_cat, d_cat)


def _flash_bwd(q, k, v, d_cat, lse, delta, *, name):
    S = q.shape[0]
    ts = _fa_tile(S)
    nb = S // ts

    def body(q_ref, do_ref, lse_ref, delta_ref, k_ref, v_ref, dq_ref, dk_ref, dv_ref, dk_sc, dv_sc):
        j = pl.program_id(1)

        @pl.when(j == 0)
        def _():
            dq_ref[...] = jnp.zeros_like(dq_ref)

        dk_sc[...] = jnp.zeros_like(dk_sc)
        dv_sc[...] = jnp.zeros_like(dv_sc)
        kb, vb = k_ref[...], v_ref[...]

        def products(i):
            q0 = pl.multiple_of(i * ts, ts)
            qb = q_ref[pl.ds(q0, ts), :]
            dob_ = do_ref[pl.ds(q0, ts), :]
            st = lax.dot_general(kb, qb, _DN["nt"], preferred_element_type=F32)
            dpt = lax.dot_general(vb, dob_, _DN["nt"], preferred_element_type=F32)
            return q0, qb, dob_, st, dpt

        def accumulate(q0, qb, dob_, st, dpt, masked):
            pt = jnp.exp2(st * SCORE_SCALE_LOG2 - lse_ref[0:1, pl.ds(q0, ts)])
            if masked:
                pt = jnp.where(lax.broadcasted_iota(jnp.int32, (ts, ts), 0) > lax.broadcasted_iota(jnp.int32, (ts, ts), 1),
                               0.0, pt)
            dv_sc[...] += jnp.dot(pt.astype(BF16), dob_, preferred_element_type=F32)
            dst = (pt * (dpt - delta_ref[0:1, pl.ds(q0, ts)])).astype(BF16)
            dk_sc[...] += jnp.dot(dst, qb, preferred_element_type=F32) * SCORE_SCALE
            dq_ref[pl.ds(q0, ts), :] += lax.dot_general(dst, kb, _DN["tn"], preferred_element_type=F32) * SCORE_SCALE

        accumulate(*products(j), True)
        n_below = nb - 1 - j

        def pair(t, carry):
            a, b = products(j + 1 + 2 * t), products(j + 2 + 2 * t)
            accumulate(*a, False)
            accumulate(*b, False)
            return carry

        lax.fori_loop(0, n_below // 2, pair, 0)

        @pl.when(n_below % 2 == 1)
        def _():
            accumulate(*products(nb - 1), False)

        dk_ref[...] = dk_sc[...]
        dv_ref[...] = dv_sc[...]

    return pl.pallas_call(
        body, grid=(HEADS, nb),
        in_specs=[pl.BlockSpec((S, HEAD_PAD), lambda h, j: (0, h)),
                  pl.BlockSpec((S, 128), lambda h, j: (0, 4 + h)),
                  pl.BlockSpec((None, 8, S), lambda h, j: (h, 0, 0)),
                  pl.BlockSpec((None, 8, S), lambda h, j: (h, 0, 0)),
                  pl.BlockSpec((ts, HEAD_PAD), lambda h, j: (j, h)),
                  pl.BlockSpec((ts, 128), lambda h, j: (j, h))],
        out_specs=[pl.BlockSpec((S, HEAD_PAD), lambda h, j: (0, h)),
                   pl.BlockSpec((ts, HEAD_PAD), lambda h, j: (j, h)),
                   pl.BlockSpec((ts, 128), lambda h, j: (j, h))],
        out_shape=[jax.ShapeDtypeStruct((S, HEADS * HEAD_PAD), F32), jax.ShapeDtypeStruct((S, HEADS * HEAD_PAD), F32),
                   jax.ShapeDtypeStruct((S, HEADS * 128), F32)],
        scratch_shapes=[pltpu.VMEM((ts, HEAD_PAD), F32), pltpu.VMEM((ts, 128), F32)],
        compiler_params=_cp(("parallel", "arbitrary")), name=name)(q, d_cat, lse, delta, k, v)


def _memk_fwd(mkv, gkx, *, name):
    M = mkv.shape[0]
    XW = X_HEADS * X_DIM

    def body(mkv_ref, g_ref, k_ref, v_ref):
        for h in range(X_HEADS):
            cs = slice(h * X_DIM, (h + 1) * X_DIM)
            xv = mkv_ref[:, cs]
            r = lax.rsqrt(jnp.mean(xv * xv, axis=-1, keepdims=True) + NORM_EPS)
            k_ref[:, cs] = ((xv * r) * g_ref[...]).astype(BF16)
        v_ref[...] = mkv_ref[:, XW:].astype(BF16)

    return pl.pallas_call(
        body, grid=(1,),
        in_specs=[pl.BlockSpec((M, 2 * XW), lambda i: (0, 0)), pl.BlockSpec((1, X_DIM), lambda i: (0, 0))],
        out_specs=[pl.BlockSpec((M, XW), lambda i: (0, 0)), pl.BlockSpec((M, XW), lambda i: (0, 0))],
        out_shape=[jax.ShapeDtypeStruct((M, XW), BF16), jax.ShapeDtypeStruct((M, XW), BF16)],
        compiler_params=_cp(("arbitrary",)), name=name)(mkv, gkx)


def _memk_bwd(mkv, gkx, dk, dv, *, name):
    M = mkv.shape[0]
    XW = X_HEADS * X_DIM

    def body(mkv_ref, g_ref, dk_ref, dv_ref, o_ref, gg_ref):
        gg = jnp.zeros((1, X_DIM), F32)
        for h in range(X_HEADS):
            cs = slice(h * X_DIM, (h + 1) * X_DIM)
            xv = mkv_ref[:, cs]
            r = lax.rsqrt(jnp.mean(xv * xv, axis=-1, keepdims=True) + NORM_EPS)
            n = xv * r
            dkv = dk_ref[:, cs]
            gg += jnp.sum(dkv * n, axis=0, keepdims=True)
            dn = dkv * g_ref[...]
            c = jnp.mean(dn * n, axis=-1, keepdims=True)
            o_ref[:, cs] = (r * (dn - n * c)).astype(BF16)
        o_ref[:, XW:] = dv_ref[...].astype(BF16)
        gg_ref[...] = gg

    full = lambda c: pl.BlockSpec((M, c), lambda i: (0, 0))
    return pl.pallas_call(
        body, grid=(1,),
        in_specs=[full(2 * XW), pl.BlockSpec((1, X_DIM), lambda i: (0, 0)), full(XW), full(XW)],
        out_specs=[full(2 * XW), pl.BlockSpec((1, X_DIM), lambda i: (0, 0))],
        out_shape=[jax.ShapeDtypeStruct((M, 2 * XW), BF16), jax.ShapeDtypeStruct((1, X_DIM), F32)],
        compiler_params=_cp(("arbitrary",)), name=name)(mkv, gkx, dk, dv)


def _xq_norm(z_ref, g_ref, h):
    xv = z_ref[:, h * X_DIM:(h + 1) * X_DIM]
    r = lax.rsqrt(jnp.mean(xv * xv, axis=-1, keepdims=True) + NORM_EPS)
    n = xv * r
    return n, r, n * g_ref[...]


def _xprobs(qb, k_ref, h):
    s = lax.dot_general(qb, k_ref[:, h * X_DIM:(h + 1) * X_DIM], _DN["nt"],
                        preferred_element_type=F32) * (1.0 / math.sqrt(X_DIM))
    e = jnp.exp(s - jnp.max(s, axis=-1, keepdims=True))
    return e / jnp.sum(e, axis=-1, keepdims=True)


def _memattn_fwd(z, kx, vx, gqx, *, name):
    S = z.shape[0]
    M = kx.shape[0]
    tm = _big_row_tile(S)
    XW = X_HEADS * X_DIM

    def body(z_ref, k_ref, v_ref, g_ref, o_ref):
        for h in range(X_HEADS):
            cs = slice(h * X_DIM, (h + 1) * X_DIM)
            _, _, qn = _xq_norm(z_ref, g_ref, h)
            p = _xprobs(qn.astype(BF16), k_ref, h)
            o_ref[:, cs] = jnp.dot(p.astype(BF16), v_ref[:, cs], preferred_element_type=F32).astype(BF16)

    return pl.pallas_call(
        body, grid=(S // tm,),
        in_specs=[pl.BlockSpec((tm, XW), lambda i: (i, Z_MQ_CB)), pl.BlockSpec((M, XW), lambda i: (0, 0)),
                  pl.BlockSpec((M, XW), lambda i: (0, 0)), pl.BlockSpec((1, X_DIM), lambda i: (0, 0))],
        out_specs=pl.BlockSpec((tm, XW), lambda i: (i, 0)),
        out_shape=jax.ShapeDtypeStruct((S, XW), BF16),
        compiler_params=_cp(("parallel",)), name=name)(z, kx, vx, gqx)


def _memattn_bwd(z, kx, vx, gqx, d_cat, *, name):
    S = z.shape[0]
    M = kx.shape[0]
    tm = _big_row_tile(S)
    XW = X_HEADS * X_DIM
    scale = 1.0 / math.sqrt(X_DIM)

    def body(z_ref, k_ref, v_ref, g_ref, do_ref, dz_ref, dk_ref, dv_ref, gg_ref):
        i = pl.program_id(0)

        @pl.when(i == 0)
        def _():
            dk_ref[...] = jnp.zeros_like(dk_ref)
            dv_ref[...] = jnp.zeros_like(dv_ref)
            gg_ref[...] = jnp.zeros_like(gg_ref)

        gg = jnp.zeros((1, X_DIM), F32)
        for h in range(X_HEADS):
            cs = slice(h * X_DIM, (h + 1) * X_DIM)
            n, r, qn = _xq_norm(z_ref, g_ref, h)
            qb = qn.astype(BF16)
            p = _xprobs(qb, k_ref, h)
            pb = p.astype(BF16)
            dob = do_ref[:, cs].astype(BF16)
            dv_ref[:, cs] += lax.dot_general(pb, dob, _DN["tn"], preferred_element_type=F32)
            dp = lax.dot_general(dob, v_ref[:, cs], _DN["nt"], preferred_element_type=F32)
            ds = (p * (dp - jnp.sum(dp * p, axis=-1, keepdims=True))).astype(BF16)
            dk_ref[:, cs] += lax.dot_general(ds, qb, _DN["tn"], preferred_element_type=F32) * scale
            dqn = jnp.dot(ds, k_ref[:, cs], preferred_element_type=F32) * scale
            gg += jnp.sum(dqn * n, axis=0, keepdims=True)
            dn = dqn * g_ref[...]
            c = jnp.mean(dn * n, axis=-1, keepdims=True)
            dz_ref[:, cs] = (r * (dn - n * c)).astype(BF16)
        gg_ref[...] += gg

    full = pl.BlockSpec((M, XW), lambda i: (0, 0))
    vec = pl.BlockSpec((1, X_DIM), lambda i: (0, 0))
    return pl.pallas_call(
        body, grid=(S // tm,),
        in_specs=[pl.BlockSpec((tm, XW), lambda i: (i, Z_MQ_CB)), full, full, vec,
                  pl.BlockSpec((tm, XW), lambda i: (i, 3))],
        out_specs=[pl.BlockSpec((tm, XW), lambda i: (i, 0)), full, full, vec],
        out_shape=[jax.ShapeDtypeStruct((S, XW), BF16), jax.ShapeDtypeStruct((M, XW), F32),
                   jax.ShapeDtypeStruct((M, XW), F32), jax.ShapeDtypeStruct((1, X_DIM), F32)],
        compiler_params=_cp(("arbitrary",)), name=name)(z, kx, vx, gqx, d_cat)


def _silu_parts(x):
    h = 0.5 * x
    return h, jnp.tanh(h)


GLU_HALO = 16


def _glu_tiles(S, F):
    return _big_row_tile(S), _pick(F, (1408, 512, 256, 128))


def _glu_fwd(g, u, conv_w, conv_b, *, name):
    S, F = g.shape
    tm, tc = _glu_tiles(S, F)
    hb = tm // GLU_HALO

    def body(g_ref, gp_ref, u_ref, w_ref, b_ref, a_ref):
        i = pl.program_id(1)
        gt = g_ref[...].astype(F32)
        ext = jnp.concatenate([gp_ref[...].astype(F32) * (i > 0).astype(F32), gt], axis=0)
        gc = b_ref[...] + w_ref[0:1, :] * pltpu.roll(ext, 2, 0)[GLU_HALO:]
        gc = gc + w_ref[1:2, :] * pltpu.roll(ext, 1, 0)[GLU_HALO:]
        gc = gc + w_ref[2:3, :] * gt
        h, t = _silu_parts(gc)
        a_ref[...] = ((h * (1.0 + t)) * u_ref[...].astype(F32)).astype(BF16)

    return pl.pallas_call(
        body, grid=(F // tc, S // tm),
        in_specs=[pl.BlockSpec((tm, tc), lambda j, i: (i, j)),
                  pl.BlockSpec((GLU_HALO, tc), lambda j, i: (jnp.maximum(i * hb - 1, 0), j)),
                  pl.BlockSpec((tm, tc), lambda j, i: (i, j)),
                  pl.BlockSpec((3, tc), lambda j, i: (0, j)),
                  pl.BlockSpec((1, tc), lambda j, i: (0, j))],
        out_specs=pl.BlockSpec((tm, tc), lambda j, i: (i, j)),
        out_shape=jax.ShapeDtypeStruct((S, F), BF16),
        compiler_params=_cp(("parallel", "parallel")), name=name)(g, g, u, conv_w, conv_b)


def _glu_bwd(g, u, d_a, conv_w, conv_b, *, name):
    S, F = g.shape
    tm, tc = _glu_tiles(S, F)
    hb = tm // GLU_HALO
    nt = S // tm
    E = tm + GLU_HALO

    def body(g_ref, gp_ref, gn_ref, u_ref, un_ref, da_ref, dan_ref, w_ref, b_ref,
             dg_ref, du_ref, gw_ref, gb_ref):
        i = pl.program_id(1)

        @pl.when(i == 0)
        def _():
            gw_ref[...] = jnp.zeros_like(gw_ref)
            gb_ref[...] = jnp.zeros_like(gb_ref)

        w0, w1, w2 = w_ref[0:1, :], w_ref[1:2, :], w_ref[2:3, :]
        gext = jnp.concatenate([gp_ref[...].astype(F32) * (i > 0).astype(F32), g_ref[...].astype(F32),
                                gn_ref[...].astype(F32)], axis=0)
        g1 = pltpu.roll(gext, 1, 0)[GLU_HALO:]
        g2 = pltpu.roll(gext, 2, 0)[GLU_HALO:]
        g0 = gext[GLU_HALO:]
        gc = b_ref[...] + w0 * g2
        gc = gc + w1 * g1
        gc = gc + w2 * g0
        h, t = _silu_parts(gc)
        t1 = 1.0 + t
        da = jnp.concatenate([da_ref[...].astype(F32), dan_ref[...].astype(F32) * (i < nt - 1).astype(F32)], axis=0)
        uu = jnp.concatenate([u_ref[...].astype(F32), un_ref[...].astype(F32)], axis=0)
        du_ref[...] = (da[:tm] * (h[:tm] * t1[:tm])).astype(BF16)
        dgc = (da * uu) * (0.5 * (t1 + h * (1.0 - t * t)))
        dg = w2 * dgc[:tm] + w1 * pltpu.roll(dgc, E - 1, 0)[:tm] + w0 * pltpu.roll(dgc, E - 2, 0)[:tm]
        dg_ref[...] = dg.astype(BF16)
        dgt = dgc[:tm]
        gb_ref[...] += jnp.sum(dgt, axis=0, keepdims=True)
        gw_ref[0:1, :] += jnp.sum(dgt * g2[:tm], axis=0, keepdims=True)
        gw_ref[1:2, :] += jnp.sum(dgt * g1[:tm], axis=0, keepdims=True)
        gw_ref[2:3, :] += jnp.sum(dgt * g0[:tm], axis=0, keepdims=True)

    tile = pl.BlockSpec((tm, tc), lambda j, i: (i, j))
    nxt = pl.BlockSpec((GLU_HALO, tc), lambda j, i: (jnp.minimum((i + 1) * hb, S // GLU_HALO - 1), j))
    prv = pl.BlockSpec((GLU_HALO, tc), lambda j, i: (jnp.maximum(i * hb - 1, 0), j))
    return pl.pallas_call(
        body, grid=(F // tc, nt),
        in_specs=[tile, prv, nxt, tile, nxt, tile, nxt,
                  pl.BlockSpec((3, tc), lambda j, i: (0, j)), pl.BlockSpec((1, tc), lambda j, i: (0, j))],
        out_specs=[tile, tile, pl.BlockSpec((3, tc), lambda j, i: (0, j)), pl.BlockSpec((1, tc), lambda j, i: (0, j))],
        out_shape=[jax.ShapeDtypeStruct((S, F), BF16), jax.ShapeDtypeStruct((S, F), BF16),
                   jax.ShapeDtypeStruct((3, F), F32), jax.ShapeDtypeStruct((1, F), F32)],
        compiler_params=_cp(("parallel", "arbitrary")), name=name)(g, g, g, u, u, d_a, d_a, conv_w, conv_b)


def _loss_head(y, target, *, name):
    S, D = y.shape
    tm = _big_row_tile(S)
    nt = S // tm

    def body(y_ref, t_ref, dy_ref, dyb_ref, loss_ref, acc):
        i = pl.program_id(0)

        @pl.when(i == 0)
        def _():
            acc[...] = jnp.zeros_like(acc)

        e = y_ref[...] - t_ref[...]
        dy = e * (1.0 / D)
        dy_ref[...] = dy
        dyb_ref[...] = dy.astype(BF16)
        acc[...] += jnp.sum(e * e, axis=0, keepdims=True)

        @pl.when(i == nt - 1)
        def _():
            loss_ref[...] = jnp.broadcast_to(jnp.sum(acc[...], axis=1, keepdims=True) * (0.5 / D), (1, 128))

    row = pl.BlockSpec((tm, D), lambda i: (i, 0))
    return pl.pallas_call(
        body, grid=(nt,), in_specs=[row, row],
        out_specs=[row, row, pl.BlockSpec((1, 128), lambda i: (0, 0))],
        out_shape=[jax.ShapeDtypeStruct((S, D), F32), jax.ShapeDtypeStruct((S, D), BF16),
                   jax.ShapeDtypeStruct((1, 128), F32)],
        scratch_shapes=[pltpu.VMEM((1, D), F32)],
        compiler_params=_cp(("arbitrary",)), name=name)(y, target)


def _adamw_math(w, g, m, v):
    m = ADAM_B1 * m + (1.0 - ADAM_B1) * g
    v = ADAM_B2 * v + (1.0 - ADAM_B2) * (g * g)
    m_hat = m / (1.0 - ADAM_B1 ** ADAM_STEP)
    v_hat = v / (1.0 - ADAM_B2 ** ADAM_STEP)
    delta = -ADAM_LR * (m_hat / (jnp.sqrt(v_hat) + ADAM_EPS) + ADAM_WD * w)
    return delta, m, v


def _adamw(w, m, v, parts, *, name):
    R, C = w.shape
    tr = 128 if R % 128 == 0 else R
    n_parts = len(parts)

    def body(*refs):
        w_ref, m_ref, v_ref = refs[:3]
        p_refs = refs[3:3 + n_parts]
        g_ref, d_ref, mo_ref, vo_ref = refs[3 + n_parts:]
        g = p_refs[0][...]
        for p in p_refs[1:]:
            g = g + p[...]
        delta, mn, vn = _adamw_math(w_ref[...], g, m_ref[...], v_ref[...])
        g_ref[...] = g
        d_ref[...] = delta
        mo_ref[...] = mn
        vo_ref[...] = vn

    blk = pl.BlockSpec((tr, C), lambda i: (i, 0))
    return pl.pallas_call(
        body, grid=(R // tr,), in_specs=[blk] * (3 + n_parts), out_specs=[blk] * 4,
        out_shape=[jax.ShapeDtypeStruct((R, C), F32)] * 4,
        compiler_params=_cp(("parallel",)), name=name)(w, m, v, *parts)


def _sum4(g_stack, recv, me, *, name):
    _, R, C = g_stack.shape
    tr = 128 if R % 128 == 0 else R

    def body(me_ref, g_ref, r_ref, o_ref):
        acc = g_ref[...]
        for j in range(N_CHIPS - 1):
            acc = acc + r_ref[j].astype(F32)
        o_ref[...] = acc

    grid_spec = pltpu.PrefetchScalarGridSpec(
        num_scalar_prefetch=1, grid=(R // tr,),
        in_specs=[pl.BlockSpec((None, tr, C), lambda i, me_ref: (me_ref[0], i, 0)),
                  pl.BlockSpec((N_CHIPS - 1, tr, C), lambda i, me_ref: (0, i, 0))],
        out_specs=pl.BlockSpec((tr, C), lambda i, me_ref: (i, 0)))
    return pl.pallas_call(
        body, grid_spec=grid_spec, out_shape=jax.ShapeDtypeStruct((R, C), F32),
        compiler_params=_cp(("parallel",)), name=name)(me, g_stack, recv)


def _sum8(gathered, *, name):
    _, R, C = gathered.shape

    def body(g_ref, o_ref):
        acc = g_ref[0]
        for d in range(1, N_DEV):
            acc = acc + g_ref[d]
        o_ref[...] = acc

    return pl.pallas_call(
        body, grid=(1,), in_specs=[pl.BlockSpec((N_DEV, R, C), lambda i: (0, 0, 0))],
        out_specs=pl.BlockSpec((R, C), lambda i: (0, 0)),
        out_shape=jax.ShapeDtypeStruct((R, C), F32),
        compiler_params=_cp(("arbitrary",)), name=name)(gathered)


def _place():
    return lax.axis_index("x"), lax.axis_index("y"), lax.axis_index("c")


def _other_chips(x, y):
    return [(1 - x, y), (x, 1 - y), (1 - x, 1 - y)]


_ANY = pl.BlockSpec(memory_space=pl.ANY)


_HBM = pl.BlockSpec(memory_space=pltpu.HBM)
_SEM = pl.BlockSpec(memory_space=pltpu.SEMAPHORE)
_EFFECT = pltpu.SideEffectType.DATAFLOW_SIDE_EFFECTING


def _gather_copies(srcs, lands, send_sems, recv_sems):
    x, y, c = _place()
    me = 2 * x + y
    return [pltpu.make_async_remote_copy(
        src_ref=srcs[i], dst_ref=lands[i].at[me], send_sem=send_sems.at[3 * i + j],
        recv_sem=recv_sems.at[3 * i + j], device_id=(px, py, c), device_id_type=MESH)
        for i in range(len(srcs)) for j, (px, py) in enumerate(_other_chips(x, y))]


def _scatter_copies(srcs, lands, send_sems, recv_sems):
    x, y, c = _place()
    return [pltpu.make_async_remote_copy(
        src_ref=srcs[i].at[2 * px + py], dst_ref=lands[i].at[j], send_sem=send_sems.at[3 * i + j],
        recv_sem=recv_sems.at[3 * i + j], device_id=(px, py, c), device_id_type=MESH)
        for i in range(len(srcs)) for j, (px, py) in enumerate(_other_chips(x, y))]


def _sibling_copies(srcs, lands, send_sems, recv_sems):
    x, y, c = _place()
    return [pltpu.make_async_remote_copy(
        src_ref=srcs[i], dst_ref=lands[i], send_sem=send_sems.at[i], recv_sem=recv_sems.at[i],
        device_id=(x, y, 1 - c), device_id_type=MESH) for i in range(len(srcs))]


def _copies_start(srcs, lands, make_copies, thru=(), *, per=N_CHIPS - 1, name):
    n = len(srcs)
    n_ops = 2 * n + len(thru)

    def body(*refs):
        send_sems, recv_sems = refs[n_ops], refs[n_ops + 1]
        for cp in make_copies(refs[:n], refs[n:2 * n], send_sems, recv_sems):
            cp.start()
        refs[-1][...] = jnp.zeros_like(refs[-1])

    ops = list(srcs) + list(lands) + list(thru)
    outs = pl.pallas_call(
        body, name=name,
        out_shape=(pltpu.SemaphoreType.DMA((per * n,)), pltpu.SemaphoreType.DMA((per * n,)),
                   *[pltpu.HBM(a.shape, a.dtype) for a in ops], jax.ShapeDtypeStruct((8, 128), F32)),
        in_specs=[_HBM] * n_ops,
        out_specs=(_SEM, _SEM, *[_HBM] * n_ops, pl.BlockSpec(memory_space=pltpu.VMEM)),
        input_output_aliases={i: 2 + i for i in range(n_ops)},
        compiler_params=pltpu.CompilerParams(has_side_effects=_EFFECT),
    )(*[pltpu.with_memory_space_constraint(a, pltpu.HBM) for a in ops])
    return outs[0], outs[1], list(outs[2:2 + n]), list(outs[2 + n:2 + 2 * n]), outs[-1], list(outs[2 + 2 * n:-1])


def _copies_wait(handle, after, make_copies, *, with_srcs=False, name):
    send_sems, recv_sems, srcs, lands = handle[:4]
    n = len(srcs)

    def body(*refs):
        for cp in make_copies(refs[:n], refs[n:2 * n], refs[2 * n], refs[2 * n + 1]):
            cp.wait_send()
            cp.wait_recv()

    ops = list(srcs) + list(lands)
    outs = pl.pallas_call(
        body, name=name,
        out_shape=tuple(pltpu.HBM(a.shape, a.dtype) for a in ops),
        in_specs=[_HBM] * (2 * n) + [_SEM, _SEM, _ANY],
        out_specs=tuple([_HBM] * (2 * n)),
        input_output_aliases={i: i for i in range(2 * n)},
        compiler_params=pltpu.CompilerParams(has_side_effects=_EFFECT),
    )(*ops, send_sems, recv_sems, after)
    return (list(outs[:n]), list(outs[n:])) if with_srcs else list(outs[n:])


def _everyone_copies(srcs, lands, send_sems, recv_sems):
    x, y, c = _place()
    me = 4 * x + 2 * y + c
    copies = []
    for i in range(len(srcs)):
        for k in range(1, N_DEV):
            px = 1 - x if (k >> 2) & 1 else x
            py = 1 - y if (k >> 1) & 1 else y
            pc = 1 - c if k & 1 else c
            copies.append(pltpu.make_async_remote_copy(
                src_ref=srcs[i], dst_ref=lands[i].at[me], send_sem=send_sems.at[(N_DEV - 1) * i + k - 1],
                recv_sem=recv_sems.at[(N_DEV - 1) * i + k - 1], device_id=(px, py, pc), device_id_type=MESH))
    return copies


def _w_in_to_z(w):
    pad = jnp.zeros(w.shape[:-1] + (64,), w.dtype)
    return jnp.concatenate([w[..., 0:512], w[..., 512:1024], w[..., 1344:1856], w[..., 1024:1280],
                            w[..., 1280:1344], pad], axis=-1)


def _z_to_w_in(g):
    return jnp.concatenate([g[..., 0:512], g[..., 512:1024], g[..., 1536:1792], g[..., 1792:1856],
                            g[..., 1024:1536]], axis=-1)


def _pad_heads(w, nh):
    w = w.reshape(w.shape[:-1] + (nh, QK))
    w = jnp.concatenate([w, jnp.zeros(w.shape[:-1] + (HEAD_PAD - QK,), w.dtype)], axis=-1)
    return w.reshape(w.shape[:-2] + (nh * HEAD_PAD,))


def _unpad_heads(g, nh):
    g = g.reshape(g.shape[:-1] + (nh, HEAD_PAD))[..., :QK]
    return g.reshape(g.shape[:-2] + (nh * QK,))


def _kv_split(w, nh):
    w = w.reshape(w.shape[:-1] + (nh, 2, 128))
    return jnp.swapaxes(w, -3, -2).reshape(w.shape[:-3] + (nh * 256,))


def _kv_join(g, nh):
    g = g.reshape(g.shape[:-1] + (2, nh, 128))
    return jnp.swapaxes(g, -3, -2).reshape(g.shape[:-3] + (nh * 256,))


def _pad_gain(g):
    return jnp.concatenate([g, jnp.zeros((1, HEAD_PAD - QK), g.dtype)], axis=1)


_SMALL = ("g_mix", "g_q_lat", "g_kv_lat", "g_q_mla", "g_k_mla", "w_pool", "pool_scale", "g_mem", "g_q_x",
          "g_k_x", "g_ffn", "conv_b", "conv_w")


def _pack(arrs, extra=0):
    flat = jnp.concatenate([a.reshape(-1) for a in arrs])
    n = flat.shape[0] + extra
    rows = -(-n // 1024) * 8
    return jnp.pad(flat, (0, rows * 128 - flat.shape[0])).reshape(rows, 128)


def _unpack(buf, shapes):
    flat = buf.reshape(-1)
    out, off = [], 0
    for s in shapes:
        n = int(np.prod(s))
        out.append(flat[off:off + n].reshape(s))
        off += n
    return out, off


def _tied(a, token):
    return a + token[:1, :1].astype(a.dtype)


def _local_step(x, mem, target, W, fetch=None, ship=None):
    fetch = fetch or (lambda group, after: None)
    ship = ship or (lambda group, G: jnp.zeros((8, 128), F32))
    S, D = x.shape
    F = W["conv_b"].shape[1]
    tabs = _rope_tables(S)
    tm = 512 if S % 512 == 0 else 128
    tl = 1024 if S % 1024 == 0 else tm
    tk = _pick(S, (1024, 512, 128))

    h = _rms_fwd(x, W["g_mix"], C=D, name="norm_mix")
    fetch("g1", h)
    z = mm_nn(h, W["w_in"], tm=tl, tn=Z_COLS, tk=D, name="z_proj")
    fetch("g2", z)
    y_pool = _pool_fwd(z, W["w_pool"], W["pool_scale"], name="pool_fwd")
    ql = _rms_fwd(z, W["g_q_lat"], C=Q_RANK, cb=Z_Q_CB, name="norm_qlat")
    kvl = _rms_fwd(z, W["g_kv_lat"], C=KV_RANK, cb=Z_KV_CB, name="norm_kvlat")
    qraw = mm_nn(ql, W["w_q_up"], nsh=N_CHIPS, tm=tl, tn=512, tk=Q_RANK, name="q_up")
    kvraw = mm_nn(kvl, W["w_kv_up"], nsh=N_CHIPS, tm=tl, tn=512, tk=KV_RANK, name="kv_up")
    q, k, v, vt = _qkrope_fwd(qraw, kvraw, z, W["g_q_mla"], W["g_k_mla"], tabs, name="qk_norm_rope")
    o, y_mla, lse = _flash_fwd(q, k, vt, name="mla_fwd")
    memn = _rms_fwd(mem, W["g_mem"], C=D, name="norm_mem")
    M = mem.shape[0]
    mkv = mm_nn(memn, W["w_mem_kv"], tm=M, tn=1024, tk=D, name="mem_kv")
    kx, vx = _memk_fwd(mkv, W["g_k_x"], name="memk_fwd")
    y_mem = _memattn_fwd(z, kx, vx, W["g_q_x"], name="memattn_fwd")
    cat = jnp.concatenate([y_pool, y_mla, y_mem], axis=1)
    fetch("g2b", cat)
    x2 = mm_nn(cat, W["w_o"], tm=tm, tn=D, tk=D, add=x, name="o_proj")
    h2 = _rms_fwd(x2, W["g_ffn"], C=D, name="norm_ffn")
    fetch("g3", h2)
    fn = F // N_CHIPS
    g = mm_nn(h2, W["w_gate"], nsh=N_CHIPS, tm=tl, tn=fn, tk=D, out_dtypes=(BF16,), name="gate_proj")
    u = mm_nn(h2, W["w_up"], nsh=N_CHIPS, tm=tl, tn=fn, tk=D, out_dtypes=(BF16,), name="up_proj")
    a = _glu_fwd(g, u, W["conv_w"], W["conv_b"], name="glu_fwd")
    y = mm_nn(a, W["w_down"], tm=tm, tn=1024, tk=F // 2, add=x2, name="down_proj")
    dy, dyb, loss_row = _loss_head(y, target, name="loss_head")

    G = {}
    d_a = mm_nt(dyb, W["w_down"], tm=tl, to=512, tc=D, out_dtypes=(BF16,), name="d_a")
    G["w_down"] = mm_tn(a, dyb, to=fn, tn=1024, tk=tk, out_dtypes=(F32, BF16), name="grad_w_down")
    d_g, d_u, G["conv_w"], G["conv_b"] = _glu_bwd(g, u, d_a, W["conv_w"], W["conv_b"], name="glu_bwd")
    G["w_gate"] = mm_tn(h2, d_g, nsh=N_CHIPS, to=1024, tn=fn, tk=tk, out_dtypes=(F32, BF16), name="grad_w_gate")
    G["w_up"] = mm_tn(h2, d_u, nsh=N_CHIPS, to=1024, tn=fn, tk=tk, out_dtypes=(F32, BF16), name="grad_w_up")
    tok = ship("s1", G)
    d_h2 = mm_nt_pair(d_g, W["w_gate"], d_u, W["w_up"], nsh=N_CHIPS, tm=tm, to=1024, tc=fn, out_dtype=BF16,
                      name="d_h2")
    d_x2, d_x2b, G["g_ffn"] = _rms_bwd(x2, d_h2, _tied(W["g_ffn"], tok), C=D, res=dy, out_dtypes=(F32, BF16),
                                       name="norm_ffn_bwd")

    d_cat = mm_nt(d_x2b, W["w_o"], tm=tl, to=1024, tc=D, out_dtypes=(BF16,), name="d_cat")
    G["w_o"] = mm_tn(cat, d_x2b, to=1024, tn=1024, tk=tk, out_dtypes=(F32, BF16), name="grad_w_o")
    tok = ship("s2", G)
    dz_pool, G["w_pool"], G["pool_scale"] = _pool_bwd(z, d_cat, W["w_pool"], _tied(W["pool_scale"], tok),
                                                      name="pool_bwd")
    dz_mq, dkx, dvx, G["g_q_x"] = _memattn_bwd(z, kx, vx, W["g_q_x"], d_cat, name="memattn_bwd")
    d_mkv, G["g_k_x"] = _memk_bwd(mkv, W["g_k_x"], dkx, dvx, name="memk_bwd")
    G["w_mem_kv"] = mm_tn(memn, d_mkv, to=1024, tn=1024, tk=M, out_dtypes=(F32, BF16), name="grad_w_mem_kv")
    d_memn = mm_nt(d_mkv, W["w_mem_kv"], tm=M, to=D, tc=1024, name="d_memn")
    _, G["g_mem"] = _rms_bwd(mem, d_memn, W["g_mem"], C=D, name="norm_mem_bwd")
    delta = _attn_bwd_prep(o, d_cat, name="mla_bwd_prep")
    dq, dk, dv = _flash_bwd(q, k, v, d_cat, lse, delta, name="mla_bwd")
    d_qraw, d_kvraw, dz_kr, G["g_q_mla"], G["g_k_mla"] = _qkrope_bwd(
        qraw, kvraw, z, W["g_q_mla"], W["g_k_mla"], tabs, dq, dk, dv, name="qk_norm_rope_bwd")
    G["w_q_up"] = mm_tn(ql, d_qraw, nsh=N_CHIPS, to=Q_RANK, tn=512, tk=tk, out_dtypes=(F32, BF16), name="grad_w_q_up")
    d_ql = mm_nt(d_qraw, W["w_q_up"], nsh=N_CHIPS, tm=tl, to=Q_RANK, tc=512, name="d_ql")
    G["w_kv_up"] = mm_tn(kvl, d_kvraw, nsh=N_CHIPS, to=KV_RANK, tn=512, tk=tk, out_dtypes=(F32, BF16),
                         name="grad_w_kv_up")
    d_kvl = mm_nt(d_kvraw, W["w_kv_up"], nsh=N_CHIPS, tm=tl, to=KV_RANK, tc=512, name="d_kvl")
    dz_q, G["g_q_lat"] = _rms_bwd(z, d_ql, W["g_q_lat"], C=Q_RANK, cb=Z_Q_CB, out_dtypes=(BF16,), name="norm_qlat_bwd")
    dz_kv, G["g_kv_lat"] = _rms_bwd(z, d_kvl, W["g_kv_lat"], C=KV_RANK, cb=Z_KV_CB, out_dtypes=(BF16,),
                                    name="norm_kvlat_bwd")
    d_z = jnp.concatenate([dz_pool, dz_q, dz_mq, dz_kv, dz_kr], axis=1)
    G["w_in"] = mm_tn(h, d_z, to=512, tn=Z_COLS, tk=tk, out_dtypes=(F32, BF16), name="grad_w_in")
    tok = ship("s3", G)
    d_h = mm_nt(d_z, W["w_in"], tm=tl, to=1024, tc=Z_COLS, out_dtypes=(BF16,), name="d_h")
    grad_x, G["g_mix"] = _rms_bwd(x, d_h, _tied(W["g_mix"], tok), C=D, res=d_x2, name="norm_mix_bwd")
    return loss_row, grad_x, G


_BIG = ("w_in", "w_q_up", "w_kv_up", "w_mem_kv", "w_o", "w_gate", "w_up", "w_down")
_WEIGHTS = ("g_mix", "w_in", "g_q_lat", "w_q_up", "g_kv_lat", "w_kv_up", "g_q_mla", "g_k_mla", "w_pool",
            "pool_scale", "g_mem", "w_mem_kv", "g_q_x", "g_k_x", "w_o", "g_ffn", "w_gate", "w_up", "conv_w",
            "conv_b", "w_down")


def _to_compute_layout(name, w):
    if name == "w_in":
        return _w_in_to_z(w)
    if name == "w_q_up":
        return _pad_heads(w, w.shape[-1] // QK)
    if name == "w_kv_up":
        return _kv_split(w, w.shape[-1] // 256)
    return w


def _from_compute_layout(name, g):
    if name == "w_in":
        return _z_to_w_in(g)
    if name == "w_q_up":
        return _unpad_heads(g, g.shape[-1] // HEAD_PAD)
    if name == "w_kv_up":
        return _kv_join(g, g.shape[-1] // 256)
    return g


def kernel(x, mem, g_mix, w_in, g_q_lat, w_q_up, g_kv_lat, w_kv_up, g_q_mla, g_k_mla, w_pool, pool_scale, g_mem, w_mem_kv, g_q_x, g_k_x, w_o, g_ffn, w_gate, w_up, conv_w, conv_b, w_down, loss_target, m_g_mix, m_w_in, m_g_q_lat, m_w_q_up, m_g_kv_lat, m_w_kv_up, m_g_q_mla, m_g_k_mla, m_w_pool, m_pool_scale, m_g_mem, m_w_mem_kv, m_g_q_x, m_g_k_x, m_w_o, m_g_ffn, m_w_gate, m_w_up, m_conv_w, m_conv_b, m_w_down, v_g_mix, v_w_in, v_g_q_lat, v_w_q_up, v_g_kv_lat, v_w_kv_up, v_g_q_mla, v_g_k_mla, v_w_pool, v_pool_scale, v_g_mem, v_w_mem_kv, v_g_q_x, v_g_k_x, v_w_o, v_g_ffn, v_w_gate, v_w_up, v_conv_w, v_conv_b, v_w_down):
    P = dict(g_mix=g_mix, w_in=w_in, g_q_lat=g_q_lat, w_q_up=w_q_up, g_kv_lat=g_kv_lat, w_kv_up=w_kv_up,
             g_q_mla=g_q_mla, g_k_mla=g_k_mla, w_pool=w_pool, pool_scale=pool_scale, g_mem=g_mem,
             w_mem_kv=w_mem_kv, g_q_x=g_q_x, g_k_x=g_k_x, w_o=w_o, g_ffn=g_ffn, w_gate=w_gate, w_up=w_up,
             conv_w=conv_w, conv_b=conv_b, w_down=w_down)
    Mo = dict(g_mix=m_g_mix, w_in=m_w_in, g_q_lat=m_g_q_lat, w_q_up=m_w_q_up, g_kv_lat=m_g_kv_lat,
              w_kv_up=m_w_kv_up, g_q_mla=m_g_q_mla, g_k_mla=m_g_k_mla, w_pool=m_w_pool,
              pool_scale=m_pool_scale, g_mem=m_g_mem, w_mem_kv=m_w_mem_kv, g_q_x=m_g_q_x, g_k_x=m_g_k_x,
              w_o=m_w_o, g_ffn=m_g_ffn, w_gate=m_w_gate, w_up=m_w_up, conv_w=m_conv_w, conv_b=m_conv_b,
              w_down=m_w_down)
    Vo = dict(g_mix=v_g_mix, w_in=v_w_in, g_q_lat=v_g_q_lat, w_q_up=v_w_q_up, g_kv_lat=v_g_kv_lat,
              w_kv_up=v_w_kv_up, g_q_mla=v_g_q_mla, g_k_mla=v_g_k_mla, w_pool=v_w_pool,
              pool_scale=v_pool_scale, g_mem=v_g_mem, w_mem_kv=v_w_mem_kv, g_q_x=v_g_q_x, g_k_x=v_g_k_x,
              w_o=v_w_o, g_ffn=v_g_ffn, w_gate=v_w_gate, w_up=v_w_up, conv_w=v_conv_w, conv_b=v_conv_b,
              w_down=v_w_down)
    xi, yi, ci = _place()
    me = (2 * xi + yi).astype(jnp.int32).reshape(1)

    shard = {n: _to_compute_layout(n, P[n][0]).astype(BF16) for n in _BIG}
    shard["conv_w"] = conv_w[0]
    gather_groups = {"g1": ("w_in",), "g2": ("w_q_up", "w_kv_up", "w_mem_kv"), "g2b": ("w_o",),
                     "g3": ("w_gate", "w_up", "w_down", "conv_w")}
    gathers = {}

    def landing(n):
        s = shard[n]
        return lax.dynamic_update_slice(lax.empty((N_CHIPS,) + s.shape, s.dtype), s[None], (me[0], 0, 0))

    def start_gather(grp, thru=()):
        names = gather_groups[grp]
        gathers[grp] = _copies_start([shard[n] for n in names], [landing(n) for n in names], _gather_copies, thru,
                                     name="gather_start_" + grp)
        return gathers[grp][5]

    start_gather("g1")
    W = {}
    W["g_q_mla"], W["g_k_mla"] = _pad_gain(g_q_mla), _pad_gain(g_k_mla)
    W["w_pool"] = w_pool[0].astype(BF16)
    for n in ("g_mix", "g_q_lat", "g_kv_lat", "pool_scale", "g_mem", "g_q_x", "g_k_x", "g_ffn", "conv_b"):
        W[n] = P[n]
    W["g_mix"] = _tied(W["g_mix"], gathers["g1"][4])

    def fetch(grp, after):
        stacks = _copies_wait(gathers[grp], after, _gather_copies, name="gather_wait_" + grp)
        if grp == "g1":
            stacks = start_gather("g3", start_gather("g2b", start_gather("g2", stacks)))
        for n, s in zip(gather_groups[grp], stacks):
            if n == "conv_w":
                W[n] = jnp.swapaxes(s, 0, 1).reshape(3, -1)
            else:
                W[n] = s.reshape(-1, s.shape[-1])

    shard_shape = {n: shard[n].shape for n in _BIG}
    scatter_groups = {"s1": ("w_down", "w_gate", "w_up"), "s2": ("w_o",),
                      "s3": ("w_mem_kv", "w_q_up", "w_kv_up", "w_in")}
    scatters = {}

    def parts_of(n, g):
        return g.reshape((N_CHIPS,) + shard_shape[n])

    def ship(grp, G):
        names = scatter_groups[grp]
        srcs = [parts_of(n, G[n][1]) for n in names]
        lands = [lax.empty((N_CHIPS - 1,) + shard_shape[n], BF16) for n in names]
        scatters[grp] = _copies_start(srcs, lands, _scatter_copies, name="scatter_start_" + grp)
        return scatters[grp][4]

    loss_row, grad_x, G = _local_step(x[0], mem[0], loss_target[0], W, fetch, ship)

    swaps, part = {}, {}
    after = grad_x
    for grp, names in (("s1", scatter_groups["s1"]), ("s23", scatter_groups["s2"] + scatter_groups["s3"])):
        recv = {}
        for sg_ in (("s1",) if grp == "s1" else ("s2", "s3")):
            for n, r in zip(scatter_groups[sg_],
                            _copies_wait(scatters[sg_], after, _scatter_copies, name="scatter_wait_" + sg_)):
                recv[n] = r
        for n in names:
            part[n] = _from_compute_layout(n, _sum4(parts_of(n, G[n][0]), recv[n], me, name="sum4_" + n))
        swaps[grp] = _copies_start([part[n] for n in names], [lax.empty(part[n].shape, F32) for n in names],
                                   _sibling_copies, per=1, name="swap_start_" + grp)
        after = swaps[grp][4]

    conv_w_full_grad = G["conv_w"]
    small_g = [G["g_mix"], G["g_q_lat"], G["g_kv_lat"], G["g_q_mla"][:, :QK], G["g_k_mla"][:, :QK], G["w_pool"],
               G["pool_scale"], G["g_mem"], G["g_q_x"], G["g_k_x"], G["g_ffn"], G["conv_b"], conv_w_full_grad]
    packed = _tied(_pack(small_g + [loss_row[:, :1]]), after)
    me8 = 4 * xi + 2 * yi + ci
    everyone = _copies_start([packed], [lax.dynamic_update_slice(lax.empty((N_DEV,) + packed.shape, F32),
                                                                 packed[None], (me8, 0, 0))],
                             _everyone_copies, per=N_DEV - 1, name="gather_small_start")
    out = {}
    after = everyone[4]
    for grp, names in (("s1", scatter_groups["s1"]), ("s23", scatter_groups["s2"] + scatter_groups["s3"])):
        mine, sib = _copies_wait(swaps[grp], after, _sibling_copies, with_srcs=True, name="swap_wait_" + grp)
        for n, p, s in zip(names, mine, sib):
            out[n] = [r[None] for r in _adamw(P[n][0], Mo[n][0], Vo[n][0], [p, s], name="adamw_" + n)]
        after = out[names[-1]][0]
    total = _sum8(_copies_wait(everyone, after, _everyone_copies, name="gather_small_wait")[0], name="sum_small")
    shapes = [a.shape for a in small_g] + [(1, 1)]
    (parts, _) = _unpack(total, shapes)
    loss = parts[-1].reshape(())
    F = conv_b.shape[1]
    fn = F // N_CHIPS
    col0 = (2 * xi + yi) * fn
    sg = dict(zip(_SMALL, parts[:-1]))
    sg["conv_w"] = lax.dynamic_slice(sg["conv_w"], (0, col0), (3, fn))
    sw = [P[n].reshape(sg[n].shape) for n in _SMALL]
    sm = [Mo[n].reshape(sg[n].shape) for n in _SMALL]
    sv = [Vo[n].reshape(sg[n].shape) for n in _SMALL]
    gp = _pack([sg[n] for n in _SMALL])
    res = _adamw(_pack(sw), _pack(sm), _pack(sv), [gp], name="adamw_small")
    sshapes = [sg[n].shape for n in _SMALL]
    for kind, buf in zip(range(4), res):
        vals, _ = _unpack(buf, sshapes)
        for n, val in zip(_SMALL, vals):
            out.setdefault(n, [None] * 4)[kind] = val.reshape(P[n].shape)

    return (loss, grad_x[None], *[out[n][0] for n in _WEIGHTS], *[out[n][1] for n in _WEIGHTS],
            *[out[n][2] for n in _WEIGHTS], *[out[n][3] for n in _WEIGHTS])
```
